```python
import numpy as np
import jax
import jax.numpy as jnp
from jax import lax

D_MODEL = 2048
BATCH = 32
SEQ = 256
DEPTH = 2
DEC_BATCH = 4
DEC_SEQ = 4096
PAST_LEN = 256

GRID_W = 64
N_AB = (DEPTH + 1) // 2
N_CD = DEPTH // 2
EPS = 1e-6
CHUNK = 64
Q_BLOCK = 128
ROPE_BASE = 10000.0

GLA_HEADS = 8
GLA_DK = 64
GLA_DV = 128
GATE_RANK = 16
GATE_TAU = 16.0

MLA_HEADS = 8
MLA_Q_LORA = 512
MLA_KV_LORA = 256
MLA_NOPE = 128
MLA_ROPE = 64
MLA_DV = 128

RET_HEADS = 8
RET_DK = 64
RET_DV = 128

NA_HEADS = 8
NA_HD = 128
NA_ROWS = 8
NA_COLS = 16
NA_BAND = 2 * NA_COLS
NA_NCB = GRID_W // NA_COLS

N_EXPERTS = 32
N_GROUPS = 8
EXPERTS_PER_GROUP = N_EXPERTS // N_GROUPS
TOP_K = 2
D_EXPERT = 512
MOE_BLOCK = 128

D_MIX = GLA_HEADS * GLA_DV + MLA_HEADS * MLA_DV
AB_SPLITS = (GLA_HEADS * GLA_DK, GLA_HEADS * GLA_DK, GLA_HEADS * GLA_DV, GLA_HEADS * GLA_DV,
             2 * GATE_RANK, MLA_Q_LORA, MLA_KV_LORA, MLA_ROPE)
CD_SPLITS = (RET_HEADS * RET_DK, RET_HEADS * RET_DK, RET_HEADS * RET_DV, RET_HEADS * RET_DV,
             NA_HEADS * NA_HD, NA_HEADS * NA_HD, NA_HEADS * NA_HD)
AB_WIDTH = sum(AB_SPLITS)
CD_WIDTH = sum(CD_SPLITS)

kernel_name = 'hybrid_diffusion_prefix_step'

F32 = jnp.float32


def rmsnorm(x, g=None):
    xf = x.astype(F32)
    y = xf * lax.rsqrt(jnp.mean(xf * xf, axis=-1, keepdims=True) + EPS)
    if g is not None:
        y = y * g.astype(F32)
    return y.astype(x.dtype)


def split_cols(z, sizes):
    return jnp.split(z, np.cumsum(sizes)[:-1].tolist(), axis=-1)


def to_heads(x, n_heads):
    b, n, _ = x.shape
    return x.reshape(b, n, n_heads, -1).transpose(0, 2, 1, 3)


def from_heads(x):
    b, h, n, d = x.shape
    return x.transpose(0, 2, 1, 3).reshape(b, n, h * d)


def flip(x):
    return jnp.flip(x, axis=2)


def apply_axial_rope(x):
    n, d = x.shape[-2], x.shape[-1]
    quarter = d // 4
    t = jnp.arange(n)
    row = (t // GRID_W).astype(F32)
    col = (t % GRID_W).astype(F32)
    inv_freq = ROPE_BASE ** (-jnp.arange(quarter, dtype=F32) / quarter)

    def rot(xh, pos):
        ang = pos[:, None] * inv_freq[None, :]
        cos, sin = jnp.cos(ang), jnp.sin(ang)
        x1, x2 = xh[..., :quarter], xh[..., quarter:]
        return jnp.concatenate([x1 * cos - x2 * sin, x1 * sin + x2 * cos], axis=-1)

    xf = x.astype(F32)
    out = jnp.concatenate([rot(xf[..., :d // 2], row), rot(xf[..., d // 2:], col)], axis=-1)
    return out.astype(x.dtype)


def attention(q, k, v, scale):
    b, h, n, d = q.shape
    nb = n // Q_BLOCK
    qb = q.reshape(b, h, nb, Q_BLOCK, d).transpose(2, 0, 1, 3, 4)

    def one_block(qi):
        s = jnp.einsum('bhqd,bhkd->bhqk', qi, k).astype(F32) * scale
        p = jax.nn.softmax(s, axis=-1)
        return jnp.einsum('bhqk,bhkd->bhqd', p.astype(v.dtype), v)

    o = lax.map(one_block, qb)
    return o.transpose(1, 2, 0, 3, 4).reshape(b, h, n, v.shape[-1])


def to_chunks(x):
    b, h, n, d = x.shape
    return x.reshape(b, h, n // CHUNK, CHUNK, d).transpose(2, 0, 1, 3, 4)


def from_chunks(x):
    nc, b, h, l, d = x.shape
    return x.transpose(1, 2, 0, 3, 4).reshape(b, h, nc * l, d)


def gla_scan(q, k, v, log_a, s0):
    causal = jnp.tril(jnp.ones((CHUNK, CHUNK), dtype=bool))[:, :, None]

    def step(s, inp):
        qc, kc, vc, ac = inp
        bcum = jnp.cumsum(ac, axis=-2)
        o_inter = jnp.einsum('bhld,bhdv->bhlv', qc * jnp.exp(bcum), s)
        diff = bcum[:, :, :, None, :] - bcum[:, :, None, :, :]
        decay = jnp.where(causal, jnp.exp(jnp.where(causal, diff, 0.0)), 0.0)
        scores = jnp.einsum('bhtd,bhsd,bhtsd->bhts', qc, kc, decay)
        o = o_inter + jnp.einsum('bhts,bhsv->bhtv', scores, vc)
        b_last = bcum[:, :, -1:, :]
        s_new = jnp.exp(b_last[:, :, 0, :, None]) * s + jnp.einsum('bhsd,bhsv->bhdv', kc * jnp.exp(b_last - bcum), vc)
        return s_new, o

    xs = tuple(to_chunks(t.astype(F32)) for t in (q, k, v, log_a))
    s_fin, o = lax.scan(step, s0.astype(F32), xs)
    return from_chunks(o), s_fin


def retention_scan(q, k, v, log_gamma, s0):
    lg = log_gamma.astype(F32)[:, None]
    pos = jnp.arange(CHUNK, dtype=F32)
    dist = pos[:, None] - pos[None, :]
    causal = (dist >= 0)[None]
    intra = jnp.where(causal, jnp.exp(jnp.where(causal, dist[None], 0.0) * lg[:, :, None]), 0.0)
    inter = jnp.exp((pos[None, :] + 1.0) * lg)
    to_state = jnp.exp((CHUNK - 1.0 - pos[None, :]) * lg)
    chunk_decay = jnp.exp(CHUNK * lg)

    def step(s, inp):
        qc, kc, vc = inp
        o = (jnp.einsum('bhld,bhdv->bhlv', qc * inter[None, :, :, None], s)
             + jnp.einsum('bhts,bhsv->bhtv', jnp.einsum('bhtd,bhsd->bhts', qc, kc) * intra[None], vc))
        s_new = chunk_decay[None, :, :, None] * s + jnp.einsum('bhsd,bhsv->bhdv', kc * to_state[None, :, :, None], vc)
        return s_new, o

    xs = tuple(to_chunks(t.astype(F32)) for t in (q, k, v))
    s_fin, o = lax.scan(step, s0.astype(F32), xs)
    return from_chunks(o), s_fin


def na_latent(q, k, v, k_ctx, v_ctx, bias_tab, scale):
    b, h, n, d = q.shape
    rows = n // GRID_W
    kr = min(NA_ROWS, rows)
    band_start = np.clip(np.arange(NA_NCB) * NA_COLS - NA_COLS // 2, 0, GRID_W - NA_BAND)
    key_col = band_start[:, None] + np.arange(NA_BAND)[None, :]
    q_col = np.arange(NA_NCB)[:, None] * NA_COLS + np.arange(NA_COLS)[None, :]
    win_start = np.clip(q_col - NA_COLS // 2, 0, GRID_W - NA_COLS)
    kc3 = key_col[:, None, :]
    col_valid = jnp.asarray((kc3 >= win_start[..., None]) & (kc3 < win_start[..., None] + NA_COLS))
    col_off = jnp.asarray(np.clip(kc3 - q_col[..., None] + NA_COLS - 1, 0, 2 * NA_COLS - 2))
    qg = q.reshape(b, h, rows, GRID_W, d)
    kg = k.reshape(b, h, rows, GRID_W, d)[:, :, :, key_col]
    vg = v.reshape(b, h, rows, GRID_W, d)[:, :, :, key_col]

    def one_row(r):
        rs = jnp.clip(r - kr // 2, 0, rows - kr)
        kb = lax.dynamic_slice_in_dim(kg, rs, kr, axis=2)
        vb = lax.dynamic_slice_in_dim(vg, rs, kr, axis=2)
        qr = lax.dynamic_index_in_dim(qg, r, axis=2, keepdims=False).reshape(b, h, NA_NCB, NA_COLS, d)
        s_loc = jnp.einsum('bhjqd,bhrjkd->bhjqrk', qr, kb).astype(F32) * scale
        row_off = rs + jnp.arange(kr) - r + NA_ROWS - 1
        bias = bias_tab[:, row_off[:, None, None, None], col_off[None]].astype(F32).transpose(0, 2, 3, 1, 4)
        s_loc = jnp.where(col_valid[:, :, None, :], s_loc + bias, -jnp.inf)
        s_ctx = jnp.einsum('bhjqd,bhmd->bhjqm', qr, k_ctx).astype(F32) * scale
        s_all = jnp.concatenate([s_loc.reshape(b, h, NA_NCB, NA_COLS, kr * NA_BAND), s_ctx], axis=-1)
        p = jax.nn.softmax(s_all, axis=-1).astype(v.dtype)
        p_loc = p[..., :kr * NA_BAND].reshape(b, h, NA_NCB, NA_COLS, kr, NA_BAND)
        p_ctx = p[..., kr * NA_BAND:]
        o = jnp.einsum('bhjqrk,bhrjkd->bhjqd', p_loc, vb) + jnp.einsum('bhjqm,bhmd->bhjqd', p_ctx, v_ctx)
        return o.reshape(b, h, GRID_W, d)

    o = lax.map(one_row, jnp.arange(rows))
    return o.transpose(1, 2, 0, 3, 4).reshape(b, h, n, d)


def moe_ffn(h, router_w, router_bias, w1, w3, w2):
    b, n, d = h.shape
    t = b * n
    h2d = h.reshape(t, d)
    affinity = jax.nn.sigmoid(h2d.astype(F32) @ router_w.astype(F32))
    sel = (affinity + router_bias.astype(F32)).reshape(t, N_GROUPS, EXPERTS_PER_GROUP)
    group_score = lax.top_k(sel, TOP_K)[0].sum(axis=-1)
    g = jnp.argmax(group_score, axis=-1)
    in_group = jnp.take_along_axis(sel, g[:, None, None], axis=1)[:, 0]
    _, local = lax.top_k(in_group, TOP_K)
    expert = g[:, None] * EXPERTS_PER_GROUP + local
    gate = jnp.take_along_axis(affinity, expert, axis=-1)
    gate = gate / jnp.sum(gate, axis=-1, keepdims=True)
    a = t * TOP_K
    flat_e = expert.reshape(-1)
    flat_tok = jnp.broadcast_to(jnp.arange(t)[:, None], (t, TOP_K)).reshape(-1)
    order = jnp.argsort(flat_e)
    se, st, sg = flat_e[order], flat_tok[order], gate.reshape(-1)[order]
    counts = jnp.bincount(flat_e, length=N_EXPERTS)
    starts = jnp.cumsum(counts) - counts
    padded = (counts + MOE_BLOCK - 1) // MOE_BLOCK * MOE_BLOCK
    padded_end = jnp.cumsum(padded)
    padded_start = padded_end - padded
    dest = padded_start[se] + jnp.arange(a) - starts[se]
    n_blocks = -(-a // MOE_BLOCK) + N_EXPERTS
    slots = n_blocks * MOE_BLOCK
    slot_tok = jnp.full((slots,), t, dtype=jnp.int32).at[dest].set(st)
    slot_gate = jnp.zeros((slots,), F32).at[dest].set(sg)
    block_expert = jnp.minimum(jnp.searchsorted(padded_end, jnp.arange(n_blocks) * MOE_BLOCK, side='right'), N_EXPERTS - 1)
    x_pad = jnp.concatenate([h2d, jnp.zeros((1, d), h2d.dtype)], axis=0)
    xb = x_pad[slot_tok].reshape(n_blocks, MOE_BLOCK, d)

    def expert_block(args):
        xe, e = args
        return (jax.nn.silu(xe @ w1[e]) * (xe @ w3[e])) @ w2[e]

    yb = lax.map(expert_block, (xb, block_expert)).reshape(slots, d)
    y = jax.ops.segment_sum(yb.astype(F32) * slot_gate[:, None], slot_tok, num_segments=t + 1)[:t]
    return y.reshape(b, n, d).astype(h.dtype)


def mixer_ab(h, w_in, gate_w, gate_b, gla_gain, q_gain, w_uq, kv_gain, w_ukv, w_out, ctx=None):
    b, n, _ = h.shape
    latent = ctx is not None
    gq, gk, gv, gr, glr, cq, ckv, k_pe = split_cols(h @ w_in, AB_SPLITS)
    q = to_heads(gq, GLA_HEADS) * GLA_DK ** -0.5
    k = to_heads(gk, GLA_HEADS)
    v = to_heads(gv, GLA_HEADS)
    gate_pre = jnp.einsum('bnzr,zre->zbne', glr.reshape(b, n, 2, GATE_RANK), gate_w) + gate_b[:, None, None, :]
    log_a = (jax.nn.log_sigmoid(gate_pre.astype(F32)) / GATE_TAU).reshape(2, b, n, GLA_HEADS, GLA_DK).transpose(0, 1, 3, 2, 4)
    s0 = ctx[0] if latent else jnp.zeros((b, 2, GLA_HEADS, GLA_DK, GLA_DV), F32)
    o_f, s_f = gla_scan(q, k, v, log_a[0], s0[:, 0])
    o_b, s_b = gla_scan(flip(q), flip(k), flip(v), flip(log_a[1]), s0[:, 1])
    o_gla = from_heads(rmsnorm(o_f + flip(o_b))) * gla_gain.astype(F32) * jax.nn.silu(gr.astype(F32))
    ckv_n = rmsnorm(ckv, kv_gain)
    qf = to_heads(rmsnorm(cq, q_gain) @ w_uq, MLA_HEADS)
    q_nope, q_pe = qf[..., :MLA_NOPE], qf[..., MLA_NOPE:]
    kvf = to_heads(ckv_n @ w_ukv, MLA_HEADS)
    k_nope, v_m = kvf[..., :MLA_NOPE], kvf[..., MLA_NOPE:]
    k_pe_h = k_pe[:, None]
    if latent:
        q_pe = apply_axial_rope(q_pe)
        k_pe_h = apply_axial_rope(k_pe_h)
    q_m = jnp.concatenate([q_nope, q_pe], axis=-1)
    k_m = jnp.concatenate([k_nope, jnp.broadcast_to(k_pe_h, (b, MLA_HEADS, n, MLA_ROPE))], axis=-1)
    if latent:
        ckv_c, kpe_c = ctx[1], ctx[2]
        m = ckv_c.shape[1]
        kvc = to_heads(ckv_c @ w_ukv, MLA_HEADS)
        k_c = jnp.concatenate([kvc[..., :MLA_NOPE], jnp.broadcast_to(kpe_c[:, None], (b, MLA_HEADS, m, MLA_ROPE))], axis=-1)
        k_m = jnp.concatenate([k_c, k_m], axis=2)
        v_m = jnp.concatenate([kvc[..., MLA_NOPE:], v_m], axis=2)
    o_mla = from_heads(attention(q_m, k_m, v_m, (MLA_NOPE + MLA_ROPE) ** -0.5))
    out = jnp.concatenate([o_gla.astype(h.dtype), o_mla.astype(h.dtype)], axis=-1) @ w_out
    if latent:
        return out
    return out, (jnp.stack([s_f, s_b], axis=1), ckv_n, k_pe)


def mixer_cd(h, w_in, log_decay, ret_gain, na_bias, w_out, ctx=None):
    b, n, _ = h.shape
    latent = ctx is not None
    rq, rk, rv, rg, nq, nk, nv = split_cols(h @ w_in, CD_SPLITS)
    q = to_heads(rq, RET_HEADS) * RET_DK ** -0.5
    k = to_heads(rk, RET_HEADS)
    v = to_heads(rv, RET_HEADS)
    if latent:
        q = apply_axial_rope(q)
        k = apply_axial_rope(k)
    s0 = ctx[0] if latent else jnp.zeros((b, 2, RET_HEADS, RET_DK, RET_DV), F32)
    o_f, s_f = retention_scan(q, k, v, log_decay[0], s0[:, 0])
    o_b, s_b = retention_scan(flip(q), flip(k), flip(v), log_decay[1], s0[:, 1])
    o_ret = from_heads(rmsnorm(o_f + flip(o_b))) * ret_gain.astype(F32) * jax.nn.silu(rg.astype(F32))
    qn = to_heads(nq, NA_HEADS)
    kn = to_heads(nk, NA_HEADS)
    vn = to_heads(nv, NA_HEADS)
    if latent:
        o_na = na_latent(qn, kn, vn, ctx[1], ctx[2], na_bias, NA_HD ** -0.5)
    else:
        o_na = attention(qn, kn, vn, NA_HD ** -0.5)
    out = jnp.concatenate([o_ret.astype(h.dtype), from_heads(o_na).astype(h.dtype)], axis=-1) @ w_out
    if latent:
        return out
    return out, (jnp.stack([s_f, s_b], axis=1), kn, vn)


def adaln(cond, w, b):
    m = (jax.nn.silu(cond) @ w + b)[:, None, :]
    return jnp.split(m, 6, axis=-1)


def modulate(x, g, shift, scale):
    return rmsnorm(x, g) * (1.0 + scale) + shift


def setup_inputs(seed: int = 0) -> dict:
    key = jax.random.key(seed)
    keys = jax.random.split(key, 48)
    counter = [0]

    def nrm(shape, scale=1.0):
        kk = keys[counter[0]]
        counter[0] += 1
        return jax.random.normal(kk, shape, F32) * scale

    def gain(shape):
        return 1.0 + nrm(shape, 0.02)

    D = D_MODEL
    base_decay = jnp.log(1.0 - 2.0 ** (-5.0 - jnp.arange(RET_HEADS, dtype=F32)))
    return {
        'x_prompt': nrm((BATCH, SEQ, D)),
        'x_sample': nrm((DEC_BATCH, DEC_SEQ, D)),
        'state_gla': nrm((DEC_BATCH, N_AB, 2, GLA_HEADS, GLA_DK, GLA_DV)),
        'cache_mla_ckv': nrm((DEC_BATCH, N_AB, PAST_LEN, MLA_KV_LORA)),
        'cache_mla_kpe': nrm((DEC_BATCH, N_AB, PAST_LEN, MLA_ROPE)),
        'state_ret': nrm((DEC_BATCH, N_CD, 2, RET_HEADS, RET_DK, RET_DV)),
        'cache_na_k': nrm((DEC_BATCH, N_CD, NA_HEADS, PAST_LEN, NA_HD)),
        'cache_na_v': nrm((DEC_BATCH, N_CD, NA_HEADS, PAST_LEN, NA_HD)),
        'c': nrm((DEC_BATCH, D)),
        'c_ctx': nrm((D,)),
        'ada_w': nrm((DEPTH, D, 6 * D), 0.5 * D ** -0.5),
        'ada_b': nrm((DEPTH, 6 * D), 0.02),
        'norm1': gain((DEPTH, D)),
        'norm2': gain((DEPTH, D)),
        'w_in_ab': nrm((N_AB, D, AB_WIDTH), D ** -0.5),
        'gla_gate_w': nrm((N_AB, 2, GATE_RANK, GLA_HEADS * GLA_DK), GATE_RANK ** -0.5),
        'gla_gate_b': nrm((N_AB, 2, GLA_HEADS * GLA_DK), 0.1),
        'gla_gain': gain((N_AB, GLA_HEADS * GLA_DV)),
        'mla_q_gain': gain((N_AB, MLA_Q_LORA)),
        'mla_w_uq': nrm((N_AB, MLA_Q_LORA, MLA_HEADS * (MLA_NOPE + MLA_ROPE)), MLA_Q_LORA ** -0.5),
        'mla_kv_gain': gain((N_AB, MLA_KV_LORA)),
        'mla_w_ukv': nrm((N_AB, MLA_KV_LORA, MLA_HEADS * (MLA_NOPE + MLA_DV)), MLA_KV_LORA ** -0.5),
        'w_in_cd': nrm((N_CD, D, CD_WIDTH), D ** -0.5),
        'ret_log_decay': base_decay * jnp.exp(nrm((N_CD, 2, RET_HEADS), 0.1)),
        'ret_gain': gain((N_CD, RET_HEADS * RET_DV)),
        'na_bias': nrm((N_CD, NA_HEADS, 2 * NA_ROWS - 1, 2 * NA_COLS - 1), 0.1),
        'w_out': nrm((DEPTH, D_MIX, D), D_MIX ** -0.5),
        'router_w': nrm((D, N_EXPERTS), D ** -0.5),
        'router_bias': nrm((N_EXPERTS,), 0.01),
        'exp_w1': nrm((DEPTH, N_EXPERTS, D, D_EXPERT), D ** -0.5),
        'exp_w3': nrm((DEPTH, N_EXPERTS, D, D_EXPERT), D ** -0.5),
        'exp_w2': nrm((DEPTH, N_EXPERTS, D_EXPERT, D), D_EXPERT ** -0.5),
        'final_norm': gain((D,)),
    }


def reference(x_prompt, x_sample, state_gla, cache_mla_ckv, cache_mla_kpe, state_ret, cache_na_k, cache_na_v,
              c, c_ctx, ada_w, ada_b, norm1, norm2, w_in_ab, gla_gate_w, gla_gate_b, gla_gain, mla_q_gain,
              mla_w_uq, mla_kv_gain, mla_w_ukv, w_in_cd, ret_log_decay, ret_gain, na_bias, w_out,
              router_w, router_bias, exp_w1, exp_w3, exp_w2, final_norm):
    yp, ys = x_prompt, x_sample
    new_gla, new_ckv, new_kpe, new_ret, new_k, new_v = [], [], [], [], [], []
    for l in range(DEPTH):
        mp = adaln(c_ctx[None, :], ada_w[l], ada_b[l])
        ms = adaln(c, ada_w[l], ada_b[l])
        hp = modulate(yp, norm1[l], mp[0], mp[1])
        hs = modulate(ys, norm1[l], ms[0], ms[1])
        i = l // 2
        if l % 2 == 0:
            wts = (w_in_ab[i], gla_gate_w[i], gla_gate_b[i], gla_gain[i], mla_q_gain[i], mla_w_uq[i],
                   mla_kv_gain[i], mla_w_ukv[i], w_out[l])
            out_p, (st, ckv, kpe) = mixer_ab(hp, *wts)
            out_s = mixer_ab(hs, *wts, ctx=(state_gla[:, i], cache_mla_ckv[:, i], cache_mla_kpe[:, i]))
            new_gla.append(st)
            new_ckv.append(ckv)
            new_kpe.append(kpe)
        else:
            wts = (w_in_cd[i], ret_log_decay[i], ret_gain[i], na_bias[i], w_out[l])
            out_p, (st, kn, vn) = mixer_cd(hp, *wts)
            out_s = mixer_cd(hs, *wts, ctx=(state_ret[:, i], cache_na_k[:, i], cache_na_v[:, i]))
            new_ret.append(st)
            new_k.append(kn)
            new_v.append(vn)
        yp = yp + mp[2] * out_p
        ys = ys + ms[2] * out_s
        yp = yp + mp[5] * moe_ffn(modulate(yp, norm2[l], mp[3], mp[4]), router_w, router_bias,
                                  exp_w1[l], exp_w3[l], exp_w2[l])
        ys = ys + ms[5] * moe_ffn(modulate(ys, norm2[l], ms[3], ms[4]), router_w, router_bias,
                                  exp_w1[l], exp_w3[l], exp_w2[l])
    y_prompt = rmsnorm(yp, final_norm)
    y_sample = rmsnorm(ys, final_norm)
    return (y_prompt, y_sample, jnp.stack(new_gla, axis=1), jnp.stack(new_ckv, axis=1), jnp.stack(new_kpe, axis=1),
            jnp.stack(new_ret, axis=1), jnp.stack(new_k, axis=1), jnp.stack(new_v, axis=1))
```

```python
import functools

import numpy as np
import jax
import jax.numpy as jnp
from jax import lax
from jax.experimental import pallas as pl
from jax.experimental.pallas import tpu as pltpu

F32 = jnp.float32
BF16 = jnp.bfloat16

EPS = 1e-6
GRID_W = 64
ROPE_BASE = 10000.0
GATE_RANK = 16
GATE_TAU = 16.0
GLA_HEADS, GLA_DK, GLA_DV = 8, 64, 128
MLA_HEADS, MLA_Q_LORA, MLA_KV_LORA, MLA_NOPE, MLA_ROPE, MLA_DV = 8, 512, 256, 128, 64, 128
RET_HEADS, RET_DK, RET_DV = 8, 64, 128
NA_HEADS, NA_HD, NA_ROWS, NA_COLS = 8, 128, 8, 16
N_EXPERTS, N_GROUPS, TOP_K, D_EXPERT = 32, 8, 2, 512
EXPERTS_PER_GROUP = N_EXPERTS // N_GROUPS
MOE_BLOCK = 128
LANES = 128

AB_GQ, AB_GK, AB_GV, AB_GR, AB_CQ, AB_CKV, AB_MISC, AB_WIDTH = 0, 512, 1024, 2048, 3072, 3584, 3840, 4096
CD_RQ, CD_RK, CD_RV, CD_RG, CD_NQ, CD_NK, CD_NV, CD_WIDTH = 0, 512, 1024, 2048, 3072, 4096, 5120, 6144

VMEM_LIMIT = 56 * 1024 * 1024


def _cparams(sem):
    return pltpu.CompilerParams(dimension_semantics=sem, vmem_limit_bytes=VMEM_LIMIT)


def _dot(a, b):
    return jnp.dot(a, b, preferred_element_type=F32)


def _dot_nt(a, b):
    return lax.dot_general(a, b, (((1,), (1,)), ((), ())), preferred_element_type=F32)


def _adaln_kernel(c_ref, w_ref, b_ref, o_ref):
    c = c_ref[...]
    s = (c * jax.nn.sigmoid(c)).astype(BF16)
    o_ref[...] = _dot(s, w_ref[...].astype(BF16)) + b_ref[...]


def adaln_all(cond8, ada_w, ada_b, tn=512):
    depth, d, n6 = ada_w.shape
    return pl.pallas_call(
        _adaln_kernel,
        grid=(depth, n6 // tn),
        in_specs=[pl.BlockSpec((8, d), lambda l, j: (0, 0)),
                  pl.BlockSpec((None, d, tn), lambda l, j: (l, 0, j)),
                  pl.BlockSpec((None, 1, tn), lambda l, j: (l, 0, j))],
        out_specs=pl.BlockSpec((None, 8, tn), lambda l, j: (l, 0, j)),
        out_shape=jax.ShapeDtypeStruct((depth, 8, n6), F32),
        compiler_params=_cparams(("parallel", "parallel")),
        name="adaln",
    )(cond8, ada_w, ada_b.reshape(depth, 1, n6))


def _mod_index(i, tm, n_prompt_rows, sample_rows):
    tp = n_prompt_rows // tm
    return jnp.where(i < tp, 0, 1 + (i - tp) // (sample_rows // tm))


def _win_kernel(x_ref, mod_ref, g_ref, w_ref, o_ref, h_ref, *, d):
    @pl.when(pl.program_id(1) == 0)
    def _():
        x = x_ref[...]
        xn = x * lax.rsqrt(jnp.mean(x * x, axis=-1, keepdims=True) + EPS) * g_ref[...]
        h = xn * (1.0 + mod_ref[:, d:2 * d]) + mod_ref[:, 0:d]
        h_ref[...] = h.astype(BF16)

    o_ref[...] = _dot(h_ref[...], w_ref[...])


def win_project(y, mod_l, gain, w_bf16, n_prompt_rows, sample_rows, tm=512, tn=512):
    t, d = y.shape
    width = w_bf16.shape[1]
    midx = functools.partial(_mod_index, tm=tm, n_prompt_rows=n_prompt_rows, sample_rows=sample_rows)
    return pl.pallas_call(
        functools.partial(_win_kernel, d=d),
        grid=(t // tm, width // tn),
        in_specs=[pl.BlockSpec((tm, d), lambda i, j: (i, 0)),
                  pl.BlockSpec((None, 1, 6 * d), lambda i, j: (midx(i), 0, 0)),
                  pl.BlockSpec((1, d), lambda i, j: (0, 0)),
                  pl.BlockSpec((d, tn), lambda i, j: (0, j))],
        out_specs=pl.BlockSpec((tm, tn), lambda i, j: (i, j)),
        out_shape=jax.ShapeDtypeStruct((t, width), F32),
        scratch_shapes=[pltpu.VMEM((tm, d), BF16)],
        compiler_params=_cparams(("parallel", "arbitrary")),
        name="win_project",
    )(y, mod_l, gain.reshape(1, d), w_bf16)


def _wout_kernel(oa_ref, ob_ref, y_ref, mod_ref, wa_ref, wb_ref, g_ref, rw_ref, ynew_ref, h2_ref, lg_ref, *, d):
    out = _dot(oa_ref[...], wa_ref[...]) + _dot(ob_ref[...], wb_ref[...])
    y1 = y_ref[...] + mod_ref[:, 2 * d:3 * d] * out
    ynew_ref[...] = y1
    xn = y1 * lax.rsqrt(jnp.mean(y1 * y1, axis=-1, keepdims=True) + EPS) * g_ref[...]
    h = xn * (1.0 + mod_ref[:, 4 * d:5 * d]) + mod_ref[:, 3 * d:4 * d]
    hh = h.astype(BF16)
    h2_ref[...] = hh
    hl = (h - hh.astype(F32)).astype(BF16)
    lg_ref[...] = _dot(hh, rw_ref[0]) + _dot(hl, rw_ref[0]) + _dot(hh, rw_ref[1])


def wout_project(o_a, o_b, y, mod_l, w_out_bf16, gain2, rw_split, n_prompt_rows, sample_rows, tm=256):
    t, d = y.shape
    ka, kb = o_a.shape[1], o_b.shape[1]
    midx = functools.partial(_mod_index, tm=tm, n_prompt_rows=n_prompt_rows, sample_rows=sample_rows)
    return pl.pallas_call(
        functools.partial(_wout_kernel, d=d),
        grid=(t // tm,),
        in_specs=[pl.BlockSpec((tm, ka), lambda i: (i, 0)),
                  pl.BlockSpec((tm, kb), lambda i: (i, 0)),
                  pl.BlockSpec((tm, d), lambda i: (i, 0)),
                  pl.BlockSpec((None, 1, 6 * d), lambda i: (midx(i), 0, 0)),
                  pl.BlockSpec((ka, d), lambda i: (0, 0)),
                  pl.BlockSpec((kb, d), lambda i: (1, 0)),
                  pl.BlockSpec((1, d), lambda i: (0, 0)),
                  pl.BlockSpec((2, d, LANES), lambda i: (0, 0, 0))],
        out_specs=[pl.BlockSpec((tm, d), lambda i: (i, 0)),
                   pl.BlockSpec((tm, d), lambda i: (i, 0)),
                   pl.BlockSpec((tm, LANES), lambda i: (i, 0))],
        out_shape=[jax.ShapeDtypeStruct((t, d), F32),
                   jax.ShapeDtypeStruct((t, d), BF16),
                   jax.ShapeDtypeStruct((t, LANES), F32)],
        compiler_params=_cparams(("parallel",)),
        name="wout_project",
    )(o_a, o_b, y, mod_l, w_out_bf16, w_out_bf16, gain2.reshape(1, d), rw_split)


def _group_shift(x, k, sub):
    fwd = pltpu.roll(x, LANES - k, axis=1)
    back = pltpu.roll(x, EXPERTS_PER_GROUP - k, axis=1)
    return jnp.where(sub + k < EXPERTS_PER_GROUP, fwd, back)


def _group_reduce(x, sub, op):
    r = x
    for k in range(1, EXPERTS_PER_GROUP):
        r = op(r, _group_shift(x, k, sub))
    return r


def _route_kernel(lg_ref, bias_ref, info_ref, cnt_ref, run_ref, *, tm):
    @pl.when(pl.program_id(0) == 0)
    def _():
        run_ref[...] = jnp.zeros_like(run_ref)

    lg = lg_ref[...]
    lane = lax.broadcasted_iota(jnp.int32, lg.shape, 1)
    lane_f = lane.astype(F32)
    sub = (lane & (EXPERTS_PER_GROUP - 1)).astype(F32)
    grp = (lane >> 2).astype(F32)
    aff = jax.nn.sigmoid(lg)
    neg = jnp.float32(-jnp.inf)
    sel = jnp.where(lane < N_EXPERTS, aff + bias_ref[...], neg)
    m1 = _group_reduce(sel, sub, jnp.maximum)
    i1 = _group_reduce(jnp.where(sel == m1, sub, 4.0), sub, jnp.minimum)
    sel2 = jnp.where(sub == i1, neg, sel)
    m2 = _group_reduce(sel2, sub, jnp.maximum)
    i2 = _group_reduce(jnp.where(sel2 == m2, sub, 4.0), sub, jnp.minimum)
    gscore = m1 + m2
    gmax = jnp.max(gscore, axis=1, keepdims=True)
    g = jnp.min(jnp.where(gscore == gmax, grp, 99.0), axis=1, keepdims=True)
    oh1 = (grp == g) & (sub == i1)
    oh2 = (grp == g) & (sub == i2)

    def pick(mask, v):
        return jnp.sum(jnp.where(mask, v, 0.0), axis=1, keepdims=True)

    a1, a2 = pick(oh1, aff), pick(oh2, aff)
    den = a1 + a2
    e1, e2 = pick(oh1, lane_f), pick(oh2, lane_f)
    oh = jnp.where(oh1 | oh2, 1.0, 0.0)
    r_i = lax.broadcasted_iota(jnp.int32, (tm, tm), 0)
    c_i = lax.broadcasted_iota(jnp.int32, (tm, tm), 1)
    tri = jnp.where(c_i < r_i, 1.0, 0.0).astype(BF16)
    cum = _dot(tri, oh.astype(BF16)) + run_ref[0:1, :]
    r1, r2 = pick(oh1, cum), pick(oh2, cum)
    run_new = run_ref[0:1, :] + jnp.sum(oh, axis=0, keepdims=True)
    run_ref[...] = jnp.broadcast_to(run_new, run_ref.shape)
    cnt_ref[...] = jnp.broadcast_to(run_new, cnt_ref.shape)
    info = jnp.zeros_like(lg)
    for k, v in enumerate((e1, e2, a1 / den, a2 / den, r1, r2)):
        info = jnp.where(lane == k, v, info)
    info_ref[...] = info


def route(logits, router_bias, tm=256):
    t = logits.shape[0]
    bias = jnp.zeros((1, LANES), F32).at[0, :N_EXPERTS].set(router_bias.astype(F32))
    return pl.pallas_call(
        functools.partial(_route_kernel, tm=tm),
        grid=(t // tm,),
        in_specs=[pl.BlockSpec((tm, LANES), lambda i: (i, 0)),
                  pl.BlockSpec((1, LANES), lambda i: (0, 0))],
        out_specs=[pl.BlockSpec((tm, LANES), lambda i: (i, 0)),
                   pl.BlockSpec((8, LANES), lambda i: (0, 0))],
        out_shape=[jax.ShapeDtypeStruct((t, LANES), F32),
                   jax.ShapeDtypeStruct((8, LANES), F32)],
        scratch_shapes=[pltpu.VMEM((8, LANES), F32)],
        compiler_params=_cparams(("arbitrary",)),
        name="route",
    )(logits, bias)


def _expert_kernel(be_ref, nu_ref, x_ref, w1_ref, w3_ref, w2_ref, o_ref):
    i = pl.program_id(0)

    @pl.when(i < nu_ref[0])
    def _():
        x = x_ref[...]
        a = _dot(x, w1_ref[...])
        b = _dot(x, w3_ref[...])
        mid = (a * jax.nn.sigmoid(a)) * b
        o_ref[...] = _dot(mid.astype(BF16), w2_ref[...])

    @pl.when(i >= nu_ref[0])
    def _():
        o_ref[...] = jnp.zeros_like(o_ref)


def expert_ffn(xs, block_expert, n_used, w1, w3, w2):
    slots, d = xs.shape
    n_blocks = slots // MOE_BLOCK
    de = w1.shape[-1]
    grid_spec = pltpu.PrefetchScalarGridSpec(
        num_scalar_prefetch=2,
        grid=(n_blocks,),
        in_specs=[pl.BlockSpec((MOE_BLOCK, d), lambda i, be, nu: (i, 0)),
                  pl.BlockSpec((None, d, de), lambda i, be, nu: (be[i], 0, 0)),
                  pl.BlockSpec((None, d, de), lambda i, be, nu: (be[i], 0, 0)),
                  pl.BlockSpec((None, de, d), lambda i, be, nu: (be[i], 0, 0))],
        out_specs=pl.BlockSpec((MOE_BLOCK, d), lambda i, be, nu: (i, 0)),
    )
    return pl.pallas_call(
        _expert_kernel,
        grid_spec=grid_spec,
        out_shape=jax.ShapeDtypeStruct((slots, d), F32),
        compiler_params=_cparams(("arbitrary",)),
        name="expert_ffn",
    )(block_expert, n_used, xs, w1, w3, w2)


def _combine_kernel(y_ref, mod_ref, m_ref, g_ref, o_ref, *, d, final):
    y2 = y_ref[...] + mod_ref[:, 5 * d:6 * d] * m_ref[...]
    if final:
        y2 = y2 * lax.rsqrt(jnp.mean(y2 * y2, axis=-1, keepdims=True) + EPS) * g_ref[...]
    o_ref[...] = y2


def combine(y1, mod_l, y_moe, final_gain, n_prompt_rows, sample_rows, final, tm=512):
    t, d = y1.shape
    midx = functools.partial(_mod_index, tm=tm, n_prompt_rows=n_prompt_rows, sample_rows=sample_rows)
    return pl.pallas_call(
        functools.partial(_combine_kernel, d=d, final=final),
        grid=(t // tm,),
        in_specs=[pl.BlockSpec((tm, d), lambda i: (i, 0)),
                  pl.BlockSpec((None, 1, 6 * d), lambda i: (midx(i), 0, 0)),
                  pl.BlockSpec((tm, d), lambda i: (i, 0)),
                  pl.BlockSpec((1, d), lambda i: (0, 0))],
        out_specs=pl.BlockSpec((tm, d), lambda i: (i, 0)),
        out_shape=jax.ShapeDtypeStruct((t, d), F32),
        compiler_params=_cparams(("parallel",)),
        name="combine",
    )(y1, mod_l, y_moe, final_gain.reshape(1, d))


def moe_layer(y1, h2, logits, mod_l, router_bias, w1, w3, w2, final_gain, n_prompt_rows, sample_rows, final):
    t, d = y1.shape
    info, cnt = route(logits, router_bias)
    expert = info[:, 0:2].astype(jnp.int32)
    gate = info[:, 2:4]
    rank = info[:, 4:6].astype(jnp.int32)
    counts = cnt[0, :N_EXPERTS].astype(jnp.int32)
    padded = (counts + MOE_BLOCK - 1) // MOE_BLOCK * MOE_BLOCK
    padded_end = jnp.cumsum(padded)
    padded_start = padded_end - padded
    dest = padded_start[expert] + rank
    n_blocks = (t * TOP_K) // MOE_BLOCK + N_EXPERTS
    slots = n_blocks * MOE_BLOCK
    block_expert = jnp.minimum(
        jnp.searchsorted(padded_end, jnp.arange(n_blocks, dtype=jnp.int32) * MOE_BLOCK, side='right'),
        N_EXPERTS - 1).astype(jnp.int32)
    n_used = (padded_end[-1:] // MOE_BLOCK).astype(jnp.int32)
    tok = jnp.broadcast_to(jnp.arange(t, dtype=jnp.int32)[:, None], (t, TOP_K))
    slot_tok = jnp.full((slots,), t, jnp.int32).at[dest.reshape(-1)].set(tok.reshape(-1))
    h2_pad = jnp.concatenate([h2, jnp.zeros((1, d), h2.dtype)], axis=0)
    xs = h2_pad[slot_tok]
    yb = expert_ffn(xs, block_expert, n_used, w1, w3, w2)
    y_moe = gate[:, 0:1] * yb[dest[:, 0]] + gate[:, 1:2] * yb[dest[:, 1]]
    return combine(y1, mod_l, y_moe, final_gain, n_prompt_rows, sample_rows, final)


def rope_tables(n, lead):
    quarter = MLA_ROPE // 4
    tpos = np.arange(n)
    row = (tpos // GRID_W).astype(np.float32)
    col = (tpos % GRID_W).astype(np.float32)
    inv_freq = (np.float32(ROPE_BASE) ** (-np.arange(quarter, dtype=np.float32) / np.float32(quarter))).astype(np.float32)
    lane = np.arange(64)
    pos = np.where(lane[None, :] < 32, row[:, None], col[:, None]).astype(np.float32)
    ang = pos * inv_freq[lane % quarter][None, :]
    cos = np.cos(ang).astype(np.float32)
    sin = np.sin(ang).astype(np.float32)
    sgn = np.where((lane % 32) < quarter, -1.0, 1.0).astype(np.float32)
    sin = sin * sgn[None, :]
    cos = np.concatenate([cos, cos], axis=1)
    sin = np.concatenate([sin, sin], axis=1)
    cos = np.concatenate([np.ones((lead, LANES), np.float32), cos], axis=0)
    sin = np.concatenate([np.zeros((lead, LANES), np.float32), sin], axis=0)
    return jnp.asarray(cos), jnp.asarray(sin)


def _rope(x, cos, sin_signed):
    lane = lax.broadcasted_iota(jnp.int32, x.shape, 1)
    partner = jnp.where((lane & 31) < 16, pltpu.roll(x, LANES - 16, axis=1), pltpu.roll(x, 16, axis=1))
    return x * cos + partner * sin_signed


def _mla_proj_kernel(cq_ref, ckv_ref, misc_ref, qg_ref, kvg_ref, wq_ref, wkv_ref, cos_ref, sin_ref,
                     q_ref, kv_ref, ckvn_ref, kpe_ref, *, scale):
    cq = cq_ref[...]
    qn = cq * lax.rsqrt(jnp.mean(cq * cq, axis=-1, keepdims=True) + EPS) * qg_ref[...]
    qf = _dot(qn.astype(BF16), wq_ref[...])
    half = qf.shape[1] // 2
    cos, sin = cos_ref[...], sin_ref[...]
    q_ref[:, 0:half] = (qf[:, 0:half] * scale).astype(BF16)
    for h in range(MLA_HEADS):
        pe = qf[:, half + h * LANES: half + (h + 1) * LANES]
        q_ref[:, half + h * LANES: half + (h + 1) * LANES] = (_rope(pe, cos, sin) * scale).astype(BF16)
    ckv = ckv_ref[...]
    ckvn = ckv * lax.rsqrt(jnp.mean(ckv * ckv, axis=-1, keepdims=True) + EPS) * kvg_ref[...]
    ckvn_ref[...] = ckvn
    kv_ref[...] = _dot(ckvn.astype(BF16), wkv_ref[...]).astype(BF16)
    misc = misc_ref[...]
    lane = lax.broadcasted_iota(jnp.int32, misc.shape, 1)
    kpe = jnp.where(lane < MLA_ROPE, misc, 0.0)
    kpe_ref[...] = _rope(kpe, cos, sin).astype(BF16)


def mla_project(z, q_gain, kv_gain, wq, wkv, cos_t, sin_t, n_prompt_rows, sample_rows, tm=512):
    t = z.shape[0]
    tp = n_prompt_rows // tm
    per = sample_rows // tm

    def tab(i):
        return (jnp.where(i < tp, 0, 1 + (i - tp) % per), 0)

    scale = float((MLA_NOPE + MLA_ROPE) ** -0.5)
    nq = wq.shape[1]
    return pl.pallas_call(
        functools.partial(_mla_proj_kernel, scale=scale),
        grid=(t // tm,),
        in_specs=[pl.BlockSpec((tm, MLA_Q_LORA), lambda i: (i, AB_CQ // MLA_Q_LORA)),
                  pl.BlockSpec((tm, MLA_KV_LORA), lambda i: (i, AB_CKV // MLA_KV_LORA)),
                  pl.BlockSpec((tm, LANES), lambda i: (i, AB_MISC // LANES)),
                  pl.BlockSpec((1, MLA_Q_LORA), lambda i: (0, 0)),
                  pl.BlockSpec((1, MLA_KV_LORA), lambda i: (0, 0)),
                  pl.BlockSpec(wq.shape, lambda i: (0, 0)),
                  pl.BlockSpec(wkv.shape, lambda i: (0, 0)),
                  pl.BlockSpec((tm, LANES), tab),
                  pl.BlockSpec((tm, LANES), tab)],
        out_specs=[pl.BlockSpec((tm, nq), lambda i: (i, 0)),
                   pl.BlockSpec((tm, wkv.shape[1]), lambda i: (i, 0)),
                   pl.BlockSpec((tm, MLA_KV_LORA), lambda i: (i, 0)),
                   pl.BlockSpec((tm, LANES), lambda i: (i, 0))],
        out_shape=[jax.ShapeDtypeStruct((t, nq), BF16),
                   jax.ShapeDtypeStruct((t, wkv.shape[1]), BF16),
                   jax.ShapeDtypeStruct((t, MLA_KV_LORA), F32),
                   jax.ShapeDtypeStruct((t, LANES), BF16)],
        compiler_params=_cparams(("parallel",)),
        name="mla_project",
    )(z, z, z, q_gain.reshape(1, -1), kv_gain.reshape(1, -1), wq, wkv, cos_t, sin_t)


def _ctx_kv_kernel(ckv_ref, w_ref, o_ref):
    o_ref[...] = _dot(ckv_ref[...].astype(BF16), w_ref[...]).astype(BF16)


def ctx_kv_project(ckv_c, wkv):
    rows = ckv_c.shape[0]
    return pl.pallas_call(
        _ctx_kv_kernel,
        grid=(1,),
        in_specs=[pl.BlockSpec(ckv_c.shape, lambda i: (0, 0)), pl.BlockSpec(wkv.shape, lambda i: (0, 0))],
        out_specs=pl.BlockSpec((rows, wkv.shape[1]), lambda i: (0, 0)),
        out_shape=jax.ShapeDtypeStruct((rows, wkv.shape[1]), BF16),
        compiler_params=_cparams(("arbitrary",)),
        name="ctx_kv_project",
    )(ckv_c, wkv)


def _softmax_pv(s, v):
    m = jnp.max(s, axis=-1, keepdims=True)
    p = jnp.exp(s - m)
    l = jnp.sum(p, axis=-1, keepdims=True)
    return _dot(p.astype(BF16), v) / l


def _mla_prompt_attn_kernel(qn_ref, qp_ref, kn_ref, kp_ref, v_ref, o_ref):
    q = jnp.concatenate([qn_ref[...], qp_ref[...]], axis=1)
    k = jnp.concatenate([kn_ref[...], kp_ref[...]], axis=1)
    o_ref[...] = _softmax_pv(_dot_nt(q, k), v_ref[...]).astype(o_ref.dtype)


def mla_prompt_attention(q_m, kv, kpe, batch, n):
    h = MLA_HEADS
    return pl.pallas_call(
        _mla_prompt_attn_kernel,
        grid=(batch, h),
        in_specs=[pl.BlockSpec((n, LANES), lambda b, hh: (b, hh)),
                  pl.BlockSpec((n, LANES), lambda b, hh: (b, h + hh)),
                  pl.BlockSpec((n, LANES), lambda b, hh: (b, hh)),
                  pl.BlockSpec((n, LANES), lambda b, hh: (b, 0)),
                  pl.BlockSpec((n, LANES), lambda b, hh: (b, h + hh))],
        out_specs=pl.BlockSpec((n, LANES), lambda b, hh: (b, hh)),
        out_shape=jax.ShapeDtypeStruct((batch * n, h * MLA_DV), BF16),
        compiler_params=_cparams(("parallel", "parallel")),
        name="mla_prompt_attention",
    )(q_m, q_m, kv, kpe, kv)


def _mla_sample_attn_kernel(qn_ref, qp_ref, kcn_ref, kcp_ref, vc_ref, kn_ref, kp_ref, v_ref, o_ref,
                            m_ref, l_ref, acc_ref, *, tk):
    q = jnp.concatenate([qn_ref[...], qp_ref[...]], axis=1)
    kc = jnp.concatenate([kcn_ref[...], kcp_ref[...].astype(BF16)], axis=1)
    s = _dot_nt(q, kc)
    m = jnp.max(s, axis=-1, keepdims=True)
    p = jnp.exp(s - m)
    m_ref[...] = m
    l_ref[...] = jnp.sum(p, axis=-1, keepdims=True)
    acc_ref[...] = _dot(p.astype(BF16), vc_ref[...])
    nk = kn_ref.shape[0] // tk

    def body(j, carry):
        off = pl.multiple_of(j * tk, tk)
        k = jnp.concatenate([kn_ref[pl.ds(off, tk), :], kp_ref[pl.ds(off, tk), :]], axis=1)
        s = _dot_nt(q, k)
        m_old = m_ref[...]
        m_new = jnp.maximum(m_old, jnp.max(s, axis=-1, keepdims=True))
        alpha = jnp.exp(m_old - m_new)
        p = jnp.exp(s - m_new)
        l_ref[...] = alpha * l_ref[...] + jnp.sum(p, axis=-1, keepdims=True)
        acc_ref[...] = alpha * acc_ref[...] + _dot(p.astype(BF16), v_ref[pl.ds(off, tk), :])
        m_ref[...] = m_new
        return carry

    lax.fori_loop(0, nk, body, 0)
    o_ref[...] = (acc_ref[...] / l_ref[...]).astype(o_ref.dtype)


def mla_sample_attention(q_m, kv, kpe, kv_ctx, kpe_ctx, row0, batch, n, past, tq=512, tk=512):
    h = MLA_HEADS
    qb = n // tq
    r0 = row0 // n
    q0 = row0 // tq
    return pl.pallas_call(
        functools.partial(_mla_sample_attn_kernel, tk=tk),
        grid=(batch, h, qb),
        in_specs=[pl.BlockSpec((tq, LANES), lambda b, hh, i: (q0 + b * qb + i, hh)),
                  pl.BlockSpec((tq, LANES), lambda b, hh, i: (q0 + b * qb + i, h + hh)),
                  pl.BlockSpec((past, LANES), lambda b, hh, i: (b, hh)),
                  pl.BlockSpec((past, LANES), lambda b, hh, i: (b, 0)),
                  pl.BlockSpec((past, LANES), lambda b, hh, i: (b, h + hh)),
                  pl.BlockSpec((n, LANES), lambda b, hh, i: (r0 + b, hh)),
                  pl.BlockSpec((n, LANES), lambda b, hh, i: (r0 + b, 0)),
                  pl.BlockSpec((n, LANES), lambda b, hh, i: (r0 + b, h + hh))],
        out_specs=pl.BlockSpec((tq, LANES), lambda b, hh, i: (b * qb + i, hh)),
        out_shape=jax.ShapeDtypeStruct((batch * n, h * MLA_DV), BF16),
        scratch_shapes=[pltpu.VMEM((tq, 1), F32), pltpu.VMEM((tq, 1), F32), pltpu.VMEM((tq, MLA_DV), F32)],
        compiler_params=_cparams(("parallel", "parallel", "arbitrary")),
        name="mla_sample_attention",
    )(q_m, q_m, kv_ctx, kpe_ctx, kv_ctx, kv, kpe, kv)


def _na_prompt_attn_kernel(q_ref, k_ref, v_ref, o_ref, ko_ref, vo_ref, *, scale):
    k = k_ref[...]
    v = v_ref[...]
    ko_ref[...] = k
    vo_ref[...] = v
    q = (q_ref[...] * scale).astype(BF16)
    o_ref[...] = _softmax_pv(_dot_nt(q, k.astype(BF16)), v.astype(BF16)).astype(o_ref.dtype)


def na_prompt_attention(z, batch, n):
    h = NA_HEADS
    scale = float(NA_HD ** -0.5)
    cq, ck, cv = CD_NQ // LANES, CD_NK // LANES, CD_NV // LANES
    return pl.pallas_call(
        functools.partial(_na_prompt_attn_kernel, scale=scale),
        grid=(batch, h),
        in_specs=[pl.BlockSpec((n, LANES), lambda b, hh: (b, cq + hh)),
                  pl.BlockSpec((n, LANES), lambda b, hh: (b, ck + hh)),
                  pl.BlockSpec((n, LANES), lambda b, hh: (b, cv + hh))],
        out_specs=[pl.BlockSpec((n, LANES), lambda b, hh: (b, hh)),
                   pl.BlockSpec((None, None, n, NA_HD), lambda b, hh: (b, hh, 0, 0)),
                   pl.BlockSpec((None, None, n, NA_HD), lambda b, hh: (b, hh, 0, 0))],
        out_shape=[jax.ShapeDtypeStruct((batch * n, h * NA_HD), BF16),
                   jax.ShapeDtypeStruct((batch, h, n, NA_HD), F32),
                   jax.ShapeDtypeStruct((batch, h, n, NA_HD), F32)],
        compiler_params=_cparams(("parallel", "parallel")),
        name="na_prompt_attention",
    )(z, z, z)


CHUNK = 64


def _rmsnorm(x, g=None):
    y = x * lax.rsqrt(jnp.mean(x * x, axis=-1, keepdims=True) + EPS)
    return y if g is None else y * g


def _to_heads(x, nh):
    b, n, _ = x.shape
    return x.reshape(b, n, nh, -1).transpose(0, 2, 1, 3)


def _from_heads(x):
    b, h, n, d = x.shape
    return x.transpose(0, 2, 1, 3).reshape(b, n, h * d)


def _to_chunks(x):
    b, h, n, d = x.shape
    return x.reshape(b, h, n // CHUNK, CHUNK, d).transpose(2, 0, 1, 3, 4)


def _from_chunks(x):
    nc, b, h, l, d = x.shape
    return x.transpose(1, 2, 0, 3, 4).reshape(b, h, nc * l, d)


def _flip(x):
    return jnp.flip(x, axis=2)


def _axial_rope(x):
    n, d = x.shape[-2], x.shape[-1]
    quarter = d // 4
    t = jnp.arange(n)
    row = (t // GRID_W).astype(F32)
    col = (t % GRID_W).astype(F32)
    inv_freq = ROPE_BASE ** (-jnp.arange(quarter, dtype=F32) / quarter)

    def rot(xh, pos):
        ang = pos[:, None] * inv_freq[None, :]
        cos, sin = jnp.cos(ang), jnp.sin(ang)
        x1, x2 = xh[..., :quarter], xh[..., quarter:]
        return jnp.concatenate([x1 * cos - x2 * sin, x1 * sin + x2 * cos], axis=-1)

    return jnp.concatenate([rot(x[..., :d // 2], row), rot(x[..., d // 2:], col)], axis=-1)


def _gla_scan(q, k, v, log_a, s0):
    causal = jnp.tril(jnp.ones((CHUNK, CHUNK), dtype=bool))[:, :, None]

    def step(s, inp):
        qc, kc, vc, ac = inp
        bcum = jnp.cumsum(ac, axis=-2)
        o_inter = jnp.einsum('bhld,bhdv->bhlv', qc * jnp.exp(bcum), s)
        diff = bcum[:, :, :, None, :] - bcum[:, :, None, :, :]
        decay = jnp.where(causal, jnp.exp(jnp.where(causal, diff, 0.0)), 0.0)
        scores = jnp.einsum('bhtd,bhsd,bhtsd->bhts', qc, kc, decay)
        o = o_inter + jnp.einsum('bhts,bhsv->bhtv', scores, vc)
        b_last = bcum[:, :, -1:, :]
        s_new = jnp.exp(b_last[:, :, 0, :, None]) * s + jnp.einsum('bhsd,bhsv->bhdv', kc * jnp.exp(b_last - bcum), vc)
        return s_new, o

    xs = tuple(_to_chunks(t) for t in (q, k, v, log_a))
    s_fin, o = lax.scan(step, s0, xs)
    return _from_chunks(o), s_fin


def _retention_scan(q, k, v, log_gamma, s0):
    lg = log_gamma.astype(F32)[:, None]
    pos = jnp.arange(CHUNK, dtype=F32)
    dist = pos[:, None] - pos[None, :]
    causal = (dist >= 0)[None]
    intra = jnp.where(causal, jnp.exp(jnp.where(causal, dist[None], 0.0) * lg[:, :, None]), 0.0)
    inter = jnp.exp((pos[None, :] + 1.0) * lg)
    to_state = jnp.exp((CHUNK - 1.0 - pos[None, :]) * lg)
    chunk_decay = jnp.exp(CHUNK * lg)

    def step(s, inp):
        qc, kc, vc = inp
        o = (jnp.einsum('bhld,bhdv->bhlv', qc * inter[None, :, :, None], s)
             + jnp.einsum('bhts,bhsv->bhtv', jnp.einsum('bhtd,bhsd->bhts', qc, kc) * intra[None], vc))
        s_new = chunk_decay[None, :, :, None] * s + jnp.einsum('bhsd,bhsv->bhdv', kc * to_state[None, :, :, None], vc)
        return s_new, o

    xs = tuple(_to_chunks(t) for t in (q, k, v))
    s_fin, o = lax.scan(step, s0, xs)
    return _from_chunks(o), s_fin


def _na_latent(q, k, v, k_ctx, v_ctx, bias_tab, scale):
    b, h, n, d = q.shape
    rows = n // GRID_W
    kr = min(NA_ROWS, rows)
    ncb = GRID_W // NA_COLS
    band = 2 * NA_COLS
    band_start = np.clip(np.arange(ncb) * NA_COLS - NA_COLS // 2, 0, GRID_W - band)
    key_col = band_start[:, None] + np.arange(band)[None, :]
    q_col = np.arange(ncb)[:, None] * NA_COLS + np.arange(NA_COLS)[None, :]
    win_start = np.clip(q_col - NA_COLS // 2, 0, GRID_W - NA_COLS)
    kc3 = key_col[:, None, :]
    col_valid = jnp.asarray((kc3 >= win_start[..., None]) & (kc3 < win_start[..., None] + NA_COLS))
    col_off = jnp.asarray(np.clip(kc3 - q_col[..., None] + NA_COLS - 1, 0, 2 * NA_COLS - 2))
    qg = q.reshape(b, h, rows, GRID_W, d)
    kg = k.reshape(b, h, rows, GRID_W, d)[:, :, :, key_col]
    vg = v.reshape(b, h, rows, GRID_W, d)[:, :, :, key_col]

    def one_row(r):
        rs = jnp.clip(r - kr // 2, 0, rows - kr)
        kb = lax.dynamic_slice_in_dim(kg, rs, kr, axis=2)
        vb = lax.dynamic_slice_in_dim(vg, rs, kr, axis=2)
        qr = lax.dynamic_index_in_dim(qg, r, axis=2, keepdims=False).reshape(b, h, ncb, NA_COLS, d)
        s_loc = jnp.einsum('bhjqd,bhrjkd->bhjqrk', qr, kb).astype(F32) * scale
        row_off = rs + jnp.arange(kr) - r + NA_ROWS - 1
        bias = bias_tab[:, row_off[:, None, None, None], col_off[None]].astype(F32).transpose(0, 2, 3, 1, 4)
        s_loc = jnp.where(col_valid[:, :, None, :], s_loc + bias, -jnp.inf)
        s_ctx = jnp.einsum('bhjqd,bhmd->bhjqm', qr, k_ctx).astype(F32) * scale
        s_all = jnp.concatenate([s_loc.reshape(b, h, ncb, NA_COLS, kr * band), s_ctx], axis=-1)
        p = jax.nn.softmax(s_all, axis=-1).astype(v.dtype)
        p_loc = p[..., :kr * band].reshape(b, h, ncb, NA_COLS, kr, band)
        p_ctx = p[..., kr * band:]
        o = jnp.einsum('bhjqrk,bhrjkd->bhjqd', p_loc, vb) + jnp.einsum('bhjqm,bhmd->bhjqd', p_ctx, v_ctx)
        return o.reshape(b, h, GRID_W, d)

    o = lax.map(one_row, jnp.arange(rows))
    return o.transpose(1, 2, 0, 3, 4).reshape(b, h, n, d)


def _gla_mixer_jnp(zb, gate_w, gate_b, gla_gain, s0):
    b, n, _ = zb.shape
    gq, gk = zb[..., AB_GQ:AB_GK], zb[..., AB_GK:AB_GV]
    gv, gr = zb[..., AB_GV:AB_GR], zb[..., AB_GR:AB_CQ]
    glr = zb[..., AB_MISC + MLA_ROPE: AB_MISC + MLA_ROPE + 2 * GATE_RANK]
    q = _to_heads(gq, GLA_HEADS) * GLA_DK ** -0.5
    k = _to_heads(gk, GLA_HEADS)
    v = _to_heads(gv, GLA_HEADS)
    gate_pre = jnp.einsum('bnzr,zre->zbne', glr.reshape(b, n, 2, GATE_RANK), gate_w) + gate_b[:, None, None, :]
    log_a = (jax.nn.log_sigmoid(gate_pre) / GATE_TAU).reshape(2, b, n, GLA_HEADS, GLA_DK).transpose(0, 1, 3, 2, 4)
    o_f, s_f = _gla_scan(q, k, v, log_a[0], s0[:, 0])
    o_b, s_b = _gla_scan(_flip(q), _flip(k), _flip(v), _flip(log_a[1]), s0[:, 1])
    o = _from_heads(_rmsnorm(o_f + _flip(o_b))) * gla_gain * jax.nn.silu(gr)
    return o, jnp.stack([s_f, s_b], axis=1)


def _ret_mixer_jnp(zb, log_decay, ret_gain, s0, latent):
    q = _to_heads(zb[..., CD_RQ:CD_RK], RET_HEADS) * RET_DK ** -0.5
    k = _to_heads(zb[..., CD_RK:CD_RV], RET_HEADS)
    v = _to_heads(zb[..., CD_RV:CD_RG], RET_HEADS)
    rg = zb[..., CD_RG:CD_NQ]
    if latent:
        q, k = _axial_rope(q), _axial_rope(k)
    o_f, s_f = _retention_scan(q, k, v, log_decay[0], s0[:, 0])
    o_b, s_b = _retention_scan(_flip(q), _flip(k), _flip(v), log_decay[1], s0[:, 1])
    o = _from_heads(_rmsnorm(o_f + _flip(o_b))) * ret_gain * jax.nn.silu(rg)
    return o, jnp.stack([s_f, s_b], axis=1)


def _prep_w_in_ab(w):
    d = w.shape[0]
    sizes = (512, 512, 1024, 1024, 2 * GATE_RANK, MLA_Q_LORA, MLA_KV_LORA, MLA_ROPE)
    gq, gk, gv, gr, glr, cq, ckv, kpe = jnp.split(w, np.cumsum(sizes)[:-1].tolist(), axis=1)
    pad = jnp.zeros((d, AB_WIDTH - AB_MISC - MLA_ROPE - 2 * GATE_RANK), w.dtype)
    return jnp.concatenate([gq, gk, gv, gr, cq, ckv, kpe, glr, pad], axis=1).astype(BF16)


def _prep_w_uq(w):
    r = w.shape[0]
    w3 = w.reshape(r, MLA_HEADS, MLA_NOPE + MLA_ROPE)
    nope = w3[:, :, :MLA_NOPE].reshape(r, MLA_HEADS * MLA_NOPE)
    pe = jnp.pad(w3[:, :, MLA_NOPE:], ((0, 0), (0, 0), (0, LANES - MLA_ROPE))).reshape(r, MLA_HEADS * LANES)
    return jnp.concatenate([nope, pe], axis=1).astype(BF16)


def _prep_w_ukv(w):
    r = w.shape[0]
    w3 = w.reshape(r, MLA_HEADS, MLA_NOPE + MLA_DV)
    return jnp.concatenate([w3[:, :, :MLA_NOPE].reshape(r, -1), w3[:, :, MLA_NOPE:].reshape(r, -1)], axis=1).astype(BF16)


def _prep_router(router_w):
    d = router_w.shape[0]
    w = jnp.zeros((d, LANES), F32).at[:, :N_EXPERTS].set(router_w.astype(F32))
    hi = w.astype(BF16)
    lo = (w - hi.astype(F32)).astype(BF16)
    return jnp.stack([hi, lo], axis=0)


def kernel(x_prompt, x_sample, state_gla, cache_mla_ckv, cache_mla_kpe, state_ret, cache_na_k, cache_na_v,
           c, c_ctx, ada_w, ada_b, norm1, norm2, w_in_ab, gla_gate_w, gla_gate_b, gla_gain, mla_q_gain,
           mla_w_uq, mla_kv_gain, mla_w_ukv, w_in_cd, ret_log_decay, ret_gain, na_bias, w_out,
           router_w, router_bias, exp_w1, exp_w3, exp_w2, final_norm):
    bp, n_p, d = x_prompt.shape
    bs, n_s, _ = x_sample.shape
    past = cache_mla_ckv.shape[2]
    depth = ada_w.shape[0]
    tp, ts = bp * n_p, bs * n_s
    y = jnp.concatenate([x_prompt.reshape(tp, d), x_sample.reshape(ts, d)], axis=0)

    cond8 = jnp.zeros((8, d), F32).at[0].set(c_ctx).at[1:1 + bs].set(c)
    mod = adaln_all(cond8, ada_w, ada_b)
    rw_split = _prep_router(router_w)
    w_out_bf = w_out.astype(BF16)
    cos_t, sin_t = rope_tables(n_s, 512)

    outs = {}
    for l in range(depth):
        i = l // 2
        mod_l = mod[l].reshape(8, 1, 6 * d)
        final = l == depth - 1
        if l % 2 == 0:
            z = win_project(y, mod_l, norm1[l], _prep_w_in_ab(w_in_ab[i]), tp, n_s)
            zp = z[:tp].reshape(bp, n_p, -1)
            zs = z[tp:].reshape(bs, n_s, -1)
            s0p = jnp.zeros((bp, 2, GLA_HEADS, GLA_DK, GLA_DV), F32)
            og_p, st_p = _gla_mixer_jnp(zp, gla_gate_w[i], gla_gate_b[i], gla_gain[i], s0p)
            og_s, _ = _gla_mixer_jnp(zs, gla_gate_w[i], gla_gate_b[i], gla_gain[i], state_gla[:, i])
            o_a = jnp.concatenate([og_p.reshape(tp, -1), og_s.reshape(ts, -1)], axis=0).astype(BF16)
            wq, wkv = _prep_w_uq(mla_w_uq[i]), _prep_w_ukv(mla_w_ukv[i])
            q_m, kv, ckvn, kpe = mla_project(z, mla_q_gain[i], mla_kv_gain[i], wq, wkv, cos_t, sin_t, tp, n_s)
            kv_ctx = ctx_kv_project(cache_mla_ckv[:, i].reshape(bs * past, -1), wkv)
            kpe_ctx = jnp.pad(cache_mla_kpe[:, i].reshape(bs * past, -1), ((0, 0), (0, LANES - MLA_ROPE)))
            om_p = mla_prompt_attention(q_m, kv, kpe, bp, n_p)
            om_s = mla_sample_attention(q_m, kv, kpe, kv_ctx, kpe_ctx, tp, bs, n_s, past)
            o_b = jnp.concatenate([om_p, om_s], axis=0)
            outs.setdefault('gla', []).append(st_p)
            outs.setdefault('ckv', []).append(ckvn[:tp].reshape(bp, n_p, -1))
            outs.setdefault('kpe', []).append(z[:tp, AB_MISC:AB_MISC + MLA_ROPE].reshape(bp, n_p, -1))
        else:
            z = win_project(y, mod_l, norm1[l], w_in_cd[i].astype(BF16), tp, n_s)
            zp = z[:tp].reshape(bp, n_p, -1)
            zs = z[tp:].reshape(bs, n_s, -1)
            s0p = jnp.zeros((bp, 2, RET_HEADS, RET_DK, RET_DV), F32)
            or_p, st_p = _ret_mixer_jnp(zp, ret_log_decay[i], ret_gain[i], s0p, False)
            or_s, _ = _ret_mixer_jnp(zs, ret_log_decay[i], ret_gain[i], state_ret[:, i], True)
            o_a = jnp.concatenate([or_p.reshape(tp, -1), or_s.reshape(ts, -1)], axis=0).astype(BF16)
            on_p, kn, vn = na_prompt_attention(z, bp, n_p)
            qn = _to_heads(zs[..., CD_NQ:CD_NK], NA_HEADS)
            kns = _to_heads(zs[..., CD_NK:CD_NV], NA_HEADS)
            vns = _to_heads(zs[..., CD_NV:], NA_HEADS)
            on_s = _from_heads(_na_latent(qn, kns, vns, cache_na_k[:, i], cache_na_v[:, i], na_bias[i], NA_HD ** -0.5))
            o_b = jnp.concatenate([on_p, on_s.reshape(ts, -1).astype(BF16)], axis=0)
            outs.setdefault('ret', []).append(st_p)
            outs.setdefault('nak', []).append(kn)
            outs.setdefault('nav', []).append(vn)
        y1, h2, logits = wout_project(o_a, o_b, y, mod_l, w_out_bf[l], norm2[l], rw_split, tp, n_s)
        y = moe_layer(y1, h2, logits, mod_l, router_bias, exp_w1[l].astype(BF16), exp_w3[l].astype(BF16),
                      exp_w2[l].astype(BF16), final_norm, tp, n_s, final)

    return (y[:tp].reshape(bp, n_p, d), y[tp:].reshape(bs, n_s, d),
            jnp.stack(outs['gla'], axis=1), jnp.stack(outs['ckv'], axis=1), jnp.stack(outs['kpe'], axis=1),
            jnp.stack(outs['ret'], axis=1), jnp.stack(outs['nak'], axis=1), jnp.stack(outs['nav'], axis=1))
```

```python
import functools

import numpy as np
import jax
import jax.numpy as jnp
from jax import lax
from jax.experimental import pallas as pl
from jax.experimental.pallas import tpu as pltpu

F32 = jnp.float32
BF16 = jnp.bfloat16

EPS = 1e-6
GRID_W = 64
ROPE_BASE = 10000.0
GATE_RANK = 16
GATE_TAU = 16.0
GLA_HEADS, GLA_DK, GLA_DV = 8, 64, 128
MLA_HEADS, MLA_Q_LORA, MLA_KV_LORA, MLA_NOPE, MLA_ROPE, MLA_DV = 8, 512, 256, 128, 64, 128
RET_HEADS, RET_DK, RET_DV = 8, 64, 128
NA_HEADS, NA_HD, NA_ROWS, NA_COLS = 8, 128, 8, 16
N_EXPERTS, N_GROUPS, TOP_K, D_EXPERT = 32, 8, 2, 512
EXPERTS_PER_GROUP = N_EXPERTS // N_GROUPS
MOE_BLOCK = 128
LANES = 128

AB_GQ, AB_GK, AB_GV, AB_GR, AB_CQ, AB_CKV, AB_MISC, AB_WIDTH = 0, 512, 1024, 2048, 3072, 3584, 3840, 4096
CD_RQ, CD_RK, CD_RV, CD_RG, CD_NQ, CD_NK, CD_NV, CD_WIDTH = 0, 512, 1024, 2048, 3072, 4096, 5120, 6144

VMEM_LIMIT = 56 * 1024 * 1024


def _cparams(sem):
    return pltpu.CompilerParams(dimension_semantics=sem, vmem_limit_bytes=VMEM_LIMIT)


def _dot(a, b):
    return jnp.dot(a, b, preferred_element_type=F32)


def _dot_nt(a, b):
    return lax.dot_general(a, b, (((1,), (1,)), ((), ())), preferred_element_type=F32)


def _adaln_kernel(c_ref, w_ref, b_ref, o_ref):
    c = c_ref[...]
    s = (c * jax.nn.sigmoid(c)).astype(BF16)
    o_ref[...] = _dot(s, w_ref[...].astype(BF16)) + b_ref[...]


def adaln_all(cond8, ada_w, ada_b, tn=512):
    depth, d, n6 = ada_w.shape
    return pl.pallas_call(
        _adaln_kernel,
        grid=(depth, n6 // tn),
        in_specs=[pl.BlockSpec((8, d), lambda l, j: (0, 0)),
                  pl.BlockSpec((None, d, tn), lambda l, j: (l, 0, j)),
                  pl.BlockSpec((None, 1, tn), lambda l, j: (l, 0, j))],
        out_specs=pl.BlockSpec((None, 8, tn), lambda l, j: (l, 0, j)),
        out_shape=jax.ShapeDtypeStruct((depth, 8, n6), F32),
        compiler_params=_cparams(("parallel", "parallel")),
        name="adaln",
    )(cond8, ada_w, ada_b.reshape(depth, 1, n6))


def _mod_index(i, tm, n_prompt_rows, sample_rows):
    tp = n_prompt_rows // tm
    return jnp.where(i < tp, 0, 1 + (i - tp) // (sample_rows // tm))


def _win_kernel(x_ref, mod_ref, g_ref, w_ref, o_ref, h_ref, *, d):
    @pl.when(pl.program_id(1) == 0)
    def _():
        x = x_ref[...]
        xn = x * lax.rsqrt(jnp.mean(x * x, axis=-1, keepdims=True) + EPS) * g_ref[...]
        h = xn * (1.0 + mod_ref[:, d:2 * d]) + mod_ref[:, 0:d]
        h_ref[...] = h.astype(BF16)

    o_ref[...] = _dot(h_ref[...], w_ref[...])


def win_project(y, mod_l, gain, w_bf16, n_prompt_rows, sample_rows, tm=512, tn=512):
    t, d = y.shape
    width = w_bf16.shape[1]
    midx = functools.partial(_mod_index, tm=tm, n_prompt_rows=n_prompt_rows, sample_rows=sample_rows)
    return pl.pallas_call(
        functools.partial(_win_kernel, d=d),
        grid=(t // tm, width // tn),
        in_specs=[pl.BlockSpec((tm, d), lambda i, j: (i, 0)),
                  pl.BlockSpec((None, 1, 6 * d), lambda i, j: (midx(i), 0, 0)),
                  pl.BlockSpec((1, d), lambda i, j: (0, 0)),
                  pl.BlockSpec((d, tn), lambda i, j: (0, j))],
        out_specs=pl.BlockSpec((tm, tn), lambda i, j: (i, j)),
        out_shape=jax.ShapeDtypeStruct((t, width), F32),
        scratch_shapes=[pltpu.VMEM((tm, d), BF16)],
        compiler_params=_cparams(("parallel", "arbitrary")),
        name="win_project",
    )(y, mod_l, gain.reshape(1, d), w_bf16)


def _pack_bf16_pairs(x):
    c = x.shape[1] // 2
    u = lax.bitcast_convert_type(x.astype(F32), jnp.uint32)
    return u[:, c:] | (u[:, :c] >> 16)


def _unpack_bf16_pairs(w):
    lo = lax.bitcast_convert_type(w << 16, F32).astype(BF16)
    hi = lax.bitcast_convert_type(w & jnp.uint32(0xFFFF0000), F32).astype(BF16)
    return lo, hi


def _wout_kernel(oa_ref, ob_ref, y_ref, mod_ref, wa_ref, wb_ref, g_ref, rw_ref, ynew_ref, h2_ref, lg_ref, *, d):
    out = _dot(oa_ref[...], wa_ref[...]) + _dot(ob_ref[...], wb_ref[...])
    y1 = y_ref[...] + mod_ref[:, 2 * d:3 * d] * out
    ynew_ref[...] = y1
    xn = y1 * lax.rsqrt(jnp.mean(y1 * y1, axis=-1, keepdims=True) + EPS) * g_ref[...]
    h = xn * (1.0 + mod_ref[:, 4 * d:5 * d]) + mod_ref[:, 3 * d:4 * d]
    hh = h.astype(BF16)
    h2_ref[...] = _pack_bf16_pairs(hh)
    hl = (h - hh.astype(F32)).astype(BF16)
    lg_ref[...] = _dot(hh, rw_ref[0]) + _dot(hl, rw_ref[0]) + _dot(hh, rw_ref[1])


def wout_project(o_a, o_b, y, mod_l, w_out_bf16, gain2, rw_split, n_prompt_rows, sample_rows, tm=256):
    t, d = y.shape
    ka, kb = o_a.shape[1], o_b.shape[1]
    midx = functools.partial(_mod_index, tm=tm, n_prompt_rows=n_prompt_rows, sample_rows=sample_rows)
    return pl.pallas_call(
        functools.partial(_wout_kernel, d=d),
        grid=(t // tm,),
        in_specs=[pl.BlockSpec((tm, ka), lambda i: (i, 0)),
                  pl.BlockSpec((tm, kb), lambda i: (i, 0)),
                  pl.BlockSpec((tm, d), lambda i: (i, 0)),
                  pl.BlockSpec((None, 1, 6 * d), lambda i: (midx(i), 0, 0)),
                  pl.BlockSpec((ka, d), lambda i: (0, 0)),
                  pl.BlockSpec((kb, d), lambda i: (1, 0)),
                  pl.BlockSpec((1, d), lambda i: (0, 0)),
                  pl.BlockSpec((2, d, LANES), lambda i: (0, 0, 0))],
        out_specs=[pl.BlockSpec((tm, d), lambda i: (i, 0)),
                   pl.BlockSpec((tm, d // 2), lambda i: (i, 0)),
                   pl.BlockSpec((tm, LANES), lambda i: (i, 0))],
        out_shape=[jax.ShapeDtypeStruct((t, d), F32),
                   jax.ShapeDtypeStruct((t, d // 2), jnp.uint32),
                   jax.ShapeDtypeStruct((t, LANES), F32)],
        compiler_params=_cparams(("parallel",)),
        name="wout_project",
    )(o_a, o_b, y, mod_l, w_out_bf16, w_out_bf16, gain2.reshape(1, d), rw_split)


def _group_shift(x, k, sub):
    fwd = pltpu.roll(x, LANES - k, axis=1)
    back = pltpu.roll(x, EXPERTS_PER_GROUP - k, axis=1)
    return jnp.where(sub + k < EXPERTS_PER_GROUP, fwd, back)


def _group_reduce(x, sub, op):
    r = x
    for k in range(1, EXPERTS_PER_GROUP):
        r = op(r, _group_shift(x, k, sub))
    return r


def _route_kernel(lg_ref, bias_ref, info_ref, cnt_ref, run_ref, *, tm):
    @pl.when(pl.program_id(0) == 0)
    def _():
        run_ref[...] = jnp.zeros_like(run_ref)

    lg = lg_ref[...]
    lane = lax.broadcasted_iota(jnp.int32, lg.shape, 1)
    lane_f = lane.astype(F32)
    sub = (lane & (EXPERTS_PER_GROUP - 1)).astype(F32)
    grp = (lane >> 2).astype(F32)
    aff = jax.nn.sigmoid(lg)
    neg = jnp.float32(-jnp.inf)
    sel = jnp.where(lane < N_EXPERTS, aff + bias_ref[...], neg)
    m1 = _group_reduce(sel, sub, jnp.maximum)
    i1 = _group_reduce(jnp.where(sel == m1, sub, 4.0), sub, jnp.minimum)
    sel2 = jnp.where(sub == i1, neg, sel)
    m2 = _group_reduce(sel2, sub, jnp.maximum)
    i2 = _group_reduce(jnp.where(sel2 == m2, sub, 4.0), sub, jnp.minimum)
    gscore = m1 + m2
    gmax = jnp.max(gscore, axis=1, keepdims=True)
    g = jnp.min(jnp.where(gscore == gmax, grp, 99.0), axis=1, keepdims=True)
    oh1 = (grp == g) & (sub == i1)
    oh2 = (grp == g) & (sub == i2)

    def pick(mask, v):
        return jnp.sum(jnp.where(mask, v, 0.0), axis=1, keepdims=True)

    a1, a2 = pick(oh1, aff), pick(oh2, aff)
    den = a1 + a2
    e1, e2 = pick(oh1, lane_f), pick(oh2, lane_f)
    oh = jnp.where(oh1 | oh2, 1.0, 0.0)
    r_i = lax.broadcasted_iota(jnp.int32, (tm, tm), 0)
    c_i = lax.broadcasted_iota(jnp.int32, (tm, tm), 1)
    tri = jnp.where(c_i < r_i, 1.0, 0.0).astype(BF16)
    cum = _dot(tri, oh.astype(BF16)) + run_ref[0:1, :]
    r1, r2 = pick(oh1, cum), pick(oh2, cum)
    run_new = run_ref[0:1, :] + jnp.sum(oh, axis=0, keepdims=True)
    run_ref[...] = jnp.broadcast_to(run_new, run_ref.shape)
    cnt_ref[...] = jnp.broadcast_to(run_new, cnt_ref.shape)
    info = jnp.zeros_like(lg)
    for k, v in enumerate((e1, e2, a1 / den, a2 / den, r1, r2)):
        info = jnp.where(lane == k, v, info)
    info_ref[...] = info


def route(logits, router_bias, tm=256):
    t = logits.shape[0]
    bias = jnp.zeros((1, LANES), F32).at[0, :N_EXPERTS].set(router_bias.astype(F32))
    return pl.pallas_call(
        functools.partial(_route_kernel, tm=tm),
        grid=(t // tm,),
        in_specs=[pl.BlockSpec((tm, LANES), lambda i: (i, 0)),
                  pl.BlockSpec((1, LANES), lambda i: (0, 0))],
        out_specs=[pl.BlockSpec((tm, LANES), lambda i: (i, 0)),
                   pl.BlockSpec((8, LANES), lambda i: (0, 0))],
        out_shape=[jax.ShapeDtypeStruct((t, LANES), F32),
                   jax.ShapeDtypeStruct((8, LANES), F32)],
        scratch_shapes=[pltpu.VMEM((8, LANES), F32)],
        compiler_params=_cparams(("arbitrary",)),
        name="route",
    )(logits, bias)


def _expert_kernel(be_ref, nu_ref, x_ref, w1_ref, w3_ref, w2_ref, o_ref):
    i = pl.program_id(0)

    @pl.when(i < nu_ref[0])
    def _():
        x_lo, x_hi = _unpack_bf16_pairs(x_ref[...])
        c = x_lo.shape[1]
        a = _dot(x_lo, w1_ref[0:c, :]) + _dot(x_hi, w1_ref[c:, :])
        b = _dot(x_lo, w3_ref[0:c, :]) + _dot(x_hi, w3_ref[c:, :])
        mid = (a * jax.nn.sigmoid(a)) * b
        o_ref[...] = _dot(mid.astype(BF16), w2_ref[...])

    @pl.when(i >= nu_ref[0])
    def _():
        o_ref[...] = jnp.zeros_like(o_ref)


def expert_ffn(xs, block_expert, n_used, w1, w3, w2):
    slots = xs.shape[0]
    n_blocks = slots // MOE_BLOCK
    d, de = w1.shape[-2], w1.shape[-1]
    grid_spec = pltpu.PrefetchScalarGridSpec(
        num_scalar_prefetch=2,
        grid=(n_blocks,),
        in_specs=[pl.BlockSpec((MOE_BLOCK, d // 2), lambda i, be, nu: (i, 0)),
                  pl.BlockSpec((None, d, de), lambda i, be, nu: (be[i], 0, 0)),
                  pl.BlockSpec((None, d, de), lambda i, be, nu: (be[i], 0, 0)),
                  pl.BlockSpec((None, de, d), lambda i, be, nu: (be[i], 0, 0))],
        out_specs=pl.BlockSpec((MOE_BLOCK, d), lambda i, be, nu: (i, 0)),
    )
    return pl.pallas_call(
        _expert_kernel,
        grid_spec=grid_spec,
        out_shape=jax.ShapeDtypeStruct((slots, d), F32),
        compiler_params=_cparams(("arbitrary",)),
        name="expert_ffn",
    )(block_expert, n_used, xs, w1, w3, w2)


def _dispatch_kernel(dest_ref, h_hbm, xs_in_hbm, xs_hbm, sem, *, tm, nt):
    del xs_in_hbm
    i = pl.program_id(0)
    base = i * tm

    def row_copy(src_row, dst_row):
        return pltpu.make_async_copy(h_hbm.at[pl.ds(src_row, 1)], xs_hbm.at[pl.ds(dst_row, 1)], sem)

    def issue(j, carry):
        for k in range(TOP_K):
            row_copy(base + j, dest_ref[TOP_K * j + k]).start()
        return carry

    def drain(j, carry):
        row_copy(0, 0).wait()
        return carry

    lax.fori_loop(0, tm, issue, 0)

    @pl.when(i > 0)
    def _():
        lax.fori_loop(0, TOP_K * tm, drain, 0)

    @pl.when(i == nt - 1)
    def _():
        lax.fori_loop(0, TOP_K * tm, drain, 0)


def dispatch(h2p, dest_flat, slots, tm=256):
    t, c = h2p.shape
    nt = t // tm
    xs0 = jnp.zeros((slots, c), h2p.dtype)
    return pl.pallas_call(
        functools.partial(_dispatch_kernel, tm=tm, nt=nt),
        grid=(nt,),
        in_specs=[pl.BlockSpec((TOP_K * tm,), lambda i: (i,), memory_space=pltpu.SMEM),
                  pl.BlockSpec(memory_space=pl.ANY),
                  pl.BlockSpec(memory_space=pl.ANY)],
        out_specs=pl.BlockSpec(memory_space=pl.ANY),
        out_shape=jax.ShapeDtypeStruct((slots, c), h2p.dtype),
        scratch_shapes=[pltpu.SemaphoreType.DMA(())],
        input_output_aliases={2: 0},
        compiler_params=_cparams(("arbitrary",)),
        name="dispatch",
    )(dest_flat, h2p, xs0)


def _combine_kernel(dcur_ref, dnext_ref, y_ref, mod_ref, info_ref, g_ref, yb_hbm, o_ref, buf, sem,
                    *, tm, nt, d, final):
    i = pl.program_id(0)
    slot = i % 2

    def row_copy(src_row, s, k, j):
        return pltpu.make_async_copy(yb_hbm.at[pl.ds(src_row, 1)], buf.at[s, k, pl.ds(j, 1)], sem.at[s])

    def issue(dref, s):
        def body(j, carry):
            for k in range(TOP_K):
                row_copy(dref[TOP_K * j + k], s, k, j).start()
            return carry

        lax.fori_loop(0, tm, body, 0)

    @pl.when(i == 0)
    def _():
        issue(dcur_ref, 0)

    @pl.when(i + 1 < nt)
    def _():
        issue(dnext_ref, 1 - slot)

    def drain(j, carry):
        row_copy(0, slot, 0, 0).wait()
        return carry

    lax.fori_loop(0, TOP_K * tm, drain, 0)
    info = info_ref[...]
    moe = info[:, 2:3] * buf[slot, 0] + info[:, 3:4] * buf[slot, 1]
    y2 = y_ref[...] + mod_ref[:, 5 * d:6 * d] * moe
    if final:
        y2 = y2 * lax.rsqrt(jnp.mean(y2 * y2, axis=-1, keepdims=True) + EPS) * g_ref[...]
    o_ref[...] = y2


def combine(y1, mod_l, info, dest_flat, yb, final_gain, n_prompt_rows, sample_rows, final, tm=256):
    t, d = y1.shape
    nt = t // tm
    midx = functools.partial(_mod_index, tm=tm, n_prompt_rows=n_prompt_rows, sample_rows=sample_rows)
    return pl.pallas_call(
        functools.partial(_combine_kernel, tm=tm, nt=nt, d=d, final=final),
        grid=(nt,),
        in_specs=[pl.BlockSpec((TOP_K * tm,), lambda i: (i,), memory_space=pltpu.SMEM),
                  pl.BlockSpec((TOP_K * tm,), lambda i: (jnp.minimum(i + 1, nt - 1),), memory_space=pltpu.SMEM),
                  pl.BlockSpec((tm, d), lambda i: (i, 0)),
                  pl.BlockSpec((None, 1, 6 * d), lambda i: (midx(i), 0, 0)),
                  pl.BlockSpec((tm, LANES), lambda i: (i, 0)),
                  pl.BlockSpec((1, d), lambda i: (0, 0)),
                  pl.BlockSpec(memory_space=pl.ANY)],
        out_specs=pl.BlockSpec((tm, d), lambda i: (i, 0)),
        out_shape=jax.ShapeDtypeStruct((t, d), F32),
        scratch_shapes=[pltpu.VMEM((2, TOP_K, tm, d), F32), pltpu.SemaphoreType.DMA((2,))],
        compiler_params=_cparams(("arbitrary",)),
        name="combine",
    )(dest_flat, dest_flat, y1, mod_l, info, final_gain.reshape(1, d), yb)


def moe_layer(y1, h2p, logits, mod_l, router_bias, w1, w3, w2, final_gain, n_prompt_rows, sample_rows, final):
    t, d = y1.shape
    info, cnt = route(logits, router_bias)
    expert = info[:, 0:2].astype(jnp.int32)
    rank = info[:, 4:6].astype(jnp.int32)
    counts = cnt[0, :N_EXPERTS].astype(jnp.int32)
    padded = (counts + MOE_BLOCK - 1) // MOE_BLOCK * MOE_BLOCK
    padded_end = jnp.cumsum(padded)
    padded_start = padded_end - padded
    dest = (padded_start[expert] + rank).reshape(-1)
    n_blocks = (t * TOP_K) // MOE_BLOCK + N_EXPERTS
    slots = n_blocks * MOE_BLOCK
    block_lo = jnp.arange(n_blocks, dtype=jnp.int32)[:, None] * MOE_BLOCK
    block_expert = jnp.minimum(jnp.sum((padded_end[None, :] <= block_lo).astype(jnp.int32), axis=1), N_EXPERTS - 1)
    n_used = (padded_end[-1:] // MOE_BLOCK).astype(jnp.int32)
    xs = dispatch(h2p, dest, slots)
    yb = expert_ffn(xs, block_expert, n_used, w1, w3, w2)
    return combine(y1, mod_l, info, dest, yb, final_gain, n_prompt_rows, sample_rows, final)


def rope_tables(n, lead):
    quarter = MLA_ROPE // 4
    tpos = np.arange(n)
    row = (tpos // GRID_W).astype(np.float32)
    col = (tpos % GRID_W).astype(np.float32)
    inv_freq = (np.float32(ROPE_BASE) ** (-np.arange(quarter, dtype=np.float32) / np.float32(quarter))).astype(np.float32)
    lane = np.arange(64)
    pos = np.where(lane[None, :] < 32, row[:, None], col[:, None]).astype(np.float32)
    ang = pos * inv_freq[lane % quarter][None, :]
    cos = np.cos(ang).astype(np.float32)
    sin = np.sin(ang).astype(np.float32)
    sgn = np.where((lane % 32) < quarter, -1.0, 1.0).astype(np.float32)
    sin = sin * sgn[None, :]
    cos = np.concatenate([cos, cos], axis=1)
    sin = np.concatenate([sin, sin], axis=1)
    cos = np.concatenate([np.ones((lead, LANES), np.float32), cos], axis=0)
    sin = np.concatenate([np.zeros((lead, LANES), np.float32), sin], axis=0)
    return jnp.asarray(cos), jnp.asarray(sin)


def _rope(x, cos, sin_signed):
    lane = lax.broadcasted_iota(jnp.int32, x.shape, 1)
    partner = jnp.where((lane & 31) < 16, pltpu.roll(x, LANES - 16, axis=1), pltpu.roll(x, 16, axis=1))
    return x * cos + partner * sin_signed


def _mla_proj_kernel(cq_ref, ckv_ref, misc_ref, qg_ref, kvg_ref, wq_ref, wkv_ref, cos_ref, sin_ref,
                     q_ref, kv_ref, ckvn_ref, kpe_ref, *, scale):
    cq = cq_ref[...]
    qn = cq * lax.rsqrt(jnp.mean(cq * cq, axis=-1, keepdims=True) + EPS) * qg_ref[...]
    qf = _dot(qn.astype(BF16), wq_ref[...])
    half = qf.shape[1] // 2
    cos, sin = cos_ref[...], sin_ref[...]
    q_ref[:, 0:half] = (qf[:, 0:half] * scale).astype(BF16)
    for h in range(MLA_HEADS):
        pe = qf[:, half + h * LANES: half + (h + 1) * LANES]
        q_ref[:, half + h * LANES: half + (h + 1) * LANES] = (_rope(pe, cos, sin) * scale).astype(BF16)
    ckv = ckv_ref[...]
    ckvn = ckv * lax.rsqrt(jnp.mean(ckv * ckv, axis=-1, keepdims=True) + EPS) * kvg_ref[...]
    ckvn_ref[...] = ckvn
    kv_ref[...] = _dot(ckvn.astype(BF16), wkv_ref[...]).astype(BF16)
    misc = misc_ref[...]
    lane = lax.broadcasted_iota(jnp.int32, misc.shape, 1)
    kpe = jnp.where(lane < MLA_ROPE, misc, 0.0)
    kpe_ref[...] = _rope(kpe, cos, sin).astype(BF16)


def mla_project(z, q_gain, kv_gain, wq, wkv, cos_t, sin_t, n_prompt_rows, sample_rows, tm=512):
    t = z.shape[0]
    tp = n_prompt_rows // tm
    per = sample_rows // tm

    def tab(i):
        return (jnp.where(i < tp, 0, 1 + (i - tp) % per), 0)

    scale = float((MLA_NOPE + MLA_ROPE) ** -0.5)
    nq = wq.shape[1]
    return pl.pallas_call(
        functools.partial(_mla_proj_kernel, scale=scale),
        grid=(t // tm,),
        in_specs=[pl.BlockSpec((tm, MLA_Q_LORA), lambda i: (i, AB_CQ // MLA_Q_LORA)),
                  pl.BlockSpec((tm, MLA_KV_LORA), lambda i: (i, AB_CKV // MLA_KV_LORA)),
                  pl.BlockSpec((tm, LANES), lambda i: (i, AB_MISC // LANES)),
                  pl.BlockSpec((1, MLA_Q_LORA), lambda i: (0, 0)),
                  pl.BlockSpec((1, MLA_KV_LORA), lambda i: (0, 0)),
                  pl.BlockSpec(wq.shape, lambda i: (0, 0)),
                  pl.BlockSpec(wkv.shape, lambda i: (0, 0)),
                  pl.BlockSpec((tm, LANES), tab),
                  pl.BlockSpec((tm, LANES), tab)],
        out_specs=[pl.BlockSpec((tm, nq), lambda i: (i, 0)),
                   pl.BlockSpec((tm, wkv.shape[1]), lambda i: (i, 0)),
                   pl.BlockSpec((tm, MLA_KV_LORA), lambda i: (i, 0)),
                   pl.BlockSpec((tm, LANES), lambda i: (i, 0))],
        out_shape=[jax.ShapeDtypeStruct((t, nq), BF16),
                   jax.ShapeDtypeStruct((t, wkv.shape[1]), BF16),
                   jax.ShapeDtypeStruct((t, MLA_KV_LORA), F32),
                   jax.ShapeDtypeStruct((t, LANES), BF16)],
        compiler_params=_cparams(("parallel",)),
        name="mla_project",
    )(z, z, z, q_gain.reshape(1, -1), kv_gain.reshape(1, -1), wq, wkv, cos_t, sin_t)


def _ctx_kv_kernel(ckv_ref, w_ref, o_ref):
    o_ref[...] = _dot(ckv_ref[...].astype(BF16), w_ref[...]).astype(BF16)


def ctx_kv_project(ckv_c, wkv):
    rows = ckv_c.shape[0]
    return pl.pallas_call(
        _ctx_kv_kernel,
        grid=(1,),
        in_specs=[pl.BlockSpec(ckv_c.shape, lambda i: (0, 0)), pl.BlockSpec(wkv.shape, lambda i: (0, 0))],
        out_specs=pl.BlockSpec((rows, wkv.shape[1]), lambda i: (0, 0)),
        out_shape=jax.ShapeDtypeStruct((rows, wkv.shape[1]), BF16),
        compiler_params=_cparams(("arbitrary",)),
        name="ctx_kv_project",
    )(ckv_c, wkv)


def _softmax_pv(s, v):
    m = jnp.max(s, axis=-1, keepdims=True)
    p = jnp.exp(s - m)
    l = jnp.sum(p, axis=-1, keepdims=True)
    return _dot(p.astype(BF16), v) / l


def _mla_prompt_attn_kernel(qn_ref, qp_ref, kn_ref, kp_ref, v_ref, o_ref):
    q = jnp.concatenate([qn_ref[...], qp_ref[...]], axis=1)
    k = jnp.concatenate([kn_ref[...], kp_ref[...]], axis=1)
    o_ref[...] = _softmax_pv(_dot_nt(q, k), v_ref[...]).astype(o_ref.dtype)


def mla_prompt_attention(q_m, kv, kpe, batch, n):
    h = MLA_HEADS
    return pl.pallas_call(
        _mla_prompt_attn_kernel,
        grid=(batch, h),
        in_specs=[pl.BlockSpec((n, LANES), lambda b, hh: (b, hh)),
                  pl.BlockSpec((n, LANES), lambda b, hh: (b, h + hh)),
                  pl.BlockSpec((n, LANES), lambda b, hh: (b, hh)),
                  pl.BlockSpec((n, LANES), lambda b, hh: (b, 0)),
                  pl.BlockSpec((n, LANES), lambda b, hh: (b, h + hh))],
        out_specs=pl.BlockSpec((n, LANES), lambda b, hh: (b, hh)),
        out_shape=jax.ShapeDtypeStruct((batch * n, h * MLA_DV), BF16),
        compiler_params=_cparams(("parallel", "parallel")),
        name="mla_prompt_attention",
    )(q_m, q_m, kv, kpe, kv)


def _mla_sample_attn_kernel(qn_ref, qp_ref, kcn_ref, kcp_ref, vc_ref, kn_ref, kp_ref, v_ref, o_ref,
                            m_ref, l_ref, acc_ref, *, tk):
    q = jnp.concatenate([qn_ref[...], qp_ref[...]], axis=1)
    kc = jnp.concatenate([kcn_ref[...], kcp_ref[...].astype(BF16)], axis=1)
    s = _dot_nt(q, kc)
    m = jnp.max(s, axis=-1, keepdims=True)
    p = jnp.exp(s - m)
    m_ref[...] = m
    l_ref[...] = jnp.sum(p, axis=-1, keepdims=True)
    acc_ref[...] = _dot(p.astype(BF16), vc_ref[...])
    nk = kn_ref.shape[0] // tk

    def body(j, carry):
        off = pl.multiple_of(j * tk, tk)
        k = jnp.concatenate([kn_ref[pl.ds(off, tk), :], kp_ref[pl.ds(off, tk), :]], axis=1)
        s = _dot_nt(q, k)
        m_old = m_ref[...]
        m_new = jnp.maximum(m_old, jnp.max(s, axis=-1, keepdims=True))
        alpha = jnp.exp(m_old - m_new)
        p = jnp.exp(s - m_new)
        l_ref[...] = alpha * l_ref[...] + jnp.sum(p, axis=-1, keepdims=True)
        acc_ref[...] = alpha * acc_ref[...] + _dot(p.astype(BF16), v_ref[pl.ds(off, tk), :])
        m_ref[...] = m_new
        return carry

    lax.fori_loop(0, nk, body, 0)
    o_ref[...] = (acc_ref[...] / l_ref[...]).astype(o_ref.dtype)


def mla_sample_attention(q_m, kv, kpe, kv_ctx, kpe_ctx, row0, batch, n, past, tq=512, tk=512):
    h = MLA_HEADS
    qb = n // tq
    r0 = row0 // n
    q0 = row0 // tq
    return pl.pallas_call(
        functools.partial(_mla_sample_attn_kernel, tk=tk),
        grid=(batch, h, qb),
        in_specs=[pl.BlockSpec((tq, LANES), lambda b, hh, i: (q0 + b * qb + i, hh)),
                  pl.BlockSpec((tq, LANES), lambda b, hh, i: (q0 + b * qb + i, h + hh)),
                  pl.BlockSpec((past, LANES), lambda b, hh, i: (b, hh)),
                  pl.BlockSpec((past, LANES), lambda b, hh, i: (b, 0)),
                  pl.BlockSpec((past, LANES), lambda b, hh, i: (b, h + hh)),
                  pl.BlockSpec((n, LANES), lambda b, hh, i: (r0 + b, hh)),
                  pl.BlockSpec((n, LANES), lambda b, hh, i: (r0 + b, 0)),
                  pl.BlockSpec((n, LANES), lambda b, hh, i: (r0 + b, h + hh))],
        out_specs=pl.BlockSpec((tq, LANES), lambda b, hh, i: (b * qb + i, hh)),
        out_shape=jax.ShapeDtypeStruct((batch * n, h * MLA_DV), BF16),
        scratch_shapes=[pltpu.VMEM((tq, 1), F32), pltpu.VMEM((tq, 1), F32), pltpu.VMEM((tq, MLA_DV), F32)],
        compiler_params=_cparams(("parallel", "parallel", "arbitrary")),
        name="mla_sample_attention",
    )(q_m, q_m, kv_ctx, kpe_ctx, kv_ctx, kv, kpe, kv)


def _na_prompt_attn_kernel(q_ref, k_ref, v_ref, o_ref, ko_ref, vo_ref, *, scale):
    k = k_ref[...]
    v = v_ref[...]
    ko_ref[...] = k
    vo_ref[...] = v
    q = (q_ref[...] * scale).astype(BF16)
    o_ref[...] = _softmax_pv(_dot_nt(q, k.astype(BF16)), v.astype(BF16)).astype(o_ref.dtype)


def na_prompt_attention(z, batch, n):
    h = NA_HEADS
    scale = float(NA_HD ** -0.5)
    cq, ck, cv = CD_NQ // LANES, CD_NK // LANES, CD_NV // LANES
    return pl.pallas_call(
        functools.partial(_na_prompt_attn_kernel, scale=scale),
        grid=(batch, h),
        in_specs=[pl.BlockSpec((n, LANES), lambda b, hh: (b, cq + hh)),
                  pl.BlockSpec((n, LANES), lambda b, hh: (b, ck + hh)),
                  pl.BlockSpec((n, LANES), lambda b, hh: (b, cv + hh))],
        out_specs=[pl.BlockSpec((n, LANES), lambda b, hh: (b, hh)),
                   pl.BlockSpec((None, None, n, NA_HD), lambda b, hh: (b, hh, 0, 0)),
                   pl.BlockSpec((None, None, n, NA_HD), lambda b, hh: (b, hh, 0, 0))],
        out_shape=[jax.ShapeDtypeStruct((batch * n, h * NA_HD), BF16),
                   jax.ShapeDtypeStruct((batch, h, n, NA_HD), F32),
                   jax.ShapeDtypeStruct((batch, h, n, NA_HD), F32)],
        compiler_params=_cparams(("parallel", "parallel")),
        name="na_prompt_attention",
    )(z, z, z)


def _dot_tn(a, b):
    return lax.dot_general(a, b, (((0,), (0,)), ((), ())), preferred_element_type=F32)


def _head_norm_gate(acc, gate_in, gain):
    outs = []
    for h in range(2):
        a = acc[:, h * LANES:(h + 1) * LANES]
        outs.append(a * lax.rsqrt(jnp.mean(a * a, axis=-1, keepdims=True) + EPS))
    g = gate_in
    return jnp.concatenate(outs, axis=1) * gain * (g * jax.nn.sigmoid(g))


RET_CHUNK = 128
FIN_ROWS = 256


def _ret_kernel(ld_ref, q_ref, k_ref, v_ref, g_ref, cos_ref, sin_ref, gain_ref, s0_ref, o_ref, sf_ref, acc_ref, *, n):
    L = RET_CHUNK
    nc = n // L
    hp = pl.program_id(1)
    lane = lax.broadcasted_iota(jnp.int32, (1, LANES), 1)
    lo_lanes = lane < RET_DK
    row128 = lax.broadcasted_iota(jnp.int32, (LANES, 1), 0)
    ri = lax.broadcasted_iota(jnp.int32, (L, L), 0)
    ci = lax.broadcasted_iota(jnp.int32, (L, L), 1)
    dist = (ri - ci).astype(F32)
    pos = lax.broadcasted_iota(jnp.int32, (L, 1), 0).astype(F32)

    consts = []
    for dr in range(2):
        lg0 = ld_ref[dr, 2 * hp]
        lg1 = ld_ref[dr, 2 * hp + 1]
        lg2 = jnp.where(lo_lanes, lg0, lg1)
        lgc = jnp.where(row128 < RET_DK, lg0, lg1)
        sd = dist if dr == 0 else -dist
        ok = sd >= 0
        intra = [jnp.where(ok, jnp.exp(jnp.where(ok, sd, 0.0) * lg), 0.0) for lg in (lg0, lg1)]
        if dr == 0:
            inter = jnp.exp((pos + 1.0) * lg2)
            to_state = jnp.exp((L - 1.0 - pos) * lg2)
        else:
            inter = jnp.exp((L - pos) * lg2)
            to_state = jnp.exp(pos * lg2)
        consts.append((intra, inter, to_state, jnp.exp(float(L) * lgc)))

    def clear(i, carry):
        acc_ref[pl.ds(pl.multiple_of(i * FIN_ROWS, FIN_ROWS), FIN_ROWS), :] = jnp.zeros((FIN_ROWS, 2 * LANES), F32)
        return carry

    lax.fori_loop(0, n // FIN_ROWS, clear, 0)

    def step(c, dr, s2):
        intra, inter, to_state, cdec = consts[dr]
        rows = pl.ds(pl.multiple_of(c * L, L), L)
        cos, sin = cos_ref[rows, :], sin_ref[rows, :]
        q = _rope(q_ref[rows, :] * (RET_DK ** -0.5), cos, sin)
        k = _rope(k_ref[rows, :], cos, sin)
        kb = k.astype(BF16)
        s2b = s2.astype(BF16)
        kt = (k * to_state).astype(BF16)
        upd = []
        for h in range(2):
            qm = jnp.where(lo_lanes if h == 0 else ~lo_lanes, q, 0.0)
            vh = v_ref[rows, h * LANES:(h + 1) * LANES].astype(BF16)
            sc = _dot_nt(qm.astype(BF16), kb) * intra[h]
            o = _dot(sc.astype(BF16), vh) + _dot((qm * inter).astype(BF16), s2b)
            acc_ref[rows, h * LANES:(h + 1) * LANES] += o
            upd.append(_dot_tn(kt, vh))
        return cdec * s2 + jnp.where(row128 < RET_DK, upd[0], upd[1])

    def body(i, carry):
        sf, sb = carry
        return step(i, 0, sf), step(nc - 1 - i, 1, sb)

    sf, sb = lax.fori_loop(0, nc, body, (s0_ref[0], s0_ref[1]))
    sf_ref[0] = sf
    sf_ref[1] = sb

    def fin(i, carry):
        rows = pl.ds(pl.multiple_of(i * FIN_ROWS, FIN_ROWS), FIN_ROWS)
        o_ref[rows, :] = _head_norm_gate(acc_ref[rows, :], g_ref[rows, :], gain_ref[...]).astype(o_ref.dtype)
        return carry

    lax.fori_loop(0, n // FIN_ROWS, fin, 0)


def retention_mixer(z, log_decay, ret_gain, s0, cos_t, sin_t, row0, batch, n):
    hp = RET_HEADS // 2
    r0 = row0 // n
    cq, ck, cv, cg = CD_RQ // LANES, CD_RK // LANES, CD_RV // (2 * LANES), CD_RG // (2 * LANES)
    grid_spec = pltpu.PrefetchScalarGridSpec(
        num_scalar_prefetch=1,
        grid=(batch, hp),
        in_specs=[pl.BlockSpec((n, LANES), lambda b, p, ld: (r0 + b, cq + p)),
                  pl.BlockSpec((n, LANES), lambda b, p, ld: (r0 + b, ck + p)),
                  pl.BlockSpec((n, 2 * LANES), lambda b, p, ld: (r0 + b, cv + p)),
                  pl.BlockSpec((n, 2 * LANES), lambda b, p, ld: (r0 + b, cg + p)),
                  pl.BlockSpec((n, LANES), lambda b, p, ld: (0, 0)),
                  pl.BlockSpec((n, LANES), lambda b, p, ld: (0, 0)),
                  pl.BlockSpec((1, 2 * LANES), lambda b, p, ld: (0, p)),
                  pl.BlockSpec((None, 2, LANES, LANES), lambda b, p, ld: (b, 0, p, 0))],
        out_specs=[pl.BlockSpec((n, 2 * LANES), lambda b, p, ld: (b, p)),
                   pl.BlockSpec((None, 2, LANES, LANES), lambda b, p, ld: (b, 0, p, 0))],
        scratch_shapes=[pltpu.VMEM((n, 2 * LANES), F32)],
    )
    return pl.pallas_call(
        functools.partial(_ret_kernel, n=n),
        grid_spec=grid_spec,
        out_shape=[jax.ShapeDtypeStruct((batch * n, RET_HEADS * RET_DV), BF16),
                   jax.ShapeDtypeStruct((batch, 2, RET_HEADS * RET_DK, RET_DV), F32)],
        compiler_params=_cparams(("parallel", "parallel")),
        name="retention_mixer",
    )(log_decay.astype(F32), z, z, z, z, cos_t, sin_t, ret_gain.reshape(1, -1), s0)


GLA_CHUNK = 64
GLA_SUB = 16
GLA_SAFE_EXPONENT = 60.0


def _gla_kernel(q_ref, k_ref, v_ref, g_ref, misc_ref, gw_ref, gb_ref, gain_ref, s0_ref, o_ref, sf_ref,
                la_ref, acc_ref, *, n):
    C, SB = GLA_CHUNK, GLA_SUB
    nsb = C // SB
    nc = n // C
    lane = lax.broadcasted_iota(jnp.int32, (1, LANES), 1)
    lo_lanes = lane < GLA_DK
    head_mask = (lo_lanes, ~lo_lanes)
    ri = lax.broadcasted_iota(jnp.int32, (C, C), 0)
    ci = lax.broadcasted_iota(jnp.int32, (C, C), 1)
    rowc = lax.broadcasted_iota(jnp.int32, (C, 1), 0)
    row8 = lax.broadcasted_iota(jnp.int32, (8, 1), 0)
    tri = (jnp.where(ci <= ri, 1.0, 0.0).astype(BF16), jnp.where(ci >= ri, 1.0, 0.0).astype(BF16))
    causal = (ci <= ri, ci >= ri)

    def prep(i, carry):
        rows = pl.ds(pl.multiple_of(i * FIN_ROWS, FIN_ROWS), FIN_ROWS)
        slab = misc_ref[rows, :].astype(BF16)
        for dr in range(2):
            pre = _dot(slab, gw_ref[dr]) + gb_ref[dr]
            la_ref[dr, rows, :] = (jnp.minimum(pre, 0.0) - jnp.log1p(jnp.exp(-jnp.abs(pre)))) * (1.0 / GATE_TAU)
        acc_ref[rows, :] = jnp.zeros((FIN_ROWS, 2 * LANES), F32)
        return carry

    lax.fori_loop(0, n // FIN_ROWS, prep, 0)

    def step(c, dr, st):
        rev = dr == 1
        base = pl.multiple_of(c * C, C)
        rows = pl.ds(base, C)
        la = la_ref[dr, rows, :]
        hi = la.astype(BF16)
        lo = (la - hi.astype(F32)).astype(BF16)
        b = _dot(tri[dr], hi) + _dot(tri[dr], lo)
        zero = jnp.zeros((1, LANES), F32)
        if not rev:
            bref = [zero if i == 0 else b[SB * i - 1:SB * i, :] for i in range(nsb)]
            bend = [b[SB * i + SB - 1:SB * i + SB, :] for i in range(nsb)]
            blast = b[C - 1:C, :]
        else:
            bref = [zero if i == nsb - 1 else b[SB * (i + 1):SB * (i + 1) + 1, :] for i in range(nsb)]
            bend = [b[SB * i:SB * i + 1, :] for i in range(nsb)]
            blast = b[0:1, :]
        spread = bref[0] - bend[0]
        for i in range(1, nsb):
            spread = jnp.maximum(spread, bref[i] - bend[i])
        worst = jnp.max(spread)

        def fast(st):
            q = q_ref[rows, :] * (GLA_DK ** -0.5)
            k = k_ref[rows, :]
            a_rows = ([], [])
            for i in range(nsb):
                qe = q[SB * i:SB * (i + 1), :] * jnp.exp(b[SB * i:SB * (i + 1), :] - bref[i])
                valid = (rowc < SB * (i + 1)) if not rev else (rowc >= SB * i)
                ke = jnp.where(valid, k * jnp.exp(jnp.where(valid, bref[i] - b, 0.0)), 0.0).astype(BF16)
                for h in range(2):
                    a_rows[h].append(_dot_nt(jnp.where(head_mask[h], qe, 0.0).astype(BF16), ke))
            qi = q * jnp.exp(b)
            ks = (k * jnp.exp(blast - b)).astype(BF16)
            stb = st.astype(BF16)
            upd = []
            for h in range(2):
                a = jnp.where(causal[dr], jnp.concatenate(a_rows[h], axis=0), 0.0)
                vh = v_ref[rows, h * LANES:(h + 1) * LANES].astype(BF16)
                o = _dot(a.astype(BF16), vh) + _dot_nt(jnp.where(head_mask[h], qi, 0.0).astype(BF16), stb)
                acc_ref[rows, h * LANES:(h + 1) * LANES] += o
                upd.append(_dot_tn(vh, ks))
            return jnp.exp(blast) * st + jnp.where(lo_lanes, upd[0], upd[1])

        def slow(st):
            def row_step(r, st):
                t = r if not rev else C - 1 - r
                slab = pl.ds(pl.multiple_of(base + (t // 8) * 8, 8), 8)
                pick = row8 == (t % 8)
                q8 = jnp.where(pick, q_ref[slab, :] * (GLA_DK ** -0.5), 0.0)
                k8 = jnp.where(pick, k_ref[slab, :], 0.0).astype(BF16)
                la_t = jnp.sum(jnp.where(pick, la_ref[dr, slab, :], 0.0), axis=0, keepdims=True)
                upd = []
                for h in range(2):
                    v8 = jnp.where(pick, v_ref[slab, h * LANES:(h + 1) * LANES], 0.0).astype(BF16)
                    upd.append(_dot_tn(v8, k8))
                st = jnp.exp(la_t) * st + jnp.where(lo_lanes, upd[0], upd[1])
                stb = st.astype(BF16)
                for h in range(2):
                    qh = jnp.where(head_mask[h], q8, 0.0).astype(BF16)
                    acc_ref[slab, h * LANES:(h + 1) * LANES] += _dot_nt(qh, stb)
                return st

            return lax.fori_loop(0, C + 0 * pl.program_id(0), row_step, st)

        return lax.cond(worst <= GLA_SAFE_EXPONENT, fast, slow, st)

    def body(i, carry):
        sf, sb = carry
        return step(i, 0, sf), step(nc - 1 - i, 1, sb)

    sf, sb = lax.fori_loop(0, nc, body, (s0_ref[0].T, s0_ref[1].T))
    sf_ref[0] = sf.T
    sf_ref[1] = sb.T

    def fin(i, carry):
        rows = pl.ds(pl.multiple_of(i * FIN_ROWS, FIN_ROWS), FIN_ROWS)
        o_ref[rows, :] = _head_norm_gate(acc_ref[rows, :], g_ref[rows, :], gain_ref[...]).astype(o_ref.dtype)
        return carry

    lax.fori_loop(0, n // FIN_ROWS, fin, 0)


def _prep_gate(gate_w, gate_b):
    gw = jnp.zeros((2, LANES, gate_w.shape[-1]), F32)
    for dr in range(2):
        lo = MLA_ROPE + dr * GATE_RANK
        gw = gw.at[dr, lo:lo + GATE_RANK, :].set(gate_w[dr].astype(F32))
    return gw.astype(BF16), gate_b.astype(F32).reshape(2, 1, -1)


def gla_mixer(z, gate_w, gate_b, gla_gain, s0, row0, batch, n):
    hp = GLA_HEADS // 2
    r0 = row0 // n
    gw, gb = _prep_gate(gate_w, gate_b)
    cq, ck, cv, cg, cm = AB_GQ // LANES, AB_GK // LANES, AB_GV // (2 * LANES), AB_GR // (2 * LANES), AB_MISC // LANES
    return pl.pallas_call(
        functools.partial(_gla_kernel, n=n),
        grid=(batch, hp),
        in_specs=[pl.BlockSpec((n, LANES), lambda b, p: (r0 + b, cq + p)),
                  pl.BlockSpec((n, LANES), lambda b, p: (r0 + b, ck + p)),
                  pl.BlockSpec((n, 2 * LANES), lambda b, p: (r0 + b, cv + p)),
                  pl.BlockSpec((n, 2 * LANES), lambda b, p: (r0 + b, cg + p)),
                  pl.BlockSpec((n, LANES), lambda b, p: (r0 + b, cm)),
                  pl.BlockSpec((2, LANES, LANES), lambda b, p: (0, 0, p)),
                  pl.BlockSpec((2, 1, LANES), lambda b, p: (0, 0, p)),
                  pl.BlockSpec((1, 2 * LANES), lambda b, p: (0, p)),
                  pl.BlockSpec((None, 2, LANES, LANES), lambda b, p: (b, 0, p, 0))],
        out_specs=[pl.BlockSpec((n, 2 * LANES), lambda b, p: (b, p)),
                   pl.BlockSpec((None, 2, LANES, LANES), lambda b, p: (b, 0, p, 0))],
        out_shape=[jax.ShapeDtypeStruct((batch * n, GLA_HEADS * GLA_DV), BF16),
                   jax.ShapeDtypeStruct((batch, 2, GLA_HEADS * GLA_DK, GLA_DV), F32)],
        scratch_shapes=[pltpu.VMEM((2, n, LANES), F32), pltpu.VMEM((n, 2 * LANES), F32)],
        compiler_params=_cparams(("parallel", "parallel")),
        name="gla_mixer",
    )(z, z, z, z, z, gw, gb, gla_gain.reshape(1, -1), s0)


NA_QROWS = 8
NA_WROWS = 16
NA_NEG = -1e30
NA_TAB = 32


def _na_bias_table(na_bias_l):
    qc = np.arange(GRID_W)[:, None]
    kc = np.arange(GRID_W)[None, :]
    ws = np.clip(qc - NA_COLS // 2, 0, GRID_W - NA_COLS)
    col_ok = (kc >= ws) & (kc < ws + NA_COLS)
    col_off = np.clip(kc - qc + NA_COLS - 1, 0, 2 * NA_COLS - 2)
    aa = (np.arange(NA_TAB) - NA_QROWS)[:, None] + np.arange(2)[None, :]
    in_tab = (aa >= 0) & (aa <= 2 * NA_ROWS - 2)
    aa_c = np.clip(aa, 0, 2 * NA_ROWS - 2)
    val = na_bias_l.astype(F32)[:, aa_c[:, :, None, None], col_off[None, None]]
    keep = jnp.asarray(in_tab[:, :, None, None] & col_ok[None, None])
    fill = jnp.asarray(np.where(col_ok[None, None], 0.0, NA_NEG).astype(np.float32))
    tab = jnp.where(keep[None], val, fill[None])
    h = na_bias_l.shape[0]
    return tab.transpose(0, 1, 3, 2, 4).reshape(h, NA_TAB, GRID_W, 2 * GRID_W)


def _na_latent_kernel(q_ref, k_ref, v_ref, kc_ref, vc_ref, tab_ref, o_ref, s_ref, *, scale, rows_total):
    g = pl.program_id(2)
    r0 = g * NA_QROWS
    start = jnp.clip(r0 - NA_ROWS // 2, 0, rows_total - NA_WROWS)
    nloc = NA_WROWS * GRID_W
    krows = pl.ds(pl.multiple_of(start * GRID_W, GRID_W), nloc)
    q = (q_ref[...] * scale).astype(BF16)
    kw = k_ref[krows, :].astype(BF16)
    s_ref[:, 0:nloc] = _dot_nt(q, kw)
    s_ref[:, nloc:] = _dot_nt(q, kc_ref[...].astype(BF16))
    lane = lax.broadcasted_iota(jnp.int32, (1, LANES), 1)
    for qr in range(NA_QROWS):
        r = r0 + qr
        rs = jnp.clip(r - NA_ROWS // 2, 0, rows_total - NA_ROWS)
        for kp in range(NA_WROWS // 2):
            k0 = start + 2 * kp
            ok0 = (k0 >= rs) & (k0 < rs + NA_ROWS)
            ok1 = (k0 + 1 >= rs) & (k0 + 1 < rs + NA_ROWS)
            neg = jnp.where(lane < GRID_W, jnp.where(ok0, 0.0, NA_NEG), jnp.where(ok1, 0.0, NA_NEG))
            slot = k0 - r + (NA_ROWS - 1) + NA_QROWS
            s_ref[qr * GRID_W:(qr + 1) * GRID_W, kp * LANES:(kp + 1) * LANES] += tab_ref[slot] + neg
    s = s_ref[...]
    m = jnp.max(s, axis=-1, keepdims=True)
    p = jnp.exp(s - m)
    l = jnp.sum(p, axis=-1, keepdims=True)
    pb = p.astype(BF16)
    o = _dot(pb[:, 0:nloc], v_ref[krows, :].astype(BF16)) + _dot(pb[:, nloc:], vc_ref[...].astype(BF16))
    o_ref[...] = (o / l).astype(o_ref.dtype)


def na_latent_attention(z, k_ctx, v_ctx, na_bias_l, row0, batch, n):
    h = NA_HEADS
    past = k_ctx.shape[2]
    rows_total = n // GRID_W
    tq = NA_QROWS * GRID_W
    qb = n // tq
    r0 = row0 // n
    q0 = row0 // tq
    cq, ck, cv = CD_NQ // LANES, CD_NK // LANES, CD_NV // LANES
    tab = _na_bias_table(na_bias_l)
    return pl.pallas_call(
        functools.partial(_na_latent_kernel, scale=float(NA_HD ** -0.5), rows_total=rows_total),
        grid=(batch, h, qb),
        in_specs=[pl.BlockSpec((tq, LANES), lambda b, hh, g: (q0 + b * qb + g, cq + hh)),
                  pl.BlockSpec((n, LANES), lambda b, hh, g: (r0 + b, ck + hh)),
                  pl.BlockSpec((n, LANES), lambda b, hh, g: (r0 + b, cv + hh)),
                  pl.BlockSpec((None, None, past, NA_HD), lambda b, hh, g: (b, hh, 0, 0)),
                  pl.BlockSpec((None, None, past, NA_HD), lambda b, hh, g: (b, hh, 0, 0)),
                  pl.BlockSpec((None, NA_TAB, GRID_W, 2 * GRID_W), lambda b, hh, g: (hh, 0, 0, 0))],
        out_specs=pl.BlockSpec((tq, LANES), lambda b, hh, g: (b * qb + g, hh)),
        out_shape=jax.ShapeDtypeStruct((batch * n, h * NA_HD), BF16),
        scratch_shapes=[pltpu.VMEM((tq, NA_WROWS * GRID_W + past), F32)],
        compiler_params=_cparams(("parallel", "parallel", "arbitrary")),
        name="na_latent_attention",
    )(z, z, z, k_ctx, v_ctx, tab)


CHUNK = 64


def _rmsnorm(x, g=None):
    y = x * lax.rsqrt(jnp.mean(x * x, axis=-1, keepdims=True) + EPS)
    return y if g is None else y * g


def _to_heads(x, nh):
    b, n, _ = x.shape
    return x.reshape(b, n, nh, -1).transpose(0, 2, 1, 3)


def _from_heads(x):
    b, h, n, d = x.shape
    return x.transpose(0, 2, 1, 3).reshape(b, n, h * d)


def _to_chunks(x):
    b, h, n, d = x.shape
    return x.reshape(b, h, n // CHUNK, CHUNK, d).transpose(2, 0, 1, 3, 4)


def _from_chunks(x):
    nc, b, h, l, d = x.shape
    return x.transpose(1, 2, 0, 3, 4).reshape(b, h, nc * l, d)


def _flip(x):
    return jnp.flip(x, axis=2)


def _axial_rope(x):
    n, d = x.shape[-2], x.shape[-1]
    quarter = d // 4
    t = jnp.arange(n)
    row = (t // GRID_W).astype(F32)
    col = (t % GRID_W).astype(F32)
    inv_freq = ROPE_BASE ** (-jnp.arange(quarter, dtype=F32) / quarter)

    def rot(xh, pos):
        ang = pos[:, None] * inv_freq[None, :]
        cos, sin = jnp.cos(ang), jnp.sin(ang)
        x1, x2 = xh[..., :quarter], xh[..., quarter:]
        return jnp.concatenate([x1 * cos - x2 * sin, x1 * sin + x2 * cos], axis=-1)

    return jnp.concatenate([rot(x[..., :d // 2], row), rot(x[..., d // 2:], col)], axis=-1)


def _gla_scan(q, k, v, log_a, s0):
    causal = jnp.tril(jnp.ones((CHUNK, CHUNK), dtype=bool))[:, :, None]

    def step(s, inp):
        qc, kc, vc, ac = inp
        bcum = jnp.cumsum(ac, axis=-2)
        o_inter = jnp.einsum('bhld,bhdv->bhlv', qc * jnp.exp(bcum), s)
        diff = bcum[:, :, :, None, :] - bcum[:, :, None, :, :]
        decay = jnp.where(causal, jnp.exp(jnp.where(causal, diff, 0.0)), 0.0)
        scores = jnp.einsum('bhtd,bhsd,bhtsd->bhts', qc, kc, decay)
        o = o_inter + jnp.einsum('bhts,bhsv->bhtv', scores, vc)
        b_last = bcum[:, :, -1:, :]
        s_new = jnp.exp(b_last[:, :, 0, :, None]) * s + jnp.einsum('bhsd,bhsv->bhdv', kc * jnp.exp(b_last - bcum), vc)
        return s_new, o

    xs = tuple(_to_chunks(t) for t in (q, k, v, log_a))
    s_fin, o = lax.scan(step, s0, xs)
    return _from_chunks(o), s_fin


def _retention_scan(q, k, v, log_gamma, s0):
    lg = log_gamma.astype(F32)[:, None]
    pos = jnp.arange(CHUNK, dtype=F32)
    dist = pos[:, None] - pos[None, :]
    causal = (dist >= 0)[None]
    intra = jnp.where(causal, jnp.exp(jnp.where(causal, dist[None], 0.0) * lg[:, :, None]), 0.0)
    inter = jnp.exp((pos[None, :] + 1.0) * lg)
    to_state = jnp.exp((CHUNK - 1.0 - pos[None, :]) * lg)
    chunk_decay = jnp.exp(CHUNK * lg)

    def step(s, inp):
        qc, kc, vc = inp
        o = (jnp.einsum('bhld,bhdv->bhlv', qc * inter[None, :, :, None], s)
             + jnp.einsum('bhts,bhsv->bhtv', jnp.einsum('bhtd,bhsd->bhts', qc, kc) * intra[None], vc))
        s_new = chunk_decay[None, :, :, None] * s + jnp.einsum('bhsd,bhsv->bhdv', kc * to_state[None, :, :, None], vc)
        return s_new, o

    xs = tuple(_to_chunks(t) for t in (q, k, v))
    s_fin, o = lax.scan(step, s0, xs)
    return _from_chunks(o), s_fin


def _na_latent(q, k, v, k_ctx, v_ctx, bias_tab, scale):
    b, h, n, d = q.shape
    rows = n // GRID_W
    kr = min(NA_ROWS, rows)
    ncb = GRID_W // NA_COLS
    band = 2 * NA_COLS
    band_start = np.clip(np.arange(ncb) * NA_COLS - NA_COLS // 2, 0, GRID_W - band)
    key_col = band_start[:, None] + np.arange(band)[None, :]
    q_col = np.arange(ncb)[:, None] * NA_COLS + np.arange(NA_COLS)[None, :]
    win_start = np.clip(q_col - NA_COLS // 2, 0, GRID_W - NA_COLS)
    kc3 = key_col[:, None, :]
    col_valid = jnp.asarray((kc3 >= win_start[..., None]) & (kc3 < win_start[..., None] + NA_COLS))
    col_off = jnp.asarray(np.clip(kc3 - q_col[..., None] + NA_COLS - 1, 0, 2 * NA_COLS - 2))
    qg = q.reshape(b, h, rows, GRID_W, d)
    kg = k.reshape(b, h, rows, GRID_W, d)[:, :, :, key_col]
    vg = v.reshape(b, h, rows, GRID_W, d)[:, :, :, key_col]

    def one_row(r):
        rs = jnp.clip(r - kr // 2, 0, rows - kr)
        kb = lax.dynamic_slice_in_dim(kg, rs, kr, axis=2)
        vb = lax.dynamic_slice_in_dim(vg, rs, kr, axis=2)
        qr = lax.dynamic_index_in_dim(qg, r, axis=2, keepdims=False).reshape(b, h, ncb, NA_COLS, d)
        s_loc = jnp.einsum('bhjqd,bhrjkd->bhjqrk', qr, kb).astype(F32) * scale
        row_off = rs + jnp.arange(kr) - r + NA_ROWS - 1
        bias = bias_tab[:, row_off[:, None, None, None], col_off[None]].astype(F32).transpose(0, 2, 3, 1, 4)
        s_loc = jnp.where(col_valid[:, :, None, :], s_loc + bias, -jnp.inf)
        s_ctx = jnp.einsum('bhjqd,bhmd->bhjqm', qr, k_ctx).astype(F32) * scale
        s_all = jnp.concatenate([s_loc.reshape(b, h, ncb, NA_COLS, kr * band), s_ctx], axis=-1)
        p = jax.nn.softmax(s_all, axis=-1).astype(v.dtype)
        p_loc = p[..., :kr * band].reshape(b, h, ncb, NA_COLS, kr, band)
        p_ctx = p[..., kr * band:]
        o = jnp.einsum('bhjqrk,bhrjkd->bhjqd', p_loc, vb) + jnp.einsum('bhjqm,bhmd->bhjqd', p_ctx, v_ctx)
        return o.reshape(b, h, GRID_W, d)

    o = lax.map(one_row, jnp.arange(rows))
    return o.transpose(1, 2, 0, 3, 4).reshape(b, h, n, d)


def _gla_mixer_jnp(zb, gate_w, gate_b, gla_gain, s0):
    b, n, _ = zb.shape
    gq, gk = zb[..., AB_GQ:AB_GK], zb[..., AB_GK:AB_GV]
    gv, gr = zb[..., AB_GV:AB_GR], zb[..., AB_GR:AB_CQ]
    glr = zb[..., AB_MISC + MLA_ROPE: AB_MISC + MLA_ROPE + 2 * GATE_RANK]
    q = _to_heads(gq, GLA_HEADS) * GLA_DK ** -0.5
    k = _to_heads(gk, GLA_HEADS)
    v = _to_heads(gv, GLA_HEADS)
    gate_pre = jnp.einsum('bnzr,zre->zbne', glr.reshape(b, n, 2, GATE_RANK), gate_w) + gate_b[:, None, None, :]
    log_a = (jax.nn.log_sigmoid(gate_pre) / GATE_TAU).reshape(2, b, n, GLA_HEADS, GLA_DK).transpose(0, 1, 3, 2, 4)
    o_f, s_f = _gla_scan(q, k, v, log_a[0], s0[:, 0])
    o_b, s_b = _gla_scan(_flip(q), _flip(k), _flip(v), _flip(log_a[1]), s0[:, 1])
    o = _from_heads(_rmsnorm(o_f + _flip(o_b))) * gla_gain * jax.nn.silu(gr)
    return o, jnp.stack([s_f, s_b], axis=1)


def _ret_mixer_jnp(zb, log_decay, ret_gain, s0, latent):
    q = _to_heads(zb[..., CD_RQ:CD_RK], RET_HEADS) * RET_DK ** -0.5
    k = _to_heads(zb[..., CD_RK:CD_RV], RET_HEADS)
    v = _to_heads(zb[..., CD_RV:CD_RG], RET_HEADS)
    rg = zb[..., CD_RG:CD_NQ]
    if latent:
        q, k = _axial_rope(q), _axial_rope(k)
    o_f, s_f = _retention_scan(q, k, v, log_decay[0], s0[:, 0])
    o_b, s_b = _retention_scan(_flip(q), _flip(k), _flip(v), log_decay[1], s0[:, 1])
    o = _from_heads(_rmsnorm(o_f + _flip(o_b))) * ret_gain * jax.nn.silu(rg)
    return o, jnp.stack([s_f, s_b], axis=1)


def _prep_w_in_ab(w):
    d = w.shape[0]
    sizes = (512, 512, 1024, 1024, 2 * GATE_RANK, MLA_Q_LORA, MLA_KV_LORA, MLA_ROPE)
    gq, gk, gv, gr, glr, cq, ckv, kpe = jnp.split(w, np.cumsum(sizes)[:-1].tolist(), axis=1)
    pad = jnp.zeros((d, AB_WIDTH - AB_MISC - MLA_ROPE - 2 * GATE_RANK), w.dtype)
    return jnp.concatenate([gq, gk, gv, gr, cq, ckv, kpe, glr, pad], axis=1).astype(BF16)


def _prep_w_uq(w):
    r = w.shape[0]
    w3 = w.reshape(r, MLA_HEADS, MLA_NOPE + MLA_ROPE)
    nope = w3[:, :, :MLA_NOPE].reshape(r, MLA_HEADS * MLA_NOPE)
    pe = jnp.pad(w3[:, :, MLA_NOPE:], ((0, 0), (0, 0), (0, LANES - MLA_ROPE))).reshape(r, MLA_HEADS * LANES)
    return jnp.concatenate([nope, pe], axis=1).astype(BF16)


def _prep_w_ukv(w):
    r = w.shape[0]
    w3 = w.reshape(r, MLA_HEADS, MLA_NOPE + MLA_DV)
    return jnp.concatenate([w3[:, :, :MLA_NOPE].reshape(r, -1), w3[:, :, MLA_NOPE:].reshape(r, -1)], axis=1).astype(BF16)


def _prep_router(router_w):
    d = router_w.shape[0]
    w = jnp.zeros((d, LANES), F32).at[:, :N_EXPERTS].set(router_w.astype(F32))
    hi = w.astype(BF16)
    lo = (w - hi.astype(F32)).astype(BF16)
    return jnp.stack([hi, lo], axis=0)


def kernel(x_prompt, x_sample, state_gla, cache_mla_ckv, cache_mla_kpe, state_ret, cache_na_k, cache_na_v,
           c, c_ctx, ada_w, ada_b, norm1, norm2, w_in_ab, gla_gate_w, gla_gate_b, gla_gain, mla_q_gain,
           mla_w_uq, mla_kv_gain, mla_w_ukv, w_in_cd, ret_log_decay, ret_gain, na_bias, w_out,
           router_w, router_bias, exp_w1, exp_w3, exp_w2, final_norm):
    bp, n_p, d = x_prompt.shape
    bs, n_s, _ = x_sample.shape
    past = cache_mla_ckv.shape[2]
    depth = ada_w.shape[0]
    tp, ts = bp * n_p, bs * n_s
    y = jnp.concatenate([x_prompt.reshape(tp, d), x_sample.reshape(ts, d)], axis=0)

    cond8 = jnp.zeros((8, d), F32).at[0].set(c_ctx).at[1:1 + bs].set(c)
    mod = adaln_all(cond8, ada_w, ada_b)
    rw_split = _prep_router(router_w)
    w_out_bf = w_out.astype(BF16)
    cos_t, sin_t = rope_tables(n_s, 512)
    cos_s, sin_s = cos_t[512:], sin_t[512:]
    cos_id, sin_id = cos_t[:n_p], sin_t[:n_p]

    outs = {}
    for l in range(depth):
        i = l // 2
        mod_l = mod[l].reshape(8, 1, 6 * d)
        final = l == depth - 1
        if l % 2 == 0:
            z = win_project(y, mod_l, norm1[l], _prep_w_in_ab(w_in_ab[i]), tp, n_s)
            s0p = jnp.zeros((bp, 2, GLA_HEADS * GLA_DK, GLA_DV), F32)
            s0s = state_gla[:, i].reshape(bs, 2, GLA_HEADS * GLA_DK, GLA_DV)
            og_p, st_p = gla_mixer(z, gla_gate_w[i], gla_gate_b[i], gla_gain[i], s0p, 0, bp, n_p)
            og_s, _ = gla_mixer(z, gla_gate_w[i], gla_gate_b[i], gla_gain[i], s0s, tp, bs, n_s)
            st_p = st_p.reshape(bp, 2, GLA_HEADS, GLA_DK, GLA_DV)
            o_a = jnp.concatenate([og_p, og_s], axis=0)
            wq, wkv = _prep_w_uq(mla_w_uq[i]), _prep_w_ukv(mla_w_ukv[i])
            q_m, kv, ckvn, kpe = mla_project(z, mla_q_gain[i], mla_kv_gain[i], wq, wkv, cos_t, sin_t, tp, n_s)
            kv_ctx = ctx_kv_project(cache_mla_ckv[:, i].reshape(bs * past, -1), wkv)
            kpe_ctx = jnp.pad(cache_mla_kpe[:, i].reshape(bs * past, -1), ((0, 0), (0, LANES - MLA_ROPE)))
            om_p = mla_prompt_attention(q_m, kv, kpe, bp, n_p)
            om_s = mla_sample_attention(q_m, kv, kpe, kv_ctx, kpe_ctx, tp, bs, n_s, past)
            o_b = jnp.concatenate([om_p, om_s], axis=0)
            outs.setdefault('gla', []).append(st_p)
            outs.setdefault('ckv', []).append(ckvn[:tp].reshape(bp, n_p, -1))
            outs.setdefault('kpe', []).append(z[:tp, AB_MISC:AB_MISC + MLA_ROPE].reshape(bp, n_p, -1))
        else:
            z = win_project(y, mod_l, norm1[l], w_in_cd[i].astype(BF16), tp, n_s)
            zp = z[:tp].reshape(bp, n_p, -1)
            zs = z[tp:].reshape(bs, n_s, -1)
            s0p = jnp.zeros((bp, 2, RET_HEADS * RET_DK, RET_DV), F32)
            s0s = state_ret[:, i].reshape(bs, 2, RET_HEADS * RET_DK, RET_DV)
            or_p, st_p = retention_mixer(z, ret_log_decay[i], ret_gain[i], s0p, cos_id, sin_id, 0, bp, n_p)
            or_s, _ = retention_mixer(z, ret_log_decay[i], ret_gain[i], s0s, cos_s, sin_s, tp, bs, n_s)
            st_p = st_p.reshape(bp, 2, RET_HEADS, RET_DK, RET_DV)
            o_a = jnp.concatenate([or_p, or_s], axis=0)
            on_p, kn, vn = na_prompt_attention(z, bp, n_p)
            on_s = na_latent_attention(z, cache_na_k[:, i], cache_na_v[:, i], na_bias[i], tp, bs, n_s)
            o_b = jnp.concatenate([on_p, on_s], axis=0)
            outs.setdefault('ret', []).append(st_p)
            outs.setdefault('nak', []).append(kn)
            outs.setdefault('nav', []).append(vn)
        y1, h2, logits = wout_project(o_a, o_b, y, mod_l, w_out_bf[l], norm2[l], rw_split, tp, n_s)
        y = moe_layer(y1, h2, logits, mod_l, router_bias, exp_w1[l].astype(BF16), exp_w3[l].astype(BF16),
                      exp_w2[l].astype(BF16), final_norm, tp, n_s, final)

    return (y[:tp].reshape(bp, n_p, d), y[tp:].reshape(bs, n_s, d),
            jnp.stack(outs['gla'], axis=1), jnp.stack(outs['ckv'], axis=1), jnp.stack(outs['kpe'], axis=1),
            jnp.stack(outs['ret'], axis=1), jnp.stack(outs['nak'], axis=1), jnp.stack(outs['nav'], axis=1))
```

```python
import functools

import numpy as np
import jax
import jax.numpy as jnp
from jax import lax
from jax.experimental import pallas as pl
from jax.experimental.pallas import tpu as pltpu

F32 = jnp.float32
BF16 = jnp.bfloat16

EPS = 1e-6
GRID_W = 64
ROPE_BASE = 10000.0
GATE_RANK = 16
GATE_TAU = 16.0
GLA_HEADS, GLA_DK, GLA_DV = 8, 64, 128
MLA_HEADS, MLA_Q_LORA, MLA_KV_LORA, MLA_NOPE, MLA_ROPE, MLA_DV = 8, 512, 256, 128, 64, 128
RET_HEADS, RET_DK, RET_DV = 8, 64, 128
NA_HEADS, NA_HD, NA_ROWS, NA_COLS = 8, 128, 8, 16
N_EXPERTS, N_GROUPS, TOP_K, D_EXPERT = 32, 8, 2, 512
EXPERTS_PER_GROUP = N_EXPERTS // N_GROUPS
MOE_BLOCK = 128
LANES = 128

AB_GQ, AB_GK, AB_GV, AB_GR, AB_CQ, AB_CKV, AB_MISC, AB_WIDTH = 0, 512, 1024, 2048, 3072, 3584, 3840, 4096
CD_RQ, CD_RK, CD_RV, CD_RG, CD_NQ, CD_NK, CD_NV, CD_WIDTH = 0, 512, 1024, 2048, 3072, 4096, 5120, 6144

VMEM_LIMIT = 56 * 1024 * 1024


def _cparams(sem):
    return pltpu.CompilerParams(dimension_semantics=sem, vmem_limit_bytes=VMEM_LIMIT)


def _dot(a, b):
    return jnp.dot(a, b, preferred_element_type=F32)


def _dot_nt(a, b):
    return lax.dot_general(a, b, (((1,), (1,)), ((), ())), preferred_element_type=F32)


def _adaln_kernel(c_ref, w_ref, b_ref, o_ref):
    c = c_ref[...]
    s = (c * jax.nn.sigmoid(c)).astype(BF16)
    o_ref[...] = _dot(s, w_ref[...].astype(BF16)) + b_ref[...]


def adaln_all(cond8, ada_w, ada_b, tn=512):
    depth, d, n6 = ada_w.shape
    return pl.pallas_call(
        _adaln_kernel,
        grid=(depth, n6 // tn),
        in_specs=[pl.BlockSpec((8, d), lambda l, j: (0, 0)),
                  pl.BlockSpec((None, d, tn), lambda l, j: (l, 0, j)),
                  pl.BlockSpec((None, 1, tn), lambda l, j: (l, 0, j))],
        out_specs=pl.BlockSpec((None, 8, tn), lambda l, j: (l, 0, j)),
        out_shape=jax.ShapeDtypeStruct((depth, 8, n6), F32),
        compiler_params=_cparams(("parallel", "parallel")),
        name="adaln",
    )(cond8, ada_w, ada_b.reshape(depth, 1, n6))


def _mod_index(i, tm, n_prompt_rows, sample_rows):
    tp = n_prompt_rows // tm
    return jnp.where(i < tp, 0, 1 + (i - tp) // (sample_rows // tm))


def _split_specs(block, tiles_prompt, n_grid_axes=1):
    if n_grid_axes == 1:
        return [pl.BlockSpec(block, lambda i: (jnp.minimum(i, tiles_prompt - 1), 0)),
                pl.BlockSpec(block, lambda i: (jnp.maximum(i - tiles_prompt, 0), 0))]
    return [pl.BlockSpec(block, lambda i, j: (jnp.minimum(i, tiles_prompt - 1), 0)),
            pl.BlockSpec(block, lambda i, j: (jnp.maximum(i - tiles_prompt, 0), 0))]


def _win_kernel(*refs, d, tiles_prompt, split):
    if split:
        xp_ref, xs_ref, mod_ref, g_ref, w_ref, o_ref, h_ref = refs
    else:
        xp_ref, mod_ref, g_ref, w_ref, o_ref, h_ref = refs

    def norm_mod(x_ref):
        x = x_ref[...]
        xn = x * lax.rsqrt(jnp.mean(x * x, axis=-1, keepdims=True) + EPS) * g_ref[...]
        h = xn * (1.0 + mod_ref[:, d:2 * d]) + mod_ref[:, 0:d]
        h_ref[...] = h.astype(BF16)

    first = pl.program_id(1) == 0
    if split:
        is_prompt = pl.program_id(0) < tiles_prompt
        pl.when(first & is_prompt)(lambda: norm_mod(xp_ref))
        pl.when(first & jnp.logical_not(is_prompt))(lambda: norm_mod(xs_ref))
    else:
        pl.when(first)(lambda: norm_mod(xp_ref))

    o_ref[...] = _dot(h_ref[...], w_ref[...])


def win_project(xs, mod_l, gain, w_bf16, n_prompt_rows, sample_rows, tm=1024, tn=512):
    split = len(xs) == 2
    d = xs[0].shape[1]
    t = sum(x.shape[0] for x in xs)
    width = w_bf16.shape[1]
    tiles_prompt = n_prompt_rows // tm
    midx = functools.partial(_mod_index, tm=tm, n_prompt_rows=n_prompt_rows, sample_rows=sample_rows)
    x_specs = (_split_specs((tm, d), tiles_prompt, 2) if split else [pl.BlockSpec((tm, d), lambda i, j: (i, 0))])
    return pl.pallas_call(
        functools.partial(_win_kernel, d=d, tiles_prompt=tiles_prompt, split=split),
        grid=(t // tm, width // tn),
        in_specs=x_specs + [pl.BlockSpec((None, 1, 6 * d), lambda i, j: (midx(i), 0, 0)),
                            pl.BlockSpec((1, d), lambda i, j: (0, 0)),
                            pl.BlockSpec((d, tn), lambda i, j: (0, j))],
        out_specs=pl.BlockSpec((tm, tn), lambda i, j: (i, j)),
        out_shape=jax.ShapeDtypeStruct((t, width), F32),
        scratch_shapes=[pltpu.VMEM((tm, d), BF16)],
        compiler_params=_cparams(("parallel", "arbitrary")),
        name="win_project",
    )(*xs, mod_l, gain.reshape(1, d), w_bf16)


def _wout_kernel(*refs, d, tiles_prompt, split_y):
    if split_y:
        (oap_ref, oas_ref, obp_ref, obs_ref, yp_ref, ys_ref, mod_ref, wa_ref, wb_ref, g_ref, rw_ref,
         ynew_ref, h2_ref, lg_ref) = refs
    else:
        (oap_ref, oas_ref, obp_ref, obs_ref, yp_ref, mod_ref, wa_ref, wb_ref, g_ref, rw_ref,
         ynew_ref, h2_ref, lg_ref) = refs
        ys_ref = yp_ref
    is_prompt = pl.program_id(0) < tiles_prompt
    oa = jnp.where(is_prompt, oap_ref[...], oas_ref[...])
    ob = jnp.where(is_prompt, obp_ref[...], obs_ref[...])
    y = jnp.where(is_prompt, yp_ref[...], ys_ref[...]) if split_y else yp_ref[...]
    out = _dot(oa, wa_ref[...]) + _dot(ob, wb_ref[...])
    y1 = y + mod_ref[:, 2 * d:3 * d] * out
    ynew_ref[...] = y1
    xn = y1 * lax.rsqrt(jnp.mean(y1 * y1, axis=-1, keepdims=True) + EPS) * g_ref[...]
    h = xn * (1.0 + mod_ref[:, 4 * d:5 * d]) + mod_ref[:, 3 * d:4 * d]
    h2_ref[...] = h
    hh = h.astype(BF16)
    hl = (h - hh.astype(F32)).astype(BF16)
    lg_ref[...] = _dot(hh, rw_ref[0]) + _dot(hl, rw_ref[0]) + _dot(hh, rw_ref[1])


def wout_project(o_a, o_b, ys, mod_l, w_out_bf16, gain2, rw_split, n_prompt_rows, sample_rows, tm=256):
    split_y = len(ys) == 2
    d = ys[0].shape[1]
    t = sum(x.shape[0] for x in ys)
    ka, kb = o_a[0].shape[1], o_b[0].shape[1]
    tiles_prompt = n_prompt_rows // tm
    midx = functools.partial(_mod_index, tm=tm, n_prompt_rows=n_prompt_rows, sample_rows=sample_rows)
    y_specs = _split_specs((tm, d), tiles_prompt) if split_y else [pl.BlockSpec((tm, d), lambda i: (i, 0))]
    return pl.pallas_call(
        functools.partial(_wout_kernel, d=d, tiles_prompt=tiles_prompt, split_y=split_y),
        grid=(t // tm,),
        in_specs=_split_specs((tm, ka), tiles_prompt) + _split_specs((tm, kb), tiles_prompt) + y_specs + [
                  pl.BlockSpec((None, 1, 6 * d), lambda i: (midx(i), 0, 0)),
                  pl.BlockSpec((ka, d), lambda i: (0, 0)),
                  pl.BlockSpec((kb, d), lambda i: (1, 0)),
                  pl.BlockSpec((1, d), lambda i: (0, 0)),
                  pl.BlockSpec((2, d, LANES), lambda i: (0, 0, 0))],
        out_specs=[pl.BlockSpec((tm, d), lambda i: (i, 0)),
                   pl.BlockSpec((tm, d), lambda i: (i, 0)),
                   pl.BlockSpec((tm, LANES), lambda i: (i, 0))],
        out_shape=[jax.ShapeDtypeStruct((t, d), F32),
                   jax.ShapeDtypeStruct((t, d), F32),
                   jax.ShapeDtypeStruct((t, LANES), F32)],
        compiler_params=_cparams(("parallel",)),
        name="wout_project",
    )(*o_a, *o_b, *ys, mod_l, w_out_bf16, w_out_bf16, gain2.reshape(1, d), rw_split)


def _group_shift(x, k, sub):
    fwd = pltpu.roll(x, LANES - k, axis=1)
    back = pltpu.roll(x, EXPERTS_PER_GROUP - k, axis=1)
    return jnp.where(sub + k < EXPERTS_PER_GROUP, fwd, back)


def _group_reduce(x, sub, op):
    r = x
    for k in range(1, EXPERTS_PER_GROUP):
        r = op(r, _group_shift(x, k, sub))
    return r


def _route_kernel(lg_ref, bias_ref, info_ref, cnt_ref, run_ref, *, tm):
    @pl.when(pl.program_id(0) == 0)
    def _():
        run_ref[...] = jnp.zeros_like(run_ref)

    lg = lg_ref[...]
    lane = lax.broadcasted_iota(jnp.int32, lg.shape, 1)
    lane_f = lane.astype(F32)
    sub = (lane & (EXPERTS_PER_GROUP - 1)).astype(F32)
    grp = (lane >> 2).astype(F32)
    aff = jax.nn.sigmoid(lg)
    neg = jnp.float32(-jnp.inf)
    sel = jnp.where(lane < N_EXPERTS, aff + bias_ref[...], neg)
    m1 = _group_reduce(sel, sub, jnp.maximum)
    i1 = _group_reduce(jnp.where(sel == m1, sub, 4.0), sub, jnp.minimum)
    sel2 = jnp.where(sub == i1, neg, sel)
    m2 = _group_reduce(sel2, sub, jnp.maximum)
    i2 = _group_reduce(jnp.where(sel2 == m2, sub, 4.0), sub, jnp.minimum)
    gscore = m1 + m2
    gmax = jnp.max(gscore, axis=1, keepdims=True)
    g = jnp.min(jnp.where(gscore == gmax, grp, 99.0), axis=1, keepdims=True)
    oh1 = (grp == g) & (sub == i1)
    oh2 = (grp == g) & (sub == i2)

    def pick(mask, v):
        return jnp.sum(jnp.where(mask, v, 0.0), axis=1, keepdims=True)

    a1, a2 = pick(oh1, aff), pick(oh2, aff)
    den = a1 + a2
    e1, e2 = pick(oh1, lane_f), pick(oh2, lane_f)
    oh = jnp.where(oh1 | oh2, 1.0, 0.0)
    r_i = lax.broadcasted_iota(jnp.int32, (tm, tm), 0)
    c_i = lax.broadcasted_iota(jnp.int32, (tm, tm), 1)
    tri = jnp.where(c_i < r_i, 1.0, 0.0).astype(BF16)
    cum = _dot(tri, oh.astype(BF16)) + run_ref[0:1, :]
    r1, r2 = pick(oh1, cum), pick(oh2, cum)
    run_new = run_ref[0:1, :] + jnp.sum(oh, axis=0, keepdims=True)
    run_ref[...] = jnp.broadcast_to(run_new, run_ref.shape)
    cnt_ref[...] = jnp.broadcast_to(run_new, cnt_ref.shape)
    info = jnp.zeros_like(lg)
    for k, v in enumerate((e1, e2, a1 / den, a2 / den, r1, r2)):
        info = jnp.where(lane == k, v, info)
    info_ref[...] = info


def route(logits, router_bias, tm=256):
    t = logits.shape[0]
    bias = jnp.zeros((1, LANES), F32).at[0, :N_EXPERTS].set(router_bias.astype(F32))
    return pl.pallas_call(
        functools.partial(_route_kernel, tm=tm),
        grid=(t // tm,),
        in_specs=[pl.BlockSpec((tm, LANES), lambda i: (i, 0)),
                  pl.BlockSpec((1, LANES), lambda i: (0, 0))],
        out_specs=[pl.BlockSpec((tm, LANES), lambda i: (i, 0)),
                   pl.BlockSpec((8, LANES), lambda i: (0, 0))],
        out_shape=[jax.ShapeDtypeStruct((t, LANES), F32),
                   jax.ShapeDtypeStruct((8, LANES), F32)],
        scratch_shapes=[pltpu.VMEM((8, LANES), F32)],
        compiler_params=_cparams(("arbitrary",)),
        name="route",
    )(logits, bias)


def _expert_kernel(be_ref, nu_ref, tok_ref, tok_next_ref, h_hbm, w1_ref, w3_ref, w2_ref, o_ref,
                   xbuf, w1b, w3b, w2b, sem):
    i = pl.program_id(0)
    slot = i % 2
    nu = nu_ref[0]

    def gather(idx_ref, s):
        for j in range(MOE_BLOCK):
            pltpu.make_async_copy(h_hbm.at[pl.ds(idx_ref[j], 1)], xbuf.at[s, pl.ds(j, 1)], sem.at[s]).start()

    def wait_block(s):
        pltpu.make_async_copy(h_hbm.at[pl.ds(0, MOE_BLOCK)], xbuf.at[s], sem.at[s]).wait()

    @pl.when(i == 0)
    def _():
        gather(tok_ref, 0)

    @pl.when(i < nu)
    def _():
        @pl.when((i == 0) | (be_ref[i] != be_ref[jnp.maximum(i - 1, 0)]))
        def _():
            w1b[...] = w1_ref[...].astype(BF16)
            w3b[...] = w3_ref[...].astype(BF16)
            w2b[...] = w2_ref[...].astype(BF16)

        wait_block(slot)
        gather(tok_next_ref, 1 - slot)
        x = xbuf[slot].astype(BF16)
        a = _dot(x, w1b[...])
        b = _dot(x, w3b[...])
        mid = (a * jax.nn.sigmoid(a)) * b
        o_ref[...] = _dot(mid.astype(BF16), w2b[...])

    @pl.when(i >= nu)
    def _():
        @pl.when(i == nu)
        def _():
            wait_block(slot)

        o_ref[...] = jnp.zeros_like(o_ref)


def expert_ffn(h2, slot_tok, block_expert, n_used, w1, w3, w2, layer):
    n_steps = slot_tok.shape[0] // MOE_BLOCK
    d, de = w1.shape[-2], w1.shape[-1]
    grid_spec = pltpu.PrefetchScalarGridSpec(
        num_scalar_prefetch=2,
        grid=(n_steps,),
        in_specs=[pl.BlockSpec((MOE_BLOCK,), lambda i, be, nu: (i,), memory_space=pltpu.SMEM),
                  pl.BlockSpec((MOE_BLOCK,), lambda i, be, nu: (jnp.minimum(i + 1, n_steps - 1),),
                               memory_space=pltpu.SMEM),
                  pl.BlockSpec(memory_space=pl.ANY),
                  pl.BlockSpec((None, None, d, de), lambda i, be, nu: (layer, be[i], 0, 0)),
                  pl.BlockSpec((None, None, d, de), lambda i, be, nu: (layer, be[i], 0, 0)),
                  pl.BlockSpec((None, None, de, d), lambda i, be, nu: (layer, be[i], 0, 0))],
        out_specs=pl.BlockSpec((MOE_BLOCK, d), lambda i, be, nu: (i, 0)),
        scratch_shapes=[pltpu.VMEM((2, MOE_BLOCK, d), F32), pltpu.VMEM((d, de), BF16), pltpu.VMEM((d, de), BF16),
                        pltpu.VMEM((de, d), BF16), pltpu.SemaphoreType.DMA((2,))],
    )
    return pl.pallas_call(
        _expert_kernel,
        grid_spec=grid_spec,
        out_shape=jax.ShapeDtypeStruct((n_steps * MOE_BLOCK, d), F32),
        compiler_params=_cparams(("arbitrary",)),
        name="expert_ffn",
    )(block_expert, n_used, slot_tok, slot_tok, h2, w1, w3, w2)


def _combine_kernel(dcur_ref, dnext_ref, y_ref, mod_ref, info_ref, g_ref, yb_hbm, *rest, tm, nt, d, final,
                    tiles_prompt):
    if final:
        op_ref, os_ref, buf, sem = rest
    else:
        o_ref, buf, sem = rest
    i = pl.program_id(0)
    slot = i % 2

    def row_copy(src_row, s, k, j):
        return pltpu.make_async_copy(yb_hbm.at[pl.ds(src_row, 1)], buf.at[s, k, pl.ds(j, 1)], sem.at[s])

    def issue(dref, s):
        def body(j, carry):
            for k in range(TOP_K):
                row_copy(dref[TOP_K * j + k], s, k, j).start()
            return carry

        lax.fori_loop(0, tm, body, 0, unroll=8)

    @pl.when(i == 0)
    def _():
        issue(dcur_ref, 0)

    @pl.when(i + 1 < nt)
    def _():
        issue(dnext_ref, 1 - slot)

    for k in range(TOP_K):
        pltpu.make_async_copy(yb_hbm.at[pl.ds(0, tm)], buf.at[slot, k], sem.at[slot]).wait()
    info = info_ref[...]
    moe = info[:, 2:3] * buf[slot, 0] + info[:, 3:4] * buf[slot, 1]
    y2 = y_ref[...] + mod_ref[:, 5 * d:6 * d] * moe
    if not final:
        o_ref[...] = y2
        return
    y2 = y2 * lax.rsqrt(jnp.mean(y2 * y2, axis=-1, keepdims=True) + EPS) * g_ref[...]

    @pl.when(i < tiles_prompt)
    def _():
        op_ref[...] = y2

    @pl.when(i >= tiles_prompt)
    def _():
        os_ref[...] = y2


def combine(y1, mod_l, info, dest_flat, yb, final_gain, n_prompt_rows, sample_rows, final, tm=256):
    t, d = y1.shape
    nt = t // tm
    tiles_prompt = n_prompt_rows // tm
    midx = functools.partial(_mod_index, tm=tm, n_prompt_rows=n_prompt_rows, sample_rows=sample_rows)
    if final:
        out_specs = _split_specs((tm, d), tiles_prompt)
        out_shape = [jax.ShapeDtypeStruct((n_prompt_rows, d), F32), jax.ShapeDtypeStruct((t - n_prompt_rows, d), F32)]
    else:
        out_specs = pl.BlockSpec((tm, d), lambda i: (i, 0))
        out_shape = jax.ShapeDtypeStruct((t, d), F32)
    return pl.pallas_call(
        functools.partial(_combine_kernel, tm=tm, nt=nt, d=d, final=final, tiles_prompt=tiles_prompt),
        grid=(nt,),
        in_specs=[pl.BlockSpec((TOP_K * tm,), lambda i: (i,), memory_space=pltpu.SMEM),
                  pl.BlockSpec((TOP_K * tm,), lambda i: (jnp.minimum(i + 1, nt - 1),), memory_space=pltpu.SMEM),
                  pl.BlockSpec((tm, d), lambda i: (i, 0)),
                  pl.BlockSpec((None, 1, 6 * d), lambda i: (midx(i), 0, 0)),
                  pl.BlockSpec((tm, LANES), lambda i: (i, 0)),
                  pl.BlockSpec((1, d), lambda i: (0, 0)),
                  pl.BlockSpec(memory_space=pl.ANY)],
        out_specs=out_specs,
        out_shape=out_shape,
        scratch_shapes=[pltpu.VMEM((2, TOP_K, tm, d), F32), pltpu.SemaphoreType.DMA((2,))],
        compiler_params=_cparams(("arbitrary",)),
        name="combine",
    )(dest_flat, dest_flat, y1, mod_l, info, final_gain.reshape(1, d), yb)


def moe_layer(y1, h2p, logits, mod_l, router_bias, w1, w3, w2, layer, final_gain, n_prompt_rows, sample_rows, final):
    t, d = y1.shape
    info, cnt = route(logits, router_bias)
    expert = info[:, 0:2].astype(jnp.int32)
    rank = info[:, 4:6].astype(jnp.int32)
    counts = cnt[0, :N_EXPERTS].astype(jnp.int32)
    padded = (counts + MOE_BLOCK - 1) // MOE_BLOCK * MOE_BLOCK
    padded_end = jnp.cumsum(padded)
    padded_start = padded_end - padded
    dest = (padded_start[expert] + rank).reshape(-1)
    n_blocks = (t * TOP_K) // MOE_BLOCK + N_EXPERTS + 1
    slots = n_blocks * MOE_BLOCK
    block_lo = jnp.arange(n_blocks, dtype=jnp.int32)[:, None] * MOE_BLOCK
    block_expert = jnp.minimum(jnp.sum((padded_end[None, :] <= block_lo).astype(jnp.int32), axis=1), N_EXPERTS - 1)
    n_used = (padded_end[-1:] // MOE_BLOCK).astype(jnp.int32)
    tok = jnp.broadcast_to(jnp.arange(t, dtype=jnp.int32)[:, None], (t, TOP_K)).reshape(-1)
    slot_tok = jnp.zeros((slots,), jnp.int32).at[dest].set(tok)
    yb = expert_ffn(h2p, slot_tok, block_expert, n_used, w1, w3, w2, layer)
    return combine(y1, mod_l, info, dest, yb, final_gain, n_prompt_rows, sample_rows, final)


def rope_tables(n, lead):
    quarter = MLA_ROPE // 4
    tpos = np.arange(n)
    row = (tpos // GRID_W).astype(np.float32)
    col = (tpos % GRID_W).astype(np.float32)
    inv_freq = (np.float32(ROPE_BASE) ** (-np.arange(quarter, dtype=np.float32) / np.float32(quarter))).astype(np.float32)
    lane = np.arange(64)
    pos = np.where(lane[None, :] < 32, row[:, None], col[:, None]).astype(np.float32)
    ang = pos * inv_freq[lane % quarter][None, :]
    cos = np.cos(ang).astype(np.float32)
    sin = np.sin(ang).astype(np.float32)
    sgn = np.where((lane % 32) < quarter, -1.0, 1.0).astype(np.float32)
    sin = sin * sgn[None, :]
    cos = np.concatenate([cos, cos], axis=1)
    sin = np.concatenate([sin, sin], axis=1)
    cos = np.concatenate([np.ones((lead, LANES), np.float32), cos], axis=0)
    sin = np.concatenate([np.zeros((lead, LANES), np.float32), sin], axis=0)
    return jnp.asarray(cos), jnp.asarray(sin)


def _rope(x, cos, sin_signed):
    lane = lax.broadcasted_iota(jnp.int32, x.shape, 1)
    partner = jnp.where((lane & 31) < 16, pltpu.roll(x, LANES - 16, axis=1), pltpu.roll(x, 16, axis=1))
    return x * cos + partner * sin_signed


def _mla_proj_kernel(cq_ref, ckv_ref, misc_ref, qg_ref, kvg_ref, wq_ref, wkv_ref, cos_ref, sin_ref,
                     q_ref, kv_ref, ckvn_ref, kpe_ref, *, scale):
    cq = cq_ref[...]
    qn = cq * lax.rsqrt(jnp.mean(cq * cq, axis=-1, keepdims=True) + EPS) * qg_ref[...]
    qf = _dot(qn.astype(BF16), wq_ref[...])
    half = qf.shape[1] // 2
    cos, sin = cos_ref[...], sin_ref[...]
    q_ref[:, 0:half] = (qf[:, 0:half] * scale).astype(BF16)
    for h in range(MLA_HEADS):
        pe = qf[:, half + h * LANES: half + (h + 1) * LANES]
        q_ref[:, half + h * LANES: half + (h + 1) * LANES] = (_rope(pe, cos, sin) * scale).astype(BF16)
    ckv = ckv_ref[...]
    ckvn = ckv * lax.rsqrt(jnp.mean(ckv * ckv, axis=-1, keepdims=True) + EPS) * kvg_ref[...]
    ckvn_ref[...] = ckvn
    kv_ref[...] = _dot(ckvn.astype(BF16), wkv_ref[...]).astype(BF16)
    misc = misc_ref[...]
    lane = lax.broadcasted_iota(jnp.int32, misc.shape, 1)
    kpe = jnp.where(lane < MLA_ROPE, misc, 0.0)
    kpe_ref[...] = _rope(kpe, cos, sin).astype(BF16)


def mla_project(z, q_gain, kv_gain, wq, wkv, cos_t, sin_t, n_prompt_rows, sample_rows, tm=512):
    t = z.shape[0]
    tp = n_prompt_rows // tm
    per = sample_rows // tm

    def tab(i):
        return (jnp.where(i < tp, 0, 1 + (i - tp) % per), 0)

    scale = float((MLA_NOPE + MLA_ROPE) ** -0.5)
    nq = wq.shape[1]
    return pl.pallas_call(
        functools.partial(_mla_proj_kernel, scale=scale),
        grid=(t // tm,),
        in_specs=[pl.BlockSpec((tm, MLA_Q_LORA), lambda i: (i, AB_CQ // MLA_Q_LORA)),
                  pl.BlockSpec((tm, MLA_KV_LORA), lambda i: (i, AB_CKV // MLA_KV_LORA)),
                  pl.BlockSpec((tm, LANES), lambda i: (i, AB_MISC // LANES)),
                  pl.BlockSpec((1, MLA_Q_LORA), lambda i: (0, 0)),
                  pl.BlockSpec((1, MLA_KV_LORA), lambda i: (0, 0)),
                  pl.BlockSpec(wq.shape, lambda i: (0, 0)),
                  pl.BlockSpec(wkv.shape, lambda i: (0, 0)),
                  pl.BlockSpec((tm, LANES), tab),
                  pl.BlockSpec((tm, LANES), tab)],
        out_specs=[pl.BlockSpec((tm, nq), lambda i: (i, 0)),
                   pl.BlockSpec((tm, wkv.shape[1]), lambda i: (i, 0)),
                   pl.BlockSpec((tm, MLA_KV_LORA), lambda i: (i, 0)),
                   pl.BlockSpec((tm, LANES), lambda i: (i, 0))],
        out_shape=[jax.ShapeDtypeStruct((t, nq), BF16),
                   jax.ShapeDtypeStruct((t, wkv.shape[1]), BF16),
                   jax.ShapeDtypeStruct((t, MLA_KV_LORA), F32),
                   jax.ShapeDtypeStruct((t, LANES), BF16)],
        compiler_params=_cparams(("parallel",)),
        name="mla_project",
    )(z, z, z, q_gain.reshape(1, -1), kv_gain.reshape(1, -1), wq, wkv, cos_t, sin_t)


def _ctx_kv_kernel(ckv_ref, w_ref, o_ref):
    o_ref[...] = _dot(ckv_ref[...].astype(BF16), w_ref[...]).astype(BF16)


def ctx_kv_project(ckv_c, wkv):
    rows = ckv_c.shape[0]
    return pl.pallas_call(
        _ctx_kv_kernel,
        grid=(1,),
        in_specs=[pl.BlockSpec(ckv_c.shape, lambda i: (0, 0)), pl.BlockSpec(wkv.shape, lambda i: (0, 0))],
        out_specs=pl.BlockSpec((rows, wkv.shape[1]), lambda i: (0, 0)),
        out_shape=jax.ShapeDtypeStruct((rows, wkv.shape[1]), BF16),
        compiler_params=_cparams(("arbitrary",)),
        name="ctx_kv_project",
    )(ckv_c, wkv)


def _softmax_pv(s, v):
    m = jnp.max(s, axis=-1, keepdims=True)
    p = jnp.exp(s - m)
    l = jnp.sum(p, axis=-1, keepdims=True)
    return _dot(p.astype(BF16), v) / l


def _mla_prompt_attn_kernel(qn_ref, qp_ref, kn_ref, kp_ref, v_ref, o_ref):
    q = jnp.concatenate([qn_ref[...], qp_ref[...]], axis=1)
    k = jnp.concatenate([kn_ref[...], kp_ref[...]], axis=1)
    o_ref[...] = _softmax_pv(_dot_nt(q, k), v_ref[...]).astype(o_ref.dtype)


def mla_prompt_attention(q_m, kv, kpe, batch, n):
    h = MLA_HEADS
    return pl.pallas_call(
        _mla_prompt_attn_kernel,
        grid=(batch, h),
        in_specs=[pl.BlockSpec((n, LANES), lambda b, hh: (b, hh)),
                  pl.BlockSpec((n, LANES), lambda b, hh: (b, h + hh)),
                  pl.BlockSpec((n, LANES), lambda b, hh: (b, hh)),
                  pl.BlockSpec((n, LANES), lambda b, hh: (b, 0)),
                  pl.BlockSpec((n, LANES), lambda b, hh: (b, h + hh))],
        out_specs=pl.BlockSpec((n, LANES), lambda b, hh: (b, hh)),
        out_shape=jax.ShapeDtypeStruct((batch * n, h * MLA_DV), BF16),
        compiler_params=_cparams(("parallel", "parallel")),
        name="mla_prompt_attention",
    )(q_m, q_m, kv, kpe, kv)


def _lane_tile_reduce(x, op):
    r = x[:, 0:LANES]
    for c in range(1, x.shape[1] // LANES):
        r = op(r, x[:, c * LANES:(c + 1) * LANES])
    return r


def _mla_sample_attn_kernel(qn_ref, qp_ref, kcn_ref, kcp_ref, vc_ref, kn_ref, kp_ref, v_ref, o_ref,
                            s_ref, m_ref, l_ref, acc_ref, *, tk):
    past = kcn_ref.shape[0]
    nk = kn_ref.shape[0] // tk
    q = jnp.concatenate([qn_ref[...], qp_ref[...]], axis=1)
    kc = jnp.concatenate([kcn_ref[...], kcp_ref[...].astype(BF16)], axis=1)
    s = _dot_nt(q, kc)
    s_ref[:, 0:past] = s
    m_ref[...] = _lane_tile_reduce(s, jnp.maximum)

    def scores(j, carry):
        off = pl.multiple_of(j * tk, tk)
        k = jnp.concatenate([kn_ref[pl.ds(off, tk), :], kp_ref[pl.ds(off, tk), :]], axis=1)
        s = _dot_nt(q, k)
        s_ref[:, pl.ds(pl.multiple_of(past + j * tk, LANES), tk)] = s
        m_ref[...] = jnp.maximum(m_ref[...], _lane_tile_reduce(s, jnp.maximum))
        return carry

    lax.fori_loop(0, nk, scores, 0, unroll=2)
    m_ref[...] = jnp.broadcast_to(jnp.max(m_ref[...], axis=-1, keepdims=True), m_ref.shape)

    def probs(s, v):
        m = m_ref[...]
        parts = [jnp.exp(s[:, c * LANES:(c + 1) * LANES] - m) for c in range(s.shape[1] // LANES)]
        lsum = parts[0]
        for pc in parts[1:]:
            lsum = lsum + pc
        p = jnp.concatenate([pc.astype(BF16) for pc in parts], axis=1)
        return lsum, _dot(p, v)

    l0, a0 = probs(s_ref[:, 0:past], vc_ref[...])
    l_ref[...] = l0
    acc_ref[...] = a0

    def weighted(j, carry):
        off = pl.multiple_of(j * tk, tk)
        lj, aj = probs(s_ref[:, pl.ds(pl.multiple_of(past + j * tk, LANES), tk)], v_ref[pl.ds(off, tk), :])
        l_ref[...] += lj
        acc_ref[...] += aj
        return carry

    lax.fori_loop(0, nk, weighted, 0, unroll=2)
    l = jnp.sum(l_ref[...], axis=-1, keepdims=True)
    o_ref[...] = (acc_ref[...] / l).astype(o_ref.dtype)


def mla_sample_attention(q_m, kv, kpe, kv_ctx, kpe_ctx, row0, batch, n, past, tq=512, tk=512):
    h = MLA_HEADS
    qb = n // tq
    r0 = row0 // n
    q0 = row0 // tq
    return pl.pallas_call(
        functools.partial(_mla_sample_attn_kernel, tk=tk),
        grid=(batch, h, qb),
        in_specs=[pl.BlockSpec((tq, LANES), lambda b, hh, i: (q0 + b * qb + i, hh)),
                  pl.BlockSpec((tq, LANES), lambda b, hh, i: (q0 + b * qb + i, h + hh)),
                  pl.BlockSpec((past, LANES), lambda b, hh, i: (b, hh)),
                  pl.BlockSpec((past, LANES), lambda b, hh, i: (b, 0)),
                  pl.BlockSpec((past, LANES), lambda b, hh, i: (b, h + hh)),
                  pl.BlockSpec((n, LANES), lambda b, hh, i: (r0 + b, hh)),
                  pl.BlockSpec((n, LANES), lambda b, hh, i: (r0 + b, 0)),
                  pl.BlockSpec((n, LANES), lambda b, hh, i: (r0 + b, h + hh))],
        out_specs=pl.BlockSpec((tq, LANES), lambda b, hh, i: (b * qb + i, hh)),
        out_shape=jax.ShapeDtypeStruct((batch * n, h * MLA_DV), BF16),
        scratch_shapes=[pltpu.VMEM((tq, past + n), F32), pltpu.VMEM((tq, LANES), F32), pltpu.VMEM((tq, LANES), F32),
                        pltpu.VMEM((tq, MLA_DV), F32)],
        compiler_params=_cparams(("parallel", "parallel", "arbitrary")),
        name="mla_sample_attention",
    )(q_m, q_m, kv_ctx, kpe_ctx, kv_ctx, kv, kpe, kv)


def _na_prompt_attn_kernel(q_ref, k_ref, v_ref, o_ref, ko_ref, vo_ref, *, scale):
    k = k_ref[...]
    v = v_ref[...]
    ko_ref[...] = k
    vo_ref[...] = v
    q = (q_ref[...] * scale).astype(BF16)
    o_ref[...] = _softmax_pv(_dot_nt(q, k.astype(BF16)), v.astype(BF16)).astype(o_ref.dtype)


def na_prompt_attention(z, batch, n):
    h = NA_HEADS
    scale = float(NA_HD ** -0.5)
    cq, ck, cv = CD_NQ // LANES, CD_NK // LANES, CD_NV // LANES
    return pl.pallas_call(
        functools.partial(_na_prompt_attn_kernel, scale=scale),
        grid=(batch, h),
        in_specs=[pl.BlockSpec((n, LANES), lambda b, hh: (b, cq + hh)),
                  pl.BlockSpec((n, LANES), lambda b, hh: (b, ck + hh)),
                  pl.BlockSpec((n, LANES), lambda b, hh: (b, cv + hh))],
        out_specs=[pl.BlockSpec((n, LANES), lambda b, hh: (b, hh)),
                   pl.BlockSpec((None, None, n, NA_HD), lambda b, hh: (b, hh, 0, 0)),
                   pl.BlockSpec((None, None, n, NA_HD), lambda b, hh: (b, hh, 0, 0))],
        out_shape=[jax.ShapeDtypeStruct((batch * n, h * NA_HD), BF16),
                   jax.ShapeDtypeStruct((batch, h, n, NA_HD), F32),
                   jax.ShapeDtypeStruct((batch, h, n, NA_HD), F32)],
        compiler_params=_cparams(("parallel", "parallel")),
        name="na_prompt_attention",
    )(z, z, z)


def _dot_tn(a, b):
    return lax.dot_general(a, b, (((0,), (0,)), ((), ())), preferred_element_type=F32)


def _head_norm_gate(acc, gate_in, gain):
    outs = []
    for h in range(2):
        a = acc[:, h * LANES:(h + 1) * LANES]
        outs.append(a * lax.rsqrt(jnp.mean(a * a, axis=-1, keepdims=True) + EPS))
    g = gate_in
    return jnp.concatenate(outs, axis=1) * gain * (g * jax.nn.sigmoid(g))


RET_CHUNK = 128
FIN_ROWS = 256


def _ret_kernel(ld_ref, q_ref, k_ref, v_ref, g_ref, cos_ref, sin_ref, gain_ref, s0_ref, o_ref, sf_ref, acc_ref, *, n):
    L = RET_CHUNK
    nc = n // L
    hp = pl.program_id(1)
    lane = lax.broadcasted_iota(jnp.int32, (1, LANES), 1)
    lo_lanes = lane < RET_DK
    row128 = lax.broadcasted_iota(jnp.int32, (LANES, 1), 0)
    ri = lax.broadcasted_iota(jnp.int32, (L, L), 0)
    ci = lax.broadcasted_iota(jnp.int32, (L, L), 1)
    dist = (ri - ci).astype(F32)
    pos = lax.broadcasted_iota(jnp.int32, (L, 1), 0).astype(F32)

    consts = []
    for dr in range(2):
        lg0 = ld_ref[dr, 2 * hp]
        lg1 = ld_ref[dr, 2 * hp + 1]
        lg2 = jnp.where(lo_lanes, lg0, lg1)
        lgc = jnp.where(row128 < RET_DK, lg0, lg1)
        sd = dist if dr == 0 else -dist
        ok = sd >= 0
        intra = [jnp.where(ok, jnp.exp(jnp.where(ok, sd, 0.0) * lg), 0.0) for lg in (lg0, lg1)]
        if dr == 0:
            inter = jnp.exp((pos + 1.0) * lg2)
            to_state = jnp.exp((L - 1.0 - pos) * lg2)
        else:
            inter = jnp.exp((L - pos) * lg2)
            to_state = jnp.exp(pos * lg2)
        consts.append((intra, inter, to_state, jnp.exp(float(L) * lgc)))

    def clear(i, carry):
        acc_ref[pl.ds(pl.multiple_of(i * FIN_ROWS, FIN_ROWS), FIN_ROWS), :] = jnp.zeros((FIN_ROWS, 2 * LANES), F32)
        return carry

    lax.fori_loop(0, n // FIN_ROWS, clear, 0)

    def step(c, dr, s2):
        intra, inter, to_state, cdec = consts[dr]
        rows = pl.ds(pl.multiple_of(c * L, L), L)
        cos, sin = cos_ref[rows, :], sin_ref[rows, :]
        q = _rope(q_ref[rows, :] * (RET_DK ** -0.5), cos, sin)
        k = _rope(k_ref[rows, :], cos, sin)
        kb = k.astype(BF16)
        s2b = s2.astype(BF16)
        kt = (k * to_state).astype(BF16)
        upd = []
        for h in range(2):
            qm = jnp.where(lo_lanes if h == 0 else ~lo_lanes, q, 0.0)
            vh = v_ref[rows, h * LANES:(h + 1) * LANES].astype(BF16)
            sc = _dot_nt(qm.astype(BF16), kb) * intra[h]
            o = _dot(sc.astype(BF16), vh) + _dot((qm * inter).astype(BF16), s2b)
            acc_ref[rows, h * LANES:(h + 1) * LANES] += o
            upd.append(_dot_tn(kt, vh))
        return cdec * s2 + jnp.where(row128 < RET_DK, upd[0], upd[1])

    def body(i, carry):
        sf, sb = carry
        return step(i, 0, sf), step(nc - 1 - i, 1, sb)

    sf, sb = lax.fori_loop(0, nc, body, (s0_ref[0], s0_ref[1]))
    sf_ref[0] = sf
    sf_ref[1] = sb

    def fin(i, carry):
        rows = pl.ds(pl.multiple_of(i * FIN_ROWS, FIN_ROWS), FIN_ROWS)
        o_ref[rows, :] = _head_norm_gate(acc_ref[rows, :], g_ref[rows, :], gain_ref[...]).astype(o_ref.dtype)
        return carry

    lax.fori_loop(0, n // FIN_ROWS, fin, 0)


def retention_mixer(z, log_decay, ret_gain, s0, cos_t, sin_t, row0, batch, n):
    hp = RET_HEADS // 2
    r0 = row0 // n
    cq, ck, cv, cg = CD_RQ // LANES, CD_RK // LANES, CD_RV // (2 * LANES), CD_RG // (2 * LANES)
    grid_spec = pltpu.PrefetchScalarGridSpec(
        num_scalar_prefetch=1,
        grid=(batch, hp),
        in_specs=[pl.BlockSpec((n, LANES), lambda b, p, ld: (r0 + b, cq + p)),
                  pl.BlockSpec((n, LANES), lambda b, p, ld: (r0 + b, ck + p)),
                  pl.BlockSpec((n, 2 * LANES), lambda b, p, ld: (r0 + b, cv + p)),
                  pl.BlockSpec((n, 2 * LANES), lambda b, p, ld: (r0 + b, cg + p)),
                  pl.BlockSpec((n, LANES), lambda b, p, ld: (0, 0)),
                  pl.BlockSpec((n, LANES), lambda b, p, ld: (0, 0)),
                  pl.BlockSpec((1, 2 * LANES), lambda b, p, ld: (0, p)),
                  pl.BlockSpec((None, 2, LANES, LANES), lambda b, p, ld: (b, 0, p, 0))],
        out_specs=[pl.BlockSpec((n, 2 * LANES), lambda b, p, ld: (b, p)),
                   pl.BlockSpec((None, 2, LANES, LANES), lambda b, p, ld: (b, 0, p, 0))],
        scratch_shapes=[pltpu.VMEM((n, 2 * LANES), F32)],
    )
    return pl.pallas_call(
        functools.partial(_ret_kernel, n=n),
        grid_spec=grid_spec,
        out_shape=[jax.ShapeDtypeStruct((batch * n, RET_HEADS * RET_DV), BF16),
                   jax.ShapeDtypeStruct((batch, 2, RET_HEADS * RET_DK, RET_DV), F32)],
        compiler_params=_cparams(("parallel", "parallel")),
        name="retention_mixer",
    )(log_decay.astype(F32), z, z, z, z, cos_t, sin_t, ret_gain.reshape(1, -1), s0)


GLA_CHUNK = 128
GLA_SUB = 32
GLA_SAFE_EXPONENT = 60.0


def _gla_kernel(q_ref, k_ref, v_ref, g_ref, misc_ref, gw_ref, gb_ref, gain_ref, s0_ref, o_ref, sf_ref,
                la_ref, acc_ref, *, n):
    C, SB = GLA_CHUNK, GLA_SUB
    nsb = C // SB
    nc = n // C
    lane = lax.broadcasted_iota(jnp.int32, (1, LANES), 1)
    lo_lanes = lane < GLA_DK
    head_mask = (lo_lanes, ~lo_lanes)
    ri = lax.broadcasted_iota(jnp.int32, (C, C), 0)
    ci = lax.broadcasted_iota(jnp.int32, (C, C), 1)
    rowc = lax.broadcasted_iota(jnp.int32, (C, 1), 0)
    row8 = lax.broadcasted_iota(jnp.int32, (8, 1), 0)
    tri = (jnp.where(ci <= ri, 1.0, 0.0).astype(BF16), jnp.where(ci >= ri, 1.0, 0.0).astype(BF16))
    causal = (ci <= ri, ci >= ri)

    def prep(i, carry):
        rows = pl.ds(pl.multiple_of(i * FIN_ROWS, FIN_ROWS), FIN_ROWS)
        slab = misc_ref[rows, :].astype(BF16)
        for dr in range(2):
            pre = _dot(slab, gw_ref[dr]) + gb_ref[dr]
            la_ref[dr, rows, :] = (jnp.minimum(pre, 0.0) - jnp.log1p(jnp.exp(-jnp.abs(pre)))) * (1.0 / GATE_TAU)
        acc_ref[rows, :] = jnp.zeros((FIN_ROWS, 2 * LANES), F32)
        return carry

    lax.fori_loop(0, n // FIN_ROWS, prep, 0)

    def step(c, dr):
        rev = dr == 1
        base = pl.multiple_of(c * C, C)
        rows = pl.ds(base, C)
        la = la_ref[dr, rows, :]
        hi = la.astype(BF16)
        lo = (la - hi.astype(F32)).astype(BF16)
        b = _dot(tri[dr], hi) + _dot(tri[dr], lo)
        zero = jnp.zeros((1, LANES), F32)
        if not rev:
            bref = [zero if i == 0 else b[SB * i - 1:SB * i, :] for i in range(nsb)]
            bend = [b[SB * i + SB - 1:SB * i + SB, :] for i in range(nsb)]
            blast = b[C - 1:C, :]
        else:
            bref = [zero if i == nsb - 1 else b[SB * (i + 1):SB * (i + 1) + 1, :] for i in range(nsb)]
            bend = [b[SB * i:SB * i + 1, :] for i in range(nsb)]
            blast = b[0:1, :]
        spread = bref[0] - bend[0]
        for i in range(1, nsb):
            spread = jnp.maximum(spread, bref[i] - bend[i])
        worst = jnp.max(spread)

        def fast(st):
            q = q_ref[rows, :] * (GLA_DK ** -0.5)
            k = k_ref[rows, :]
            a_rows = ([], [])
            for i in range(nsb):
                qe = q[SB * i:SB * (i + 1), :] * jnp.exp(b[SB * i:SB * (i + 1), :] - bref[i])
                valid = (rowc < SB * (i + 1)) if not rev else (rowc >= SB * i)
                ke = jnp.where(valid, k * jnp.exp(jnp.where(valid, bref[i] - b, 0.0)), 0.0).astype(BF16)
                for h in range(2):
                    a_rows[h].append(_dot_nt(jnp.where(head_mask[h], qe, 0.0).astype(BF16), ke))
            qi = q * jnp.exp(b)
            ks = (k * jnp.exp(blast - b)).astype(BF16)
            stb = st.astype(BF16)
            upd = []
            for h in range(2):
                a = jnp.where(causal[dr], jnp.concatenate(a_rows[h], axis=0), 0.0)
                vh = v_ref[rows, h * LANES:(h + 1) * LANES].astype(BF16)
                o = _dot(a.astype(BF16), vh) + _dot_nt(jnp.where(head_mask[h], qi, 0.0).astype(BF16), stb)
                acc_ref[rows, h * LANES:(h + 1) * LANES] += o
                upd.append(_dot_tn(vh, ks))
            return jnp.exp(blast) * st + jnp.where(lo_lanes, upd[0], upd[1])

        def slow(st):
            def row_step(r, st):
                t = r if not rev else C - 1 - r
                slab = pl.ds(pl.multiple_of(base + (t // 8) * 8, 8), 8)
                pick = row8 == (t % 8)
                q8 = jnp.where(pick, q_ref[slab, :] * (GLA_DK ** -0.5), 0.0)
                k8 = jnp.where(pick, k_ref[slab, :], 0.0).astype(BF16)
                la_t = jnp.sum(jnp.where(pick, la_ref[dr, slab, :], 0.0), axis=0, keepdims=True)
                upd = []
                for h in range(2):
                    v8 = jnp.where(pick, v_ref[slab, h * LANES:(h + 1) * LANES], 0.0).astype(BF16)
                    upd.append(_dot_tn(v8, k8))
                st = jnp.exp(la_t) * st + jnp.where(lo_lanes, upd[0], upd[1])
                stb = st.astype(BF16)
                for h in range(2):
                    qh = jnp.where(head_mask[h], q8, 0.0).astype(BF16)
                    acc_ref[slab, h * LANES:(h + 1) * LANES] += _dot_nt(qh, stb)
                return st

            return lax.fori_loop(0, C + 0 * pl.program_id(0), row_step, st)

        return worst, fast, slow

    def body(i, carry):
        worst_f, fast_f, slow_f = step(i, 0)
        worst_b, fast_b, slow_b = step(nc - 1 - i, 1)
        return lax.cond(jnp.maximum(worst_f, worst_b) <= GLA_SAFE_EXPONENT,
                        lambda st: (fast_f(st[0]), fast_b(st[1])),
                        lambda st: (slow_f(st[0]), slow_b(st[1])), carry)

    sf, sb = lax.fori_loop(0, nc, body, (s0_ref[0].T, s0_ref[1].T))
    sf_ref[0] = sf.T
    sf_ref[1] = sb.T

    def fin(i, carry):
        rows = pl.ds(pl.multiple_of(i * FIN_ROWS, FIN_ROWS), FIN_ROWS)
        o_ref[rows, :] = _head_norm_gate(acc_ref[rows, :], g_ref[rows, :], gain_ref[...]).astype(o_ref.dtype)
        return carry

    lax.fori_loop(0, n // FIN_ROWS, fin, 0)


def _prep_gate(gate_w, gate_b):
    gw = jnp.zeros((2, LANES, gate_w.shape[-1]), F32)
    for dr in range(2):
        lo = MLA_ROPE + dr * GATE_RANK
        gw = gw.at[dr, lo:lo + GATE_RANK, :].set(gate_w[dr].astype(F32))
    return gw.astype(BF16), gate_b.astype(F32).reshape(2, 1, -1)


def gla_mixer(z, gate_w, gate_b, gla_gain, s0, row0, batch, n):
    hp = GLA_HEADS // 2
    r0 = row0 // n
    gw, gb = _prep_gate(gate_w, gate_b)
    cq, ck, cv, cg, cm = AB_GQ // LANES, AB_GK // LANES, AB_GV // (2 * LANES), AB_GR // (2 * LANES), AB_MISC // LANES
    return pl.pallas_call(
        functools.partial(_gla_kernel, n=n),
        grid=(batch, hp),
        in_specs=[pl.BlockSpec((n, LANES), lambda b, p: (r0 + b, cq + p)),
                  pl.BlockSpec((n, LANES), lambda b, p: (r0 + b, ck + p)),
                  pl.BlockSpec((n, 2 * LANES), lambda b, p: (r0 + b, cv + p)),
                  pl.BlockSpec((n, 2 * LANES), lambda b, p: (r0 + b, cg + p)),
                  pl.BlockSpec((n, LANES), lambda b, p: (r0 + b, cm)),
                  pl.BlockSpec((2, LANES, LANES), lambda b, p: (0, 0, p)),
                  pl.BlockSpec((2, 1, LANES), lambda b, p: (0, 0, p)),
                  pl.BlockSpec((1, 2 * LANES), lambda b, p: (0, p)),
                  pl.BlockSpec((None, 2, LANES, LANES), lambda b, p: (b, 0, p, 0))],
        out_specs=[pl.BlockSpec((n, 2 * LANES), lambda b, p: (b, p)),
                   pl.BlockSpec((None, 2, LANES, LANES), lambda b, p: (b, 0, p, 0))],
        out_shape=[jax.ShapeDtypeStruct((batch * n, GLA_HEADS * GLA_DV), BF16),
                   jax.ShapeDtypeStruct((batch, 2, GLA_HEADS * GLA_DK, GLA_DV), F32)],
        scratch_shapes=[pltpu.VMEM((2, n, LANES), F32), pltpu.VMEM((n, 2 * LANES), F32)],
        compiler_params=_cparams(("parallel", "parallel")),
        name="gla_mixer",
    )(z, z, z, z, z, gw, gb, gla_gain.reshape(1, -1), s0)


NA_QROWS = 8
NA_WROWS = 16
NA_NEG = -1e30
NA_TAB = 32


def _na_bias_table(na_bias_l):
    qc = np.arange(GRID_W)[:, None]
    kc = np.arange(GRID_W)[None, :]
    ws = np.clip(qc - NA_COLS // 2, 0, GRID_W - NA_COLS)
    col_ok = (kc >= ws) & (kc < ws + NA_COLS)
    col_off = np.clip(kc - qc + NA_COLS - 1, 0, 2 * NA_COLS - 2)
    aa = (np.arange(NA_TAB) - NA_QROWS)[:, None] + np.arange(2)[None, :]
    n_row, n_col = 2 * NA_ROWS - 1, 2 * NA_COLS - 1
    sel_row = (aa[:, :, None] == np.arange(n_row)[None, None, :]).astype(np.float32)
    sel_col = ((col_off[:, :, None] == np.arange(n_col)[None, None, :]) & col_ok[:, :, None]).astype(np.float32)
    fill = np.where(col_ok, 0.0, NA_NEG).astype(np.float32)
    tab = jnp.einsum('hac,sea,qkc->hsqek', na_bias_l.astype(F32), jnp.asarray(sel_row), jnp.asarray(sel_col),
                     precision=lax.Precision.HIGHEST)
    tab = tab + jnp.asarray(fill)[None, None, :, None, :]
    h = na_bias_l.shape[0]
    return tab.reshape(h, NA_TAB, GRID_W, 2 * GRID_W)


def _na_latent_kernel(q_ref, k_ref, v_ref, kc_ref, vc_ref, tab_ref, o_ref, s_ref, *, scale, rows_total):
    g = pl.program_id(2)
    r0 = g * NA_QROWS
    start = jnp.clip(r0 - NA_ROWS // 2, 0, rows_total - NA_WROWS)
    nloc = NA_WROWS * GRID_W
    krows = pl.ds(pl.multiple_of(start * GRID_W, GRID_W), nloc)
    q = (q_ref[...] * scale).astype(BF16)
    kw = k_ref[krows, :].astype(BF16)
    s_ref[:, 0:nloc] = _dot_nt(q, kw)
    s_ref[:, nloc:] = _dot_nt(q, kc_ref[...].astype(BF16))
    lane = lax.broadcasted_iota(jnp.int32, (1, LANES), 1)
    for qr in range(NA_QROWS):
        r = r0 + qr
        rs = jnp.clip(r - NA_ROWS // 2, 0, rows_total - NA_ROWS)
        for kp in range(NA_WROWS // 2):
            k0 = start + 2 * kp
            ok0 = (k0 >= rs) & (k0 < rs + NA_ROWS)
            ok1 = (k0 + 1 >= rs) & (k0 + 1 < rs + NA_ROWS)
            neg = jnp.where(lane < GRID_W, jnp.where(ok0, 0.0, NA_NEG), jnp.where(ok1, 0.0, NA_NEG))
            slot = k0 - r + (NA_ROWS - 1) + NA_QROWS
            s_ref[qr * GRID_W:(qr + 1) * GRID_W, kp * LANES:(kp + 1) * LANES] += tab_ref[slot] + neg
    s = s_ref[...]
    m = jnp.max(s, axis=-1, keepdims=True)
    p = jnp.exp(s - m)
    l = jnp.sum(p, axis=-1, keepdims=True)
    pb = p.astype(BF16)
    o = _dot(pb[:, 0:nloc], v_ref[krows, :].astype(BF16)) + _dot(pb[:, nloc:], vc_ref[...].astype(BF16))
    o_ref[...] = (o / l).astype(o_ref.dtype)


def na_latent_attention(z, k_ctx, v_ctx, na_bias_l, row0, batch, n):
    h = NA_HEADS
    past = k_ctx.shape[2]
    rows_total = n // GRID_W
    tq = NA_QROWS * GRID_W
    qb = n // tq
    r0 = row0 // n
    q0 = row0 // tq
    cq, ck, cv = CD_NQ // LANES, CD_NK // LANES, CD_NV // LANES
    tab = _na_bias_table(na_bias_l)
    return pl.pallas_call(
        functools.partial(_na_latent_kernel, scale=float(NA_HD ** -0.5), rows_total=rows_total),
        grid=(batch, h, qb),
        in_specs=[pl.BlockSpec((tq, LANES), lambda b, hh, g: (q0 + b * qb + g, cq + hh)),
                  pl.BlockSpec((n, LANES), lambda b, hh, g: (r0 + b, ck + hh)),
                  pl.BlockSpec((n, LANES), lambda b, hh, g: (r0 + b, cv + hh)),
                  pl.BlockSpec((None, None, past, NA_HD), lambda b, hh, g: (b, hh, 0, 0)),
                  pl.BlockSpec((None, None, past, NA_HD), lambda b, hh, g: (b, hh, 0, 0)),
                  pl.BlockSpec((None, NA_TAB, GRID_W, 2 * GRID_W), lambda b, hh, g: (hh, 0, 0, 0))],
        out_specs=pl.BlockSpec((tq, LANES), lambda b, hh, g: (b * qb + g, hh)),
        out_shape=jax.ShapeDtypeStruct((batch * n, h * NA_HD), BF16),
        scratch_shapes=[pltpu.VMEM((tq, NA_WROWS * GRID_W + past), F32)],
        compiler_params=_cparams(("parallel", "parallel", "arbitrary")),
        name="na_latent_attention",
    )(z, z, z, k_ctx, v_ctx, tab)


CHUNK = 64


def _rmsnorm(x, g=None):
    y = x * lax.rsqrt(jnp.mean(x * x, axis=-1, keepdims=True) + EPS)
    return y if g is None else y * g


def _to_heads(x, nh):
    b, n, _ = x.shape
    return x.reshape(b, n, nh, -1).transpose(0, 2, 1, 3)


def _from_heads(x):
    b, h, n, d = x.shape
    return x.transpose(0, 2, 1, 3).reshape(b, n, h * d)


def _to_chunks(x):
    b, h, n, d = x.shape
    return x.reshape(b, h, n // CHUNK, CHUNK, d).transpose(2, 0, 1, 3, 4)


def _from_chunks(x):
    nc, b, h, l, d = x.shape
    return x.transpose(1, 2, 0, 3, 4).reshape(b, h, nc * l, d)


def _flip(x):
    return jnp.flip(x, axis=2)


def _axial_rope(x):
    n, d = x.shape[-2], x.shape[-1]
    quarter = d // 4
    t = jnp.arange(n)
    row = (t // GRID_W).astype(F32)
    col = (t % GRID_W).astype(F32)
    inv_freq = ROPE_BASE ** (-jnp.arange(quarter, dtype=F32) / quarter)

    def rot(xh, pos):
        ang = pos[:, None] * inv_freq[None, :]
        cos, sin = jnp.cos(ang), jnp.sin(ang)
        x1, x2 = xh[..., :quarter], xh[..., quarter:]
        return jnp.concatenate([x1 * cos - x2 * sin, x1 * sin + x2 * cos], axis=-1)

    return jnp.concatenate([rot(x[..., :d // 2], row), rot(x[..., d // 2:], col)], axis=-1)


def _gla_scan(q, k, v, log_a, s0):
    causal = jnp.tril(jnp.ones((CHUNK, CHUNK), dtype=bool))[:, :, None]

    def step(s, inp):
        qc, kc, vc, ac = inp
        bcum = jnp.cumsum(ac, axis=-2)
        o_inter = jnp.einsum('bhld,bhdv->bhlv', qc * jnp.exp(bcum), s)
        diff = bcum[:, :, :, None, :] - bcum[:, :, None, :, :]
        decay = jnp.where(causal, jnp.exp(jnp.where(causal, diff, 0.0)), 0.0)
        scores = jnp.einsum('bhtd,bhsd,bhtsd->bhts', qc, kc, decay)
        o = o_inter + jnp.einsum('bhts,bhsv->bhtv', scores, vc)
        b_last = bcum[:, :, -1:, :]
        s_new = jnp.exp(b_last[:, :, 0, :, None]) * s + jnp.einsum('bhsd,bhsv->bhdv', kc * jnp.exp(b_last - bcum), vc)
        return s_new, o

    xs = tuple(_to_chunks(t) for t in (q, k, v, log_a))
    s_fin, o = lax.scan(step, s0, xs)
    return _from_chunks(o), s_fin


def _retention_scan(q, k, v, log_gamma, s0):
    lg = log_gamma.astype(F32)[:, None]
    pos = jnp.arange(CHUNK, dtype=F32)
    dist = pos[:, None] - pos[None, :]
    causal = (dist >= 0)[None]
    intra = jnp.where(causal, jnp.exp(jnp.where(causal, dist[None], 0.0) * lg[:, :, None]), 0.0)
    inter = jnp.exp((pos[None, :] + 1.0) * lg)
    to_state = jnp.exp((CHUNK - 1.0 - pos[None, :]) * lg)
    chunk_decay = jnp.exp(CHUNK * lg)

    def step(s, inp):
        qc, kc, vc = inp
        o = (jnp.einsum('bhld,bhdv->bhlv', qc * inter[None, :, :, None], s)
             + jnp.einsum('bhts,bhsv->bhtv', jnp.einsum('bhtd,bhsd->bhts', qc, kc) * intra[None], vc))
        s_new = chunk_decay[None, :, :, None] * s + jnp.einsum('bhsd,bhsv->bhdv', kc * to_state[None, :, :, None], vc)
        return s_new, o

    xs = tuple(_to_chunks(t) for t in (q, k, v))
    s_fin, o = lax.scan(step, s0, xs)
    return _from_chunks(o), s_fin


def _na_latent(q, k, v, k_ctx, v_ctx, bias_tab, scale):
    b, h, n, d = q.shape
    rows = n // GRID_W
    kr = min(NA_ROWS, rows)
    ncb = GRID_W // NA_COLS
    band = 2 * NA_COLS
    band_start = np.clip(np.arange(ncb) * NA_COLS - NA_COLS // 2, 0, GRID_W - band)
    key_col = band_start[:, None] + np.arange(band)[None, :]
    q_col = np.arange(ncb)[:, None] * NA_COLS + np.arange(NA_COLS)[None, :]
    win_start = np.clip(q_col - NA_COLS // 2, 0, GRID_W - NA_COLS)
    kc3 = key_col[:, None, :]
    col_valid = jnp.asarray((kc3 >= win_start[..., None]) & (kc3 < win_start[..., None] + NA_COLS))
    col_off = jnp.asarray(np.clip(kc3 - q_col[..., None] + NA_COLS - 1, 0, 2 * NA_COLS - 2))
    qg = q.reshape(b, h, rows, GRID_W, d)
    kg = k.reshape(b, h, rows, GRID_W, d)[:, :, :, key_col]
    vg = v.reshape(b, h, rows, GRID_W, d)[:, :, :, key_col]

    def one_row(r):
        rs = jnp.clip(r - kr // 2, 0, rows - kr)
        kb = lax.dynamic_slice_in_dim(kg, rs, kr, axis=2)
        vb = lax.dynamic_slice_in_dim(vg, rs, kr, axis=2)
        qr = lax.dynamic_index_in_dim(qg, r, axis=2, keepdims=False).reshape(b, h, ncb, NA_COLS, d)
        s_loc = jnp.einsum('bhjqd,bhrjkd->bhjqrk', qr, kb).astype(F32) * scale
        row_off = rs + jnp.arange(kr) - r + NA_ROWS - 1
        bias = bias_tab[:, row_off[:, None, None, None], col_off[None]].astype(F32).transpose(0, 2, 3, 1, 4)
        s_loc = jnp.where(col_valid[:, :, None, :], s_loc + bias, -jnp.inf)
        s_ctx = jnp.einsum('bhjqd,bhmd->bhjqm', qr, k_ctx).astype(F32) * scale
        s_all = jnp.concatenate([s_loc.reshape(b, h, ncb, NA_COLS, kr * band), s_ctx], axis=-1)
        p = jax.nn.softmax(s_all, axis=-1).astype(v.dtype)
        p_loc = p[..., :kr * band].reshape(b, h, ncb, NA_COLS, kr, band)
        p_ctx = p[..., kr * band:]
        o = jnp.einsum('bhjqrk,bhrjkd->bhjqd', p_loc, vb) + jnp.einsum('bhjqm,bhmd->bhjqd', p_ctx, v_ctx)
        return o.reshape(b, h, GRID_W, d)

    o = lax.map(one_row, jnp.arange(rows))
    return o.transpose(1, 2, 0, 3, 4).reshape(b, h, n, d)


def _gla_mixer_jnp(zb, gate_w, gate_b, gla_gain, s0):
    b, n, _ = zb.shape
    gq, gk = zb[..., AB_GQ:AB_GK], zb[..., AB_GK:AB_GV]
    gv, gr = zb[..., AB_GV:AB_GR], zb[..., AB_GR:AB_CQ]
    glr = zb[..., AB_MISC + MLA_ROPE: AB_MISC + MLA_ROPE + 2 * GATE_RANK]
    q = _to_heads(gq, GLA_HEADS) * GLA_DK ** -0.5
    k = _to_heads(gk, GLA_HEADS)
    v = _to_heads(gv, GLA_HEADS)
    gate_pre = jnp.einsum('bnzr,zre->zbne', glr.reshape(b, n, 2, GATE_RANK), gate_w) + gate_b[:, None, None, :]
    log_a = (jax.nn.log_sigmoid(gate_pre) / GATE_TAU).reshape(2, b, n, GLA_HEADS, GLA_DK).transpose(0, 1, 3, 2, 4)
    o_f, s_f = _gla_scan(q, k, v, log_a[0], s0[:, 0])
    o_b, s_b = _gla_scan(_flip(q), _flip(k), _flip(v), _flip(log_a[1]), s0[:, 1])
    o = _from_heads(_rmsnorm(o_f + _flip(o_b))) * gla_gain * jax.nn.silu(gr)
    return o, jnp.stack([s_f, s_b], axis=1)


def _ret_mixer_jnp(zb, log_decay, ret_gain, s0, latent):
    q = _to_heads(zb[..., CD_RQ:CD_RK], RET_HEADS) * RET_DK ** -0.5
    k = _to_heads(zb[..., CD_RK:CD_RV], RET_HEADS)
    v = _to_heads(zb[..., CD_RV:CD_RG], RET_HEADS)
    rg = zb[..., CD_RG:CD_NQ]
    if latent:
        q, k = _axial_rope(q), _axial_rope(k)
    o_f, s_f = _retention_scan(q, k, v, log_decay[0], s0[:, 0])
    o_b, s_b = _retention_scan(_flip(q), _flip(k), _flip(v), log_decay[1], s0[:, 1])
    o = _from_heads(_rmsnorm(o_f + _flip(o_b))) * ret_gain * jax.nn.silu(rg)
    return o, jnp.stack([s_f, s_b], axis=1)


def _prep_w_in_ab(w):
    d = w.shape[0]
    sizes = (512, 512, 1024, 1024, 2 * GATE_RANK, MLA_Q_LORA, MLA_KV_LORA, MLA_ROPE)
    gq, gk, gv, gr, glr, cq, ckv, kpe = jnp.split(w, np.cumsum(sizes)[:-1].tolist(), axis=1)
    pad = jnp.zeros((d, AB_WIDTH - AB_MISC - MLA_ROPE - 2 * GATE_RANK), w.dtype)
    return jnp.concatenate([gq, gk, gv, gr, cq, ckv, kpe, glr, pad], axis=1).astype(BF16)


def _prep_w_uq(w):
    r = w.shape[0]
    w3 = w.reshape(r, MLA_HEADS, MLA_NOPE + MLA_ROPE)
    nope = w3[:, :, :MLA_NOPE].reshape(r, MLA_HEADS * MLA_NOPE)
    pe = jnp.pad(w3[:, :, MLA_NOPE:], ((0, 0), (0, 0), (0, LANES - MLA_ROPE))).reshape(r, MLA_HEADS * LANES)
    return jnp.concatenate([nope, pe], axis=1).astype(BF16)


def _prep_w_ukv(w):
    r = w.shape[0]
    w3 = w.reshape(r, MLA_HEADS, MLA_NOPE + MLA_DV)
    return jnp.concatenate([w3[:, :, :MLA_NOPE].reshape(r, -1), w3[:, :, MLA_NOPE:].reshape(r, -1)], axis=1).astype(BF16)


def _prep_router(router_w):
    d = router_w.shape[0]
    w = jnp.zeros((d, LANES), F32).at[:, :N_EXPERTS].set(router_w.astype(F32))
    hi = w.astype(BF16)
    lo = (w - hi.astype(F32)).astype(BF16)
    return jnp.stack([hi, lo], axis=0)


def kernel(x_prompt, x_sample, state_gla, cache_mla_ckv, cache_mla_kpe, state_ret, cache_na_k, cache_na_v,
           c, c_ctx, ada_w, ada_b, norm1, norm2, w_in_ab, gla_gate_w, gla_gate_b, gla_gain, mla_q_gain,
           mla_w_uq, mla_kv_gain, mla_w_ukv, w_in_cd, ret_log_decay, ret_gain, na_bias, w_out,
           router_w, router_bias, exp_w1, exp_w3, exp_w2, final_norm):
    bp, n_p, d = x_prompt.shape
    bs, n_s, _ = x_sample.shape
    past = cache_mla_ckv.shape[2]
    depth = ada_w.shape[0]
    tp, ts = bp * n_p, bs * n_s
    ys = [x_prompt.reshape(tp, d), x_sample.reshape(ts, d)]

    cond8 = jnp.zeros((8, d), F32).at[0].set(c_ctx).at[1:1 + bs].set(c)
    mod = adaln_all(cond8, ada_w, ada_b)
    rw_split = _prep_router(router_w)
    w_out_bf = w_out.astype(BF16)
    cos_t, sin_t = rope_tables(n_s, 512)
    cos_s, sin_s = cos_t[512:], sin_t[512:]
    cos_id, sin_id = cos_t[:n_p], sin_t[:n_p]

    outs = {}
    for l in range(depth):
        i = l // 2
        mod_l = mod[l].reshape(8, 1, 6 * d)
        final = l == depth - 1
        if l % 2 == 0:
            z = win_project(ys, mod_l, norm1[l], _prep_w_in_ab(w_in_ab[i]), tp, n_s)
            s0p = jnp.zeros((bp, 2, GLA_HEADS * GLA_DK, GLA_DV), F32)
            s0s = state_gla[:, i].reshape(bs, 2, GLA_HEADS * GLA_DK, GLA_DV)
            og_p, st_p = gla_mixer(z, gla_gate_w[i], gla_gate_b[i], gla_gain[i], s0p, 0, bp, n_p)
            og_s, _ = gla_mixer(z, gla_gate_w[i], gla_gate_b[i], gla_gain[i], s0s, tp, bs, n_s)
            st_p = st_p.reshape(bp, 2, GLA_HEADS, GLA_DK, GLA_DV)
            o_a = (og_p, og_s)
            wq, wkv = _prep_w_uq(mla_w_uq[i]), _prep_w_ukv(mla_w_ukv[i])
            q_m, kv, ckvn, kpe = mla_project(z, mla_q_gain[i], mla_kv_gain[i], wq, wkv, cos_t, sin_t, tp, n_s)
            kv_ctx = ctx_kv_project(cache_mla_ckv[:, i].reshape(bs * past, -1), wkv)
            kpe_ctx = jnp.pad(cache_mla_kpe[:, i].reshape(bs * past, -1), ((0, 0), (0, LANES - MLA_ROPE)))
            om_p = mla_prompt_attention(q_m, kv, kpe, bp, n_p)
            om_s = mla_sample_attention(q_m, kv, kpe, kv_ctx, kpe_ctx, tp, bs, n_s, past)
            o_b = (om_p, om_s)
            outs.setdefault('gla', []).append(st_p)
            outs.setdefault('ckv', []).append(ckvn[:tp].reshape(bp, n_p, -1))
            outs.setdefault('kpe', []).append(z[:tp, AB_MISC:AB_MISC + MLA_ROPE].reshape(bp, n_p, -1))
        else:
            z = win_project(ys, mod_l, norm1[l], w_in_cd[i].astype(BF16), tp, n_s)
            s0p = jnp.zeros((bp, 2, RET_HEADS * RET_DK, RET_DV), F32)
            s0s = state_ret[:, i].reshape(bs, 2, RET_HEADS * RET_DK, RET_DV)
            or_p, st_p = retention_mixer(z, ret_log_decay[i], ret_gain[i], s0p, cos_id, sin_id, 0, bp, n_p)
            or_s, _ = retention_mixer(z, ret_log_decay[i], ret_gain[i], s0s, cos_s, sin_s, tp, bs, n_s)
            st_p = st_p.reshape(bp, 2, RET_HEADS, RET_DK, RET_DV)
            o_a = (or_p, or_s)
            on_p, kn, vn = na_prompt_attention(z, bp, n_p)
            on_s = na_latent_attention(z, cache_na_k[:, i], cache_na_v[:, i], na_bias[i], tp, bs, n_s)
            o_b = (on_p, on_s)
            outs.setdefault('ret', []).append(st_p)
            outs.setdefault('nak', []).append(kn)
            outs.setdefault('nav', []).append(vn)
        y1, h2, logits = wout_project(o_a, o_b, ys, mod_l, w_out_bf[l], norm2[l], rw_split, tp, n_s)
        y = moe_layer(y1, h2, logits, mod_l, router_bias, exp_w1, exp_w3, exp_w2, l, final_norm, tp, n_s, final)
        ys = list(y) if final else [y]

    return (ys[0].reshape(bp, n_p, d), ys[1].reshape(bs, n_s, d),
            jnp.stack(outs['gla'], axis=1), jnp.stack(outs['ckv'], axis=1), jnp.stack(outs['kpe'], axis=1),
            jnp.stack(outs['ret'], axis=1), jnp.stack(outs['nak'], axis=1), jnp.stack(outs['nav'], axis=1))
```

```python
import functools

import numpy as np
import jax
import jax.numpy as jnp
from jax import lax
from jax.experimental import pallas as pl
from jax.experimental.pallas import tpu as pltpu

F32 = jnp.float32
BF16 = jnp.bfloat16

EPS = 1e-6
GRID_W = 64
ROPE_BASE = 10000.0
GATE_RANK = 16
GATE_TAU = 16.0
GLA_HEADS, GLA_DK, GLA_DV = 8, 64, 128
MLA_HEADS, MLA_Q_LORA, MLA_KV_LORA, MLA_NOPE, MLA_ROPE, MLA_DV = 8, 512, 256, 128, 64, 128
RET_HEADS, RET_DK, RET_DV = 8, 64, 128
NA_HEADS, NA_HD, NA_ROWS, NA_COLS = 8, 128, 8, 16
N_EXPERTS, N_GROUPS, TOP_K, D_EXPERT = 32, 8, 2, 512
EXPERTS_PER_GROUP = N_EXPERTS // N_GROUPS
MOE_BLOCK = 128
LANES = 128

AB_GQ, AB_GK, AB_GV, AB_GR, AB_CQ, AB_CKV, AB_MISC, AB_WIDTH = 0, 512, 1024, 2048, 3072, 3584, 3840, 4096
CD_RQ, CD_RK, CD_RV, CD_RG, CD_NQ, CD_NK, CD_NV, CD_WIDTH = 0, 512, 1024, 2048, 3072, 4096, 5120, 6144

VMEM_LIMIT = 56 * 1024 * 1024


def _cparams(sem):
    return pltpu.CompilerParams(dimension_semantics=sem, vmem_limit_bytes=VMEM_LIMIT)


def _dot(a, b):
    return jnp.dot(a, b, preferred_element_type=F32)


def _dot_nt(a, b):
    return lax.dot_general(a, b, (((1,), (1,)), ((), ())), preferred_element_type=F32)


def _adaln_kernel(c_ref, w_ref, b_ref, o_ref):
    c = c_ref[...]
    s = (c * jax.nn.sigmoid(c)).astype(BF16)
    o_ref[...] = _dot(s, w_ref[...].astype(BF16)) + b_ref[...]


def adaln_all(cond8, ada_w, ada_b, tn=512):
    depth, d, n6 = ada_w.shape
    return pl.pallas_call(
        _adaln_kernel,
        grid=(depth, n6 // tn),
        in_specs=[pl.BlockSpec((8, d), lambda l, j: (0, 0)),
                  pl.BlockSpec((None, d, tn), lambda l, j: (l, 0, j)),
                  pl.BlockSpec((None, 1, tn), lambda l, j: (l, 0, j))],
        out_specs=pl.BlockSpec((None, 8, tn), lambda l, j: (l, 0, j)),
        out_shape=jax.ShapeDtypeStruct((depth, 8, n6), F32),
        compiler_params=_cparams(("parallel", "parallel")),
        name="adaln",
    )(cond8, ada_w, ada_b.reshape(depth, 1, n6))


def _mod_index(i, tm, n_prompt_rows, sample_rows):
    tp = n_prompt_rows // tm
    return jnp.where(i < tp, 0, 1 + (i - tp) // (sample_rows // tm))


def _split_specs(block, tiles_prompt, n_grid_axes=1):
    if n_grid_axes == 1:
        return [pl.BlockSpec(block, lambda i: (jnp.minimum(i, tiles_prompt - 1), 0)),
                pl.BlockSpec(block, lambda i: (jnp.maximum(i - tiles_prompt, 0), 0))]
    return [pl.BlockSpec(block, lambda i, j: (jnp.minimum(i, tiles_prompt - 1), 0)),
            pl.BlockSpec(block, lambda i, j: (jnp.maximum(i - tiles_prompt, 0), 0))]


def _win_kernel(*refs, d, tiles_prompt, split):
    if split:
        xp_ref, xs_ref, mod_ref, g_ref, w_ref, o_ref, h_ref = refs
    else:
        xp_ref, mod_ref, g_ref, w_ref, o_ref, h_ref = refs

    def norm_mod(x_ref):
        x = x_ref[...]
        xn = x * lax.rsqrt(jnp.mean(x * x, axis=-1, keepdims=True) + EPS) * g_ref[...]
        h = xn * (1.0 + mod_ref[:, d:2 * d]) + mod_ref[:, 0:d]
        h_ref[...] = h.astype(BF16)

    first = pl.program_id(1) == 0
    if split:
        is_prompt = pl.program_id(0) < tiles_prompt
        pl.when(first & is_prompt)(lambda: norm_mod(xp_ref))
        pl.when(first & jnp.logical_not(is_prompt))(lambda: norm_mod(xs_ref))
    else:
        pl.when(first)(lambda: norm_mod(xp_ref))

    o_ref[...] = _dot(h_ref[...], w_ref[...])


def win_project(xs, mod_l, gain, w_bf16, n_prompt_rows, sample_rows, tm=1024, tn=512):
    split = len(xs) == 2
    d = xs[0].shape[1]
    t = sum(x.shape[0] for x in xs)
    width = w_bf16.shape[1]
    tiles_prompt = n_prompt_rows // tm
    midx = functools.partial(_mod_index, tm=tm, n_prompt_rows=n_prompt_rows, sample_rows=sample_rows)
    x_specs = (_split_specs((tm, d), tiles_prompt, 2) if split else [pl.BlockSpec((tm, d), lambda i, j: (i, 0))])
    return pl.pallas_call(
        functools.partial(_win_kernel, d=d, tiles_prompt=tiles_prompt, split=split),
        grid=(t // tm, width // tn),
        in_specs=x_specs + [pl.BlockSpec((None, 1, 6 * d), lambda i, j: (midx(i), 0, 0)),
                            pl.BlockSpec((1, d), lambda i, j: (0, 0)),
                            pl.BlockSpec((d, tn), lambda i, j: (0, j))],
        out_specs=pl.BlockSpec((tm, tn), lambda i, j: (i, j)),
        out_shape=jax.ShapeDtypeStruct((t, width), F32),
        scratch_shapes=[pltpu.VMEM((tm, d), BF16)],
        compiler_params=_cparams(("parallel", "arbitrary")),
        name="win_project",
    )(*xs, mod_l, gain.reshape(1, d), w_bf16)


def _wout_kernel(*refs, d, tiles_prompt, split_y):
    if split_y:
        (oap_ref, oas_ref, obp_ref, obs_ref, yp_ref, ys_ref, mod_ref, wa_ref, wb_ref, g_ref, rw_ref,
         ynew_ref, h2_ref, lg_ref) = refs
    else:
        (oap_ref, oas_ref, obp_ref, obs_ref, yp_ref, mod_ref, wa_ref, wb_ref, g_ref, rw_ref,
         ynew_ref, h2_ref, lg_ref) = refs
        ys_ref = yp_ref
    is_prompt = pl.program_id(0) < tiles_prompt
    oa = jnp.where(is_prompt, oap_ref[...], oas_ref[...])
    ob = jnp.where(is_prompt, obp_ref[...], obs_ref[...])
    y = jnp.where(is_prompt, yp_ref[...], ys_ref[...]) if split_y else yp_ref[...]
    out = _dot(oa, wa_ref[...]) + _dot(ob, wb_ref[...])
    y1 = y + mod_ref[:, 2 * d:3 * d] * out
    ynew_ref[...] = y1
    xn = y1 * lax.rsqrt(jnp.mean(y1 * y1, axis=-1, keepdims=True) + EPS) * g_ref[...]
    h = xn * (1.0 + mod_ref[:, 4 * d:5 * d]) + mod_ref[:, 3 * d:4 * d]
    h2_ref[...] = h
    hh = h.astype(BF16)
    hl = (h - hh.astype(F32)).astype(BF16)
    lg_ref[...] = _dot(hh, rw_ref[0]) + _dot(hl, rw_ref[0]) + _dot(hh, rw_ref[1])


def wout_project(o_a, o_b, ys, mod_l, w_out_bf16, gain2, rw_split, n_prompt_rows, sample_rows, tm=256):
    split_y = len(ys) == 2
    d = ys[0].shape[1]
    t = sum(x.shape[0] for x in ys)
    ka, kb = o_a[0].shape[1], o_b[0].shape[1]
    tiles_prompt = n_prompt_rows // tm
    midx = functools.partial(_mod_index, tm=tm, n_prompt_rows=n_prompt_rows, sample_rows=sample_rows)
    y_specs = _split_specs((tm, d), tiles_prompt) if split_y else [pl.BlockSpec((tm, d), lambda i: (i, 0))]
    return pl.pallas_call(
        functools.partial(_wout_kernel, d=d, tiles_prompt=tiles_prompt, split_y=split_y),
        grid=(t // tm,),
        in_specs=_split_specs((tm, ka), tiles_prompt) + _split_specs((tm, kb), tiles_prompt) + y_specs + [
                  pl.BlockSpec((None, 1, 6 * d), lambda i: (midx(i), 0, 0)),
                  pl.BlockSpec((ka, d), lambda i: (0, 0)),
                  pl.BlockSpec((kb, d), lambda i: (1, 0)),
                  pl.BlockSpec((1, d), lambda i: (0, 0)),
                  pl.BlockSpec((2, d, LANES), lambda i: (0, 0, 0))],
        out_specs=[pl.BlockSpec((tm, d), lambda i: (i, 0)),
                   pl.BlockSpec((tm, d), lambda i: (i, 0)),
                   pl.BlockSpec((tm, LANES), lambda i: (i, 0))],
        out_shape=[jax.ShapeDtypeStruct((t, d), F32),
                   jax.ShapeDtypeStruct((t, d), F32),
                   jax.ShapeDtypeStruct((t, LANES), F32)],
        compiler_params=_cparams(("parallel",)),
        name="wout_project",
    )(*o_a, *o_b, *ys, mod_l, w_out_bf16, w_out_bf16, gain2.reshape(1, d), rw_split)


def _group_shift(x, k, sub):
    fwd = pltpu.roll(x, LANES - k, axis=1)
    back = pltpu.roll(x, EXPERTS_PER_GROUP - k, axis=1)
    return jnp.where(sub + k < EXPERTS_PER_GROUP, fwd, back)


def _group_reduce(x, sub, op):
    r = x
    for k in range(1, EXPERTS_PER_GROUP):
        r = op(r, _group_shift(x, k, sub))
    return r


def _route_kernel(lg_ref, bias_ref, info_ref, cnt_ref, run_ref, *, tm):
    @pl.when(pl.program_id(0) == 0)
    def _():
        run_ref[...] = jnp.zeros_like(run_ref)

    lg = lg_ref[...]
    lane = lax.broadcasted_iota(jnp.int32, lg.shape, 1)
    lane_f = lane.astype(F32)
    sub = (lane & (EXPERTS_PER_GROUP - 1)).astype(F32)
    grp = (lane >> 2).astype(F32)
    aff = jax.nn.sigmoid(lg)
    neg = jnp.float32(-jnp.inf)
    sel = jnp.where(lane < N_EXPERTS, aff + bias_ref[...], neg)
    m1 = _group_reduce(sel, sub, jnp.maximum)
    i1 = _group_reduce(jnp.where(sel == m1, sub, 4.0), sub, jnp.minimum)
    sel2 = jnp.where(sub == i1, neg, sel)
    m2 = _group_reduce(sel2, sub, jnp.maximum)
    i2 = _group_reduce(jnp.where(sel2 == m2, sub, 4.0), sub, jnp.minimum)
    gscore = m1 + m2
    gmax = jnp.max(gscore, axis=1, keepdims=True)
    g = jnp.min(jnp.where(gscore == gmax, grp, 99.0), axis=1, keepdims=True)
    oh1 = (grp == g) & (sub == i1)
    oh2 = (grp == g) & (sub == i2)

    def pick(mask, v):
        return jnp.sum(jnp.where(mask, v, 0.0), axis=1, keepdims=True)

    a1, a2 = pick(oh1, aff), pick(oh2, aff)
    den = a1 + a2
    e1, e2 = pick(oh1, lane_f), pick(oh2, lane_f)
    oh = jnp.where(oh1 | oh2, 1.0, 0.0)
    r_i = lax.broadcasted_iota(jnp.int32, (tm, tm), 0)
    c_i = lax.broadcasted_iota(jnp.int32, (tm, tm), 1)
    tri = jnp.where(c_i < r_i, 1.0, 0.0).astype(BF16)
    cum = _dot(tri, oh.astype(BF16)) + run_ref[0:1, :]
    r1, r2 = pick(oh1, cum), pick(oh2, cum)
    run_new = run_ref[0:1, :] + jnp.sum(oh, axis=0, keepdims=True)
    run_ref[...] = jnp.broadcast_to(run_new, run_ref.shape)
    cnt_ref[...] = jnp.broadcast_to(run_new, cnt_ref.shape)
    info = jnp.zeros_like(lg)
    for k, v in enumerate((e1, e2, a1 / den, a2 / den, r1, r2)):
        info = jnp.where(lane == k, v, info)
    info_ref[...] = info


def route(logits, router_bias, tm=256):
    t = logits.shape[0]
    bias = jnp.zeros((1, LANES), F32).at[0, :N_EXPERTS].set(router_bias.astype(F32))
    return pl.pallas_call(
        functools.partial(_route_kernel, tm=tm),
        grid=(t // tm,),
        in_specs=[pl.BlockSpec((tm, LANES), lambda i: (i, 0)),
                  pl.BlockSpec((1, LANES), lambda i: (0, 0))],
        out_specs=[pl.BlockSpec((tm, LANES), lambda i: (i, 0)),
                   pl.BlockSpec((8, LANES), lambda i: (0, 0))],
        out_shape=[jax.ShapeDtypeStruct((t, LANES), F32),
                   jax.ShapeDtypeStruct((8, LANES), F32)],
        scratch_shapes=[pltpu.VMEM((8, LANES), F32)],
        compiler_params=_cparams(("arbitrary",)),
        name="route",
    )(logits, bias)


def _expert_kernel(be_ref, nu_ref, tok_ref, tok_next_ref, h_hbm, w1_ref, w3_ref, w2_ref, o_ref,
                   xbuf, w1b, w3b, w2b, sem):
    i = pl.program_id(0)
    slot = i % 2
    nu = nu_ref[0]

    def gather(idx_ref, s):
        for j in range(MOE_BLOCK):
            pltpu.make_async_copy(h_hbm.at[pl.ds(idx_ref[j], 1)], xbuf.at[s, pl.ds(j, 1)], sem.at[s]).start()

    def wait_block(s):
        pltpu.make_async_copy(h_hbm.at[pl.ds(0, MOE_BLOCK)], xbuf.at[s], sem.at[s]).wait()

    @pl.when(i == 0)
    def _():
        gather(tok_ref, 0)

    @pl.when(i < nu)
    def _():
        @pl.when((i == 0) | (be_ref[i] != be_ref[jnp.maximum(i - 1, 0)]))
        def _():
            w1b[...] = w1_ref[...].astype(BF16)
            w3b[...] = w3_ref[...].astype(BF16)
            w2b[...] = w2_ref[...].astype(BF16)

        wait_block(slot)
        gather(tok_next_ref, 1 - slot)
        x = xbuf[slot].astype(BF16)
        a = _dot(x, w1b[...])
        b = _dot(x, w3b[...])
        mid = (a * jax.nn.sigmoid(a)) * b
        o_ref[...] = _dot(mid.astype(BF16), w2b[...])

    @pl.when(i >= nu)
    def _():
        @pl.when(i == nu)
        def _():
            wait_block(slot)

        o_ref[...] = jnp.zeros_like(o_ref)


def expert_ffn(h2, slot_tok, block_expert, n_used, w1, w3, w2, layer):
    n_steps = slot_tok.shape[0] // MOE_BLOCK
    d, de = w1.shape[-2], w1.shape[-1]
    grid_spec = pltpu.PrefetchScalarGridSpec(
        num_scalar_prefetch=2,
        grid=(n_steps,),
        in_specs=[pl.BlockSpec((MOE_BLOCK,), lambda i, be, nu: (i,), memory_space=pltpu.SMEM),
                  pl.BlockSpec((MOE_BLOCK,), lambda i, be, nu: (jnp.minimum(i + 1, n_steps - 1),),
                               memory_space=pltpu.SMEM),
                  pl.BlockSpec(memory_space=pl.ANY),
                  pl.BlockSpec((None, None, d, de), lambda i, be, nu: (layer, be[i], 0, 0)),
                  pl.BlockSpec((None, None, d, de), lambda i, be, nu: (layer, be[i], 0, 0)),
                  pl.BlockSpec((None, None, de, d), lambda i, be, nu: (layer, be[i], 0, 0))],
        out_specs=pl.BlockSpec((MOE_BLOCK, d), lambda i, be, nu: (i, 0)),
        scratch_shapes=[pltpu.VMEM((2, MOE_BLOCK, d), F32), pltpu.VMEM((d, de), BF16), pltpu.VMEM((d, de), BF16),
                        pltpu.VMEM((de, d), BF16), pltpu.SemaphoreType.DMA((2,))],
    )
    return pl.pallas_call(
        _expert_kernel,
        grid_spec=grid_spec,
        out_shape=jax.ShapeDtypeStruct((n_steps * MOE_BLOCK, d), F32),
        compiler_params=_cparams(("arbitrary",)),
        name="expert_ffn",
    )(block_expert, n_used, slot_tok, slot_tok, h2, w1, w3, w2)


def _combine_kernel(dcur_ref, dnext_ref, y_ref, mod_ref, info_ref, g_ref, yb_hbm, *rest, tm, nt, d, final,
                    tiles_prompt):
    if final:
        op_ref, os_ref, buf, sem = rest
    else:
        o_ref, buf, sem = rest
    i = pl.program_id(0)
    slot = i % 2

    def row_copy(src_row, s, k, j):
        return pltpu.make_async_copy(yb_hbm.at[pl.ds(src_row, 1)], buf.at[s, k, pl.ds(j, 1)], sem.at[s])

    def issue(dref, s):
        def body(j, carry):
            for k in range(TOP_K):
                row_copy(dref[TOP_K * j + k], s, k, j).start()
            return carry

        lax.fori_loop(0, tm, body, 0, unroll=8)

    @pl.when(i == 0)
    def _():
        issue(dcur_ref, 0)

    @pl.when(i + 1 < nt)
    def _():
        issue(dnext_ref, 1 - slot)

    for k in range(TOP_K):
        pltpu.make_async_copy(yb_hbm.at[pl.ds(0, tm)], buf.at[slot, k], sem.at[slot]).wait()
    info = info_ref[...]
    moe = info[:, 2:3] * buf[slot, 0] + info[:, 3:4] * buf[slot, 1]
    y2 = y_ref[...] + mod_ref[:, 5 * d:6 * d] * moe
    if not final:
        o_ref[...] = y2
        return
    y2 = y2 * lax.rsqrt(jnp.mean(y2 * y2, axis=-1, keepdims=True) + EPS) * g_ref[...]

    @pl.when(i < tiles_prompt)
    def _():
        op_ref[...] = y2

    @pl.when(i >= tiles_prompt)
    def _():
        os_ref[...] = y2


def combine(y1, mod_l, info, dest_flat, yb, final_gain, n_prompt_rows, sample_rows, final, tm=256):
    t, d = y1.shape
    nt = t // tm
    tiles_prompt = n_prompt_rows // tm
    midx = functools.partial(_mod_index, tm=tm, n_prompt_rows=n_prompt_rows, sample_rows=sample_rows)
    if final:
        out_specs = _split_specs((tm, d), tiles_prompt)
        out_shape = [jax.ShapeDtypeStruct((n_prompt_rows, d), F32), jax.ShapeDtypeStruct((t - n_prompt_rows, d), F32)]
    else:
        out_specs = pl.BlockSpec((tm, d), lambda i: (i, 0))
        out_shape = jax.ShapeDtypeStruct((t, d), F32)
    return pl.pallas_call(
        functools.partial(_combine_kernel, tm=tm, nt=nt, d=d, final=final, tiles_prompt=tiles_prompt),
        grid=(nt,),
        in_specs=[pl.BlockSpec((TOP_K * tm,), lambda i: (i,), memory_space=pltpu.SMEM),
                  pl.BlockSpec((TOP_K * tm,), lambda i: (jnp.minimum(i + 1, nt - 1),), memory_space=pltpu.SMEM),
                  pl.BlockSpec((tm, d), lambda i: (i, 0)),
                  pl.BlockSpec((None, 1, 6 * d), lambda i: (midx(i), 0, 0)),
                  pl.BlockSpec((tm, LANES), lambda i: (i, 0)),
                  pl.BlockSpec((1, d), lambda i: (0, 0)),
                  pl.BlockSpec(memory_space=pl.ANY)],
        out_specs=out_specs,
        out_shape=out_shape,
        scratch_shapes=[pltpu.VMEM((2, TOP_K, tm, d), F32), pltpu.SemaphoreType.DMA((2,))],
        compiler_params=_cparams(("arbitrary",)),
        name="combine",
    )(dest_flat, dest_flat, y1, mod_l, info, final_gain.reshape(1, d), yb)


def moe_layer(y1, h2p, logits, mod_l, router_bias, w1, w3, w2, layer, final_gain, n_prompt_rows, sample_rows, final):
    t, d = y1.shape
    info, cnt = route(logits, router_bias)
    expert = info[:, 0:2].astype(jnp.int32)
    rank = info[:, 4:6].astype(jnp.int32)
    counts = cnt[0, :N_EXPERTS].astype(jnp.int32)
    padded = (counts + MOE_BLOCK - 1) // MOE_BLOCK * MOE_BLOCK
    padded_end = jnp.cumsum(padded)
    padded_start = padded_end - padded
    dest = (padded_start[expert] + rank).reshape(-1)
    n_blocks = (t * TOP_K) // MOE_BLOCK + N_EXPERTS + 1
    slots = n_blocks * MOE_BLOCK
    block_lo = jnp.arange(n_blocks, dtype=jnp.int32)[:, None] * MOE_BLOCK
    block_expert = jnp.minimum(jnp.sum((padded_end[None, :] <= block_lo).astype(jnp.int32), axis=1), N_EXPERTS - 1)
    n_used = (padded_end[-1:] // MOE_BLOCK).astype(jnp.int32)
    tok = jnp.broadcast_to(jnp.arange(t, dtype=jnp.int32)[:, None], (t, TOP_K)).reshape(-1)
    slot_tok = jnp.zeros((slots,), jnp.int32).at[dest].set(tok)
    yb = expert_ffn(h2p, slot_tok, block_expert, n_used, w1, w3, w2, layer)
    return combine(y1, mod_l, info, dest, yb, final_gain, n_prompt_rows, sample_rows, final)


def rope_tables(n, lead):
    quarter = MLA_ROPE // 4
    tpos = np.arange(n)
    row = (tpos // GRID_W).astype(np.float32)
    col = (tpos % GRID_W).astype(np.float32)
    inv_freq = (np.float32(ROPE_BASE) ** (-np.arange(quarter, dtype=np.float32) / np.float32(quarter))).astype(np.float32)
    lane = np.arange(64)
    pos = np.where(lane[None, :] < 32, row[:, None], col[:, None]).astype(np.float32)
    ang = pos * inv_freq[lane % quarter][None, :]
    cos = np.cos(ang).astype(np.float32)
    sin = np.sin(ang).astype(np.float32)
    sgn = np.where((lane % 32) < quarter, -1.0, 1.0).astype(np.float32)
    sin = sin * sgn[None, :]
    cos = np.concatenate([cos, cos], axis=1)
    sin = np.concatenate([sin, sin], axis=1)
    cos = np.concatenate([np.ones((lead, LANES), np.float32), cos], axis=0)
    sin = np.concatenate([np.zeros((lead, LANES), np.float32), sin], axis=0)
    return jnp.asarray(cos), jnp.asarray(sin)


def _rope(x, cos, sin_signed):
    lane = lax.broadcasted_iota(jnp.int32, x.shape, 1)
    partner = jnp.where((lane & 31) < 16, pltpu.roll(x, LANES - 16, axis=1), pltpu.roll(x, 16, axis=1))
    return x * cos + partner * sin_signed


def _mla_proj_kernel(cq_ref, ckv_ref, misc_ref, qg_ref, kvg_ref, wq_ref, wkv_ref, cos_ref, sin_ref,
                     q_ref, kv_ref, ckvn_ref, kpe_ref, *, scale):
    cq = cq_ref[...]
    qn = cq * lax.rsqrt(jnp.mean(cq * cq, axis=-1, keepdims=True) + EPS) * qg_ref[...]
    qf = _dot(qn.astype(BF16), wq_ref[...])
    half = qf.shape[1] // 2
    cos, sin = cos_ref[...], sin_ref[...]
    q_ref[:, 0:half] = (qf[:, 0:half] * scale).astype(BF16)
    for h in range(MLA_HEADS):
        pe = qf[:, half + h * LANES: half + (h + 1) * LANES]
        q_ref[:, half + h * LANES: half + (h + 1) * LANES] = (_rope(pe, cos, sin) * scale).astype(BF16)
    ckv = ckv_ref[...]
    ckvn = ckv * lax.rsqrt(jnp.mean(ckv * ckv, axis=-1, keepdims=True) + EPS) * kvg_ref[...]
    ckvn_ref[...] = ckvn
    kv_ref[...] = _dot(ckvn.astype(BF16), wkv_ref[...]).astype(BF16)
    misc = misc_ref[...]
    lane = lax.broadcasted_iota(jnp.int32, misc.shape, 1)
    kpe = jnp.where(lane < MLA_ROPE, misc, 0.0)
    kpe_ref[...] = _rope(kpe, cos, sin).astype(BF16)


def mla_project(z, q_gain, kv_gain, wq, wkv, cos_t, sin_t, n_prompt_rows, sample_rows, tm=512):
    t = z.shape[0]
    tp = n_prompt_rows // tm
    per = sample_rows // tm

    def tab(i):
        return (jnp.where(i < tp, 0, 1 + (i - tp) % per), 0)

    scale = float((MLA_NOPE + MLA_ROPE) ** -0.5)
    nq = wq.shape[1]
    return pl.pallas_call(
        functools.partial(_mla_proj_kernel, scale=scale),
        grid=(t // tm,),
        in_specs=[pl.BlockSpec((tm, MLA_Q_LORA), lambda i: (i, AB_CQ // MLA_Q_LORA)),
                  pl.BlockSpec((tm, MLA_KV_LORA), lambda i: (i, AB_CKV // MLA_KV_LORA)),
                  pl.BlockSpec((tm, LANES), lambda i: (i, AB_MISC // LANES)),
                  pl.BlockSpec((1, MLA_Q_LORA), lambda i: (0, 0)),
                  pl.BlockSpec((1, MLA_KV_LORA), lambda i: (0, 0)),
                  pl.BlockSpec(wq.shape, lambda i: (0, 0)),
                  pl.BlockSpec(wkv.shape, lambda i: (0, 0)),
                  pl.BlockSpec((tm, LANES), tab),
                  pl.BlockSpec((tm, LANES), tab)],
        out_specs=[pl.BlockSpec((tm, nq), lambda i: (i, 0)),
                   pl.BlockSpec((tm, wkv.shape[1]), lambda i: (i, 0)),
                   pl.BlockSpec((tm, MLA_KV_LORA), lambda i: (i, 0)),
                   pl.BlockSpec((tm, LANES), lambda i: (i, 0))],
        out_shape=[jax.ShapeDtypeStruct((t, nq), BF16),
                   jax.ShapeDtypeStruct((t, wkv.shape[1]), BF16),
                   jax.ShapeDtypeStruct((t, MLA_KV_LORA), F32),
                   jax.ShapeDtypeStruct((t, LANES), BF16)],
        compiler_params=_cparams(("parallel",)),
        name="mla_project",
    )(z, z, z, q_gain.reshape(1, -1), kv_gain.reshape(1, -1), wq, wkv, cos_t, sin_t)


def _ctx_kv_kernel(ckv_ref, w_ref, o_ref):
    o_ref[...] = _dot(ckv_ref[...].astype(BF16), w_ref[...]).astype(BF16)


def ctx_kv_project(ckv_c, wkv):
    rows = ckv_c.shape[0]
    return pl.pallas_call(
        _ctx_kv_kernel,
        grid=(1,),
        in_specs=[pl.BlockSpec(ckv_c.shape, lambda i: (0, 0)), pl.BlockSpec(wkv.shape, lambda i: (0, 0))],
        out_specs=pl.BlockSpec((rows, wkv.shape[1]), lambda i: (0, 0)),
        out_shape=jax.ShapeDtypeStruct((rows, wkv.shape[1]), BF16),
        compiler_params=_cparams(("arbitrary",)),
        name="ctx_kv_project",
    )(ckv_c, wkv)


def _softmax_pv(s, v):
    m = jnp.max(s, axis=-1, keepdims=True)
    p = jnp.exp(s - m)
    l = jnp.sum(p, axis=-1, keepdims=True)
    return _dot(p.astype(BF16), v) / l


def _mla_prompt_attn_kernel(q_ref, kv_ref, kp_ref, o_ref):
    nh = MLA_HEADS
    kp = kp_ref[...]
    for h in range(nh):
        q = jnp.concatenate([q_ref[:, h * LANES:(h + 1) * LANES], q_ref[:, (nh + h) * LANES:(nh + h + 1) * LANES]], axis=1)
        k = jnp.concatenate([kv_ref[:, h * LANES:(h + 1) * LANES], kp], axis=1)
        v = kv_ref[:, (nh + h) * LANES:(nh + h + 1) * LANES]
        o_ref[:, h * LANES:(h + 1) * LANES] = _softmax_pv(_dot_nt(q, k), v).astype(o_ref.dtype)


def mla_prompt_attention(q_m, kv, kpe, batch, n):
    h = MLA_HEADS
    return pl.pallas_call(
        _mla_prompt_attn_kernel,
        grid=(batch,),
        in_specs=[pl.BlockSpec((n, 2 * h * LANES), lambda b: (b, 0)),
                  pl.BlockSpec((n, 2 * h * LANES), lambda b: (b, 0)),
                  pl.BlockSpec((n, LANES), lambda b: (b, 0))],
        out_specs=pl.BlockSpec((n, h * MLA_DV), lambda b: (b, 0)),
        out_shape=jax.ShapeDtypeStruct((batch * n, h * MLA_DV), BF16),
        compiler_params=_cparams(("parallel",)),
        name="mla_prompt_attention",
    )(q_m, kv, kpe)


def _lane_tile_reduce(x, op):
    r = x[:, 0:LANES]
    for c in range(1, x.shape[1] // LANES):
        r = op(r, x[:, c * LANES:(c + 1) * LANES])
    return r


def _mla_sample_attn_kernel(qn_ref, qp_ref, kcn_ref, kcp_ref, vc_ref, kn_ref, kp_ref, v_ref, o_ref,
                            s_ref, m_ref, acc_ref, *, tk):
    past = kcn_ref.shape[0]
    nk = kn_ref.shape[0] // tk
    q = jnp.concatenate([qn_ref[...], qp_ref[...]], axis=1)
    kc = jnp.concatenate([kcn_ref[...], kcp_ref[...].astype(BF16)], axis=1)
    s = _dot_nt(q, kc)
    s_ref[:, 0:past] = s
    m_ref[...] = _lane_tile_reduce(s, jnp.maximum)

    def scores(j, carry):
        off = pl.multiple_of(j * tk, tk)
        k = jnp.concatenate([kn_ref[pl.ds(off, tk), :], kp_ref[pl.ds(off, tk), :]], axis=1)
        s = _dot_nt(q, k)
        s_ref[:, pl.ds(pl.multiple_of(past + j * tk, LANES), tk)] = s
        m_ref[...] = jnp.maximum(m_ref[...], _lane_tile_reduce(s, jnp.maximum))
        return carry

    lax.fori_loop(0, nk, scores, 0, unroll=2)
    m_ref[...] = jnp.broadcast_to(jnp.max(m_ref[...], axis=-1, keepdims=True), m_ref.shape)

    def weighted_values(s, v):
        m = m_ref[...]
        p = jnp.concatenate([jnp.exp(s[:, c * LANES:(c + 1) * LANES] - m).astype(BF16)
                             for c in range(s.shape[1] // LANES)], axis=1)
        v_ones = jnp.concatenate([v, jnp.ones((v.shape[0], LANES), BF16)], axis=1)
        return _dot(p, v_ones)

    acc_ref[...] = weighted_values(s_ref[:, 0:past], vc_ref[...])

    def weighted(j, carry):
        off = pl.multiple_of(j * tk, tk)
        acc_ref[...] += weighted_values(s_ref[:, pl.ds(pl.multiple_of(past + j * tk, LANES), tk)],
                                        v_ref[pl.ds(off, tk), :])
        return carry

    lax.fori_loop(0, nk, weighted, 0, unroll=2)
    acc = acc_ref[...]
    o_ref[...] = (acc[:, 0:MLA_DV] / acc[:, MLA_DV:]).astype(o_ref.dtype)


def mla_sample_attention(q_m, kv, kpe, kv_ctx, kpe_ctx, row0, batch, n, past, tq=1024, tk=512):
    h = MLA_HEADS
    qb = n // tq
    r0 = row0 // n
    q0 = row0 // tq
    return pl.pallas_call(
        functools.partial(_mla_sample_attn_kernel, tk=tk),
        grid=(batch, h, qb),
        in_specs=[pl.BlockSpec((tq, LANES), lambda b, hh, i: (q0 + b * qb + i, hh)),
                  pl.BlockSpec((tq, LANES), lambda b, hh, i: (q0 + b * qb + i, h + hh)),
                  pl.BlockSpec((past, LANES), lambda b, hh, i: (b, hh)),
                  pl.BlockSpec((past, LANES), lambda b, hh, i: (b, 0)),
                  pl.BlockSpec((past, LANES), lambda b, hh, i: (b, h + hh)),
                  pl.BlockSpec((n, LANES), lambda b, hh, i: (r0 + b, hh)),
                  pl.BlockSpec((n, LANES), lambda b, hh, i: (r0 + b, 0)),
                  pl.BlockSpec((n, LANES), lambda b, hh, i: (r0 + b, h + hh))],
        out_specs=pl.BlockSpec((tq, LANES), lambda b, hh, i: (b * qb + i, hh)),
        out_shape=jax.ShapeDtypeStruct((batch * n, h * MLA_DV), BF16),
        scratch_shapes=[pltpu.VMEM((tq, past + n), F32), pltpu.VMEM((tq, LANES), F32),
                        pltpu.VMEM((tq, MLA_DV + LANES), F32)],
        compiler_params=_cparams(("parallel", "parallel", "arbitrary")),
        name="mla_sample_attention",
    )(q_m, q_m, kv_ctx, kpe_ctx, kv_ctx, kv, kpe, kv)


def _na_prompt_attn_kernel(q_ref, k_ref, v_ref, o_ref, ko_ref, vo_ref, *, scale):
    for h in range(NA_HEADS):
        cols = slice(h * LANES, (h + 1) * LANES)
        k = k_ref[:, cols]
        v = v_ref[:, cols]
        ko_ref[h] = k
        vo_ref[h] = v
        q = (q_ref[:, cols] * scale).astype(BF16)
        o_ref[:, cols] = _softmax_pv(_dot_nt(q, k.astype(BF16)), v.astype(BF16)).astype(o_ref.dtype)


def na_prompt_attention(z, batch, n):
    h = NA_HEADS
    scale = float(NA_HD ** -0.5)
    w = h * NA_HD
    cq, ck, cv = CD_NQ // w, CD_NK // w, CD_NV // w
    return pl.pallas_call(
        functools.partial(_na_prompt_attn_kernel, scale=scale),
        grid=(batch,),
        in_specs=[pl.BlockSpec((n, w), lambda b: (b, cq)),
                  pl.BlockSpec((n, w), lambda b: (b, ck)),
                  pl.BlockSpec((n, w), lambda b: (b, cv))],
        out_specs=[pl.BlockSpec((n, w), lambda b: (b, 0)),
                   pl.BlockSpec((None, h, n, NA_HD), lambda b: (b, 0, 0, 0)),
                   pl.BlockSpec((None, h, n, NA_HD), lambda b: (b, 0, 0, 0))],
        out_shape=[jax.ShapeDtypeStruct((batch * n, w), BF16),
                   jax.ShapeDtypeStruct((batch, h, n, NA_HD), F32),
                   jax.ShapeDtypeStruct((batch, h, n, NA_HD), F32)],
        compiler_params=_cparams(("parallel",)),
        name="na_prompt_attention",
    )(z, z, z)


def _dot_tn(a, b):
    return lax.dot_general(a, b, (((0,), (0,)), ((), ())), preferred_element_type=F32)


def _head_norm_gate(acc, gate_in, gain):
    outs = []
    for h in range(2):
        a = acc[:, h * LANES:(h + 1) * LANES]
        outs.append(a * lax.rsqrt(jnp.mean(a * a, axis=-1, keepdims=True) + EPS))
    g = gate_in
    return jnp.concatenate(outs, axis=1) * gain * (g * jax.nn.sigmoid(g))


RET_CHUNK = 256
FIN_ROWS = 256


def _ret_kernel(ld_ref, q_ref, k_ref, v_ref, g_ref, cos_ref, sin_ref, gain_ref, s0_ref, o_ref, sf_ref, acc_ref, *, n):
    L = RET_CHUNK
    nc = n // L
    hp = pl.program_id(1)
    lane = lax.broadcasted_iota(jnp.int32, (1, LANES), 1)
    lo_lanes = lane < RET_DK
    row128 = lax.broadcasted_iota(jnp.int32, (LANES, 1), 0)
    ri = lax.broadcasted_iota(jnp.int32, (L, L), 0)
    ci = lax.broadcasted_iota(jnp.int32, (L, L), 1)
    dist = (ri - ci).astype(F32)
    pos = lax.broadcasted_iota(jnp.int32, (L, 1), 0).astype(F32)

    consts = []
    for dr in range(2):
        lg0 = ld_ref[dr, 2 * hp]
        lg1 = ld_ref[dr, 2 * hp + 1]
        lg2 = jnp.where(lo_lanes, lg0, lg1)
        lgc = jnp.where(row128 < RET_DK, lg0, lg1)
        sd = dist if dr == 0 else -dist
        ok = sd >= 0
        intra = [jnp.where(ok, jnp.exp(jnp.where(ok, sd, 0.0) * lg), 0.0) for lg in (lg0, lg1)]
        if dr == 0:
            inter = jnp.exp((pos + 1.0) * lg2)
            to_state = jnp.exp((L - 1.0 - pos) * lg2)
        else:
            inter = jnp.exp((L - pos) * lg2)
            to_state = jnp.exp(pos * lg2)
        consts.append((intra, inter, to_state, jnp.exp(float(L) * lgc)))

    def clear(i, carry):
        acc_ref[pl.ds(pl.multiple_of(i * FIN_ROWS, FIN_ROWS), FIN_ROWS), :] = jnp.zeros((FIN_ROWS, 2 * LANES), F32)
        return carry

    lax.fori_loop(0, n // FIN_ROWS, clear, 0)

    def step(c, dr, s2):
        intra, inter, to_state, cdec = consts[dr]
        rows = pl.ds(pl.multiple_of(c * L, L), L)
        cos, sin = cos_ref[rows, :], sin_ref[rows, :]
        q = _rope(q_ref[rows, :] * (RET_DK ** -0.5), cos, sin)
        k = _rope(k_ref[rows, :], cos, sin)
        kb = k.astype(BF16)
        s2b = s2.astype(BF16)
        kt = (k * to_state).astype(BF16)
        upd = []
        for h in range(2):
            qm = jnp.where(lo_lanes if h == 0 else ~lo_lanes, q, 0.0)
            vh = v_ref[rows, h * LANES:(h + 1) * LANES].astype(BF16)
            sc = _dot_nt(qm.astype(BF16), kb) * intra[h]
            o = _dot(sc.astype(BF16), vh) + _dot((qm * inter).astype(BF16), s2b)
            acc_ref[rows, h * LANES:(h + 1) * LANES] += o
            upd.append(_dot_tn(kt, vh))
        return cdec * s2 + jnp.where(row128 < RET_DK, upd[0], upd[1])

    def body(i, carry):
        sf, sb = carry
        return step(i, 0, sf), step(nc - 1 - i, 1, sb)

    sf, sb = lax.fori_loop(0, nc, body, (s0_ref[0], s0_ref[1]))
    sf_ref[0] = sf
    sf_ref[1] = sb

    def fin(i, carry):
        rows = pl.ds(pl.multiple_of(i * FIN_ROWS, FIN_ROWS), FIN_ROWS)
        o_ref[rows, :] = _head_norm_gate(acc_ref[rows, :], g_ref[rows, :], gain_ref[...]).astype(o_ref.dtype)
        return carry

    lax.fori_loop(0, n // FIN_ROWS, fin, 0)


def retention_mixer(z, log_decay, ret_gain, s0, cos_t, sin_t, row0, batch, n):
    hp = RET_HEADS // 2
    r0 = row0 // n
    cq, ck, cv, cg = CD_RQ // LANES, CD_RK // LANES, CD_RV // (2 * LANES), CD_RG // (2 * LANES)
    grid_spec = pltpu.PrefetchScalarGridSpec(
        num_scalar_prefetch=1,
        grid=(batch, hp),
        in_specs=[pl.BlockSpec((n, LANES), lambda b, p, ld: (r0 + b, cq + p)),
                  pl.BlockSpec((n, LANES), lambda b, p, ld: (r0 + b, ck + p)),
                  pl.BlockSpec((n, 2 * LANES), lambda b, p, ld: (r0 + b, cv + p)),
                  pl.BlockSpec((n, 2 * LANES), lambda b, p, ld: (r0 + b, cg + p)),
                  pl.BlockSpec((n, LANES), lambda b, p, ld: (0, 0)),
                  pl.BlockSpec((n, LANES), lambda b, p, ld: (0, 0)),
                  pl.BlockSpec((1, 2 * LANES), lambda b, p, ld: (0, p)),
                  pl.BlockSpec((None, 2, LANES, LANES), lambda b, p, ld: (b, 0, p, 0))],
        out_specs=[pl.BlockSpec((n, 2 * LANES), lambda b, p, ld: (b, p)),
                   pl.BlockSpec((None, 2, LANES, LANES), lambda b, p, ld: (b, 0, p, 0))],
        scratch_shapes=[pltpu.VMEM((n, 2 * LANES), F32)],
    )
    return pl.pallas_call(
        functools.partial(_ret_kernel, n=n),
        grid_spec=grid_spec,
        out_shape=[jax.ShapeDtypeStruct((batch * n, RET_HEADS * RET_DV), BF16),
                   jax.ShapeDtypeStruct((batch, 2, RET_HEADS * RET_DK, RET_DV), F32)],
        compiler_params=_cparams(("parallel", "parallel")),
        name="retention_mixer",
    )(log_decay.astype(F32), z, z, z, z, cos_t, sin_t, ret_gain.reshape(1, -1), s0)


GLA_CHUNK = 256
GLA_SUB = 64
GLA_SAFE_EXPONENT = 60.0


def _gla_kernel(q_ref, k_ref, v_ref, g_ref, misc_ref, gw_ref, gb_ref, gain_ref, s0_ref, o_ref, sf_ref,
                la_ref, acc_ref, *, n):
    C, SB = GLA_CHUNK, GLA_SUB
    nsb = C // SB
    nc = n // C
    lane = lax.broadcasted_iota(jnp.int32, (1, LANES), 1)
    lo_lanes = lane < GLA_DK
    head_mask = (lo_lanes, ~lo_lanes)
    ri = lax.broadcasted_iota(jnp.int32, (C, C), 0)
    ci = lax.broadcasted_iota(jnp.int32, (C, C), 1)
    rowc = lax.broadcasted_iota(jnp.int32, (C, 1), 0)
    row8 = lax.broadcasted_iota(jnp.int32, (8, 1), 0)
    tri = (jnp.where(ci <= ri, 1.0, 0.0).astype(BF16), jnp.where(ci >= ri, 1.0, 0.0).astype(BF16))
    causal = (ci <= ri, ci >= ri)

    def prep(i, carry):
        rows = pl.ds(pl.multiple_of(i * FIN_ROWS, FIN_ROWS), FIN_ROWS)
        slab = misc_ref[rows, :].astype(BF16)
        for dr in range(2):
            pre = _dot(slab, gw_ref[dr]) + gb_ref[dr]
            la_ref[dr, rows, :] = (jnp.minimum(pre, 0.0) - jnp.log1p(jnp.exp(-jnp.abs(pre)))) * (1.0 / GATE_TAU)
        acc_ref[rows, :] = jnp.zeros((FIN_ROWS, 2 * LANES), F32)
        return carry

    lax.fori_loop(0, n // FIN_ROWS, prep, 0)

    def step(c, dr):
        rev = dr == 1
        base = pl.multiple_of(c * C, C)
        rows = pl.ds(base, C)
        la = la_ref[dr, rows, :]
        hi = la.astype(BF16)
        lo = (la - hi.astype(F32)).astype(BF16)
        b = _dot(tri[dr], hi) + _dot(tri[dr], lo)
        zero = jnp.zeros((1, LANES), F32)
        if not rev:
            bref = [zero if i == 0 else b[SB * i - 1:SB * i, :] for i in range(nsb)]
            bend = [b[SB * i + SB - 1:SB * i + SB, :] for i in range(nsb)]
            blast = b[C - 1:C, :]
        else:
            bref = [zero if i == nsb - 1 else b[SB * (i + 1):SB * (i + 1) + 1, :] for i in range(nsb)]
            bend = [b[SB * i:SB * i + 1, :] for i in range(nsb)]
            blast = b[0:1, :]
        spread = bref[0] - bend[0]
        for i in range(1, nsb):
            spread = jnp.maximum(spread, bref[i] - bend[i])
        worst = jnp.max(spread)

        def fast(st):
            q = q_ref[rows, :] * (GLA_DK ** -0.5)
            k = k_ref[rows, :]
            a_rows = ([], [])
            for i in range(nsb):
                qe = q[SB * i:SB * (i + 1), :] * jnp.exp(b[SB * i:SB * (i + 1), :] - bref[i])
                valid = (rowc < SB * (i + 1)) if not rev else (rowc >= SB * i)
                ke = jnp.where(valid, k * jnp.exp(jnp.where(valid, bref[i] - b, 0.0)), 0.0).astype(BF16)
                for h in range(2):
                    a_rows[h].append(_dot_nt(jnp.where(head_mask[h], qe, 0.0).astype(BF16), ke))
            qi = q * jnp.exp(b)
            ks = (k * jnp.exp(blast - b)).astype(BF16)
            stb = st.astype(BF16)
            upd = []
            for h in range(2):
                a = jnp.where(causal[dr], jnp.concatenate(a_rows[h], axis=0), 0.0)
                vh = v_ref[rows, h * LANES:(h + 1) * LANES].astype(BF16)
                o = _dot(a.astype(BF16), vh) + _dot_nt(jnp.where(head_mask[h], qi, 0.0).astype(BF16), stb)
                acc_ref[rows, h * LANES:(h + 1) * LANES] += o
                upd.append(_dot_tn(vh, ks))
            return jnp.exp(blast) * st + jnp.where(lo_lanes, upd[0], upd[1])

        def slow(st):
            def row_step(r, st):
                t = r if not rev else C - 1 - r
                slab = pl.ds(pl.multiple_of(base + (t // 8) * 8, 8), 8)
                pick = row8 == (t % 8)
                q8 = jnp.where(pick, q_ref[slab, :] * (GLA_DK ** -0.5), 0.0)
                k8 = jnp.where(pick, k_ref[slab, :], 0.0).astype(BF16)
                la_t = jnp.sum(jnp.where(pick, la_ref[dr, slab, :], 0.0), axis=0, keepdims=True)
                upd = []
                for h in range(2):
                    v8 = jnp.where(pick, v_ref[slab, h * LANES:(h + 1) * LANES], 0.0).astype(BF16)
                    upd.append(_dot_tn(v8, k8))
                st = jnp.exp(la_t) * st + jnp.where(lo_lanes, upd[0], upd[1])
                stb = st.astype(BF16)
                for h in range(2):
                    qh = jnp.where(head_mask[h], q8, 0.0).astype(BF16)
                    acc_ref[slab, h * LANES:(h + 1) * LANES] += _dot_nt(qh, stb)
                return st

            return lax.fori_loop(0, C + 0 * pl.program_id(0), row_step, st)

        return worst, fast, slow

    def body(i, carry):
        worst_f, fast_f, slow_f = step(i, 0)
        worst_b, fast_b, slow_b = step(nc - 1 - i, 1)
        return lax.cond(jnp.maximum(worst_f, worst_b) <= GLA_SAFE_EXPONENT,
                        lambda st: (fast_f(st[0]), fast_b(st[1])),
                        lambda st: (slow_f(st[0]), slow_b(st[1])), carry)

    sf, sb = lax.fori_loop(0, nc, body, (s0_ref[0].T, s0_ref[1].T))
    sf_ref[0] = sf.T
    sf_ref[1] = sb.T

    def fin(i, carry):
        rows = pl.ds(pl.multiple_of(i * FIN_ROWS, FIN_ROWS), FIN_ROWS)
        o_ref[rows, :] = _head_norm_gate(acc_ref[rows, :], g_ref[rows, :], gain_ref[...]).astype(o_ref.dtype)
        return carry

    lax.fori_loop(0, n // FIN_ROWS, fin, 0)


def _prep_gate(gate_w, gate_b):
    gw = jnp.zeros((2, LANES, gate_w.shape[-1]), F32)
    for dr in range(2):
        lo = MLA_ROPE + dr * GATE_RANK
        gw = gw.at[dr, lo:lo + GATE_RANK, :].set(gate_w[dr].astype(F32))
    return gw.astype(BF16), gate_b.astype(F32).reshape(2, 1, -1)


def gla_mixer(z, gate_w, gate_b, gla_gain, s0, row0, batch, n):
    hp = GLA_HEADS // 2
    r0 = row0 // n
    gw, gb = _prep_gate(gate_w, gate_b)
    cq, ck, cv, cg, cm = AB_GQ // LANES, AB_GK // LANES, AB_GV // (2 * LANES), AB_GR // (2 * LANES), AB_MISC // LANES
    return pl.pallas_call(
        functools.partial(_gla_kernel, n=n),
        grid=(batch, hp),
        in_specs=[pl.BlockSpec((n, LANES), lambda b, p: (r0 + b, cq + p)),
                  pl.BlockSpec((n, LANES), lambda b, p: (r0 + b, ck + p)),
                  pl.BlockSpec((n, 2 * LANES), lambda b, p: (r0 + b, cv + p)),
                  pl.BlockSpec((n, 2 * LANES), lambda b, p: (r0 + b, cg + p)),
                  pl.BlockSpec((n, LANES), lambda b, p: (r0 + b, cm)),
                  pl.BlockSpec((2, LANES, LANES), lambda b, p: (0, 0, p)),
                  pl.BlockSpec((2, 1, LANES), lambda b, p: (0, 0, p)),
                  pl.BlockSpec((1, 2 * LANES), lambda b, p: (0, p)),
                  pl.BlockSpec((None, 2, LANES, LANES), lambda b, p: (b, 0, p, 0))],
        out_specs=[pl.BlockSpec((n, 2 * LANES), lambda b, p: (b, p)),
                   pl.BlockSpec((None, 2, LANES, LANES), lambda b, p: (b, 0, p, 0))],
        out_shape=[jax.ShapeDtypeStruct((batch * n, GLA_HEADS * GLA_DV), BF16),
                   jax.ShapeDtypeStruct((batch, 2, GLA_HEADS * GLA_DK, GLA_DV), F32)],
        scratch_shapes=[pltpu.VMEM((2, n, LANES), F32), pltpu.VMEM((n, 2 * LANES), F32)],
        compiler_params=_cparams(("parallel", "parallel")),
        name="gla_mixer",
    )(z, z, z, z, z, gw, gb, gla_gain.reshape(1, -1), s0)


NA_QROWS = 8
NA_WROWS = 16
NA_NEG = -1e30
NA_TAB = 32


def _na_bias_table(na_bias_l):
    qc = np.arange(GRID_W)[:, None]
    kc = np.arange(GRID_W)[None, :]
    ws = np.clip(qc - NA_COLS // 2, 0, GRID_W - NA_COLS)
    col_ok = (kc >= ws) & (kc < ws + NA_COLS)
    col_off = np.clip(kc - qc + NA_COLS - 1, 0, 2 * NA_COLS - 2)
    aa = (np.arange(NA_TAB) - NA_QROWS)[:, None] + np.arange(2)[None, :]
    n_row, n_col = 2 * NA_ROWS - 1, 2 * NA_COLS - 1
    sel_row = (aa[:, :, None] == np.arange(n_row)[None, None, :]).astype(np.float32)
    sel_col = ((col_off[:, :, None] == np.arange(n_col)[None, None, :]) & col_ok[:, :, None]).astype(np.float32)
    fill = np.where(col_ok, 0.0, NA_NEG).astype(np.float32)
    tab = jnp.einsum('hac,sea,qkc->hsqek', na_bias_l.astype(F32), jnp.asarray(sel_row), jnp.asarray(sel_col),
                     precision=lax.Precision.HIGHEST)
    tab = tab + jnp.asarray(fill)[None, None, :, None, :]
    h = na_bias_l.shape[0]
    return tab.reshape(h, NA_TAB, GRID_W, 2 * GRID_W)


def _na_latent_kernel(q_ref, k_ref, v_ref, kc_ref, vc_ref, tab_ref, o_ref, s_ref, *, scale, rows_total):
    g = pl.program_id(2)
    r0 = g * NA_QROWS
    start = jnp.clip(r0 - NA_ROWS // 2, 0, rows_total - NA_WROWS)
    nloc = NA_WROWS * GRID_W
    krows = pl.ds(pl.multiple_of(start * GRID_W, GRID_W), nloc)
    q = (q_ref[...] * scale).astype(BF16)
    kw = k_ref[krows, :].astype(BF16)
    s_ref[:, 0:nloc] = _dot_nt(q, kw)
    s_ref[:, nloc:] = _dot_nt(q, kc_ref[...].astype(BF16))
    lane = lax.broadcasted_iota(jnp.int32, (1, LANES), 1)
    for qr in range(NA_QROWS):
        r = r0 + qr
        rs = jnp.clip(r - NA_ROWS // 2, 0, rows_total - NA_ROWS)
        for kp in range(NA_WROWS // 2):
            k0 = start + 2 * kp
            ok0 = (k0 >= rs) & (k0 < rs + NA_ROWS)
            ok1 = (k0 + 1 >= rs) & (k0 + 1 < rs + NA_ROWS)
            neg = jnp.where(lane < GRID_W, jnp.where(ok0, 0.0, NA_NEG), jnp.where(ok1, 0.0, NA_NEG))
            slot = k0 - r + (NA_ROWS - 1) + NA_QROWS
            s_ref[qr * GRID_W:(qr + 1) * GRID_W, kp * LANES:(kp + 1) * LANES] += tab_ref[slot] + neg
    s = s_ref[...]
    m = jnp.max(s, axis=-1, keepdims=True)
    p = jnp.exp(s - m)
    l = jnp.sum(p, axis=-1, keepdims=True)
    pb = p.astype(BF16)
    o = _dot(pb[:, 0:nloc], v_ref[krows, :].astype(BF16)) + _dot(pb[:, nloc:], vc_ref[...].astype(BF16))
    o_ref[...] = (o / l).astype(o_ref.dtype)


def na_latent_attention(z, k_ctx, v_ctx, na_bias_l, row0, batch, n):
    h = NA_HEADS
    past = k_ctx.shape[2]
    rows_total = n // GRID_W
    tq = NA_QROWS * GRID_W
    qb = n // tq
    r0 = row0 // n
    q0 = row0 // tq
    cq, ck, cv = CD_NQ // LANES, CD_NK // LANES, CD_NV // LANES
    tab = _na_bias_table(na_bias_l)
    return pl.pallas_call(
        functools.partial(_na_latent_kernel, scale=float(NA_HD ** -0.5), rows_total=rows_total),
        grid=(batch, h, qb),
        in_specs=[pl.BlockSpec((tq, LANES), lambda b, hh, g: (q0 + b * qb + g, cq + hh)),
                  pl.BlockSpec((n, LANES), lambda b, hh, g: (r0 + b, ck + hh)),
                  pl.BlockSpec((n, LANES), lambda b, hh, g: (r0 + b, cv + hh)),
                  pl.BlockSpec((None, None, past, NA_HD), lambda b, hh, g: (b, hh, 0, 0)),
                  pl.BlockSpec((None, None, past, NA_HD), lambda b, hh, g: (b, hh, 0, 0)),
                  pl.BlockSpec((None, NA_TAB, GRID_W, 2 * GRID_W), lambda b, hh, g: (hh, 0, 0, 0))],
        out_specs=pl.BlockSpec((tq, LANES), lambda b, hh, g: (b * qb + g, hh)),
        out_shape=jax.ShapeDtypeStruct((batch * n, h * NA_HD), BF16),
        scratch_shapes=[pltpu.VMEM((tq, NA_WROWS * GRID_W + past), F32)],
        compiler_params=_cparams(("parallel", "parallel", "arbitrary")),
        name="na_latent_attention",
    )(z, z, z, k_ctx, v_ctx, tab)


def _prep_w_in_ab(w):
    d = w.shape[0]
    sizes = (512, 512, 1024, 1024, 2 * GATE_RANK, MLA_Q_LORA, MLA_KV_LORA, MLA_ROPE)
    gq, gk, gv, gr, glr, cq, ckv, kpe = jnp.split(w, np.cumsum(sizes)[:-1].tolist(), axis=1)
    pad = jnp.zeros((d, AB_WIDTH - AB_MISC - MLA_ROPE - 2 * GATE_RANK), w.dtype)
    return jnp.concatenate([gq, gk, gv, gr, cq, ckv, kpe, glr, pad], axis=1).astype(BF16)


def _prep_w_uq(w):
    r = w.shape[0]
    w3 = w.reshape(r, MLA_HEADS, MLA_NOPE + MLA_ROPE)
    nope = w3[:, :, :MLA_NOPE].reshape(r, MLA_HEADS * MLA_NOPE)
    pe = jnp.pad(w3[:, :, MLA_NOPE:], ((0, 0), (0, 0), (0, LANES - MLA_ROPE))).reshape(r, MLA_HEADS * LANES)
    return jnp.concatenate([nope, pe], axis=1).astype(BF16)


def _prep_w_ukv(w):
    r = w.shape[0]
    w3 = w.reshape(r, MLA_HEADS, MLA_NOPE + MLA_DV)
    return jnp.concatenate([w3[:, :, :MLA_NOPE].reshape(r, -1), w3[:, :, MLA_NOPE:].reshape(r, -1)], axis=1).astype(BF16)


def _prep_router(router_w):
    d = router_w.shape[0]
    w = jnp.zeros((d, LANES), F32).at[:, :N_EXPERTS].set(router_w.astype(F32))
    hi = w.astype(BF16)
    lo = (w - hi.astype(F32)).astype(BF16)
    return jnp.stack([hi, lo], axis=0)


def kernel(x_prompt, x_sample, state_gla, cache_mla_ckv, cache_mla_kpe, state_ret, cache_na_k, cache_na_v,
           c, c_ctx, ada_w, ada_b, norm1, norm2, w_in_ab, gla_gate_w, gla_gate_b, gla_gain, mla_q_gain,
           mla_w_uq, mla_kv_gain, mla_w_ukv, w_in_cd, ret_log_decay, ret_gain, na_bias, w_out,
           router_w, router_bias, exp_w1, exp_w3, exp_w2, final_norm):
    bp, n_p, d = x_prompt.shape
    bs, n_s, _ = x_sample.shape
    past = cache_mla_ckv.shape[2]
    depth = ada_w.shape[0]
    tp, ts = bp * n_p, bs * n_s
    ys = [x_prompt.reshape(tp, d), x_sample.reshape(ts, d)]

    cond8 = jnp.zeros((8, d), F32).at[0].set(c_ctx).at[1:1 + bs].set(c)
    mod = adaln_all(cond8, ada_w, ada_b)
    rw_split = _prep_router(router_w)
    w_out_bf = w_out.astype(BF16)
    cos_t, sin_t = rope_tables(n_s, 512)
    cos_s, sin_s = cos_t[512:], sin_t[512:]
    cos_id, sin_id = cos_t[:n_p], sin_t[:n_p]

    outs = {}
    for l in range(depth):
        i = l // 2
        mod_l = mod[l].reshape(8, 1, 6 * d)
        final = l == depth - 1
        if l % 2 == 0:
            z = win_project(ys, mod_l, norm1[l], _prep_w_in_ab(w_in_ab[i]), tp, n_s)
            s0p = jnp.zeros((bp, 2, GLA_HEADS * GLA_DK, GLA_DV), F32)
            s0s = state_gla[:, i].reshape(bs, 2, GLA_HEADS * GLA_DK, GLA_DV)
            og_p, st_p = gla_mixer(z, gla_gate_w[i], gla_gate_b[i], gla_gain[i], s0p, 0, bp, n_p)
            og_s, _ = gla_mixer(z, gla_gate_w[i], gla_gate_b[i], gla_gain[i], s0s, tp, bs, n_s)
            st_p = st_p.reshape(bp, 2, GLA_HEADS, GLA_DK, GLA_DV)
            o_a = (og_p, og_s)
            wq, wkv = _prep_w_uq(mla_w_uq[i]), _prep_w_ukv(mla_w_ukv[i])
            q_m, kv, ckvn, kpe = mla_project(z, mla_q_gain[i], mla_kv_gain[i], wq, wkv, cos_t, sin_t, tp, n_s)
            kv_ctx = ctx_kv_project(cache_mla_ckv[:, i].reshape(bs * past, -1), wkv)
            kpe_ctx = jnp.pad(cache_mla_kpe[:, i].reshape(bs * past, -1), ((0, 0), (0, LANES - MLA_ROPE)))
            om_p = mla_prompt_attention(q_m, kv, kpe, bp, n_p)
            om_s = mla_sample_attention(q_m, kv, kpe, kv_ctx, kpe_ctx, tp, bs, n_s, past)
            o_b = (om_p, om_s)
            outs.setdefault('gla', []).append(st_p)
            outs.setdefault('ckv', []).append(ckvn[:tp].reshape(bp, n_p, -1))
            outs.setdefault('kpe', []).append(z[:tp, AB_MISC:AB_MISC + MLA_ROPE].reshape(bp, n_p, -1))
        else:
            z = win_project(ys, mod_l, norm1[l], w_in_cd[i].astype(BF16), tp, n_s)
            s0p = jnp.zeros((bp, 2, RET_HEADS * RET_DK, RET_DV), F32)
            s0s = state_ret[:, i].reshape(bs, 2, RET_HEADS * RET_DK, RET_DV)
            or_p, st_p = retention_mixer(z, ret_log_decay[i], ret_gain[i], s0p, cos_id, sin_id, 0, bp, n_p)
            or_s, _ = retention_mixer(z, ret_log_decay[i], ret_gain[i], s0s, cos_s, sin_s, tp, bs, n_s)
            st_p = st_p.reshape(bp, 2, RET_HEADS, RET_DK, RET_DV)
            o_a = (or_p, or_s)
            on_p, kn, vn = na_prompt_attention(z, bp, n_p)
            on_s = na_latent_attention(z, cache_na_k[:, i], cache_na_v[:, i], na_bias[i], tp, bs, n_s)
            o_b = (on_p, on_s)
            outs.setdefault('ret', []).append(st_p)
            outs.setdefault('nak', []).append(kn)
            outs.setdefault('nav', []).append(vn)
        y1, h2, logits = wout_project(o_a, o_b, ys, mod_l, w_out_bf[l], norm2[l], rw_split, tp, n_s)
        y = moe_layer(y1, h2, logits, mod_l, router_bias, exp_w1, exp_w3, exp_w2, l, final_norm, tp, n_s, final)
        ys = list(y) if final else [y]

    return (ys[0].reshape(bp, n_p, d), ys[1].reshape(bs, n_s, d),
            jnp.stack(outs['gla'], axis=1), jnp.stack(outs['ckv'], axis=1), jnp.stack(outs['kpe'], axis=1),
            jnp.stack(outs['ret'], axis=1), jnp.stack(outs['nak'], axis=1), jnp.stack(outs['nav'], axis=1))
```

```python
import functools

import numpy as np
import jax
import jax.numpy as jnp
from jax import lax
from jax.experimental import pallas as pl
from jax.experimental.pallas import tpu as pltpu

F32 = jnp.float32
BF16 = jnp.bfloat16

EPS = 1e-6
GRID_W = 64
ROPE_BASE = 10000.0
GATE_RANK = 16
GATE_TAU = 16.0
GLA_HEADS, GLA_DK, GLA_DV = 8, 64, 128
MLA_HEADS, MLA_Q_LORA, MLA_KV_LORA, MLA_NOPE, MLA_ROPE, MLA_DV = 8, 512, 256, 128, 64, 128
RET_HEADS, RET_DK, RET_DV = 8, 64, 128
NA_HEADS, NA_HD, NA_ROWS, NA_COLS = 8, 128, 8, 16
N_EXPERTS, N_GROUPS, TOP_K, D_EXPERT = 32, 8, 2, 512
EXPERTS_PER_GROUP = N_EXPERTS // N_GROUPS
MOE_BLOCK = 128
LANES = 128

AB_GQ, AB_GK, AB_GV, AB_GR, AB_CQ, AB_CKV, AB_MISC, AB_WIDTH = 0, 512, 1024, 2048, 3072, 3584, 3840, 4096
CD_RQ, CD_RK, CD_RV, CD_RG, CD_NQ, CD_NK, CD_NV, CD_WIDTH = 0, 512, 1024, 2048, 3072, 4096, 5120, 6144

VMEM_LIMIT = 56 * 1024 * 1024


def _cparams(sem):
    return pltpu.CompilerParams(dimension_semantics=sem, vmem_limit_bytes=VMEM_LIMIT)


def _dot(a, b):
    return jnp.dot(a, b, preferred_element_type=F32)


def _dot_nt(a, b):
    return lax.dot_general(a, b, (((1,), (1,)), ((), ())), preferred_element_type=F32)


def _adaln_kernel(c_ref, w_ref, b_ref, o_ref):
    c = c_ref[...]
    s = (c * jax.nn.sigmoid(c)).astype(BF16)
    o_ref[...] = _dot(s, w_ref[...].astype(BF16)) + b_ref[...]


def adaln_all(cond8, ada_w, ada_b, tn=512):
    depth, d, n6 = ada_w.shape
    return pl.pallas_call(
        _adaln_kernel,
        grid=(depth, n6 // tn),
        in_specs=[pl.BlockSpec((8, d), lambda l, j: (0, 0)),
                  pl.BlockSpec((None, d, tn), lambda l, j: (l, 0, j)),
                  pl.BlockSpec((None, 1, tn), lambda l, j: (l, 0, j))],
        out_specs=pl.BlockSpec((None, 8, tn), lambda l, j: (l, 0, j)),
        out_shape=jax.ShapeDtypeStruct((depth, 8, n6), F32),
        compiler_params=_cparams(("parallel", "parallel")),
        name="adaln",
    )(cond8, ada_w, ada_b.reshape(depth, 1, n6))


def _mod_index(i, tm, n_prompt_rows, sample_rows):
    tp = n_prompt_rows // tm
    return jnp.where(i < tp, 0, 1 + (i - tp) // (sample_rows // tm))


def _split_specs(block, tiles_prompt, n_grid_axes=1):
    if n_grid_axes == 1:
        return [pl.BlockSpec(block, lambda i: (jnp.minimum(i, tiles_prompt - 1), 0)),
                pl.BlockSpec(block, lambda i: (jnp.maximum(i - tiles_prompt, 0), 0))]
    return [pl.BlockSpec(block, lambda i, j: (jnp.minimum(i, tiles_prompt - 1), 0)),
            pl.BlockSpec(block, lambda i, j: (jnp.maximum(i - tiles_prompt, 0), 0))]


def _win_kernel(*refs, d, tiles_prompt, split):
    if split:
        xp_ref, xs_ref, mod_ref, g_ref, w_ref, o_ref, h_ref = refs
    else:
        xp_ref, mod_ref, g_ref, w_ref, o_ref, h_ref = refs

    def norm_mod(x_ref):
        x = x_ref[...]
        xn = x * lax.rsqrt(jnp.mean(x * x, axis=-1, keepdims=True) + EPS) * g_ref[...]
        h = xn * (1.0 + mod_ref[:, d:2 * d]) + mod_ref[:, 0:d]
        h_ref[...] = h.astype(BF16)

    first = pl.program_id(1) == 0
    if split:
        is_prompt = pl.program_id(0) < tiles_prompt
        pl.when(first & is_prompt)(lambda: norm_mod(xp_ref))
        pl.when(first & jnp.logical_not(is_prompt))(lambda: norm_mod(xs_ref))
    else:
        pl.when(first)(lambda: norm_mod(xp_ref))

    o_ref[...] = _dot(h_ref[...], w_ref[...]).astype(o_ref.dtype)


def win_project(xs, mod_l, gain, w_bf16, n_prompt_rows, sample_rows, tm=1024, tn=512):
    split = len(xs) == 2
    d = xs[0].shape[1]
    t = sum(x.shape[0] for x in xs)
    width = w_bf16.shape[1]
    tiles_prompt = n_prompt_rows // tm
    midx = functools.partial(_mod_index, tm=tm, n_prompt_rows=n_prompt_rows, sample_rows=sample_rows)
    x_specs = (_split_specs((tm, d), tiles_prompt, 2) if split else [pl.BlockSpec((tm, d), lambda i, j: (i, 0))])
    return pl.pallas_call(
        functools.partial(_win_kernel, d=d, tiles_prompt=tiles_prompt, split=split),
        grid=(t // tm, width // tn),
        in_specs=x_specs + [pl.BlockSpec((None, 1, 6 * d), lambda i, j: (midx(i), 0, 0)),
                            pl.BlockSpec((1, d), lambda i, j: (0, 0)),
                            pl.BlockSpec((d, tn), lambda i, j: (0, j))],
        out_specs=pl.BlockSpec((tm, tn), lambda i, j: (i, j)),
        out_shape=jax.ShapeDtypeStruct((t, width), BF16),
        scratch_shapes=[pltpu.VMEM((tm, d), BF16)],
        compiler_params=_cparams(("parallel", "arbitrary")),
        name="win_project",
    )(*xs, mod_l, gain.reshape(1, d), w_bf16)


def _wout_kernel(*refs, d, tiles_prompt, split_y):
    if split_y:
        (oap_ref, oas_ref, obp_ref, obs_ref, yp_ref, ys_ref, mod_ref, wa_ref, wb_ref, g_ref, rw_ref,
         ynew_ref, h2_ref, lg_ref) = refs
    else:
        (oap_ref, oas_ref, obp_ref, obs_ref, yp_ref, mod_ref, wa_ref, wb_ref, g_ref, rw_ref,
         ynew_ref, h2_ref, lg_ref) = refs
        ys_ref = yp_ref
    is_prompt = pl.program_id(0) < tiles_prompt
    oa = jnp.where(is_prompt, oap_ref[...], oas_ref[...])
    ob = jnp.where(is_prompt, obp_ref[...], obs_ref[...])
    y = jnp.where(is_prompt, yp_ref[...], ys_ref[...]) if split_y else yp_ref[...]
    out = _dot(oa, wa_ref[...]) + _dot(ob, wb_ref[...])
    y1 = y + mod_ref[:, 2 * d:3 * d] * out
    ynew_ref[...] = y1
    xn = y1 * lax.rsqrt(jnp.mean(y1 * y1, axis=-1, keepdims=True) + EPS) * g_ref[...]
    h = xn * (1.0 + mod_ref[:, 4 * d:5 * d]) + mod_ref[:, 3 * d:4 * d]
    h2_ref[...] = h
    hh = h.astype(BF16)
    hl = (h - hh.astype(F32)).astype(BF16)
    lg_ref[...] = _dot(hh, rw_ref[0]) + _dot(hl, rw_ref[0]) + _dot(hh, rw_ref[1])


def wout_project(o_a, o_b, ys, mod_l, w_out_bf16, gain2, rw_split, n_prompt_rows, sample_rows, tm=256):
    split_y = len(ys) == 2
    d = ys[0].shape[1]
    t = sum(x.shape[0] for x in ys)
    ka, kb = o_a[0].shape[1], o_b[0].shape[1]
    tiles_prompt = n_prompt_rows // tm
    midx = functools.partial(_mod_index, tm=tm, n_prompt_rows=n_prompt_rows, sample_rows=sample_rows)
    y_specs = _split_specs((tm, d), tiles_prompt) if split_y else [pl.BlockSpec((tm, d), lambda i: (i, 0))]
    return pl.pallas_call(
        functools.partial(_wout_kernel, d=d, tiles_prompt=tiles_prompt, split_y=split_y),
        grid=(t // tm,),
        in_specs=_split_specs((tm, ka), tiles_prompt) + _split_specs((tm, kb), tiles_prompt) + y_specs + [
                  pl.BlockSpec((None, 1, 6 * d), lambda i: (midx(i), 0, 0)),
                  pl.BlockSpec((ka, d), lambda i: (0, 0)),
                  pl.BlockSpec((kb, d), lambda i: (1, 0)),
                  pl.BlockSpec((1, d), lambda i: (0, 0)),
                  pl.BlockSpec((2, d, LANES), lambda i: (0, 0, 0))],
        out_specs=[pl.BlockSpec((tm, d), lambda i: (i, 0)),
                   pl.BlockSpec((tm, d), lambda i: (i, 0)),
                   pl.BlockSpec((tm, LANES), lambda i: (i, 0))],
        out_shape=[jax.ShapeDtypeStruct((t, d), F32),
                   jax.ShapeDtypeStruct((t, d), F32),
                   jax.ShapeDtypeStruct((t, LANES), F32)],
        compiler_params=_cparams(("parallel",)),
        name="wout_project",
    )(*o_a, *o_b, *ys, mod_l, w_out_bf16, w_out_bf16, gain2.reshape(1, d), rw_split)


def _member_major(x_experts_last):
    shp = x_experts_last.shape[:-1]
    return x_experts_last.reshape(shp + (N_GROUPS, EXPERTS_PER_GROUP)).swapaxes(-1, -2).reshape(shp + (N_EXPERTS,))


def _route_kernel(lg_ref, bias_ref, info_ref, cnt_ref, run_ref, *, tm):
    @pl.when(pl.program_id(0) == 0)
    def _():
        run_ref[...] = jnp.zeros_like(run_ref)

    ng, nk = N_GROUPS, EXPERTS_PER_GROUP
    neg = jnp.float32(-jnp.inf)
    lgt = lg_ref[...].T
    aff = [jax.nn.sigmoid(lgt[k * ng:(k + 1) * ng, :]) for k in range(nk)]
    sel = [aff[k] + bias_ref[k * ng:(k + 1) * ng, :] for k in range(nk)]

    def first_max(vals):
        m = vals[0]
        for v in vals[1:]:
            m = jnp.maximum(m, v)
        idx = jnp.full(m.shape, float(nk), F32)
        for k in reversed(range(nk)):
            idx = jnp.where(vals[k] == m, float(k), idx)
        return m, idx

    m1, i1 = first_max(sel)
    m2, i2 = first_max([jnp.where(i1 == k, neg, sel[k]) for k in range(nk)])
    gscore = m1 + m2
    gid = lax.broadcasted_iota(jnp.int32, gscore.shape, 0).astype(F32)
    gmax = jnp.max(gscore, axis=0, keepdims=True)
    g = jnp.min(jnp.where(gscore == gmax, gid, 99.0), axis=0, keepdims=True)
    in_g = gid == g

    def pick(masks, vals):
        tot = jnp.zeros((1, tm), F32)
        for msk, v in zip(masks, vals):
            tot = tot + jnp.sum(jnp.where(msk, v, 0.0), axis=0, keepdims=True)
        return tot

    hit1 = [in_g & (i1 == k) for k in range(nk)]
    hit2 = [in_g & (i2 == k) for k in range(nk)]
    a1, a2 = pick(hit1, aff), pick(hit2, aff)
    den = a1 + a2
    e1 = g * nk + pick([in_g], [i1])
    e2 = g * nk + pick([in_g], [i2])
    oh = jnp.concatenate([jnp.where(hit1[k] | hit2[k], 1.0, 0.0) for k in range(nk)], axis=0)
    r_i = lax.broadcasted_iota(jnp.int32, (tm, tm), 0)
    c_i = lax.broadcasted_iota(jnp.int32, (tm, tm), 1)
    earlier = jnp.where(r_i < c_i, 1.0, 0.0).astype(BF16)
    cum = _dot(oh.astype(BF16), earlier) + run_ref[:, 0:1]
    cums = [cum[k * ng:(k + 1) * ng, :] for k in range(nk)]
    r1, r2 = pick(hit1, cums), pick(hit2, cums)
    run_new = run_ref[:, 0:1] + jnp.sum(oh, axis=1, keepdims=True)
    run_ref[...] = jnp.broadcast_to(run_new, run_ref.shape)
    cnt_ref[...] = jnp.broadcast_to(run_new, cnt_ref.shape)
    info_ref[...] = jnp.concatenate([e1, e2, a1 / den, a2 / den, r1, r2, jnp.zeros((2, tm), F32)], axis=0)


def route(logits, router_bias, tm=256):
    t = logits.shape[0]
    bias = jnp.broadcast_to(_member_major(router_bias.astype(F32))[:, None], (N_EXPERTS, tm))
    return pl.pallas_call(
        functools.partial(_route_kernel, tm=tm),
        grid=(t // tm,),
        in_specs=[pl.BlockSpec((tm, LANES), lambda i: (i, 0)),
                  pl.BlockSpec((N_EXPERTS, tm), lambda i: (0, 0))],
        out_specs=[pl.BlockSpec((8, tm), lambda i: (0, i)),
                   pl.BlockSpec((N_EXPERTS, LANES), lambda i: (0, 0))],
        out_shape=[jax.ShapeDtypeStruct((8, t), F32),
                   jax.ShapeDtypeStruct((N_EXPERTS, LANES), F32)],
        scratch_shapes=[pltpu.VMEM((N_EXPERTS, LANES), F32)],
        compiler_params=_cparams(("arbitrary",)),
        name="route",
    )(logits, bias)


def _expert_kernel(be_ref, nu_ref, tok_ref, tok_next_ref, h_hbm, w1_ref, w3_ref, w2_ref, o_ref,
                   xbuf, w1b, w3b, w2b, sem):
    i = pl.program_id(0)
    slot = i % 2
    nu = nu_ref[0]

    def gather(idx_ref, s):
        for j in range(MOE_BLOCK):
            pltpu.make_async_copy(h_hbm.at[pl.ds(idx_ref[j], 1)], xbuf.at[s, pl.ds(j, 1)], sem.at[s]).start()

    def wait_block(s):
        pltpu.make_async_copy(h_hbm.at[pl.ds(0, MOE_BLOCK)], xbuf.at[s], sem.at[s]).wait()

    @pl.when(i == 0)
    def _():
        gather(tok_ref, 0)

    @pl.when(i < nu)
    def _():
        @pl.when((i == 0) | (be_ref[i] != be_ref[jnp.maximum(i - 1, 0)]))
        def _():
            w1b[...] = w1_ref[...].astype(BF16)
            w3b[...] = w3_ref[...].astype(BF16)
            w2b[...] = w2_ref[...].astype(BF16)

        wait_block(slot)
        gather(tok_next_ref, 1 - slot)
        x = xbuf[slot].astype(BF16)
        a = _dot(x, w1b[...])
        b = _dot(x, w3b[...])
        mid = (a * jax.nn.sigmoid(a)) * b
        o_ref[...] = _dot(mid.astype(BF16), w2b[...])

    @pl.when(i >= nu)
    def _():
        @pl.when(i == nu)
        def _():
            wait_block(slot)

        o_ref[...] = jnp.zeros_like(o_ref)


def expert_ffn(h2, slot_tok, block_expert, n_used, w1, w3, w2, layer):
    n_steps = slot_tok.shape[0] // MOE_BLOCK
    d, de = w1.shape[-2], w1.shape[-1]
    grid_spec = pltpu.PrefetchScalarGridSpec(
        num_scalar_prefetch=2,
        grid=(n_steps,),
        in_specs=[pl.BlockSpec((MOE_BLOCK,), lambda i, be, nu: (i,), memory_space=pltpu.SMEM),
                  pl.BlockSpec((MOE_BLOCK,), lambda i, be, nu: (jnp.minimum(i + 1, n_steps - 1),),
                               memory_space=pltpu.SMEM),
                  pl.BlockSpec(memory_space=pl.ANY),
                  pl.BlockSpec((None, None, d, de), lambda i, be, nu: (layer, be[i], 0, 0)),
                  pl.BlockSpec((None, None, d, de), lambda i, be, nu: (layer, be[i], 0, 0)),
                  pl.BlockSpec((None, None, de, d), lambda i, be, nu: (layer, be[i], 0, 0))],
        out_specs=pl.BlockSpec((MOE_BLOCK, d), lambda i, be, nu: (i, 0)),
        scratch_shapes=[pltpu.VMEM((2, MOE_BLOCK, d), F32), pltpu.VMEM((d, de), BF16), pltpu.VMEM((d, de), BF16),
                        pltpu.VMEM((de, d), BF16), pltpu.SemaphoreType.DMA((2,))],
    )
    return pl.pallas_call(
        _expert_kernel,
        grid_spec=grid_spec,
        out_shape=jax.ShapeDtypeStruct((n_steps * MOE_BLOCK, d), F32),
        compiler_params=_cparams(("arbitrary",)),
        name="expert_ffn",
    )(block_expert, n_used, slot_tok, slot_tok, h2, w1, w3, w2)


def _combine_kernel(dcur_ref, dnext_ref, y_ref, mod_ref, info_ref, g_ref, yb_hbm, *rest, tm, nt, d, final,
                    tiles_prompt):
    if final:
        op_ref, os_ref, buf, sem = rest
    else:
        o_ref, buf, sem = rest
    i = pl.program_id(0)
    slot = i % 2

    def row_copy(src_row, s, k, j):
        return pltpu.make_async_copy(yb_hbm.at[pl.ds(src_row, 1)], buf.at[s, k, pl.ds(j, 1)], sem.at[s])

    def issue(dref, s):
        def body(j, carry):
            for k in range(TOP_K):
                row_copy(dref[TOP_K * j + k], s, k, j).start()
            return carry

        lax.fori_loop(0, tm, body, 0, unroll=32)

    @pl.when(i == 0)
    def _():
        issue(dcur_ref, 0)

    @pl.when(i + 1 < nt)
    def _():
        issue(dnext_ref, 1 - slot)

    for k in range(TOP_K):
        pltpu.make_async_copy(yb_hbm.at[pl.ds(0, tm)], buf.at[slot, k], sem.at[slot]).wait()
    gates = info_ref[...]
    moe = gates[:, 0:1] * buf[slot, 0] + gates[:, 1:2] * buf[slot, 1]
    y2 = y_ref[...] + mod_ref[:, 5 * d:6 * d] * moe
    if not final:
        o_ref[...] = y2
        return
    y2 = y2 * lax.rsqrt(jnp.mean(y2 * y2, axis=-1, keepdims=True) + EPS) * g_ref[...]

    @pl.when(i < tiles_prompt)
    def _():
        op_ref[...] = y2

    @pl.when(i >= tiles_prompt)
    def _():
        os_ref[...] = y2


def combine(y1, mod_l, info, dest_flat, yb, final_gain, n_prompt_rows, sample_rows, final, tm=256):
    t, d = y1.shape
    nt = t // tm
    tiles_prompt = n_prompt_rows // tm
    midx = functools.partial(_mod_index, tm=tm, n_prompt_rows=n_prompt_rows, sample_rows=sample_rows)
    if final:
        out_specs = _split_specs((tm, d), tiles_prompt)
        out_shape = [jax.ShapeDtypeStruct((n_prompt_rows, d), F32), jax.ShapeDtypeStruct((t - n_prompt_rows, d), F32)]
    else:
        out_specs = pl.BlockSpec((tm, d), lambda i: (i, 0))
        out_shape = jax.ShapeDtypeStruct((t, d), F32)
    return pl.pallas_call(
        functools.partial(_combine_kernel, tm=tm, nt=nt, d=d, final=final, tiles_prompt=tiles_prompt),
        grid=(nt,),
        in_specs=[pl.BlockSpec((TOP_K * tm,), lambda i: (i,), memory_space=pltpu.SMEM),
                  pl.BlockSpec((TOP_K * tm,), lambda i: (jnp.minimum(i + 1, nt - 1),), memory_space=pltpu.SMEM),
                  pl.BlockSpec((tm, d), lambda i: (i, 0)),
                  pl.BlockSpec((None, 1, 6 * d), lambda i: (midx(i), 0, 0)),
                  pl.BlockSpec((tm, TOP_K), lambda i: (i, 0)),
                  pl.BlockSpec((1, d), lambda i: (0, 0)),
                  pl.BlockSpec(memory_space=pl.ANY)],
        out_specs=out_specs,
        out_shape=out_shape,
        scratch_shapes=[pltpu.VMEM((2, TOP_K, tm, d), F32), pltpu.SemaphoreType.DMA((2,))],
        compiler_params=_cparams(("arbitrary",)),
        name="combine",
    )(dest_flat, dest_flat, y1, mod_l, info, final_gain.reshape(1, d), yb)


def moe_layer(y1, h2p, logits, mod_l, router_bias, w1, w3, w2, layer, final_gain, n_prompt_rows, sample_rows, final):
    t, d = y1.shape
    info_t, cnt = route(logits, router_bias)
    expert = info_t[0:2].T.astype(jnp.int32)
    rank = info_t[4:6].T.astype(jnp.int32)
    info = info_t[2:4].T
    counts = cnt[:, 0].reshape(EXPERTS_PER_GROUP, N_GROUPS).T.reshape(-1).astype(jnp.int32)
    padded = (counts + MOE_BLOCK - 1) // MOE_BLOCK * MOE_BLOCK
    padded_end = jnp.cumsum(padded)
    padded_start = padded_end - padded
    dest = (padded_start[expert] + rank).reshape(-1)
    n_blocks = (t * TOP_K) // MOE_BLOCK + N_EXPERTS + 1
    slots = n_blocks * MOE_BLOCK
    block_lo = jnp.arange(n_blocks, dtype=jnp.int32)[:, None] * MOE_BLOCK
    block_expert = jnp.minimum(jnp.sum((padded_end[None, :] <= block_lo).astype(jnp.int32), axis=1), N_EXPERTS - 1)
    n_used = (padded_end[-1:] // MOE_BLOCK).astype(jnp.int32)
    tok = jnp.broadcast_to(jnp.arange(t, dtype=jnp.int32)[:, None], (t, TOP_K)).reshape(-1)
    slot_tok = jnp.zeros((slots,), jnp.int32).at[dest].set(tok)
    yb = expert_ffn(h2p, slot_tok, block_expert, n_used, w1, w3, w2, layer)
    return combine(y1, mod_l, info, dest, yb, final_gain, n_prompt_rows, sample_rows, final)


def rope_tables(n, lead):
    quarter = MLA_ROPE // 4
    tpos = np.arange(n)
    row = (tpos // GRID_W).astype(np.float32)
    col = (tpos % GRID_W).astype(np.float32)
    inv_freq = (np.float32(ROPE_BASE) ** (-np.arange(quarter, dtype=np.float32) / np.float32(quarter))).astype(np.float32)
    lane = np.arange(64)
    pos = np.where(lane[None, :] < 32, row[:, None], col[:, None]).astype(np.float32)
    ang = pos * inv_freq[lane % quarter][None, :]
    cos = np.cos(ang).astype(np.float32)
    sin = np.sin(ang).astype(np.float32)
    sgn = np.where((lane % 32) < quarter, -1.0, 1.0).astype(np.float32)
    sin = sin * sgn[None, :]
    cos = np.concatenate([cos, cos], axis=1)
    sin = np.concatenate([sin, sin], axis=1)
    cos = np.concatenate([np.ones((lead, LANES), np.float32), cos], axis=0)
    sin = np.concatenate([np.zeros((lead, LANES), np.float32), sin], axis=0)
    return jnp.asarray(cos), jnp.asarray(sin)


def _rope(x, cos, sin_signed):
    lane = lax.broadcasted_iota(jnp.int32, x.shape, 1)
    partner = jnp.where((lane & 31) < 16, pltpu.roll(x, LANES - 16, axis=1), pltpu.roll(x, 16, axis=1))
    return x * cos + partner * sin_signed


def _mla_proj_kernel(cq_ref, ckv_ref, misc_ref, qg_ref, kvg_ref, wq_ref, wkv_ref, cos_ref, sin_ref,
                     q_ref, kv_ref, ckvn_ref, kpe_ref, *, scale):
    cq = cq_ref[...].astype(F32)
    qn = cq * lax.rsqrt(jnp.mean(cq * cq, axis=-1, keepdims=True) + EPS) * qg_ref[...]
    qf = _dot(qn.astype(BF16), wq_ref[...])
    half = qf.shape[1] // 2
    cos, sin = cos_ref[...], sin_ref[...]
    q_ref[:, 0:half] = (qf[:, 0:half] * scale).astype(BF16)
    for h in range(MLA_HEADS):
        pe = qf[:, half + h * LANES: half + (h + 1) * LANES]
        q_ref[:, half + h * LANES: half + (h + 1) * LANES] = (_rope(pe, cos, sin) * scale).astype(BF16)
    ckv = ckv_ref[...].astype(F32)
    ckvn = ckv * lax.rsqrt(jnp.mean(ckv * ckv, axis=-1, keepdims=True) + EPS) * kvg_ref[...]
    ckvn_ref[...] = ckvn
    kv_ref[...] = _dot(ckvn.astype(BF16), wkv_ref[...]).astype(BF16)
    misc = misc_ref[...].astype(F32)
    lane = lax.broadcasted_iota(jnp.int32, misc.shape, 1)
    kpe = jnp.where(lane < MLA_ROPE, misc, 0.0)
    kpe_ref[...] = _rope(kpe, cos, sin).astype(BF16)


def mla_project(z, q_gain, kv_gain, wq, wkv, cos_t, sin_t, n_prompt_rows, sample_rows, tm=512):
    t = z.shape[0]
    tp = n_prompt_rows // tm
    per = sample_rows // tm

    def tab(i):
        return (jnp.where(i < tp, 0, 1 + (i - tp) % per), 0)

    scale = float((MLA_NOPE + MLA_ROPE) ** -0.5)
    nq = wq.shape[1]
    return pl.pallas_call(
        functools.partial(_mla_proj_kernel, scale=scale),
        grid=(t // tm,),
        in_specs=[pl.BlockSpec((tm, MLA_Q_LORA), lambda i: (i, AB_CQ // MLA_Q_LORA)),
                  pl.BlockSpec((tm, MLA_KV_LORA), lambda i: (i, AB_CKV // MLA_KV_LORA)),
                  pl.BlockSpec((tm, LANES), lambda i: (i, AB_MISC // LANES)),
                  pl.BlockSpec((1, MLA_Q_LORA), lambda i: (0, 0)),
                  pl.BlockSpec((1, MLA_KV_LORA), lambda i: (0, 0)),
                  pl.BlockSpec(wq.shape, lambda i: (0, 0)),
                  pl.BlockSpec(wkv.shape, lambda i: (0, 0)),
                  pl.BlockSpec((tm, LANES), tab),
                  pl.BlockSpec((tm, LANES), tab)],
        out_specs=[pl.BlockSpec((tm, nq), lambda i: (i, 0)),
                   pl.BlockSpec((tm, wkv.shape[1]), lambda i: (i, 0)),
                   pl.BlockSpec((tm, MLA_KV_LORA), lambda i: (i, 0)),
                   pl.BlockSpec((tm, LANES), lambda i: (i, 0))],
        out_shape=[jax.ShapeDtypeStruct((t, nq), BF16),
                   jax.ShapeDtypeStruct((t, wkv.shape[1]), BF16),
                   jax.ShapeDtypeStruct((t, MLA_KV_LORA), F32),
                   jax.ShapeDtypeStruct((t, LANES), BF16)],
        compiler_params=_cparams(("parallel",)),
        name="mla_project",
    )(z, z, z, q_gain.reshape(1, -1), kv_gain.reshape(1, -1), wq, wkv, cos_t, sin_t)


def _ctx_kv_kernel(ckv_ref, w_ref, o_ref):
    o_ref[...] = _dot(ckv_ref[...].astype(BF16), w_ref[...]).astype(BF16)


def ctx_kv_project(ckv_c, wkv):
    rows = ckv_c.shape[0]
    return pl.pallas_call(
        _ctx_kv_kernel,
        grid=(1,),
        in_specs=[pl.BlockSpec(ckv_c.shape, lambda i: (0, 0)), pl.BlockSpec(wkv.shape, lambda i: (0, 0))],
        out_specs=pl.BlockSpec((rows, wkv.shape[1]), lambda i: (0, 0)),
        out_shape=jax.ShapeDtypeStruct((rows, wkv.shape[1]), BF16),
        compiler_params=_cparams(("arbitrary",)),
        name="ctx_kv_project",
    )(ckv_c, wkv)


def _softmax_pv(s, v):
    m = jnp.max(s, axis=-1, keepdims=True)
    p = jnp.exp(s - m)
    l = jnp.sum(p, axis=-1, keepdims=True)
    return _dot(p.astype(BF16), v) / l


def _mla_prompt_attn_kernel(q_ref, kv_ref, kp_ref, o_ref):
    nh = MLA_HEADS
    kp = kp_ref[...]
    for h in range(nh):
        q = jnp.concatenate([q_ref[:, h * LANES:(h + 1) * LANES], q_ref[:, (nh + h) * LANES:(nh + h + 1) * LANES]], axis=1)
        k = jnp.concatenate([kv_ref[:, h * LANES:(h + 1) * LANES], kp], axis=1)
        v = kv_ref[:, (nh + h) * LANES:(nh + h + 1) * LANES]
        o_ref[:, h * LANES:(h + 1) * LANES] = _softmax_pv(_dot_nt(q, k), v).astype(o_ref.dtype)


def mla_prompt_attention(q_m, kv, kpe, batch, n):
    h = MLA_HEADS
    return pl.pallas_call(
        _mla_prompt_attn_kernel,
        grid=(batch,),
        in_specs=[pl.BlockSpec((n, 2 * h * LANES), lambda b: (b, 0)),
                  pl.BlockSpec((n, 2 * h * LANES), lambda b: (b, 0)),
                  pl.BlockSpec((n, LANES), lambda b: (b, 0))],
        out_specs=pl.BlockSpec((n, h * MLA_DV), lambda b: (b, 0)),
        out_shape=jax.ShapeDtypeStruct((batch * n, h * MLA_DV), BF16),
        compiler_params=_cparams(("parallel",)),
        name="mla_prompt_attention",
    )(q_m, kv, kpe)


def _lane_tile_reduce(x, op):
    r = x[:, 0:LANES]
    for c in range(1, x.shape[1] // LANES):
        r = op(r, x[:, c * LANES:(c + 1) * LANES])
    return r


def _mla_sample_attn_kernel(qn_ref, qp_ref, kcn_ref, kcp_ref, vc_ref, kn_ref, kp_ref, v_ref, o_ref,
                            s_ref, m_ref, acc_ref, *, tk):
    past = kcn_ref.shape[0]
    nk = kn_ref.shape[0] // tk
    q = jnp.concatenate([qn_ref[...], qp_ref[...]], axis=1)
    kc = jnp.concatenate([kcn_ref[...], kcp_ref[...].astype(BF16)], axis=1)
    s = _dot_nt(q, kc)
    s_ref[:, 0:past] = s
    m_ref[...] = _lane_tile_reduce(s, jnp.maximum)

    def scores(j, carry):
        off = pl.multiple_of(j * tk, tk)
        k = jnp.concatenate([kn_ref[pl.ds(off, tk), :], kp_ref[pl.ds(off, tk), :]], axis=1)
        s = _dot_nt(q, k)
        s_ref[:, pl.ds(pl.multiple_of(past + j * tk, LANES), tk)] = s
        m_ref[...] = jnp.maximum(m_ref[...], _lane_tile_reduce(s, jnp.maximum))
        return carry

    lax.fori_loop(0, nk, scores, 0, unroll=2)
    m_ref[...] = jnp.broadcast_to(jnp.max(m_ref[...], axis=-1, keepdims=True), m_ref.shape)

    def weighted_values(s, v):
        m = m_ref[...]
        p = jnp.concatenate([jnp.exp(s[:, c * LANES:(c + 1) * LANES] - m).astype(BF16)
                             for c in range(s.shape[1] // LANES)], axis=1)
        v_ones = jnp.concatenate([v, jnp.ones((v.shape[0], LANES), BF16)], axis=1)
        return _dot(p, v_ones)

    acc_ref[...] = weighted_values(s_ref[:, 0:past], vc_ref[...])

    def weighted(j, carry):
        off = pl.multiple_of(j * tk, tk)
        acc_ref[...] += weighted_values(s_ref[:, pl.ds(pl.multiple_of(past + j * tk, LANES), tk)],
                                        v_ref[pl.ds(off, tk), :])
        return carry

    lax.fori_loop(0, nk, weighted, 0, unroll=2)
    acc = acc_ref[...]
    o_ref[...] = (acc[:, 0:MLA_DV] / acc[:, MLA_DV:]).astype(o_ref.dtype)


def mla_sample_attention(q_m, kv, kpe, kv_ctx, kpe_ctx, row0, batch, n, past, tq=1024, tk=512):
    h = MLA_HEADS
    qb = n // tq
    r0 = row0 // n
    q0 = row0 // tq
    return pl.pallas_call(
        functools.partial(_mla_sample_attn_kernel, tk=tk),
        grid=(batch, h, qb),
        in_specs=[pl.BlockSpec((tq, LANES), lambda b, hh, i: (q0 + b * qb + i, hh)),
                  pl.BlockSpec((tq, LANES), lambda b, hh, i: (q0 + b * qb + i, h + hh)),
                  pl.BlockSpec((past, LANES), lambda b, hh, i: (b, hh)),
                  pl.BlockSpec((past, LANES), lambda b, hh, i: (b, 0)),
                  pl.BlockSpec((past, LANES), lambda b, hh, i: (b, h + hh)),
                  pl.BlockSpec((n, LANES), lambda b, hh, i: (r0 + b, hh)),
                  pl.BlockSpec((n, LANES), lambda b, hh, i: (r0 + b, 0)),
                  pl.BlockSpec((n, LANES), lambda b, hh, i: (r0 + b, h + hh))],
        out_specs=pl.BlockSpec((tq, LANES), lambda b, hh, i: (b * qb + i, hh)),
        out_shape=jax.ShapeDtypeStruct((batch * n, h * MLA_DV), BF16),
        scratch_shapes=[pltpu.VMEM((tq, past + n), F32), pltpu.VMEM((tq, LANES), F32),
                        pltpu.VMEM((tq, MLA_DV + LANES), F32)],
        compiler_params=_cparams(("parallel", "parallel", "arbitrary")),
        name="mla_sample_attention",
    )(q_m, q_m, kv_ctx, kpe_ctx, kv_ctx, kv, kpe, kv)


def _na_prompt_attn_kernel(q_ref, k_ref, v_ref, o_ref, ko_ref, vo_ref, *, scale):
    for h in range(NA_HEADS):
        cols = slice(h * LANES, (h + 1) * LANES)
        k = k_ref[:, cols]
        v = v_ref[:, cols]
        ko_ref[h] = k.astype(F32)
        vo_ref[h] = v.astype(F32)
        q = (q_ref[:, cols].astype(F32) * scale).astype(BF16)
        o_ref[:, cols] = _softmax_pv(_dot_nt(q, k), v).astype(o_ref.dtype)


def na_prompt_attention(z, batch, n):
    h = NA_HEADS
    scale = float(NA_HD ** -0.5)
    w = h * NA_HD
    cq, ck, cv = CD_NQ // w, CD_NK // w, CD_NV // w
    return pl.pallas_call(
        functools.partial(_na_prompt_attn_kernel, scale=scale),
        grid=(batch,),
        in_specs=[pl.BlockSpec((n, w), lambda b: (b, cq)),
                  pl.BlockSpec((n, w), lambda b: (b, ck)),
                  pl.BlockSpec((n, w), lambda b: (b, cv))],
        out_specs=[pl.BlockSpec((n, w), lambda b: (b, 0)),
                   pl.BlockSpec((None, h, n, NA_HD), lambda b: (b, 0, 0, 0)),
                   pl.BlockSpec((None, h, n, NA_HD), lambda b: (b, 0, 0, 0))],
        out_shape=[jax.ShapeDtypeStruct((batch * n, w), BF16),
                   jax.ShapeDtypeStruct((batch, h, n, NA_HD), F32),
                   jax.ShapeDtypeStruct((batch, h, n, NA_HD), F32)],
        compiler_params=_cparams(("parallel",)),
        name="na_prompt_attention",
    )(z, z, z)


def _dot_tn(a, b):
    return lax.dot_general(a, b, (((0,), (0,)), ((), ())), preferred_element_type=F32)


def _head_norm_gate(acc, gate_in, gain):
    outs = []
    for h in range(2):
        a = acc[:, h * LANES:(h + 1) * LANES]
        outs.append(a * lax.rsqrt(jnp.mean(a * a, axis=-1, keepdims=True) + EPS))
    g = gate_in
    return jnp.concatenate(outs, axis=1) * gain * (g * jax.nn.sigmoid(g))


RET_CHUNK = 256
FIN_ROWS = 256


def _ret_kernel(ld_ref, q_ref, k_ref, v_ref, g_ref, cos_ref, sin_ref, gain_ref, s0_ref, o_ref, sf_ref, acc_ref, *, n):
    L = RET_CHUNK
    nc = n // L
    hp = pl.program_id(1)
    lane = lax.broadcasted_iota(jnp.int32, (1, LANES), 1)
    lo_lanes = lane < RET_DK
    row128 = lax.broadcasted_iota(jnp.int32, (LANES, 1), 0)
    ri = lax.broadcasted_iota(jnp.int32, (L, L), 0)
    ci = lax.broadcasted_iota(jnp.int32, (L, L), 1)
    dist = (ri - ci).astype(F32)
    pos = lax.broadcasted_iota(jnp.int32, (L, 1), 0).astype(F32)

    consts = []
    for dr in range(2):
        lg0 = ld_ref[dr, 2 * hp]
        lg1 = ld_ref[dr, 2 * hp + 1]
        lg2 = jnp.where(lo_lanes, lg0, lg1)
        lgc = jnp.where(row128 < RET_DK, lg0, lg1)
        sd = dist if dr == 0 else -dist
        ok = sd >= 0
        intra = [jnp.where(ok, jnp.exp(jnp.where(ok, sd, 0.0) * lg), 0.0) for lg in (lg0, lg1)]
        if dr == 0:
            inter = jnp.exp((pos + 1.0) * lg2)
            to_state = jnp.exp((L - 1.0 - pos) * lg2)
        else:
            inter = jnp.exp((L - pos) * lg2)
            to_state = jnp.exp(pos * lg2)
        consts.append((intra, inter, to_state, jnp.exp(float(L) * lgc)))

    def clear(i, carry):
        acc_ref[pl.ds(pl.multiple_of(i * FIN_ROWS, FIN_ROWS), FIN_ROWS), :] = jnp.zeros((FIN_ROWS, 2 * LANES), F32)
        return carry

    lax.fori_loop(0, n // FIN_ROWS, clear, 0)

    def step(c, dr, s2):
        intra, inter, to_state, cdec = consts[dr]
        rows = pl.ds(pl.multiple_of(c * L, L), L)
        cos, sin = cos_ref[rows, :], sin_ref[rows, :]
        q = _rope(q_ref[rows, :].astype(F32) * (RET_DK ** -0.5), cos, sin)
        k = _rope(k_ref[rows, :].astype(F32), cos, sin)
        kb = k.astype(BF16)
        s2b = s2.astype(BF16)
        kt = (k * to_state).astype(BF16)
        upd = []
        for h in range(2):
            qm = jnp.where(lo_lanes if h == 0 else ~lo_lanes, q, 0.0)
            vh = v_ref[rows, h * LANES:(h + 1) * LANES].astype(BF16)
            sc = _dot_nt(qm.astype(BF16), kb) * intra[h]
            o = _dot(sc.astype(BF16), vh) + _dot((qm * inter).astype(BF16), s2b)
            acc_ref[rows, h * LANES:(h + 1) * LANES] += o
            upd.append(_dot_tn(kt, vh))
        return cdec * s2 + jnp.where(row128 < RET_DK, upd[0], upd[1])

    def body(i, carry):
        sf, sb = carry
        return step(i, 0, sf), step(nc - 1 - i, 1, sb)

    sf, sb = lax.fori_loop(0, nc, body, (s0_ref[0], s0_ref[1]))
    sf_ref[0] = sf
    sf_ref[1] = sb

    def fin(i, carry):
        rows = pl.ds(pl.multiple_of(i * FIN_ROWS, FIN_ROWS), FIN_ROWS)
        o_ref[rows, :] = _head_norm_gate(acc_ref[rows, :], g_ref[rows, :].astype(F32), gain_ref[...]).astype(o_ref.dtype)
        return carry

    lax.fori_loop(0, n // FIN_ROWS, fin, 0)


def retention_mixer(z, log_decay, ret_gain, s0, cos_t, sin_t, row0, batch, n):
    hp = RET_HEADS // 2
    r0 = row0 // n
    cq, ck, cv, cg = CD_RQ // LANES, CD_RK // LANES, CD_RV // (2 * LANES), CD_RG // (2 * LANES)
    grid_spec = pltpu.PrefetchScalarGridSpec(
        num_scalar_prefetch=1,
        grid=(batch, hp),
        in_specs=[pl.BlockSpec((n, LANES), lambda b, p, ld: (r0 + b, cq + p)),
                  pl.BlockSpec((n, LANES), lambda b, p, ld: (r0 + b, ck + p)),
                  pl.BlockSpec((n, 2 * LANES), lambda b, p, ld: (r0 + b, cv + p)),
                  pl.BlockSpec((n, 2 * LANES), lambda b, p, ld: (r0 + b, cg + p)),
                  pl.BlockSpec((n, LANES), lambda b, p, ld: (0, 0)),
                  pl.BlockSpec((n, LANES), lambda b, p, ld: (0, 0)),
                  pl.BlockSpec((1, 2 * LANES), lambda b, p, ld: (0, p)),
                  pl.BlockSpec((None, 2, LANES, LANES), lambda b, p, ld: (b, 0, p, 0))],
        out_specs=[pl.BlockSpec((n, 2 * LANES), lambda b, p, ld: (b, p)),
                   pl.BlockSpec((None, 2, LANES, LANES), lambda b, p, ld: (b, 0, p, 0))],
        scratch_shapes=[pltpu.VMEM((n, 2 * LANES), F32)],
    )
    return pl.pallas_call(
        functools.partial(_ret_kernel, n=n),
        grid_spec=grid_spec,
        out_shape=[jax.ShapeDtypeStruct((batch * n, RET_HEADS * RET_DV), BF16),
                   jax.ShapeDtypeStruct((batch, 2, RET_HEADS * RET_DK, RET_DV), F32)],
        compiler_params=_cparams(("parallel", "parallel")),
        name="retention_mixer",
    )(log_decay.astype(F32), z, z, z, z, cos_t, sin_t, ret_gain.reshape(1, -1), s0)


GLA_CHUNK = 256
GLA_SUB = 64
GLA_SAFE_EXPONENT = 60.0


def _gla_kernel(q_ref, k_ref, v_ref, g_ref, misc_ref, gw_ref, gb_ref, gain_ref, s0_ref, o_ref, sf_ref,
                la_ref, acc_ref, *, n):
    C, SB = GLA_CHUNK, GLA_SUB
    nsb = C // SB
    nc = n // C
    lane = lax.broadcasted_iota(jnp.int32, (1, LANES), 1)
    lo_lanes = lane < GLA_DK
    head_mask = (lo_lanes, ~lo_lanes)
    ri = lax.broadcasted_iota(jnp.int32, (C, C), 0)
    ci = lax.broadcasted_iota(jnp.int32, (C, C), 1)
    rowc = lax.broadcasted_iota(jnp.int32, (C, 1), 0)
    SLAB = 16
    row8 = lax.broadcasted_iota(jnp.int32, (SLAB, 1), 0)
    tri = (jnp.where(ci <= ri, 1.0, 0.0).astype(BF16), jnp.where(ci >= ri, 1.0, 0.0).astype(BF16))
    causal = (ci <= ri, ci >= ri)

    def prep(i, carry):
        rows = pl.ds(pl.multiple_of(i * FIN_ROWS, FIN_ROWS), FIN_ROWS)
        slab = misc_ref[rows, :].astype(BF16)
        for dr in range(2):
            pre = _dot(slab, gw_ref[dr]) + gb_ref[dr]
            la_ref[dr, rows, :] = (jnp.minimum(pre, 0.0) - jnp.log1p(jnp.exp(-jnp.abs(pre)))) * (1.0 / GATE_TAU)
        acc_ref[rows, :] = jnp.zeros((FIN_ROWS, 2 * LANES), F32)
        return carry

    lax.fori_loop(0, n // FIN_ROWS, prep, 0)

    def step(c, dr):
        rev = dr == 1
        base = pl.multiple_of(c * C, C)
        rows = pl.ds(base, C)
        la = la_ref[dr, rows, :]
        hi = la.astype(BF16)
        lo = (la - hi.astype(F32)).astype(BF16)
        b = _dot(tri[dr], hi) + _dot(tri[dr], lo)
        zero = jnp.zeros((1, LANES), F32)
        if not rev:
            bref = [zero if i == 0 else b[SB * i - 1:SB * i, :] for i in range(nsb)]
            bend = [b[SB * i + SB - 1:SB * i + SB, :] for i in range(nsb)]
            blast = b[C - 1:C, :]
        else:
            bref = [zero if i == nsb - 1 else b[SB * (i + 1):SB * (i + 1) + 1, :] for i in range(nsb)]
            bend = [b[SB * i:SB * i + 1, :] for i in range(nsb)]
            blast = b[0:1, :]
        spread = bref[0] - bend[0]
        for i in range(1, nsb):
            spread = jnp.maximum(spread, bref[i] - bend[i])
        worst = jnp.max(spread)

        def fast(st):
            q = q_ref[rows, :].astype(F32) * (GLA_DK ** -0.5)
            k = k_ref[rows, :].astype(F32)
            a_rows = ([], [])
            for i in range(nsb):
                qe = q[SB * i:SB * (i + 1), :] * jnp.exp(b[SB * i:SB * (i + 1), :] - bref[i])
                valid = (rowc < SB * (i + 1)) if not rev else (rowc >= SB * i)
                ke = jnp.where(valid, k * jnp.exp(jnp.where(valid, bref[i] - b, 0.0)), 0.0).astype(BF16)
                for h in range(2):
                    a_rows[h].append(_dot_nt(jnp.where(head_mask[h], qe, 0.0).astype(BF16), ke))
            qi = q * jnp.exp(b)
            ks = (k * jnp.exp(blast - b)).astype(BF16)
            stb = st.astype(BF16)
            upd = []
            for h in range(2):
                a = jnp.where(causal[dr], jnp.concatenate(a_rows[h], axis=0), 0.0)
                vh = v_ref[rows, h * LANES:(h + 1) * LANES].astype(BF16)
                o = _dot(a.astype(BF16), vh) + _dot_nt(jnp.where(head_mask[h], qi, 0.0).astype(BF16), stb)
                acc_ref[rows, h * LANES:(h + 1) * LANES] += o
                upd.append(_dot_tn(vh, ks))
            return jnp.exp(blast) * st + jnp.where(lo_lanes, upd[0], upd[1])

        def slow(st):
            def row_step(r, st):
                t = r if not rev else C - 1 - r
                slab = pl.ds(pl.multiple_of(base + (t // SLAB) * SLAB, SLAB), SLAB)
                pick = row8 == (t % SLAB)
                q8 = jnp.where(pick, q_ref[slab, :].astype(F32) * (GLA_DK ** -0.5), 0.0)
                k8 = jnp.where(pick, k_ref[slab, :].astype(F32), 0.0).astype(BF16)
                la_t = jnp.sum(jnp.where(pick, la_ref[dr, slab, :], 0.0), axis=0, keepdims=True)
                upd = []
                for h in range(2):
                    v8 = jnp.where(pick, v_ref[slab, h * LANES:(h + 1) * LANES].astype(F32), 0.0).astype(BF16)
                    upd.append(_dot_tn(v8, k8))
                st = jnp.exp(la_t) * st + jnp.where(lo_lanes, upd[0], upd[1])
                stb = st.astype(BF16)
                for h in range(2):
                    qh = jnp.where(head_mask[h], q8, 0.0).astype(BF16)
                    acc_ref[slab, h * LANES:(h + 1) * LANES] += _dot_nt(qh, stb)
                return st

            return lax.fori_loop(0, C + 0 * pl.program_id(0), row_step, st)

        return worst, fast, slow

    def body(i, carry):
        worst_f, fast_f, slow_f = step(i, 0)
        worst_b, fast_b, slow_b = step(nc - 1 - i, 1)
        return lax.cond(jnp.maximum(worst_f, worst_b) <= GLA_SAFE_EXPONENT,
                        lambda st: (fast_f(st[0]), fast_b(st[1])),
                        lambda st: (slow_f(st[0]), slow_b(st[1])), carry)

    sf, sb = lax.fori_loop(0, nc, body, (s0_ref[0].T, s0_ref[1].T))
    sf_ref[0] = sf.T
    sf_ref[1] = sb.T

    def fin(i, carry):
        rows = pl.ds(pl.multiple_of(i * FIN_ROWS, FIN_ROWS), FIN_ROWS)
        o_ref[rows, :] = _head_norm_gate(acc_ref[rows, :], g_ref[rows, :].astype(F32), gain_ref[...]).astype(o_ref.dtype)
        return carry

    lax.fori_loop(0, n // FIN_ROWS, fin, 0)


def _prep_gate(gate_w, gate_b):
    gw = jnp.zeros((2, LANES, gate_w.shape[-1]), F32)
    for dr in range(2):
        lo = MLA_ROPE + dr * GATE_RANK
        gw = gw.at[dr, lo:lo + GATE_RANK, :].set(gate_w[dr].astype(F32))
    return gw.astype(BF16), gate_b.astype(F32).reshape(2, 1, -1)


def gla_mixer(z, gate_w, gate_b, gla_gain, s0, row0, batch, n):
    hp = GLA_HEADS // 2
    r0 = row0 // n
    gw, gb = _prep_gate(gate_w, gate_b)
    cq, ck, cv, cg, cm = AB_GQ // LANES, AB_GK // LANES, AB_GV // (2 * LANES), AB_GR // (2 * LANES), AB_MISC // LANES
    return pl.pallas_call(
        functools.partial(_gla_kernel, n=n),
        grid=(batch, hp),
        in_specs=[pl.BlockSpec((n, LANES), lambda b, p: (r0 + b, cq + p)),
                  pl.BlockSpec((n, LANES), lambda b, p: (r0 + b, ck + p)),
                  pl.BlockSpec((n, 2 * LANES), lambda b, p: (r0 + b, cv + p)),
                  pl.BlockSpec((n, 2 * LANES), lambda b, p: (r0 + b, cg + p)),
                  pl.BlockSpec((n, LANES), lambda b, p: (r0 + b, cm)),
                  pl.BlockSpec((2, LANES, LANES), lambda b, p: (0, 0, p)),
                  pl.BlockSpec((2, 1, LANES), lambda b, p: (0, 0, p)),
                  pl.BlockSpec((1, 2 * LANES), lambda b, p: (0, p)),
                  pl.BlockSpec((None, 2, LANES, LANES), lambda b, p: (b, 0, p, 0))],
        out_specs=[pl.BlockSpec((n, 2 * LANES), lambda b, p: (b, p)),
                   pl.BlockSpec((None, 2, LANES, LANES), lambda b, p: (b, 0, p, 0))],
        out_shape=[jax.ShapeDtypeStruct((batch * n, GLA_HEADS * GLA_DV), BF16),
                   jax.ShapeDtypeStruct((batch, 2, GLA_HEADS * GLA_DK, GLA_DV), F32)],
        scratch_shapes=[pltpu.VMEM((2, n, LANES), F32), pltpu.VMEM((n, 2 * LANES), F32)],
        compiler_params=_cparams(("parallel", "parallel")),
        name="gla_mixer",
    )(z, z, z, z, z, gw, gb, gla_gain.reshape(1, -1), s0)


NA_QROWS = 8
NA_WROWS = 16
NA_NEG = -1e30
NA_TAB = 32


def _na_bias_table(na_bias_l):
    qc = np.arange(GRID_W)[:, None]
    kc = np.arange(GRID_W)[None, :]
    ws = np.clip(qc - NA_COLS // 2, 0, GRID_W - NA_COLS)
    col_ok = (kc >= ws) & (kc < ws + NA_COLS)
    col_off = np.clip(kc - qc + NA_COLS - 1, 0, 2 * NA_COLS - 2)
    aa = (np.arange(NA_TAB) - NA_QROWS)[:, None] + np.arange(2)[None, :]
    n_row, n_col = 2 * NA_ROWS - 1, 2 * NA_COLS - 1
    sel_row = (aa[:, :, None] == np.arange(n_row)[None, None, :]).astype(np.float32)
    sel_col = ((col_off[:, :, None] == np.arange(n_col)[None, None, :]) & col_ok[:, :, None]).astype(np.float32)
    fill = np.where(col_ok, 0.0, NA_NEG).astype(np.float32)
    tab = jnp.einsum('hac,sea,qkc->hsqek', na_bias_l.astype(F32), jnp.asarray(sel_row), jnp.asarray(sel_col),
                     precision=lax.Precision.HIGHEST)
    tab = tab + jnp.asarray(fill)[None, None, :, None, :]
    h = na_bias_l.shape[0]
    return tab.reshape(h, NA_TAB, GRID_W, 2 * GRID_W)


def _na_latent_kernel(q_ref, k_ref, v_ref, kc_ref, vc_ref, tab_ref, o_ref, s_ref, *, scale, rows_total):
    g = pl.program_id(2)
    r0 = g * NA_QROWS
    start = jnp.clip(r0 - NA_ROWS // 2, 0, rows_total - NA_WROWS)
    nloc = NA_WROWS * GRID_W
    krows = pl.ds(pl.multiple_of(start * GRID_W, GRID_W), nloc)
    q = (q_ref[...].astype(F32) * scale).astype(BF16)
    kw = k_ref[krows, :]
    s_ref[:, 0:nloc] = _dot_nt(q, kw)
    s_ref[:, nloc:] = _dot_nt(q, kc_ref[...].astype(BF16))
    lane = lax.broadcasted_iota(jnp.int32, (1, LANES), 1)
    for qr in range(NA_QROWS):
        r = r0 + qr
        rs = jnp.clip(r - NA_ROWS // 2, 0, rows_total - NA_ROWS)
        for kp in range(NA_WROWS // 2):
            k0 = start + 2 * kp
            ok0 = (k0 >= rs) & (k0 < rs + NA_ROWS)
            ok1 = (k0 + 1 >= rs) & (k0 + 1 < rs + NA_ROWS)
            neg = jnp.where(lane < GRID_W, jnp.where(ok0, 0.0, NA_NEG), jnp.where(ok1, 0.0, NA_NEG))
            slot = k0 - r + (NA_ROWS - 1) + NA_QROWS
            s_ref[qr * GRID_W:(qr + 1) * GRID_W, kp * LANES:(kp + 1) * LANES] += tab_ref[slot] + neg
    s = s_ref[...]
    m = jnp.max(s, axis=-1, keepdims=True)
    p = jnp.exp(s - m)
    l = jnp.sum(p, axis=-1, keepdims=True)
    pb = p.astype(BF16)
    o = _dot(pb[:, 0:nloc], v_ref[krows, :]) + _dot(pb[:, nloc:], vc_ref[...].astype(BF16))
    o_ref[...] = (o / l).astype(o_ref.dtype)


def na_latent_attention(z, k_ctx, v_ctx, na_bias_l, row0, batch, n):
    h = NA_HEADS
    past = k_ctx.shape[2]
    rows_total = n // GRID_W
    tq = NA_QROWS * GRID_W
    qb = n // tq
    r0 = row0 // n
    q0 = row0 // tq
    cq, ck, cv = CD_NQ // LANES, CD_NK // LANES, CD_NV // LANES
    tab = _na_bias_table(na_bias_l)
    return pl.pallas_call(
        functools.partial(_na_latent_kernel, scale=float(NA_HD ** -0.5), rows_total=rows_total),
        grid=(batch, h, qb),
        in_specs=[pl.BlockSpec((tq, LANES), lambda b, hh, g: (q0 + b * qb + g, cq + hh)),
                  pl.BlockSpec((n, LANES), lambda b, hh, g: (r0 + b, ck + hh)),
                  pl.BlockSpec((n, LANES), lambda b, hh, g: (r0 + b, cv + hh)),
                  pl.BlockSpec((None, None, past, NA_HD), lambda b, hh, g: (b, hh, 0, 0)),
                  pl.BlockSpec((None, None, past, NA_HD), lambda b, hh, g: (b, hh, 0, 0)),
                  pl.BlockSpec((None, NA_TAB, GRID_W, 2 * GRID_W), lambda b, hh, g: (hh, 0, 0, 0))],
        out_specs=pl.BlockSpec((tq, LANES), lambda b, hh, g: (b * qb + g, hh)),
        out_shape=jax.ShapeDtypeStruct((batch * n, h * NA_HD), BF16),
        scratch_shapes=[pltpu.VMEM((tq, NA_WROWS * GRID_W + past), F32)],
        compiler_params=_cparams(("parallel", "parallel", "arbitrary")),
        name="na_latent_attention",
    )(z, z, z, k_ctx, v_ctx, tab)


def _prep_w_in_ab(w):
    d = w.shape[0]
    sizes = (512, 512, 1024, 1024, 2 * GATE_RANK, MLA_Q_LORA, MLA_KV_LORA, MLA_ROPE)
    gq, gk, gv, gr, glr, cq, ckv, kpe = jnp.split(w, np.cumsum(sizes)[:-1].tolist(), axis=1)
    pad = jnp.zeros((d, AB_WIDTH - AB_MISC - MLA_ROPE - 2 * GATE_RANK), w.dtype)
    return jnp.concatenate([gq, gk, gv, gr, cq, ckv, kpe, glr, pad], axis=1).astype(BF16)


def _prep_w_uq(w):
    r = w.shape[0]
    w3 = w.reshape(r, MLA_HEADS, MLA_NOPE + MLA_ROPE)
    nope = w3[:, :, :MLA_NOPE].reshape(r, MLA_HEADS * MLA_NOPE)
    pe = jnp.pad(w3[:, :, MLA_NOPE:], ((0, 0), (0, 0), (0, LANES - MLA_ROPE))).reshape(r, MLA_HEADS * LANES)
    return jnp.concatenate([nope, pe], axis=1).astype(BF16)


def _prep_w_ukv(w):
    r = w.shape[0]
    w3 = w.reshape(r, MLA_HEADS, MLA_NOPE + MLA_DV)
    return jnp.concatenate([w3[:, :, :MLA_NOPE].reshape(r, -1), w3[:, :, MLA_NOPE:].reshape(r, -1)], axis=1).astype(BF16)


def _prep_router(router_w):
    w = jnp.pad(_member_major(router_w.astype(F32)), ((0, 0), (0, LANES - N_EXPERTS)))
    hi = w.astype(BF16)
    lo = (w - hi.astype(F32)).astype(BF16)
    return jnp.stack([hi, lo], axis=0)


def kernel(x_prompt, x_sample, state_gla, cache_mla_ckv, cache_mla_kpe, state_ret, cache_na_k, cache_na_v,
           c, c_ctx, ada_w, ada_b, norm1, norm2, w_in_ab, gla_gate_w, gla_gate_b, gla_gain, mla_q_gain,
           mla_w_uq, mla_kv_gain, mla_w_ukv, w_in_cd, ret_log_decay, ret_gain, na_bias, w_out,
           router_w, router_bias, exp_w1, exp_w3, exp_w2, final_norm):
    bp, n_p, d = x_prompt.shape
    bs, n_s, _ = x_sample.shape
    past = cache_mla_ckv.shape[2]
    depth = ada_w.shape[0]
    tp, ts = bp * n_p, bs * n_s
    ys = [x_prompt.reshape(tp, d), x_sample.reshape(ts, d)]

    cond8 = jnp.zeros((8, d), F32).at[0].set(c_ctx).at[1:1 + bs].set(c)
    mod = adaln_all(cond8, ada_w, ada_b)
    rw_split = _prep_router(router_w)
    w_out_bf = w_out.astype(BF16)
    cos_t, sin_t = rope_tables(n_s, 512)
    cos_s, sin_s = cos_t[512:], sin_t[512:]
    cos_id, sin_id = cos_t[:n_p], sin_t[:n_p]

    outs = {}
    for l in range(depth):
        i = l // 2
        mod_l = mod[l].reshape(8, 1, 6 * d)
        final = l == depth - 1
        if l % 2 == 0:
            z = win_project(ys, mod_l, norm1[l], _prep_w_in_ab(w_in_ab[i]), tp, n_s)
            s0p = jnp.zeros((bp, 2, GLA_HEADS * GLA_DK, GLA_DV), F32)
            s0s = state_gla[:, i].reshape(bs, 2, GLA_HEADS * GLA_DK, GLA_DV)
            og_p, st_p = gla_mixer(z, gla_gate_w[i], gla_gate_b[i], gla_gain[i], s0p, 0, bp, n_p)
            og_s, _ = gla_mixer(z, gla_gate_w[i], gla_gate_b[i], gla_gain[i], s0s, tp, bs, n_s)
            st_p = st_p.reshape(bp, 2, GLA_HEADS, GLA_DK, GLA_DV)
            o_a = (og_p, og_s)
            wq, wkv = _prep_w_uq(mla_w_uq[i]), _prep_w_ukv(mla_w_ukv[i])
            q_m, kv, ckvn, kpe = mla_project(z, mla_q_gain[i], mla_kv_gain[i], wq, wkv, cos_t, sin_t, tp, n_s)
            kv_ctx = ctx_kv_project(cache_mla_ckv[:, i].reshape(bs * past, -1), wkv)
            kpe_ctx = jnp.pad(cache_mla_kpe[:, i].reshape(bs * past, -1), ((0, 0), (0, LANES - MLA_ROPE)))
            om_p = mla_prompt_attention(q_m, kv, kpe, bp, n_p)
            om_s = mla_sample_attention(q_m, kv, kpe, kv_ctx, kpe_ctx, tp, bs, n_s, past)
            o_b = (om_p, om_s)
            outs.setdefault('gla', []).append(st_p)
            outs.setdefault('ckv', []).append(ckvn[:tp].reshape(bp, n_p, -1))
            outs.setdefault('kpe', []).append(z[:tp, AB_MISC:AB_MISC + MLA_ROPE].astype(F32).reshape(bp, n_p, -1))
        else:
            z = win_project(ys, mod_l, norm1[l], w_in_cd[i].astype(BF16), tp, n_s)
            s0p = jnp.zeros((bp, 2, RET_HEADS * RET_DK, RET_DV), F32)
            s0s = state_ret[:, i].reshape(bs, 2, RET_HEADS * RET_DK, RET_DV)
            or_p, st_p = retention_mixer(z, ret_log_decay[i], ret_gain[i], s0p, cos_id, sin_id, 0, bp, n_p)
            or_s, _ = retention_mixer(z, ret_log_decay[i], ret_gain[i], s0s, cos_s, sin_s, tp, bs, n_s)
            st_p = st_p.reshape(bp, 2, RET_HEADS, RET_DK, RET_DV)
            o_a = (or_p, or_s)
            on_p, kn, vn = na_prompt_attention(z, bp, n_p)
            on_s = na_latent_attention(z, cache_na_k[:, i], cache_na_v[:, i], na_bias[i], tp, bs, n_s)
            o_b = (on_p, on_s)
            outs.setdefault('ret', []).append(st_p)
            outs.setdefault('nak', []).append(kn)
            outs.setdefault('nav', []).append(vn)
        y1, h2, logits = wout_project(o_a, o_b, ys, mod_l, w_out_bf[l], norm2[l], rw_split, tp, n_s)
        y = moe_layer(y1, h2, logits, mod_l, router_bias, exp_w1, exp_w3, exp_w2, l, final_norm, tp, n_s, final)
        ys = list(y) if final else [y]

    return (ys[0].reshape(bp, n_p, d), ys[1].reshape(bs, n_s, d),
            jnp.stack(outs['gla'], axis=1), jnp.stack(outs['ckv'], axis=1), jnp.stack(outs['kpe'], axis=1),
            jnp.stack(outs['ret'], axis=1), jnp.stack(outs['nak'], axis=1), jnp.stack(outs['nav'], axis=1))
```

```python
import functools

import numpy as np
import jax
import jax.numpy as jnp
from jax import lax
from jax.experimental import pallas as pl
from jax.experimental.pallas import tpu as pltpu

F32 = jnp.float32
BF16 = jnp.bfloat16

EPS = 1e-6
GRID_W = 64
ROPE_BASE = 10000.0
GATE_RANK = 16
GATE_TAU = 16.0
GLA_HEADS, GLA_DK, GLA_DV = 8, 64, 128
MLA_HEADS, MLA_Q_LORA, MLA_KV_LORA, MLA_NOPE, MLA_ROPE, MLA_DV = 8, 512, 256, 128, 64, 128
RET_HEADS, RET_DK, RET_DV = 8, 64, 128
NA_HEADS, NA_HD, NA_ROWS, NA_COLS = 8, 128, 8, 16
N_EXPERTS, N_GROUPS, TOP_K, D_EXPERT = 32, 8, 2, 512
EXPERTS_PER_GROUP = N_EXPERTS // N_GROUPS
MOE_BLOCK = 128
LANES = 128

AB_GQ, AB_GK, AB_GV, AB_GR, AB_CQ, AB_CKV, AB_MISC, AB_WIDTH = 0, 512, 1024, 2048, 3072, 3584, 3840, 4096
CD_RQ, CD_RK, CD_RV, CD_RG, CD_NQ, CD_NK, CD_NV, CD_WIDTH = 0, 512, 1024, 2048, 3072, 4096, 5120, 6144

VMEM_LIMIT = 56 * 1024 * 1024


def _cparams(sem):
    return pltpu.CompilerParams(dimension_semantics=sem, vmem_limit_bytes=VMEM_LIMIT)


def _dot(a, b):
    return jnp.dot(a, b, preferred_element_type=F32)


def _dot_nt(a, b):
    return lax.dot_general(a, b, (((1,), (1,)), ((), ())), preferred_element_type=F32)


def _adaln_kernel(c_ref, w_ref, b_ref, o_ref):
    c = c_ref[...]
    s = (c * jax.nn.sigmoid(c)).astype(BF16)
    o_ref[...] = _dot(s, w_ref[...].astype(BF16)) + b_ref[...]


def adaln_all(cond8, ada_w, ada_b, tn=512):
    depth, d, n6 = ada_w.shape
    return pl.pallas_call(
        _adaln_kernel,
        grid=(depth, n6 // tn),
        in_specs=[pl.BlockSpec((8, d), lambda l, j: (0, 0)),
                  pl.BlockSpec((None, d, tn), lambda l, j: (l, 0, j)),
                  pl.BlockSpec((None, 1, tn), lambda l, j: (l, 0, j))],
        out_specs=pl.BlockSpec((None, 8, tn), lambda l, j: (l, 0, j)),
        out_shape=jax.ShapeDtypeStruct((depth, 8, n6), F32),
        compiler_params=_cparams(("parallel", "parallel")),
        name="adaln",
    )(cond8, ada_w, ada_b.reshape(depth, 1, n6))


def _mod_index(i, tm, n_prompt_rows, sample_rows):
    tp = n_prompt_rows // tm
    return jnp.where(i < tp, 0, 1 + (i - tp) // (sample_rows // tm))


def _split_specs(block, tiles_prompt, n_grid_axes=1):
    if n_grid_axes == 1:
        return [pl.BlockSpec(block, lambda i: (jnp.minimum(i, tiles_prompt - 1), 0)),
                pl.BlockSpec(block, lambda i: (jnp.maximum(i - tiles_prompt, 0), 0))]
    return [pl.BlockSpec(block, lambda i, j: (jnp.minimum(i, tiles_prompt - 1), 0)),
            pl.BlockSpec(block, lambda i, j: (jnp.maximum(i - tiles_prompt, 0), 0))]


def _win_kernel(*refs, d, tiles_prompt, split):
    if split:
        xp_ref, xs_ref, mod_ref, g_ref, w_ref, o_ref, h_ref = refs
    else:
        xp_ref, mod_ref, g_ref, w_ref, o_ref, h_ref = refs

    def norm_mod(x_ref):
        x = x_ref[...]
        xn = x * lax.rsqrt(jnp.mean(x * x, axis=-1, keepdims=True) + EPS) * g_ref[...]
        h = xn * (1.0 + mod_ref[:, d:2 * d]) + mod_ref[:, 0:d]
        h_ref[...] = h.astype(BF16)

    first = pl.program_id(1) == 0
    if split:
        is_prompt = pl.program_id(0) < tiles_prompt
        pl.when(first & is_prompt)(lambda: norm_mod(xp_ref))
        pl.when(first & jnp.logical_not(is_prompt))(lambda: norm_mod(xs_ref))
    else:
        pl.when(first)(lambda: norm_mod(xp_ref))

    o_ref[...] = _dot(h_ref[...], w_ref[...]).astype(o_ref.dtype)


def win_project(xs, mod_l, gain, w_bf16, n_prompt_rows, sample_rows, tm=1024, tn=512):
    split = len(xs) == 2
    d = xs[0].shape[1]
    t = sum(x.shape[0] for x in xs)
    width = w_bf16.shape[1]
    tiles_prompt = n_prompt_rows // tm
    midx = functools.partial(_mod_index, tm=tm, n_prompt_rows=n_prompt_rows, sample_rows=sample_rows)
    x_specs = (_split_specs((tm, d), tiles_prompt, 2) if split else [pl.BlockSpec((tm, d), lambda i, j: (i, 0))])
    return pl.pallas_call(
        functools.partial(_win_kernel, d=d, tiles_prompt=tiles_prompt, split=split),
        grid=(t // tm, width // tn),
        in_specs=x_specs + [pl.BlockSpec((None, 1, 6 * d), lambda i, j: (midx(i), 0, 0)),
                            pl.BlockSpec((1, d), lambda i, j: (0, 0)),
                            pl.BlockSpec((d, tn), lambda i, j: (0, j))],
        out_specs=pl.BlockSpec((tm, tn), lambda i, j: (i, j)),
        out_shape=jax.ShapeDtypeStruct((t, width), BF16),
        scratch_shapes=[pltpu.VMEM((tm, d), BF16)],
        compiler_params=_cparams(("parallel", "arbitrary")),
        name="win_project",
    )(*xs, mod_l, gain.reshape(1, d), w_bf16)


def _wout_kernel(*refs, d, tiles_prompt, split_y):
    if split_y:
        (oap_ref, oas_ref, obp_ref, obs_ref, yp_ref, ys_ref, mod_ref, wa_ref, wb_ref, g_ref, rw_ref,
         ynew_ref, h2_ref, lg_ref) = refs
    else:
        (oap_ref, oas_ref, obp_ref, obs_ref, yp_ref, mod_ref, wa_ref, wb_ref, g_ref, rw_ref,
         ynew_ref, h2_ref, lg_ref) = refs
        ys_ref = yp_ref
    is_prompt = pl.program_id(0) < tiles_prompt
    oa = jnp.where(is_prompt, oap_ref[...], oas_ref[...])
    ob = jnp.where(is_prompt, obp_ref[...], obs_ref[...])
    y = jnp.where(is_prompt, yp_ref[...], ys_ref[...]) if split_y else yp_ref[...]
    out = _dot(oa, wa_ref[...]) + _dot(ob, wb_ref[...])
    y1 = y + mod_ref[:, 2 * d:3 * d] * out
    ynew_ref[...] = y1
    xn = y1 * lax.rsqrt(jnp.mean(y1 * y1, axis=-1, keepdims=True) + EPS) * g_ref[...]
    h = xn * (1.0 + mod_ref[:, 4 * d:5 * d]) + mod_ref[:, 3 * d:4 * d]
    h2_ref[...] = h
    hh = h.astype(BF16)
    hl = (h - hh.astype(F32)).astype(BF16)
    lg_ref[...] = _dot(hh, rw_ref[0]) + _dot(hl, rw_ref[0]) + _dot(hh, rw_ref[1])


def wout_project(o_a, o_b, ys, mod_l, w_out_bf16, gain2, rw_split, n_prompt_rows, sample_rows, tm=256):
    split_y = len(ys) == 2
    d = ys[0].shape[1]
    t = sum(x.shape[0] for x in ys)
    ka, kb = o_a[0].shape[1], o_b[0].shape[1]
    tiles_prompt = n_prompt_rows // tm
    midx = functools.partial(_mod_index, tm=tm, n_prompt_rows=n_prompt_rows, sample_rows=sample_rows)
    y_specs = _split_specs((tm, d), tiles_prompt) if split_y else [pl.BlockSpec((tm, d), lambda i: (i, 0))]
    return pl.pallas_call(
        functools.partial(_wout_kernel, d=d, tiles_prompt=tiles_prompt, split_y=split_y),
        grid=(t // tm,),
        in_specs=_split_specs((tm, ka), tiles_prompt) + _split_specs((tm, kb), tiles_prompt) + y_specs + [
                  pl.BlockSpec((None, 1, 6 * d), lambda i: (midx(i), 0, 0)),
                  pl.BlockSpec((ka, d), lambda i: (0, 0)),
                  pl.BlockSpec((kb, d), lambda i: (1, 0)),
                  pl.BlockSpec((1, d), lambda i: (0, 0)),
                  pl.BlockSpec((2, d, LANES), lambda i: (0, 0, 0))],
        out_specs=[pl.BlockSpec((tm, d), lambda i: (i, 0)),
                   pl.BlockSpec((tm, d), lambda i: (i, 0)),
                   pl.BlockSpec((tm, LANES), lambda i: (i, 0))],
        out_shape=[jax.ShapeDtypeStruct((t, d), F32),
                   jax.ShapeDtypeStruct((t, d), F32),
                   jax.ShapeDtypeStruct((t, LANES), F32)],
        compiler_params=_cparams(("parallel",)),
        name="wout_project",
    )(*o_a, *o_b, *ys, mod_l, w_out_bf16, w_out_bf16, gain2.reshape(1, d), rw_split)


def _member_major(x_experts_last):
    shp = x_experts_last.shape[:-1]
    return x_experts_last.reshape(shp + (N_GROUPS, EXPERTS_PER_GROUP)).swapaxes(-1, -2).reshape(shp + (N_EXPERTS,))


def _route_kernel(lg_ref, bias_ref, info_ref, cnt_ref, run_ref, *, tm):
    @pl.when(pl.program_id(0) == 0)
    def _():
        run_ref[...] = jnp.zeros_like(run_ref)

    ng, nk = N_GROUPS, EXPERTS_PER_GROUP
    neg = jnp.float32(-jnp.inf)
    lgt = lg_ref[...].T
    aff = [jax.nn.sigmoid(lgt[k * ng:(k + 1) * ng, :]) for k in range(nk)]
    sel = [aff[k] + bias_ref[k * ng:(k + 1) * ng, :] for k in range(nk)]

    def first_max(vals):
        m = vals[0]
        for v in vals[1:]:
            m = jnp.maximum(m, v)
        idx = jnp.full(m.shape, float(nk), F32)
        for k in reversed(range(nk)):
            idx = jnp.where(vals[k] == m, float(k), idx)
        return m, idx

    m1, i1 = first_max(sel)
    m2, i2 = first_max([jnp.where(i1 == k, neg, sel[k]) for k in range(nk)])
    gscore = m1 + m2
    gid = lax.broadcasted_iota(jnp.int32, gscore.shape, 0).astype(F32)
    gmax = jnp.max(gscore, axis=0, keepdims=True)
    g = jnp.min(jnp.where(gscore == gmax, gid, 99.0), axis=0, keepdims=True)
    in_g = gid == g

    def pick(masks, vals):
        tot = jnp.zeros((1, tm), F32)
        for msk, v in zip(masks, vals):
            tot = tot + jnp.sum(jnp.where(msk, v, 0.0), axis=0, keepdims=True)
        return tot

    hit1 = [in_g & (i1 == k) for k in range(nk)]
    hit2 = [in_g & (i2 == k) for k in range(nk)]
    a1, a2 = pick(hit1, aff), pick(hit2, aff)
    den = a1 + a2
    e1 = g * nk + pick([in_g], [i1])
    e2 = g * nk + pick([in_g], [i2])
    oh = jnp.concatenate([jnp.where(hit1[k] | hit2[k], 1.0, 0.0) for k in range(nk)], axis=0)
    r_i = lax.broadcasted_iota(jnp.int32, (tm, tm), 0)
    c_i = lax.broadcasted_iota(jnp.int32, (tm, tm), 1)
    earlier = jnp.where(r_i < c_i, 1.0, 0.0).astype(BF16)
    cum = _dot(oh.astype(BF16), earlier) + run_ref[:, 0:1]
    cums = [cum[k * ng:(k + 1) * ng, :] for k in range(nk)]
    r1, r2 = pick(hit1, cums), pick(hit2, cums)
    run_new = run_ref[:, 0:1] + jnp.sum(oh, axis=1, keepdims=True)
    run_ref[...] = jnp.broadcast_to(run_new, run_ref.shape)
    cnt_ref[...] = jnp.broadcast_to(run_new, cnt_ref.shape)
    info_ref[...] = jnp.concatenate([e1, e2, a1 / den, a2 / den, r1, r2, jnp.zeros((2, tm), F32)], axis=0)


def route(logits, router_bias, tm=256):
    t = logits.shape[0]
    bias = jnp.broadcast_to(_member_major(router_bias.astype(F32))[:, None], (N_EXPERTS, tm))
    return pl.pallas_call(
        functools.partial(_route_kernel, tm=tm),
        grid=(t // tm,),
        in_specs=[pl.BlockSpec((tm, LANES), lambda i: (i, 0)),
                  pl.BlockSpec((N_EXPERTS, tm), lambda i: (0, 0))],
        out_specs=[pl.BlockSpec((8, tm), lambda i: (0, i)),
                   pl.BlockSpec((N_EXPERTS, LANES), lambda i: (0, 0))],
        out_shape=[jax.ShapeDtypeStruct((8, t), F32),
                   jax.ShapeDtypeStruct((N_EXPERTS, LANES), F32)],
        scratch_shapes=[pltpu.VMEM((N_EXPERTS, LANES), F32)],
        compiler_params=_cparams(("arbitrary",)),
        name="route",
    )(logits, bias)


def _expert_kernel(be_ref, nu_ref, first_ref, par_ref, nxt_ref, tok_ref, tok_next_ref, h_hbm,
                   w1_hbm, w3_hbm, w2_hbm, o_ref, xbuf, ws1, ws3, ws2, w1b, w3b, w2b, sem, wsem, *, layer):
    i = pl.program_id(0)
    slot = i % 2
    nu = nu_ref[0]

    def gather(idx_ref, s):
        for j in range(MOE_BLOCK):
            pltpu.make_async_copy(h_hbm.at[pl.ds(idx_ref[j], 1)], xbuf.at[s, pl.ds(j, 1)], sem.at[s]).start()

    def wait_block(s):
        pltpu.make_async_copy(h_hbm.at[pl.ds(0, MOE_BLOCK)], xbuf.at[s], sem.at[s]).wait()

    def weight_copies(e, s):
        return [pltpu.make_async_copy(w_hbm.at[layer, e], ws.at[s], wsem.at[s])
                for w_hbm, ws in ((w1_hbm, ws1), (w3_hbm, ws3), (w2_hbm, ws2))]

    @pl.when(i == 0)
    def _():
        gather(tok_ref, 0)
        for cp in weight_copies(be_ref[0], 0):
            cp.start()

    @pl.when(i < nu)
    def _():
        @pl.when(first_ref[i] == 1)
        def _():
            s = par_ref[i]
            for cp in weight_copies(be_ref[i], s):
                cp.wait()
            w1b[...] = ws1[s].astype(BF16)
            w3b[...] = ws3[s].astype(BF16)
            w2b[...] = ws2[s].astype(BF16)

            @pl.when(nxt_ref[i] < N_EXPERTS)
            def _():
                for cp in weight_copies(nxt_ref[i], 1 - s):
                    cp.start()

        wait_block(slot)
        gather(tok_next_ref, 1 - slot)
        x = xbuf[slot].astype(BF16)
        a = _dot(x, w1b[...])
        b = _dot(x, w3b[...])
        mid = (a * jax.nn.sigmoid(a)) * b
        o_ref[...] = _dot(mid.astype(BF16), w2b[...])

    @pl.when(i >= nu)
    def _():
        @pl.when(i == nu)
        def _():
            wait_block(slot)

        o_ref[...] = jnp.zeros_like(o_ref)


def expert_ffn(h2, slot_tok, block_expert, n_used, w1, w3, w2, layer):
    n_steps = slot_tok.shape[0] // MOE_BLOCK
    d, de = w1.shape[-2], w1.shape[-1]
    step = jnp.arange(n_steps, dtype=jnp.int32)
    prev = jnp.concatenate([jnp.full((1,), -1, jnp.int32), block_expert[:-1]])
    first = ((block_expert != prev) & (step < n_used[0])).astype(jnp.int32)
    parity = (jnp.cumsum(first) - 1) % 2
    eid = jnp.arange(N_EXPERTS, dtype=jnp.int32)
    has_block = jnp.any((block_expert[None, :] == eid[:, None]) & (first[None, :] == 1), axis=1)
    later = jnp.where((eid[None, :] > eid[:, None]) & has_block[None, :], eid[None, :], N_EXPERTS)
    next_used = jnp.min(later, axis=1)[block_expert]
    grid_spec = pltpu.PrefetchScalarGridSpec(
        num_scalar_prefetch=5,
        grid=(n_steps,),
        in_specs=[pl.BlockSpec((MOE_BLOCK,), lambda i, *_: (i,), memory_space=pltpu.SMEM),
                  pl.BlockSpec((MOE_BLOCK,), lambda i, *_: (jnp.minimum(i + 1, n_steps - 1),),
                               memory_space=pltpu.SMEM),
                  pl.BlockSpec(memory_space=pl.ANY),
                  pl.BlockSpec(memory_space=pl.ANY),
                  pl.BlockSpec(memory_space=pl.ANY),
                  pl.BlockSpec(memory_space=pl.ANY)],
        out_specs=pl.BlockSpec((MOE_BLOCK, d), lambda i, *_: (i, 0)),
        scratch_shapes=[pltpu.VMEM((2, MOE_BLOCK, d), F32),
                        pltpu.VMEM((2, d, de), F32), pltpu.VMEM((2, d, de), F32), pltpu.VMEM((2, de, d), F32),
                        pltpu.VMEM((d, de), BF16), pltpu.VMEM((d, de), BF16), pltpu.VMEM((de, d), BF16),
                        pltpu.SemaphoreType.DMA((2,)), pltpu.SemaphoreType.DMA((2,))],
    )
    return pl.pallas_call(
        functools.partial(_expert_kernel, layer=layer),
        grid_spec=grid_spec,
        out_shape=jax.ShapeDtypeStruct((n_steps * MOE_BLOCK, d), F32),
        compiler_params=_cparams(("arbitrary",)),
        name="expert_ffn",
    )(block_expert, n_used, first, parity.astype(jnp.int32), next_used, slot_tok, slot_tok, h2, w1, w3, w2)


def _combine_kernel(dcur_ref, dnext_ref, y_ref, mod_ref, info_ref, g_ref, yb_hbm, *rest, tm, nt, d, final,
                    tiles_prompt):
    if final:
        op_ref, os_ref, buf, sem = rest
    else:
        o_ref, buf, sem = rest
    i = pl.program_id(0)
    slot = i % 2

    def row_copy(src_row, s, k, j):
        return pltpu.make_async_copy(yb_hbm.at[pl.ds(src_row, 1)], buf.at[s, k, pl.ds(j, 1)], sem.at[s])

    def issue(dref, s):
        def body(j, carry):
            for k in range(TOP_K):
                row_copy(dref[TOP_K * j + k], s, k, j).start()
            return carry

        lax.fori_loop(0, tm, body, 0, unroll=32)

    @pl.when(i == 0)
    def _():
        issue(dcur_ref, 0)

    @pl.when(i + 1 < nt)
    def _():
        issue(dnext_ref, 1 - slot)

    for k in range(TOP_K):
        pltpu.make_async_copy(yb_hbm.at[pl.ds(0, tm)], buf.at[slot, k], sem.at[slot]).wait()
    gates = info_ref[...]
    moe = gates[:, 0:1] * buf[slot, 0] + gates[:, 1:2] * buf[slot, 1]
    y2 = y_ref[...] + mod_ref[:, 5 * d:6 * d] * moe
    if not final:
        o_ref[...] = y2
        return
    y2 = y2 * lax.rsqrt(jnp.mean(y2 * y2, axis=-1, keepdims=True) + EPS) * g_ref[...]

    @pl.when(i < tiles_prompt)
    def _():
        op_ref[...] = y2

    @pl.when(i >= tiles_prompt)
    def _():
        os_ref[...] = y2


def combine(y1, mod_l, info, dest_flat, yb, final_gain, n_prompt_rows, sample_rows, final, tm=256):
    t, d = y1.shape
    nt = t // tm
    tiles_prompt = n_prompt_rows // tm
    midx = functools.partial(_mod_index, tm=tm, n_prompt_rows=n_prompt_rows, sample_rows=sample_rows)
    if final:
        out_specs = _split_specs((tm, d), tiles_prompt)
        out_shape = [jax.ShapeDtypeStruct((n_prompt_rows, d), F32), jax.ShapeDtypeStruct((t - n_prompt_rows, d), F32)]
    else:
        out_specs = pl.BlockSpec((tm, d), lambda i: (i, 0))
        out_shape = jax.ShapeDtypeStruct((t, d), F32)
    return pl.pallas_call(
        functools.partial(_combine_kernel, tm=tm, nt=nt, d=d, final=final, tiles_prompt=tiles_prompt),
        grid=(nt,),
        in_specs=[pl.BlockSpec((TOP_K * tm,), lambda i: (i,), memory_space=pltpu.SMEM),
                  pl.BlockSpec((TOP_K * tm,), lambda i: (jnp.minimum(i + 1, nt - 1),), memory_space=pltpu.SMEM),
                  pl.BlockSpec((tm, d), lambda i: (i, 0)),
                  pl.BlockSpec((None, 1, 6 * d), lambda i: (midx(i), 0, 0)),
                  pl.BlockSpec((tm, TOP_K), lambda i: (i, 0)),
                  pl.BlockSpec((1, d), lambda i: (0, 0)),
                  pl.BlockSpec(memory_space=pl.ANY)],
        out_specs=out_specs,
        out_shape=out_shape,
        scratch_shapes=[pltpu.VMEM((2, TOP_K, tm, d), F32), pltpu.SemaphoreType.DMA((2,))],
        compiler_params=_cparams(("arbitrary",)),
        name="combine",
    )(dest_flat, dest_flat, y1, mod_l, info, final_gain.reshape(1, d), yb)


def moe_layer(y1, h2p, logits, mod_l, router_bias, w1, w3, w2, layer, final_gain, n_prompt_rows, sample_rows, final):
    t, d = y1.shape
    info_t, cnt = route(logits, router_bias)
    expert = info_t[0:2].T.astype(jnp.int32)
    rank = info_t[4:6].T.astype(jnp.int32)
    info = info_t[2:4].T
    counts = cnt[:, 0].reshape(EXPERTS_PER_GROUP, N_GROUPS).T.reshape(-1).astype(jnp.int32)
    padded = (counts + MOE_BLOCK - 1) // MOE_BLOCK * MOE_BLOCK
    padded_end = jnp.cumsum(padded)
    padded_start = padded_end - padded
    dest = (padded_start[expert] + rank).reshape(-1)
    n_blocks = (t * TOP_K) // MOE_BLOCK + N_EXPERTS + 1
    slots = n_blocks * MOE_BLOCK
    block_lo = jnp.arange(n_blocks, dtype=jnp.int32)[:, None] * MOE_BLOCK
    block_expert = jnp.minimum(jnp.sum((padded_end[None, :] <= block_lo).astype(jnp.int32), axis=1), N_EXPERTS - 1)
    n_used = (padded_end[-1:] // MOE_BLOCK).astype(jnp.int32)
    tok = jnp.broadcast_to(jnp.arange(t, dtype=jnp.int32)[:, None], (t, TOP_K)).reshape(-1)
    slot_tok = jnp.zeros((slots,), jnp.int32).at[dest].set(tok)
    yb = expert_ffn(h2p, slot_tok, block_expert, n_used, w1, w3, w2, layer)
    return combine(y1, mod_l, info, dest, yb, final_gain, n_prompt_rows, sample_rows, final)


def rope_tables(n, lead):
    quarter = MLA_ROPE // 4
    tpos = np.arange(n)
    row = (tpos // GRID_W).astype(np.float32)
    col = (tpos % GRID_W).astype(np.float32)
    inv_freq = (np.float32(ROPE_BASE) ** (-np.arange(quarter, dtype=np.float32) / np.float32(quarter))).astype(np.float32)
    lane = np.arange(64)
    pos = np.where(lane[None, :] < 32, row[:, None], col[:, None]).astype(np.float32)
    ang = pos * inv_freq[lane % quarter][None, :]
    cos = np.cos(ang).astype(np.float32)
    sin = np.sin(ang).astype(np.float32)
    sgn = np.where((lane % 32) < quarter, -1.0, 1.0).astype(np.float32)
    sin = sin * sgn[None, :]
    cos = np.concatenate([cos, cos], axis=1)
    sin = np.concatenate([sin, sin], axis=1)
    cos = np.concatenate([np.ones((lead, LANES), np.float32), cos], axis=0)
    sin = np.concatenate([np.zeros((lead, LANES), np.float32), sin], axis=0)
    return jnp.asarray(cos), jnp.asarray(sin)


def _rope(x, cos, sin_signed):
    lane = lax.broadcasted_iota(jnp.int32, x.shape, 1)
    partner = jnp.where((lane & 31) < 16, pltpu.roll(x, LANES - 16, axis=1), pltpu.roll(x, 16, axis=1))
    return x * cos + partner * sin_signed


def _mla_proj_kernel(cq_ref, ckv_ref, misc_ref, qg_ref, kvg_ref, wq_ref, wkv_ref, cos_ref, sin_ref,
                     q_ref, kv_ref, ckvn_ref, kpe_ref, *, scale):
    cq = cq_ref[...].astype(F32)
    qn = cq * lax.rsqrt(jnp.mean(cq * cq, axis=-1, keepdims=True) + EPS) * qg_ref[...]
    qf = _dot(qn.astype(BF16), wq_ref[...])
    half = qf.shape[1] // 2
    cos, sin = cos_ref[...], sin_ref[...]
    q_ref[:, 0:half] = (qf[:, 0:half] * scale).astype(BF16)
    for h in range(MLA_HEADS):
        pe = qf[:, half + h * LANES: half + (h + 1) * LANES]
        q_ref[:, half + h * LANES: half + (h + 1) * LANES] = (_rope(pe, cos, sin) * scale).astype(BF16)
    ckv = ckv_ref[...].astype(F32)
    ckvn = ckv * lax.rsqrt(jnp.mean(ckv * ckv, axis=-1, keepdims=True) + EPS) * kvg_ref[...]
    ckvn_ref[...] = ckvn
    kv_ref[...] = _dot(ckvn.astype(BF16), wkv_ref[...]).astype(BF16)
    misc = misc_ref[...].astype(F32)
    lane = lax.broadcasted_iota(jnp.int32, misc.shape, 1)
    kpe = jnp.where(lane < MLA_ROPE, misc, 0.0)
    kpe_ref[...] = _rope(kpe, cos, sin).astype(BF16)


def mla_project(z, q_gain, kv_gain, wq, wkv, cos_t, sin_t, n_prompt_rows, sample_rows, tm=512):
    t = z.shape[0]
    tp = n_prompt_rows // tm
    per = sample_rows // tm

    def tab(i):
        return (jnp.where(i < tp, 0, 1 + (i - tp) % per), 0)

    scale = float((MLA_NOPE + MLA_ROPE) ** -0.5)
    nq = wq.shape[1]
    return pl.pallas_call(
        functools.partial(_mla_proj_kernel, scale=scale),
        grid=(t // tm,),
        in_specs=[pl.BlockSpec((tm, MLA_Q_LORA), lambda i: (i, AB_CQ // MLA_Q_LORA)),
                  pl.BlockSpec((tm, MLA_KV_LORA), lambda i: (i, AB_CKV // MLA_KV_LORA)),
                  pl.BlockSpec((tm, LANES), lambda i: (i, AB_MISC // LANES)),
                  pl.BlockSpec((1, MLA_Q_LORA), lambda i: (0, 0)),
                  pl.BlockSpec((1, MLA_KV_LORA), lambda i: (0, 0)),
                  pl.BlockSpec(wq.shape, lambda i: (0, 0)),
                  pl.BlockSpec(wkv.shape, lambda i: (0, 0)),
                  pl.BlockSpec((tm, LANES), tab),
                  pl.BlockSpec((tm, LANES), tab)],
        out_specs=[pl.BlockSpec((tm, nq), lambda i: (i, 0)),
                   pl.BlockSpec((tm, wkv.shape[1]), lambda i: (i, 0)),
                   pl.BlockSpec((tm, MLA_KV_LORA), lambda i: (i, 0)),
                   pl.BlockSpec((tm, LANES), lambda i: (i, 0))],
        out_shape=[jax.ShapeDtypeStruct((t, nq), BF16),
                   jax.ShapeDtypeStruct((t, wkv.shape[1]), BF16),
                   jax.ShapeDtypeStruct((t, MLA_KV_LORA), F32),
                   jax.ShapeDtypeStruct((t, LANES), BF16)],
        compiler_params=_cparams(("parallel",)),
        name="mla_project",
    )(z, z, z, q_gain.reshape(1, -1), kv_gain.reshape(1, -1), wq, wkv, cos_t, sin_t)


def _ctx_kv_kernel(ckv_ref, w_ref, o_ref):
    o_ref[...] = _dot(ckv_ref[...].astype(BF16), w_ref[...]).astype(BF16)


def ctx_kv_project(ckv_c, wkv):
    rows = ckv_c.shape[0]
    return pl.pallas_call(
        _ctx_kv_kernel,
        grid=(1,),
        in_specs=[pl.BlockSpec(ckv_c.shape, lambda i: (0, 0)), pl.BlockSpec(wkv.shape, lambda i: (0, 0))],
        out_specs=pl.BlockSpec((rows, wkv.shape[1]), lambda i: (0, 0)),
        out_shape=jax.ShapeDtypeStruct((rows, wkv.shape[1]), BF16),
        compiler_params=_cparams(("arbitrary",)),
        name="ctx_kv_project",
    )(ckv_c, wkv)


def _softmax_pv(s, v):
    m = jnp.max(s, axis=-1, keepdims=True)
    p = jnp.exp(s - m)
    l = jnp.sum(p, axis=-1, keepdims=True)
    return _dot(p.astype(BF16), v) / l


def _mla_prompt_attn_kernel(q_ref, kv_ref, kp_ref, o_ref):
    nh = MLA_HEADS
    kp = kp_ref[...]
    for h in range(nh):
        q = jnp.concatenate([q_ref[:, h * LANES:(h + 1) * LANES], q_ref[:, (nh + h) * LANES:(nh + h + 1) * LANES]], axis=1)
        k = jnp.concatenate([kv_ref[:, h * LANES:(h + 1) * LANES], kp], axis=1)
        v = kv_ref[:, (nh + h) * LANES:(nh + h + 1) * LANES]
        o_ref[:, h * LANES:(h + 1) * LANES] = _softmax_pv(_dot_nt(q, k), v).astype(o_ref.dtype)


def mla_prompt_attention(q_m, kv, kpe, batch, n):
    h = MLA_HEADS
    return pl.pallas_call(
        _mla_prompt_attn_kernel,
        grid=(batch,),
        in_specs=[pl.BlockSpec((n, 2 * h * LANES), lambda b: (b, 0)),
                  pl.BlockSpec((n, 2 * h * LANES), lambda b: (b, 0)),
                  pl.BlockSpec((n, LANES), lambda b: (b, 0))],
        out_specs=pl.BlockSpec((n, h * MLA_DV), lambda b: (b, 0)),
        out_shape=jax.ShapeDtypeStruct((batch * n, h * MLA_DV), BF16),
        compiler_params=_cparams(("parallel",)),
        name="mla_prompt_attention",
    )(q_m, kv, kpe)


def _lane_tile_reduce(x, op):
    r = x[:, 0:LANES]
    for c in range(1, x.shape[1] // LANES):
        r = op(r, x[:, c * LANES:(c + 1) * LANES])
    return r


def _mla_sample_attn_kernel(qn_ref, qp_ref, kcn_ref, kcp_ref, vc_ref, kn_ref, kp_ref, v_ref, o_ref,
                            s_ref, m_ref, acc_ref, *, tk):
    past = kcn_ref.shape[0]
    nk = kn_ref.shape[0] // tk
    q = jnp.concatenate([qn_ref[...], qp_ref[...]], axis=1)
    kc = jnp.concatenate([kcn_ref[...], kcp_ref[...].astype(BF16)], axis=1)
    s = _dot_nt(q, kc)
    s_ref[:, 0:past] = s
    m_ref[...] = _lane_tile_reduce(s, jnp.maximum)

    def scores(j, carry):
        off = pl.multiple_of(j * tk, tk)
        k = jnp.concatenate([kn_ref[pl.ds(off, tk), :], kp_ref[pl.ds(off, tk), :]], axis=1)
        s = _dot_nt(q, k)
        s_ref[:, pl.ds(pl.multiple_of(past + j * tk, LANES), tk)] = s
        m_ref[...] = jnp.maximum(m_ref[...], _lane_tile_reduce(s, jnp.maximum))
        return carry

    lax.fori_loop(0, nk, scores, 0, unroll=2)
    m_ref[...] = jnp.broadcast_to(jnp.max(m_ref[...], axis=-1, keepdims=True), m_ref.shape)

    def weighted_values(s, v):
        m = m_ref[...]
        p = jnp.concatenate([jnp.exp(s[:, c * LANES:(c + 1) * LANES] - m).astype(BF16)
                             for c in range(s.shape[1] // LANES)], axis=1)
        v_ones = jnp.concatenate([v, jnp.ones((v.shape[0], LANES), BF16)], axis=1)
        return _dot(p, v_ones)

    acc_ref[...] = weighted_values(s_ref[:, 0:past], vc_ref[...])

    def weighted(j, carry):
        off = pl.multiple_of(j * tk, tk)
        acc_ref[...] += weighted_values(s_ref[:, pl.ds(pl.multiple_of(past + j * tk, LANES), tk)],
                                        v_ref[pl.ds(off, tk), :])
        return carry

    lax.fori_loop(0, nk, weighted, 0, unroll=2)
    acc = acc_ref[...]
    o_ref[...] = (acc[:, 0:MLA_DV] / acc[:, MLA_DV:]).astype(o_ref.dtype)


def mla_sample_attention(q_m, kv, kpe, kv_ctx, kpe_ctx, row0, batch, n, past, tq=1024, tk=512):
    h = MLA_HEADS
    qb = n // tq
    r0 = row0 // n
    q0 = row0 // tq
    return pl.pallas_call(
        functools.partial(_mla_sample_attn_kernel, tk=tk),
        grid=(batch, h, qb),
        in_specs=[pl.BlockSpec((tq, LANES), lambda b, hh, i: (q0 + b * qb + i, hh)),
                  pl.BlockSpec((tq, LANES), lambda b, hh, i: (q0 + b * qb + i, h + hh)),
                  pl.BlockSpec((past, LANES), lambda b, hh, i: (b, hh)),
                  pl.BlockSpec((past, LANES), lambda b, hh, i: (b, 0)),
                  pl.BlockSpec((past, LANES), lambda b, hh, i: (b, h + hh)),
                  pl.BlockSpec((n, LANES), lambda b, hh, i: (r0 + b, hh)),
                  pl.BlockSpec((n, LANES), lambda b, hh, i: (r0 + b, 0)),
                  pl.BlockSpec((n, LANES), lambda b, hh, i: (r0 + b, h + hh))],
        out_specs=pl.BlockSpec((tq, LANES), lambda b, hh, i: (b * qb + i, hh)),
        out_shape=jax.ShapeDtypeStruct((batch * n, h * MLA_DV), BF16),
        scratch_shapes=[pltpu.VMEM((tq, past + n), F32), pltpu.VMEM((tq, LANES), F32),
                        pltpu.VMEM((tq, MLA_DV + LANES), F32)],
        compiler_params=_cparams(("parallel", "parallel", "arbitrary")),
        name="mla_sample_attention",
    )(q_m, q_m, kv_ctx, kpe_ctx, kv_ctx, kv, kpe, kv)


def _na_prompt_attn_kernel(q_ref, k_ref, v_ref, o_ref, ko_ref, vo_ref, *, scale):
    for h in range(NA_HEADS):
        cols = slice(h * LANES, (h + 1) * LANES)
        k = k_ref[:, cols]
        v = v_ref[:, cols]
        ko_ref[h] = k.astype(F32)
        vo_ref[h] = v.astype(F32)
        q = (q_ref[:, cols].astype(F32) * scale).astype(BF16)
        o_ref[:, cols] = _softmax_pv(_dot_nt(q, k), v).astype(o_ref.dtype)


def na_prompt_attention(z, batch, n):
    h = NA_HEADS
    scale = float(NA_HD ** -0.5)
    w = h * NA_HD
    cq, ck, cv = CD_NQ // w, CD_NK // w, CD_NV // w
    return pl.pallas_call(
        functools.partial(_na_prompt_attn_kernel, scale=scale),
        grid=(batch,),
        in_specs=[pl.BlockSpec((n, w), lambda b: (b, cq)),
                  pl.BlockSpec((n, w), lambda b: (b, ck)),
                  pl.BlockSpec((n, w), lambda b: (b, cv))],
        out_specs=[pl.BlockSpec((n, w), lambda b: (b, 0)),
                   pl.BlockSpec((None, h, n, NA_HD), lambda b: (b, 0, 0, 0)),
                   pl.BlockSpec((None, h, n, NA_HD), lambda b: (b, 0, 0, 0))],
        out_shape=[jax.ShapeDtypeStruct((batch * n, w), BF16),
                   jax.ShapeDtypeStruct((batch, h, n, NA_HD), F32),
                   jax.ShapeDtypeStruct((batch, h, n, NA_HD), F32)],
        compiler_params=_cparams(("parallel",)),
        name="na_prompt_attention",
    )(z, z, z)


def _dot_tn(a, b):
    return lax.dot_general(a, b, (((0,), (0,)), ((), ())), preferred_element_type=F32)


def _head_norm_gate(acc, gate_in, gain):
    outs = []
    for h in range(2):
        a = acc[:, h * LANES:(h + 1) * LANES]
        outs.append(a * lax.rsqrt(jnp.mean(a * a, axis=-1, keepdims=True) + EPS))
    g = gate_in
    return jnp.concatenate(outs, axis=1) * gain * (g * jax.nn.sigmoid(g))


RET_CHUNK = 256
FIN_ROWS = 256


def _ret_kernel(ld_ref, q_ref, k_ref, v_ref, g_ref, cos_ref, sin_ref, gain_ref, s0_ref, o_ref, sf_ref, acc_ref, *, n):
    L = RET_CHUNK
    nc = n // L
    hp = pl.program_id(1)
    lane = lax.broadcasted_iota(jnp.int32, (1, LANES), 1)
    lo_lanes = lane < RET_DK
    row128 = lax.broadcasted_iota(jnp.int32, (LANES, 1), 0)
    ri = lax.broadcasted_iota(jnp.int32, (L, L), 0)
    ci = lax.broadcasted_iota(jnp.int32, (L, L), 1)
    dist = (ri - ci).astype(F32)
    pos = lax.broadcasted_iota(jnp.int32, (L, 1), 0).astype(F32)

    consts = []
    for dr in range(2):
        lg0 = ld_ref[dr, 2 * hp]
        lg1 = ld_ref[dr, 2 * hp + 1]
        lg2 = jnp.where(lo_lanes, lg0, lg1)
        lgc = jnp.where(row128 < RET_DK, lg0, lg1)
        sd = dist if dr == 0 else -dist
        ok = sd >= 0
        intra = [jnp.where(ok, jnp.exp(jnp.where(ok, sd, 0.0) * lg), 0.0) for lg in (lg0, lg1)]
        if dr == 0:
            inter = jnp.exp((pos + 1.0) * lg2)
            to_state = jnp.exp((L - 1.0 - pos) * lg2)
        else:
            inter = jnp.exp((L - pos) * lg2)
            to_state = jnp.exp(pos * lg2)
        consts.append((intra, inter, to_state, jnp.exp(float(L) * lgc)))

    def clear(i, carry):
        acc_ref[pl.ds(pl.multiple_of(i * FIN_ROWS, FIN_ROWS), FIN_ROWS), :] = jnp.zeros((FIN_ROWS, 2 * LANES), F32)
        return carry

    lax.fori_loop(0, n // FIN_ROWS, clear, 0)

    def step(c, dr, s2):
        intra, inter, to_state, cdec = consts[dr]
        rows = pl.ds(pl.multiple_of(c * L, L), L)
        cos, sin = cos_ref[rows, :], sin_ref[rows, :]
        q = _rope(q_ref[rows, :].astype(F32) * (RET_DK ** -0.5), cos, sin)
        k = _rope(k_ref[rows, :].astype(F32), cos, sin)
        kb = k.astype(BF16)
        s2b = s2.astype(BF16)
        kt = (k * to_state).astype(BF16)
        upd = []
        for h in range(2):
            qm = jnp.where(lo_lanes if h == 0 else ~lo_lanes, q, 0.0)
            vh = v_ref[rows, h * LANES:(h + 1) * LANES].astype(BF16)
            sc = _dot_nt(qm.astype(BF16), kb) * intra[h]
            o = _dot(sc.astype(BF16), vh) + _dot((qm * inter).astype(BF16), s2b)
            acc_ref[rows, h * LANES:(h + 1) * LANES] += o
            upd.append(_dot_tn(kt, vh))
        return cdec * s2 + jnp.where(row128 < RET_DK, upd[0], upd[1])

    def body(i, carry):
        sf, sb = carry
        return step(i, 0, sf), step(nc - 1 - i, 1, sb)

    sf, sb = lax.fori_loop(0, nc, body, (s0_ref[0], s0_ref[1]))
    sf_ref[0] = sf
    sf_ref[1] = sb

    def fin(i, carry):
        rows = pl.ds(pl.multiple_of(i * FIN_ROWS, FIN_ROWS), FIN_ROWS)
        o_ref[rows, :] = _head_norm_gate(acc_ref[rows, :], g_ref[rows, :].astype(F32), gain_ref[...]).astype(o_ref.dtype)
        return carry

    lax.fori_loop(0, n // FIN_ROWS, fin, 0)


def retention_mixer(z, log_decay, ret_gain, s0, cos_t, sin_t, row0, batch, n):
    hp = RET_HEADS // 2
    r0 = row0 // n
    cq, ck, cv, cg = CD_RQ // LANES, CD_RK // LANES, CD_RV // (2 * LANES), CD_RG // (2 * LANES)
    grid_spec = pltpu.PrefetchScalarGridSpec(
        num_scalar_prefetch=1,
        grid=(batch, hp),
        in_specs=[pl.BlockSpec((n, LANES), lambda b, p, ld: (r0 + b, cq + p)),
                  pl.BlockSpec((n, LANES), lambda b, p, ld: (r0 + b, ck + p)),
                  pl.BlockSpec((n, 2 * LANES), lambda b, p, ld: (r0 + b, cv + p)),
                  pl.BlockSpec((n, 2 * LANES), lambda b, p, ld: (r0 + b, cg + p)),
                  pl.BlockSpec((n, LANES), lambda b, p, ld: (0, 0)),
                  pl.BlockSpec((n, LANES), lambda b, p, ld: (0, 0)),
                  pl.BlockSpec((1, 2 * LANES), lambda b, p, ld: (0, p)),
                  pl.BlockSpec((None, 2, LANES, LANES), lambda b, p, ld: (b, 0, p, 0))],
        out_specs=[pl.BlockSpec((n, 2 * LANES), lambda b, p, ld: (b, p)),
                   pl.BlockSpec((None, 2, LANES, LANES), lambda b, p, ld: (b, 0, p, 0))],
        scratch_shapes=[pltpu.VMEM((n, 2 * LANES), F32)],
    )
    return pl.pallas_call(
        functools.partial(_ret_kernel, n=n),
        grid_spec=grid_spec,
        out_shape=[jax.ShapeDtypeStruct((batch * n, RET_HEADS * RET_DV), BF16),
                   jax.ShapeDtypeStruct((batch, 2, RET_HEADS * RET_DK, RET_DV), F32)],
        compiler_params=_cparams(("parallel", "parallel")),
        name="retention_mixer",
    )(log_decay.astype(F32), z, z, z, z, cos_t, sin_t, ret_gain.reshape(1, -1), s0)


GLA_CHUNK = 256
GLA_SUB = 64
GLA_SAFE_EXPONENT = 60.0


def _gla_kernel(q_ref, k_ref, v_ref, g_ref, misc_ref, gw_ref, gb_ref, gain_ref, s0_ref, o_ref, sf_ref,
                la_ref, acc_ref, *, n):
    C, SB = GLA_CHUNK, GLA_SUB
    nsb = C // SB
    nc = n // C
    lane = lax.broadcasted_iota(jnp.int32, (1, LANES), 1)
    lo_lanes = lane < GLA_DK
    head_mask = (lo_lanes, ~lo_lanes)
    ri = lax.broadcasted_iota(jnp.int32, (C, C), 0)
    ci = lax.broadcasted_iota(jnp.int32, (C, C), 1)
    rowc = lax.broadcasted_iota(jnp.int32, (C, 1), 0)
    SLAB = 16
    row8 = lax.broadcasted_iota(jnp.int32, (SLAB, 1), 0)
    tri = (jnp.where(ci <= ri, 1.0, 0.0).astype(BF16), jnp.where(ci >= ri, 1.0, 0.0).astype(BF16))
    causal = (ci <= ri, ci >= ri)

    def prep(i, carry):
        rows = pl.ds(pl.multiple_of(i * FIN_ROWS, FIN_ROWS), FIN_ROWS)
        slab = misc_ref[rows, :].astype(BF16)
        for dr in range(2):
            pre = _dot(slab, gw_ref[dr]) + gb_ref[dr]
            la_ref[dr, rows, :] = (jnp.minimum(pre, 0.0) - jnp.log1p(jnp.exp(-jnp.abs(pre)))) * (1.0 / GATE_TAU)
        acc_ref[rows, :] = jnp.zeros((FIN_ROWS, 2 * LANES), F32)
        return carry

    lax.fori_loop(0, n // FIN_ROWS, prep, 0)

    def step(c, dr):
        rev = dr == 1
        base = pl.multiple_of(c * C, C)
        rows = pl.ds(base, C)
        la = la_ref[dr, rows, :]
        hi = la.astype(BF16)
        lo = (la - hi.astype(F32)).astype(BF16)
        b = _dot(tri[dr], hi) + _dot(tri[dr], lo)
        zero = jnp.zeros((1, LANES), F32)
        if not rev:
            bref = [zero if i == 0 else b[SB * i - 1:SB * i, :] for i in range(nsb)]
            bend = [b[SB * i + SB - 1:SB * i + SB, :] for i in range(nsb)]
            blast = b[C - 1:C, :]
        else:
            bref = [zero if i == nsb - 1 else b[SB * (i + 1):SB * (i + 1) + 1, :] for i in range(nsb)]
            bend = [b[SB * i:SB * i + 1, :] for i in range(nsb)]
            blast = b[0:1, :]
        spread = bref[0] - bend[0]
        for i in range(1, nsb):
            spread = jnp.maximum(spread, bref[i] - bend[i])
        worst = jnp.max(spread)

        def fast(st):
            q = q_ref[rows, :].astype(F32) * (GLA_DK ** -0.5)
            k = k_ref[rows, :].astype(F32)
            a_rows = ([], [])
            for i in range(nsb):
                qe = q[SB * i:SB * (i + 1), :] * jnp.exp(b[SB * i:SB * (i + 1), :] - bref[i])
                valid = (rowc < SB * (i + 1)) if not rev else (rowc >= SB * i)
                ke = jnp.where(valid, k * jnp.exp(jnp.where(valid, bref[i] - b, 0.0)), 0.0).astype(BF16)
                for h in range(2):
                    a_rows[h].append(_dot_nt(jnp.where(head_mask[h], qe, 0.0).astype(BF16), ke))
            qi = q * jnp.exp(b)
            ks = (k * jnp.exp(blast - b)).astype(BF16)
            stb = st.astype(BF16)
            upd = []
            for h in range(2):
                a = jnp.where(causal[dr], jnp.concatenate(a_rows[h], axis=0), 0.0)
                vh = v_ref[rows, h * LANES:(h + 1) * LANES].astype(BF16)
                o = _dot(a.astype(BF16), vh) + _dot_nt(jnp.where(head_mask[h], qi, 0.0).astype(BF16), stb)
                acc_ref[rows, h * LANES:(h + 1) * LANES] += o
                upd.append(_dot_tn(vh, ks))
            return jnp.exp(blast) * st + jnp.where(lo_lanes, upd[0], upd[1])

        def slow(st):
            def row_step(r, st):
                t = r if not rev else C - 1 - r
                slab = pl.ds(pl.multiple_of(base + (t // SLAB) * SLAB, SLAB), SLAB)
                pick = row8 == (t % SLAB)
                q8 = jnp.where(pick, q_ref[slab, :].astype(F32) * (GLA_DK ** -0.5), 0.0)
                k8 = jnp.where(pick, k_ref[slab, :].astype(F32), 0.0).astype(BF16)
                la_t = jnp.sum(jnp.where(pick, la_ref[dr, slab, :], 0.0), axis=0, keepdims=True)
                upd = []
                for h in range(2):
                    v8 = jnp.where(pick, v_ref[slab, h * LANES:(h + 1) * LANES].astype(F32), 0.0).astype(BF16)
                    upd.append(_dot_tn(v8, k8))
                st = jnp.exp(la_t) * st + jnp.where(lo_lanes, upd[0], upd[1])
                stb = st.astype(BF16)
                for h in range(2):
                    qh = jnp.where(head_mask[h], q8, 0.0).astype(BF16)
                    acc_ref[slab, h * LANES:(h + 1) * LANES] += _dot_nt(qh, stb)
                return st

            return lax.fori_loop(0, C + 0 * pl.program_id(0), row_step, st)

        return worst, fast, slow

    def body(i, carry):
        worst_f, fast_f, slow_f = step(i, 0)
        worst_b, fast_b, slow_b = step(nc - 1 - i, 1)
        return lax.cond(jnp.maximum(worst_f, worst_b) <= GLA_SAFE_EXPONENT,
                        lambda st: (fast_f(st[0]), fast_b(st[1])),
                        lambda st: (slow_f(st[0]), slow_b(st[1])), carry)

    sf, sb = lax.fori_loop(0, nc, body, (s0_ref[0].T, s0_ref[1].T))
    sf_ref[0] = sf.T
    sf_ref[1] = sb.T

    def fin(i, carry):
        rows = pl.ds(pl.multiple_of(i * FIN_ROWS, FIN_ROWS), FIN_ROWS)
        o_ref[rows, :] = _head_norm_gate(acc_ref[rows, :], g_ref[rows, :].astype(F32), gain_ref[...]).astype(o_ref.dtype)
        return carry

    lax.fori_loop(0, n // FIN_ROWS, fin, 0)


def _prep_gate(gate_w, gate_b):
    gw = jnp.zeros((2, LANES, gate_w.shape[-1]), F32)
    for dr in range(2):
        lo = MLA_ROPE + dr * GATE_RANK
        gw = gw.at[dr, lo:lo + GATE_RANK, :].set(gate_w[dr].astype(F32))
    return gw.astype(BF16), gate_b.astype(F32).reshape(2, 1, -1)


def gla_mixer(z, gate_w, gate_b, gla_gain, s0, row0, batch, n):
    hp = GLA_HEADS // 2
    r0 = row0 // n
    gw, gb = _prep_gate(gate_w, gate_b)
    cq, ck, cv, cg, cm = AB_GQ // LANES, AB_GK // LANES, AB_GV // (2 * LANES), AB_GR // (2 * LANES), AB_MISC // LANES
    return pl.pallas_call(
        functools.partial(_gla_kernel, n=n),
        grid=(batch, hp),
        in_specs=[pl.BlockSpec((n, LANES), lambda b, p: (r0 + b, cq + p)),
                  pl.BlockSpec((n, LANES), lambda b, p: (r0 + b, ck + p)),
                  pl.BlockSpec((n, 2 * LANES), lambda b, p: (r0 + b, cv + p)),
                  pl.BlockSpec((n, 2 * LANES), lambda b, p: (r0 + b, cg + p)),
                  pl.BlockSpec((n, LANES), lambda b, p: (r0 + b, cm)),
                  pl.BlockSpec((2, LANES, LANES), lambda b, p: (0, 0, p)),
                  pl.BlockSpec((2, 1, LANES), lambda b, p: (0, 0, p)),
                  pl.BlockSpec((1, 2 * LANES), lambda b, p: (0, p)),
                  pl.BlockSpec((None, 2, LANES, LANES), lambda b, p: (b, 0, p, 0))],
        out_specs=[pl.BlockSpec((n, 2 * LANES), lambda b, p: (b, p)),
                   pl.BlockSpec((None, 2, LANES, LANES), lambda b, p: (b, 0, p, 0))],
        out_shape=[jax.ShapeDtypeStruct((batch * n, GLA_HEADS * GLA_DV), BF16),
                   jax.ShapeDtypeStruct((batch, 2, GLA_HEADS * GLA_DK, GLA_DV), F32)],
        scratch_shapes=[pltpu.VMEM((2, n, LANES), F32), pltpu.VMEM((n, 2 * LANES), F32)],
        compiler_params=_cparams(("parallel", "parallel")),
        name="gla_mixer",
    )(z, z, z, z, z, gw, gb, gla_gain.reshape(1, -1), s0)


NA_QROWS = 8
NA_WROWS = 16
NA_NEG = -1e30
NA_TAB = 32


def _na_bias_table(na_bias_l):
    qc = np.arange(GRID_W)[:, None]
    kc = np.arange(GRID_W)[None, :]
    ws = np.clip(qc - NA_COLS // 2, 0, GRID_W - NA_COLS)
    col_ok = (kc >= ws) & (kc < ws + NA_COLS)
    col_off = np.clip(kc - qc + NA_COLS - 1, 0, 2 * NA_COLS - 2)
    aa = (np.arange(NA_TAB) - NA_QROWS)[:, None] + np.arange(2)[None, :]
    n_row, n_col = 2 * NA_ROWS - 1, 2 * NA_COLS - 1
    sel_row = (aa[:, :, None] == np.arange(n_row)[None, None, :]).astype(np.float32)
    sel_col = ((col_off[:, :, None] == np.arange(n_col)[None, None, :]) & col_ok[:, :, None]).astype(np.float32)
    fill = np.where(col_ok, 0.0, NA_NEG).astype(np.float32)
    tab = jnp.einsum('hac,sea,qkc->hsqek', na_bias_l.astype(F32), jnp.asarray(sel_row), jnp.asarray(sel_col),
                     precision=lax.Precision.HIGHEST)
    tab = tab + jnp.asarray(fill)[None, None, :, None, :]
    h = na_bias_l.shape[0]
    return tab.reshape(h, NA_TAB, GRID_W, 2 * GRID_W)


def _na_latent_kernel(q_ref, k_ref, v_ref, kc_ref, vc_ref, tab_ref, o_ref, s_ref, *, scale, rows_total):
    g = pl.program_id(2)
    r0 = g * NA_QROWS
    start = jnp.clip(r0 - NA_ROWS // 2, 0, rows_total - NA_WROWS)
    nloc = NA_WROWS * GRID_W
    krows = pl.ds(pl.multiple_of(start * GRID_W, GRID_W), nloc)
    q = (q_ref[...].astype(F32) * scale).astype(BF16)
    kw = k_ref[krows, :]
    s_ref[:, 0:nloc] = _dot_nt(q, kw)
    s_ref[:, nloc:] = _dot_nt(q, kc_ref[...].astype(BF16))
    lane = lax.broadcasted_iota(jnp.int32, (1, LANES), 1)
    for qr in range(NA_QROWS):
        r = r0 + qr
        rs = jnp.clip(r - NA_ROWS // 2, 0, rows_total - NA_ROWS)
        for kp in range(NA_WROWS // 2):
            k0 = start + 2 * kp
            ok0 = (k0 >= rs) & (k0 < rs + NA_ROWS)
            ok1 = (k0 + 1 >= rs) & (k0 + 1 < rs + NA_ROWS)
            neg = jnp.where(lane < GRID_W, jnp.where(ok0, 0.0, NA_NEG), jnp.where(ok1, 0.0, NA_NEG))
            slot = k0 - r + (NA_ROWS - 1) + NA_QROWS
            s_ref[qr * GRID_W:(qr + 1) * GRID_W, kp * LANES:(kp + 1) * LANES] += tab_ref[slot] + neg
    s = s_ref[...]
    m = jnp.max(s, axis=-1, keepdims=True)
    p = jnp.exp(s - m)
    l = jnp.sum(p, axis=-1, keepdims=True)
    pb = p.astype(BF16)
    o = _dot(pb[:, 0:nloc], v_ref[krows, :]) + _dot(pb[:, nloc:], vc_ref[...].astype(BF16))
    o_ref[...] = (o / l).astype(o_ref.dtype)


def na_latent_attention(z, k_ctx, v_ctx, na_bias_l, row0, batch, n):
    h = NA_HEADS
    past = k_ctx.shape[2]
    rows_total = n // GRID_W
    tq = NA_QROWS * GRID_W
    qb = n // tq
    r0 = row0 // n
    q0 = row0 // tq
    cq, ck, cv = CD_NQ // LANES, CD_NK // LANES, CD_NV // LANES
    tab = _na_bias_table(na_bias_l)
    return pl.pallas_call(
        functools.partial(_na_latent_kernel, scale=float(NA_HD ** -0.5), rows_total=rows_total),
        grid=(batch, h, qb),
        in_specs=[pl.BlockSpec((tq, LANES), lambda b, hh, g: (q0 + b * qb + g, cq + hh)),
                  pl.BlockSpec((n, LANES), lambda b, hh, g: (r0 + b, ck + hh)),
                  pl.BlockSpec((n, LANES), lambda b, hh, g: (r0 + b, cv + hh)),
                  pl.BlockSpec((None, None, past, NA_HD), lambda b, hh, g: (b, hh, 0, 0)),
                  pl.BlockSpec((None, None, past, NA_HD), lambda b, hh, g: (b, hh, 0, 0)),
                  pl.BlockSpec((None, NA_TAB, GRID_W, 2 * GRID_W), lambda b, hh, g: (hh, 0, 0, 0))],
        out_specs=pl.BlockSpec((tq, LANES), lambda b, hh, g: (b * qb + g, hh)),
        out_shape=jax.ShapeDtypeStruct((batch * n, h * NA_HD), BF16),
        scratch_shapes=[pltpu.VMEM((tq, NA_WROWS * GRID_W + past), F32)],
        compiler_params=_cparams(("parallel", "parallel", "arbitrary")),
        name="na_latent_attention",
    )(z, z, z, k_ctx, v_ctx, tab)


def _prep_w_in_ab(w):
    d = w.shape[0]
    sizes = (512, 512, 1024, 1024, 2 * GATE_RANK, MLA_Q_LORA, MLA_KV_LORA, MLA_ROPE)
    gq, gk, gv, gr, glr, cq, ckv, kpe = jnp.split(w, np.cumsum(sizes)[:-1].tolist(), axis=1)
    pad = jnp.zeros((d, AB_WIDTH - AB_MISC - MLA_ROPE - 2 * GATE_RANK), w.dtype)
    return jnp.concatenate([gq, gk, gv, gr, cq, ckv, kpe, glr, pad], axis=1).astype(BF16)


def _prep_w_uq(w):
    r = w.shape[0]
    w3 = w.reshape(r, MLA_HEADS, MLA_NOPE + MLA_ROPE)
    nope = w3[:, :, :MLA_NOPE].reshape(r, MLA_HEADS * MLA_NOPE)
    pe = jnp.pad(w3[:, :, MLA_NOPE:], ((0, 0), (0, 0), (0, LANES - MLA_ROPE))).reshape(r, MLA_HEADS * LANES)
    return jnp.concatenate([nope, pe], axis=1).astype(BF16)


def _prep_w_ukv(w):
    r = w.shape[0]
    w3 = w.reshape(r, MLA_HEADS, MLA_NOPE + MLA_DV)
    return jnp.concatenate([w3[:, :, :MLA_NOPE].reshape(r, -1), w3[:, :, MLA_NOPE:].reshape(r, -1)], axis=1).astype(BF16)


def _prep_router(router_w):
    w = jnp.pad(_member_major(router_w.astype(F32)), ((0, 0), (0, LANES - N_EXPERTS)))
    hi = w.astype(BF16)
    lo = (w - hi.astype(F32)).astype(BF16)
    return jnp.stack([hi, lo], axis=0)


def kernel(x_prompt, x_sample, state_gla, cache_mla_ckv, cache_mla_kpe, state_ret, cache_na_k, cache_na_v,
           c, c_ctx, ada_w, ada_b, norm1, norm2, w_in_ab, gla_gate_w, gla_gate_b, gla_gain, mla_q_gain,
           mla_w_uq, mla_kv_gain, mla_w_ukv, w_in_cd, ret_log_decay, ret_gain, na_bias, w_out,
           router_w, router_bias, exp_w1, exp_w3, exp_w2, final_norm):
    bp, n_p, d = x_prompt.shape
    bs, n_s, _ = x_sample.shape
    past = cache_mla_ckv.shape[2]
    depth = ada_w.shape[0]
    tp, ts = bp * n_p, bs * n_s
    ys = [x_prompt.reshape(tp, d), x_sample.reshape(ts, d)]

    cond8 = jnp.zeros((8, d), F32).at[0].set(c_ctx).at[1:1 + bs].set(c)
    mod = adaln_all(cond8, ada_w, ada_b)
    rw_split = _prep_router(router_w)
    w_out_bf = w_out.astype(BF16)
    cos_t, sin_t = rope_tables(n_s, 512)
    cos_s, sin_s = cos_t[512:], sin_t[512:]
    cos_id, sin_id = cos_t[:n_p], sin_t[:n_p]

    outs = {}
    for l in range(depth):
        i = l // 2
        mod_l = mod[l].reshape(8, 1, 6 * d)
        final = l == depth - 1
        if l % 2 == 0:
            z = win_project(ys, mod_l, norm1[l], _prep_w_in_ab(w_in_ab[i]), tp, n_s)
            s0p = jnp.zeros((bp, 2, GLA_HEADS * GLA_DK, GLA_DV), F32)
            s0s = state_gla[:, i].reshape(bs, 2, GLA_HEADS * GLA_DK, GLA_DV)
            og_p, st_p = gla_mixer(z, gla_gate_w[i], gla_gate_b[i], gla_gain[i], s0p, 0, bp, n_p)
            og_s, _ = gla_mixer(z, gla_gate_w[i], gla_gate_b[i], gla_gain[i], s0s, tp, bs, n_s)
            st_p = st_p.reshape(bp, 2, GLA_HEADS, GLA_DK, GLA_DV)
            o_a = (og_p, og_s)
            wq, wkv = _prep_w_uq(mla_w_uq[i]), _prep_w_ukv(mla_w_ukv[i])
            q_m, kv, ckvn, kpe = mla_project(z, mla_q_gain[i], mla_kv_gain[i], wq, wkv, cos_t, sin_t, tp, n_s)
            kv_ctx = ctx_kv_project(cache_mla_ckv[:, i].reshape(bs * past, -1), wkv)
            kpe_ctx = jnp.pad(cache_mla_kpe[:, i].reshape(bs * past, -1), ((0, 0), (0, LANES - MLA_ROPE)))
            om_p = mla_prompt_attention(q_m, kv, kpe, bp, n_p)
            om_s = mla_sample_attention(q_m, kv, kpe, kv_ctx, kpe_ctx, tp, bs, n_s, past)
            o_b = (om_p, om_s)
            outs.setdefault('gla', []).append(st_p)
            outs.setdefault('ckv', []).append(ckvn[:tp].reshape(bp, n_p, -1))
            outs.setdefault('kpe', []).append(z[:tp, AB_MISC:AB_MISC + MLA_ROPE].astype(F32).reshape(bp, n_p, -1))
        else:
            z = win_project(ys, mod_l, norm1[l], w_in_cd[i].astype(BF16), tp, n_s)
            s0p = jnp.zeros((bp, 2, RET_HEADS * RET_DK, RET_DV), F32)
            s0s = state_ret[:, i].reshape(bs, 2, RET_HEADS * RET_DK, RET_DV)
            or_p, st_p = retention_mixer(z, ret_log_decay[i], ret_gain[i], s0p, cos_id, sin_id, 0, bp, n_p)
            or_s, _ = retention_mixer(z, ret_log_decay[i], ret_gain[i], s0s, cos_s, sin_s, tp, bs, n_s)
            st_p = st_p.reshape(bp, 2, RET_HEADS, RET_DK, RET_DV)
            o_a = (or_p, or_s)
            on_p, kn, vn = na_prompt_attention(z, bp, n_p)
            on_s = na_latent_attention(z, cache_na_k[:, i], cache_na_v[:, i], na_bias[i], tp, bs, n_s)
            o_b = (on_p, on_s)
            outs.setdefault('ret', []).append(st_p)
            outs.setdefault('nak', []).append(kn)
            outs.setdefault('nav', []).append(vn)
        y1, h2, logits = wout_project(o_a, o_b, ys, mod_l, w_out_bf[l], norm2[l], rw_split, tp, n_s)
        y = moe_layer(y1, h2, logits, mod_l, router_bias, exp_w1, exp_w3, exp_w2, l, final_norm, tp, n_s, final)
        ys = list(y) if final else [y]

    return (ys[0].reshape(bp, n_p, d), ys[1].reshape(bs, n_s, d),
            jnp.stack(outs['gla'], axis=1), jnp.stack(outs['ckv'], axis=1), jnp.stack(outs['kpe'], axis=1),
            jnp.stack(outs['ret'], axis=1), jnp.stack(outs['nak'], axis=1), jnp.stack(outs['nav'], axis=1))
```

```python
import functools

import numpy as np
import jax
import jax.numpy as jnp
from jax import lax
from jax.experimental import pallas as pl
from jax.experimental.pallas import tpu as pltpu

F32 = jnp.float32
BF16 = jnp.bfloat16

EPS = 1e-6
GRID_W = 64
ROPE_BASE = 10000.0
GATE_RANK = 16
GATE_TAU = 16.0
GLA_HEADS, GLA_DK, GLA_DV = 8, 64, 128
MLA_HEADS, MLA_Q_LORA, MLA_KV_LORA, MLA_NOPE, MLA_ROPE, MLA_DV = 8, 512, 256, 128, 64, 128
RET_HEADS, RET_DK, RET_DV = 8, 64, 128
NA_HEADS, NA_HD, NA_ROWS, NA_COLS = 8, 128, 8, 16
N_EXPERTS, N_GROUPS, TOP_K, D_EXPERT = 32, 8, 2, 512
EXPERTS_PER_GROUP = N_EXPERTS // N_GROUPS
MOE_BLOCK = 128
LANES = 128

AB_GQ, AB_GK, AB_GV, AB_GR, AB_CQ, AB_CKV, AB_MISC, AB_WIDTH = 0, 512, 1024, 2048, 3072, 3584, 3840, 4096
CD_RQ, CD_RK, CD_RV, CD_RG, CD_NQ, CD_NK, CD_NV, CD_WIDTH = 0, 512, 1024, 2048, 3072, 4096, 5120, 6144

VMEM_LIMIT = 56 * 1024 * 1024


def _cparams(sem):
    return pltpu.CompilerParams(dimension_semantics=sem, vmem_limit_bytes=VMEM_LIMIT)


def _dot(a, b):
    return jnp.dot(a, b, preferred_element_type=F32)


def _dot_nt(a, b):
    return lax.dot_general(a, b, (((1,), (1,)), ((), ())), preferred_element_type=F32)


def _adaln_kernel(c_ref, w_ref, b_ref, o_ref):
    c = c_ref[...]
    s = (c * jax.nn.sigmoid(c)).astype(BF16)
    o_ref[...] = _dot(s, w_ref[...].astype(BF16)) + b_ref[...]


def adaln_all(cond8, ada_w, ada_b, tn=512):
    depth, d, n6 = ada_w.shape
    return pl.pallas_call(
        _adaln_kernel,
        grid=(depth, n6 // tn),
        in_specs=[pl.BlockSpec((8, d), lambda l, j: (0, 0)),
                  pl.BlockSpec((None, d, tn), lambda l, j: (l, 0, j)),
                  pl.BlockSpec((None, 1, tn), lambda l, j: (l, 0, j))],
        out_specs=pl.BlockSpec((None, 8, tn), lambda l, j: (l, 0, j)),
        out_shape=jax.ShapeDtypeStruct((depth, 8, n6), F32),
        compiler_params=_cparams(("parallel", "parallel")),
        name="adaln",
    )(cond8, ada_w, ada_b.reshape(depth, 1, n6))


def _mod_index(i, tm, n_prompt_rows, sample_rows):
    tp = n_prompt_rows // tm
    return jnp.where(i < tp, 0, 1 + (i - tp) // (sample_rows // tm))


def _split_specs(block, tiles_prompt, n_grid_axes=1):
    if n_grid_axes == 1:
        return [pl.BlockSpec(block, lambda i: (jnp.minimum(i, tiles_prompt - 1), 0)),
                pl.BlockSpec(block, lambda i: (jnp.maximum(i - tiles_prompt, 0), 0))]
    return [pl.BlockSpec(block, lambda i, j: (jnp.minimum(i, tiles_prompt - 1), 0)),
            pl.BlockSpec(block, lambda i, j: (jnp.maximum(i - tiles_prompt, 0), 0))]


def _win_kernel(*refs, d, tiles_prompt, split):
    if split:
        xp_ref, xs_ref, mod_ref, g_ref, w_ref, o_ref, h_ref = refs
    else:
        xp_ref, mod_ref, g_ref, w_ref, o_ref, h_ref = refs

    def norm_mod(x_ref):
        x = x_ref[...]
        xn = x * lax.rsqrt(jnp.mean(x * x, axis=-1, keepdims=True) + EPS) * g_ref[...]
        h = xn * (1.0 + mod_ref[:, d:2 * d]) + mod_ref[:, 0:d]
        h_ref[...] = h.astype(BF16)

    first = pl.program_id(1) == 0
    if split:
        is_prompt = pl.program_id(0) < tiles_prompt
        pl.when(first & is_prompt)(lambda: norm_mod(xp_ref))
        pl.when(first & jnp.logical_not(is_prompt))(lambda: norm_mod(xs_ref))
    else:
        pl.when(first)(lambda: norm_mod(xp_ref))

    o_ref[...] = _dot(h_ref[...], w_ref[...]).astype(o_ref.dtype)


def win_project(xs, mod_l, gain, w_bf16, n_prompt_rows, sample_rows, tm=1024, tn=512):
    split = len(xs) == 2
    d = xs[0].shape[1]
    t = sum(x.shape[0] for x in xs)
    width = w_bf16.shape[1]
    tiles_prompt = n_prompt_rows // tm
    midx = functools.partial(_mod_index, tm=tm, n_prompt_rows=n_prompt_rows, sample_rows=sample_rows)
    x_specs = (_split_specs((tm, d), tiles_prompt, 2) if split else [pl.BlockSpec((tm, d), lambda i, j: (i, 0))])
    return pl.pallas_call(
        functools.partial(_win_kernel, d=d, tiles_prompt=tiles_prompt, split=split),
        grid=(t // tm, width // tn),
        in_specs=x_specs + [pl.BlockSpec((None, 1, 6 * d), lambda i, j: (midx(i), 0, 0)),
                            pl.BlockSpec((1, d), lambda i, j: (0, 0)),
                            pl.BlockSpec((d, tn), lambda i, j: (0, j))],
        out_specs=pl.BlockSpec((tm, tn), lambda i, j: (i, j)),
        out_shape=jax.ShapeDtypeStruct((t, width), BF16),
        scratch_shapes=[pltpu.VMEM((tm, d), BF16)],
        compiler_params=_cparams(("parallel", "arbitrary")),
        name="win_project",
    )(*xs, mod_l, gain.reshape(1, d), w_bf16)


def _wout_kernel(*refs, d, tiles_prompt, split_y):
    if split_y:
        (oap_ref, oas_ref, obp_ref, obs_ref, yp_ref, ys_ref, mod_ref, wa_ref, wb_ref, g_ref, rw_ref,
         ynew_ref, h2_ref, lg_ref) = refs
    else:
        (oap_ref, oas_ref, obp_ref, obs_ref, yp_ref, mod_ref, wa_ref, wb_ref, g_ref, rw_ref,
         ynew_ref, h2_ref, lg_ref) = refs
        ys_ref = yp_ref
    is_prompt = pl.program_id(0) < tiles_prompt
    oa = jnp.where(is_prompt, oap_ref[...], oas_ref[...])
    ob = jnp.where(is_prompt, obp_ref[...], obs_ref[...])
    y = jnp.where(is_prompt, yp_ref[...], ys_ref[...]) if split_y else yp_ref[...]
    out = _dot(oa, wa_ref[...]) + _dot(ob, wb_ref[...])
    y1 = y + mod_ref[:, 2 * d:3 * d] * out
    ynew_ref[...] = y1
    xn = y1 * lax.rsqrt(jnp.mean(y1 * y1, axis=-1, keepdims=True) + EPS) * g_ref[...]
    h = xn * (1.0 + mod_ref[:, 4 * d:5 * d]) + mod_ref[:, 3 * d:4 * d]
    h2_ref[...] = h
    hh = h.astype(BF16)
    hl = (h - hh.astype(F32)).astype(BF16)
    lg_ref[...] = _dot(hh, rw_ref[0]) + _dot(hl, rw_ref[0]) + _dot(hh, rw_ref[1])


def wout_project(o_a, o_b, ys, mod_l, w_out_bf16, gain2, rw_split, n_prompt_rows, sample_rows, tm=256):
    split_y = len(ys) == 2
    d = ys[0].shape[1]
    t = sum(x.shape[0] for x in ys)
    ka, kb = o_a[0].shape[1], o_b[0].shape[1]
    tiles_prompt = n_prompt_rows // tm
    midx = functools.partial(_mod_index, tm=tm, n_prompt_rows=n_prompt_rows, sample_rows=sample_rows)
    y_specs = _split_specs((tm, d), tiles_prompt) if split_y else [pl.BlockSpec((tm, d), lambda i: (i, 0))]
    return pl.pallas_call(
        functools.partial(_wout_kernel, d=d, tiles_prompt=tiles_prompt, split_y=split_y),
        grid=(t // tm,),
        in_specs=_split_specs((tm, ka), tiles_prompt) + _split_specs((tm, kb), tiles_prompt) + y_specs + [
                  pl.BlockSpec((None, 1, 6 * d), lambda i: (midx(i), 0, 0)),
                  pl.BlockSpec((ka, d), lambda i: (0, 0)),
                  pl.BlockSpec((kb, d), lambda i: (1, 0)),
                  pl.BlockSpec((1, d), lambda i: (0, 0)),
                  pl.BlockSpec((2, d, LANES), lambda i: (0, 0, 0))],
        out_specs=[pl.BlockSpec((tm, d), lambda i: (i, 0)),
                   pl.BlockSpec((tm, d), lambda i: (i, 0)),
                   pl.BlockSpec((tm, LANES), lambda i: (i, 0))],
        out_shape=[jax.ShapeDtypeStruct((t, d), F32),
                   jax.ShapeDtypeStruct((t, d), F32),
                   jax.ShapeDtypeStruct((t, LANES), F32)],
        compiler_params=_cparams(("parallel",)),
        name="wout_project",
    )(*o_a, *o_b, *ys, mod_l, w_out_bf16, w_out_bf16, gain2.reshape(1, d), rw_split)


def _member_major(x_experts_last):
    shp = x_experts_last.shape[:-1]
    return x_experts_last.reshape(shp + (N_GROUPS, EXPERTS_PER_GROUP)).swapaxes(-1, -2).reshape(shp + (N_EXPERTS,))


def _route_kernel(lg_ref, bias_ref, info_ref, cnt_ref, run_ref, *, tm):
    @pl.when(pl.program_id(0) == 0)
    def _():
        run_ref[...] = jnp.zeros_like(run_ref)

    ng, nk = N_GROUPS, EXPERTS_PER_GROUP
    neg = jnp.float32(-jnp.inf)
    lgt = lg_ref[...].T
    aff = [jax.nn.sigmoid(lgt[k * ng:(k + 1) * ng, :]) for k in range(nk)]
    sel = [aff[k] + bias_ref[k * ng:(k + 1) * ng, :] for k in range(nk)]

    def first_max(vals):
        m = vals[0]
        for v in vals[1:]:
            m = jnp.maximum(m, v)
        idx = jnp.full(m.shape, float(nk), F32)
        for k in reversed(range(nk)):
            idx = jnp.where(vals[k] == m, float(k), idx)
        return m, idx

    m1, i1 = first_max(sel)
    m2, i2 = first_max([jnp.where(i1 == k, neg, sel[k]) for k in range(nk)])
    gscore = m1 + m2
    gid = lax.broadcasted_iota(jnp.int32, gscore.shape, 0).astype(F32)
    gmax = jnp.max(gscore, axis=0, keepdims=True)
    g = jnp.min(jnp.where(gscore == gmax, gid, 99.0), axis=0, keepdims=True)
    in_g = gid == g

    def pick(masks, vals):
        tot = jnp.zeros((1, tm), F32)
        for msk, v in zip(masks, vals):
            tot = tot + jnp.sum(jnp.where(msk, v, 0.0), axis=0, keepdims=True)
        return tot

    hit1 = [in_g & (i1 == k) for k in range(nk)]
    hit2 = [in_g & (i2 == k) for k in range(nk)]
    a1, a2 = pick(hit1, aff), pick(hit2, aff)
    den = a1 + a2
    e1 = g * nk + pick([in_g], [i1])
    e2 = g * nk + pick([in_g], [i2])
    oh = jnp.concatenate([jnp.where(hit1[k] | hit2[k], 1.0, 0.0) for k in range(nk)], axis=0)
    r_i = lax.broadcasted_iota(jnp.int32, (tm, tm), 0)
    c_i = lax.broadcasted_iota(jnp.int32, (tm, tm), 1)
    earlier = jnp.where(r_i < c_i, 1.0, 0.0).astype(BF16)
    cum = _dot(oh.astype(BF16), earlier) + run_ref[:, 0:1]
    cums = [cum[k * ng:(k + 1) * ng, :] for k in range(nk)]
    r1, r2 = pick(hit1, cums), pick(hit2, cums)
    run_new = run_ref[:, 0:1] + jnp.sum(oh, axis=1, keepdims=True)
    run_ref[...] = jnp.broadcast_to(run_new, run_ref.shape)
    cnt_ref[...] = jnp.broadcast_to(run_new, cnt_ref.shape)
    info_ref[...] = jnp.concatenate([e1, e2, a1 / den, a2 / den, r1, r2, jnp.zeros((2, tm), F32)], axis=0)


def route(logits, router_bias, tm=256):
    t = logits.shape[0]
    bias = jnp.broadcast_to(_member_major(router_bias.astype(F32))[:, None], (N_EXPERTS, tm))
    return pl.pallas_call(
        functools.partial(_route_kernel, tm=tm),
        grid=(t // tm,),
        in_specs=[pl.BlockSpec((tm, LANES), lambda i: (i, 0)),
                  pl.BlockSpec((N_EXPERTS, tm), lambda i: (0, 0))],
        out_specs=[pl.BlockSpec((8, tm), lambda i: (0, i)),
                   pl.BlockSpec((N_EXPERTS, LANES), lambda i: (0, 0))],
        out_shape=[jax.ShapeDtypeStruct((8, t), F32),
                   jax.ShapeDtypeStruct((N_EXPERTS, LANES), F32)],
        scratch_shapes=[pltpu.VMEM((N_EXPERTS, LANES), F32)],
        compiler_params=_cparams(("arbitrary",)),
        name="route",
    )(logits, bias)


def _expert_kernel(be_ref, nu_ref, tok_ref, tok_next_ref, h_hbm, w1_ref, w3_ref, w2_ref, o_ref,
                   xbuf, xcur, w1b, w3b, w2b, sem):
    i = pl.program_id(0)
    slot = i % 2
    nu = nu_ref[0]

    def gather(idx_ref, s):
        for j in range(MOE_BLOCK):
            pltpu.make_async_copy(h_hbm.at[pl.ds(idx_ref[j], 1)], xbuf.at[s, pl.ds(j, 1)], sem.at[s]).start()

    def wait_block(s):
        pltpu.make_async_copy(h_hbm.at[pl.ds(0, MOE_BLOCK)], xbuf.at[s], sem.at[s]).wait()

    @pl.when(i == 0)
    def _():
        gather(tok_ref, 0)

    @pl.when(i < nu)
    def _():
        @pl.when((i == 0) | (be_ref[i] != be_ref[jnp.maximum(i - 1, 0)]))
        def _():
            w1b[...] = w1_ref[...].astype(BF16)
            w3b[...] = w3_ref[...].astype(BF16)
            w2b[...] = w2_ref[...].astype(BF16)

        wait_block(slot)
        xcur[...] = xbuf[slot].astype(BF16)
        gather(tok_next_ref, 1 - slot)
        x = xcur[...]
        a = _dot(x, w1b[...])
        b = _dot(x, w3b[...])
        mid = (a * jax.nn.sigmoid(a)) * b
        o_ref[...] = _dot(mid.astype(BF16), w2b[...])

    @pl.when(i >= nu)
    def _():
        @pl.when(i == nu)
        def _():
            wait_block(slot)

        o_ref[...] = jnp.zeros_like(o_ref)


def expert_ffn(h2, slot_tok, block_expert, n_used, w1, w3, w2, layer):
    n_steps = slot_tok.shape[0] // MOE_BLOCK
    d, de = w1.shape[-2], w1.shape[-1]
    grid_spec = pltpu.PrefetchScalarGridSpec(
        num_scalar_prefetch=2,
        grid=(n_steps,),
        in_specs=[pl.BlockSpec((MOE_BLOCK,), lambda i, be, nu: (i,), memory_space=pltpu.SMEM),
                  pl.BlockSpec((MOE_BLOCK,), lambda i, be, nu: (jnp.minimum(i + 1, n_steps - 1),),
                               memory_space=pltpu.SMEM),
                  pl.BlockSpec(memory_space=pl.ANY),
                  pl.BlockSpec((None, None, d, de), lambda i, be, nu: (layer, be[i], 0, 0)),
                  pl.BlockSpec((None, None, d, de), lambda i, be, nu: (layer, be[i], 0, 0)),
                  pl.BlockSpec((None, None, de, d), lambda i, be, nu: (layer, be[i], 0, 0))],
        out_specs=pl.BlockSpec((MOE_BLOCK, d), lambda i, be, nu: (i, 0)),
        scratch_shapes=[pltpu.VMEM((2, MOE_BLOCK, d), F32), pltpu.VMEM((MOE_BLOCK, d), BF16),
                        pltpu.VMEM((d, de), BF16), pltpu.VMEM((d, de), BF16), pltpu.VMEM((de, d), BF16),
                        pltpu.SemaphoreType.DMA((2,))],
    )
    return pl.pallas_call(
        _expert_kernel,
        grid_spec=grid_spec,
        out_shape=jax.ShapeDtypeStruct((n_steps * MOE_BLOCK, d), F32),
        compiler_params=_cparams(("arbitrary",)),
        name="expert_ffn",
    )(block_expert, n_used, slot_tok, slot_tok, h2, w1, w3, w2)


def _combine_kernel(dcur_ref, dnext_ref, y_ref, mod_ref, info_ref, g_ref, yb_hbm, *rest, tm, nt, d, final,
                    tiles_prompt):
    if final:
        op_ref, os_ref, buf, sem = rest
    else:
        o_ref, buf, sem = rest
    i = pl.program_id(0)
    slot = i % 2

    def row_copy(src_row, s, k, j):
        return pltpu.make_async_copy(yb_hbm.at[pl.ds(src_row, 1)], buf.at[s, k, pl.ds(j, 1)], sem.at[s])

    def issue(dref, s):
        def body(j, carry):
            for k in range(TOP_K):
                row_copy(dref[TOP_K * j + k], s, k, j).start()
            return carry

        lax.fori_loop(0, tm, body, 0, unroll=32)

    @pl.when(i == 0)
    def _():
        issue(dcur_ref, 0)

    @pl.when(i + 1 < nt)
    def _():
        issue(dnext_ref, 1 - slot)

    for k in range(TOP_K):
        pltpu.make_async_copy(yb_hbm.at[pl.ds(0, tm)], buf.at[slot, k], sem.at[slot]).wait()
    gates = info_ref[...]
    moe = gates[:, 0:1] * buf[slot, 0] + gates[:, 1:2] * buf[slot, 1]
    y2 = y_ref[...] + mod_ref[:, 5 * d:6 * d] * moe
    if not final:
        o_ref[...] = y2
        return
    y2 = y2 * lax.rsqrt(jnp.mean(y2 * y2, axis=-1, keepdims=True) + EPS) * g_ref[...]

    @pl.when(i < tiles_prompt)
    def _():
        op_ref[...] = y2

    @pl.when(i >= tiles_prompt)
    def _():
        os_ref[...] = y2


def combine(y1, mod_l, info, dest_flat, yb, final_gain, n_prompt_rows, sample_rows, final, tm=256):
    t, d = y1.shape
    nt = t // tm
    tiles_prompt = n_prompt_rows // tm
    midx = functools.partial(_mod_index, tm=tm, n_prompt_rows=n_prompt_rows, sample_rows=sample_rows)
    if final:
        out_specs = _split_specs((tm, d), tiles_prompt)
        out_shape = [jax.ShapeDtypeStruct((n_prompt_rows, d), F32), jax.ShapeDtypeStruct((t - n_prompt_rows, d), F32)]
    else:
        out_specs = pl.BlockSpec((tm, d), lambda i: (i, 0))
        out_shape = jax.ShapeDtypeStruct((t, d), F32)
    return pl.pallas_call(
        functools.partial(_combine_kernel, tm=tm, nt=nt, d=d, final=final, tiles_prompt=tiles_prompt),
        grid=(nt,),
        in_specs=[pl.BlockSpec((TOP_K * tm,), lambda i: (i,), memory_space=pltpu.SMEM),
                  pl.BlockSpec((TOP_K * tm,), lambda i: (jnp.minimum(i + 1, nt - 1),), memory_space=pltpu.SMEM),
                  pl.BlockSpec((tm, d), lambda i: (i, 0)),
                  pl.BlockSpec((None, 1, 6 * d), lambda i: (midx(i), 0, 0)),
                  pl.BlockSpec((tm, TOP_K), lambda i: (i, 0)),
                  pl.BlockSpec((1, d), lambda i: (0, 0)),
                  pl.BlockSpec(memory_space=pl.ANY)],
        out_specs=out_specs,
        out_shape=out_shape,
        scratch_shapes=[pltpu.VMEM((2, TOP_K, tm, d), F32), pltpu.SemaphoreType.DMA((2,))],
        compiler_params=_cparams(("arbitrary",)),
        name="combine",
    )(dest_flat, dest_flat, y1, mod_l, info, final_gain.reshape(1, d), yb)


def moe_layer(y1, h2p, logits, mod_l, router_bias, w1, w3, w2, layer, final_gain, n_prompt_rows, sample_rows, final):
    t, d = y1.shape
    info_t, cnt = route(logits, router_bias)
    expert = info_t[0:2].T.astype(jnp.int32)
    rank = info_t[4:6].T.astype(jnp.int32)
    info = info_t[2:4].T
    counts = cnt[:, 0].reshape(EXPERTS_PER_GROUP, N_GROUPS).T.reshape(-1).astype(jnp.int32)
    padded = (counts + MOE_BLOCK - 1) // MOE_BLOCK * MOE_BLOCK
    padded_end = jnp.cumsum(padded)
    padded_start = padded_end - padded
    dest = (padded_start[expert] + rank).reshape(-1)
    n_blocks = (t * TOP_K) // MOE_BLOCK + N_EXPERTS + 1
    slots = n_blocks * MOE_BLOCK
    block_lo = jnp.arange(n_blocks, dtype=jnp.int32)[:, None] * MOE_BLOCK
    block_expert = jnp.minimum(jnp.sum((padded_end[None, :] <= block_lo).astype(jnp.int32), axis=1), N_EXPERTS - 1)
    n_used = (padded_end[-1:] // MOE_BLOCK).astype(jnp.int32)
    tok = jnp.broadcast_to(jnp.arange(t, dtype=jnp.int32)[:, None], (t, TOP_K)).reshape(-1)
    slot_tok = jnp.zeros((slots,), jnp.int32).at[dest].set(tok)
    yb = expert_ffn(h2p, slot_tok, block_expert, n_used, w1, w3, w2, layer)
    return combine(y1, mod_l, info, dest, yb, final_gain, n_prompt_rows, sample_rows, final)


def rope_tables(n, lead):
    quarter = MLA_ROPE // 4
    tpos = np.arange(n)
    row = (tpos // GRID_W).astype(np.float32)
    col = (tpos % GRID_W).astype(np.float32)
    inv_freq = (np.float32(ROPE_BASE) ** (-np.arange(quarter, dtype=np.float32) / np.float32(quarter))).astype(np.float32)
    lane = np.arange(64)
    pos = np.where(lane[None, :] < 32, row[:, None], col[:, None]).astype(np.float32)
    ang = pos * inv_freq[lane % quarter][None, :]
    cos = np.cos(ang).astype(np.float32)
    sin = np.sin(ang).astype(np.float32)
    sgn = np.where((lane % 32) < quarter, -1.0, 1.0).astype(np.float32)
    sin = sin * sgn[None, :]
    cos = np.concatenate([cos, cos], axis=1)
    sin = np.concatenate([sin, sin], axis=1)
    cos = np.concatenate([np.ones((lead, LANES), np.float32), cos], axis=0)
    sin = np.concatenate([np.zeros((lead, LANES), np.float32), sin], axis=0)
    return jnp.asarray(cos), jnp.asarray(sin)


def _rope(x, cos, sin_signed):
    lane = lax.broadcasted_iota(jnp.int32, x.shape, 1)
    partner = jnp.where((lane & 31) < 16, pltpu.roll(x, LANES - 16, axis=1), pltpu.roll(x, 16, axis=1))
    return x * cos + partner * sin_signed


def _mla_proj_kernel(cq_ref, ckv_ref, misc_ref, qg_ref, kvg_ref, wq_ref, wkv_ref, cos_ref, sin_ref,
                     q_ref, kv_ref, ckvn_ref, kpe_ref, *, scale):
    cq = cq_ref[...].astype(F32)
    qn = cq * lax.rsqrt(jnp.mean(cq * cq, axis=-1, keepdims=True) + EPS) * qg_ref[...]
    qf = _dot(qn.astype(BF16), wq_ref[...])
    half = qf.shape[1] // 2
    cos, sin = cos_ref[...], sin_ref[...]
    q_ref[:, 0:half] = (qf[:, 0:half] * scale).astype(BF16)
    for h in range(MLA_HEADS):
        pe = qf[:, half + h * LANES: half + (h + 1) * LANES]
        q_ref[:, half + h * LANES: half + (h + 1) * LANES] = (_rope(pe, cos, sin) * scale).astype(BF16)
    ckv = ckv_ref[...].astype(F32)
    ckvn = ckv * lax.rsqrt(jnp.mean(ckv * ckv, axis=-1, keepdims=True) + EPS) * kvg_ref[...]
    ckvn_ref[...] = ckvn
    kv_ref[...] = _dot(ckvn.astype(BF16), wkv_ref[...]).astype(BF16)
    misc = misc_ref[...].astype(F32)
    lane = lax.broadcasted_iota(jnp.int32, misc.shape, 1)
    kpe = jnp.where(lane < MLA_ROPE, misc, 0.0)
    kpe_ref[...] = _rope(kpe, cos, sin).astype(BF16)


def mla_project(z, q_gain, kv_gain, wq, wkv, cos_t, sin_t, n_prompt_rows, sample_rows, tm=512):
    t = z.shape[0]
    tp = n_prompt_rows // tm
    per = sample_rows // tm

    def tab(i):
        return (jnp.where(i < tp, 0, 1 + (i - tp) % per), 0)

    scale = float((MLA_NOPE + MLA_ROPE) ** -0.5)
    nq = wq.shape[1]
    return pl.pallas_call(
        functools.partial(_mla_proj_kernel, scale=scale),
        grid=(t // tm,),
        in_specs=[pl.BlockSpec((tm, MLA_Q_LORA), lambda i: (i, AB_CQ // MLA_Q_LORA)),
                  pl.BlockSpec((tm, MLA_KV_LORA), lambda i: (i, AB_CKV // MLA_KV_LORA)),
                  pl.BlockSpec((tm, LANES), lambda i: (i, AB_MISC // LANES)),
                  pl.BlockSpec((1, MLA_Q_LORA), lambda i: (0, 0)),
                  pl.BlockSpec((1, MLA_KV_LORA), lambda i: (0, 0)),
                  pl.BlockSpec(wq.shape, lambda i: (0, 0)),
                  pl.BlockSpec(wkv.shape, lambda i: (0, 0)),
                  pl.BlockSpec((tm, LANES), tab),
                  pl.BlockSpec((tm, LANES), tab)],
        out_specs=[pl.BlockSpec((tm, nq), lambda i: (i, 0)),
                   pl.BlockSpec((tm, wkv.shape[1]), lambda i: (i, 0)),
                   pl.BlockSpec((tm, MLA_KV_LORA), lambda i: (i, 0)),
                   pl.BlockSpec((tm, LANES), lambda i: (i, 0))],
        out_shape=[jax.ShapeDtypeStruct((t, nq), BF16),
                   jax.ShapeDtypeStruct((t, wkv.shape[1]), BF16),
                   jax.ShapeDtypeStruct((t, MLA_KV_LORA), F32),
                   jax.ShapeDtypeStruct((t, LANES), BF16)],
        compiler_params=_cparams(("parallel",)),
        name="mla_project",
    )(z, z, z, q_gain.reshape(1, -1), kv_gain.reshape(1, -1), wq, wkv, cos_t, sin_t)


def _ctx_kv_kernel(ckv_ref, w_ref, o_ref):
    o_ref[...] = _dot(ckv_ref[...].astype(BF16), w_ref[...]).astype(BF16)


def ctx_kv_project(ckv_c, wkv):
    rows = ckv_c.shape[0]
    return pl.pallas_call(
        _ctx_kv_kernel,
        grid=(1,),
        in_specs=[pl.BlockSpec(ckv_c.shape, lambda i: (0, 0)), pl.BlockSpec(wkv.shape, lambda i: (0, 0))],
        out_specs=pl.BlockSpec((rows, wkv.shape[1]), lambda i: (0, 0)),
        out_shape=jax.ShapeDtypeStruct((rows, wkv.shape[1]), BF16),
        compiler_params=_cparams(("arbitrary",)),
        name="ctx_kv_project",
    )(ckv_c, wkv)


def _softmax_pv(s, v):
    m = jnp.max(s, axis=-1, keepdims=True)
    p = jnp.exp(s - m)
    l = jnp.sum(p, axis=-1, keepdims=True)
    return _dot(p.astype(BF16), v) / l


def _mla_prompt_attn_kernel(q_ref, kv_ref, kp_ref, o_ref):
    nh = MLA_HEADS
    kp = kp_ref[...]
    for h in range(nh):
        q = jnp.concatenate([q_ref[:, h * LANES:(h + 1) * LANES], q_ref[:, (nh + h) * LANES:(nh + h + 1) * LANES]], axis=1)
        k = jnp.concatenate([kv_ref[:, h * LANES:(h + 1) * LANES], kp], axis=1)
        v = kv_ref[:, (nh + h) * LANES:(nh + h + 1) * LANES]
        o_ref[:, h * LANES:(h + 1) * LANES] = _softmax_pv(_dot_nt(q, k), v).astype(o_ref.dtype)


def mla_prompt_attention(q_m, kv, kpe, batch, n):
    h = MLA_HEADS
    return pl.pallas_call(
        _mla_prompt_attn_kernel,
        grid=(batch,),
        in_specs=[pl.BlockSpec((n, 2 * h * LANES), lambda b: (b, 0)),
                  pl.BlockSpec((n, 2 * h * LANES), lambda b: (b, 0)),
                  pl.BlockSpec((n, LANES), lambda b: (b, 0))],
        out_specs=pl.BlockSpec((n, h * MLA_DV), lambda b: (b, 0)),
        out_shape=jax.ShapeDtypeStruct((batch * n, h * MLA_DV), BF16),
        compiler_params=_cparams(("parallel",)),
        name="mla_prompt_attention",
    )(q_m, kv, kpe)


def _lane_tile_reduce(x, op):
    r = x[:, 0:LANES]
    for c in range(1, x.shape[1] // LANES):
        r = op(r, x[:, c * LANES:(c + 1) * LANES])
    return r


def _mla_sample_attn_kernel(qn_ref, qp_ref, kcn_ref, kcp_ref, vc_ref, kn_ref, kp_ref, v_ref, o_ref,
                            s_ref, m_ref, acc_ref, *, tk):
    past = kcn_ref.shape[0]
    nk = kn_ref.shape[0] // tk
    q = jnp.concatenate([qn_ref[...], qp_ref[...]], axis=1)
    kc = jnp.concatenate([kcn_ref[...], kcp_ref[...].astype(BF16)], axis=1)
    s = _dot_nt(q, kc)
    s_ref[:, 0:past] = s
    m_ref[...] = _lane_tile_reduce(s, jnp.maximum)

    def scores(j, carry):
        off = pl.multiple_of(j * tk, tk)
        k = jnp.concatenate([kn_ref[pl.ds(off, tk), :], kp_ref[pl.ds(off, tk), :]], axis=1)
        s = _dot_nt(q, k)
        s_ref[:, pl.ds(pl.multiple_of(past + j * tk, LANES), tk)] = s
        m_ref[...] = jnp.maximum(m_ref[...], _lane_tile_reduce(s, jnp.maximum))
        return carry

    lax.fori_loop(0, nk, scores, 0, unroll=2)
    m_ref[...] = jnp.broadcast_to(jnp.max(m_ref[...], axis=-1, keepdims=True), m_ref.shape)

    def weighted_values(s, v):
        m = m_ref[...]
        p = jnp.concatenate([jnp.exp(s[:, c * LANES:(c + 1) * LANES] - m).astype(BF16)
                             for c in range(s.shape[1] // LANES)], axis=1)
        v_ones = jnp.concatenate([v, jnp.ones((v.shape[0], LANES), BF16)], axis=1)
        return _dot(p, v_ones)

    acc_ref[...] = weighted_values(s_ref[:, 0:past], vc_ref[...])

    def weighted(j, carry):
        off = pl.multiple_of(j * tk, tk)
        acc_ref[...] += weighted_values(s_ref[:, pl.ds(pl.multiple_of(past + j * tk, LANES), tk)],
                                        v_ref[pl.ds(off, tk), :])
        return carry

    lax.fori_loop(0, nk, weighted, 0, unroll=2)
    acc = acc_ref[...]
    o_ref[...] = (acc[:, 0:MLA_DV] / acc[:, MLA_DV:]).astype(o_ref.dtype)


def mla_sample_attention(q_m, kv, kpe, kv_ctx, kpe_ctx, row0, batch, n, past, tq=1024, tk=512):
    h = MLA_HEADS
    qb = n // tq
    r0 = row0 // n
    q0 = row0 // tq
    return pl.pallas_call(
        functools.partial(_mla_sample_attn_kernel, tk=tk),
        grid=(batch, h, qb),
        in_specs=[pl.BlockSpec((tq, LANES), lambda b, hh, i: (q0 + b * qb + i, hh)),
                  pl.BlockSpec((tq, LANES), lambda b, hh, i: (q0 + b * qb + i, h + hh)),
                  pl.BlockSpec((past, LANES), lambda b, hh, i: (b, hh)),
                  pl.BlockSpec((past, LANES), lambda b, hh, i: (b, 0)),
                  pl.BlockSpec((past, LANES), lambda b, hh, i: (b, h + hh)),
                  pl.BlockSpec((n, LANES), lambda b, hh, i: (r0 + b, hh)),
                  pl.BlockSpec((n, LANES), lambda b, hh, i: (r0 + b, 0)),
                  pl.BlockSpec((n, LANES), lambda b, hh, i: (r0 + b, h + hh))],
        out_specs=pl.BlockSpec((tq, LANES), lambda b, hh, i: (b * qb + i, hh)),
        out_shape=jax.ShapeDtypeStruct((batch * n, h * MLA_DV), BF16),
        scratch_shapes=[pltpu.VMEM((tq, past + n), F32), pltpu.VMEM((tq, LANES), F32),
                        pltpu.VMEM((tq, MLA_DV + LANES), F32)],
        compiler_params=_cparams(("parallel", "parallel", "arbitrary")),
        name="mla_sample_attention",
    )(q_m, q_m, kv_ctx, kpe_ctx, kv_ctx, kv, kpe, kv)


def _na_prompt_attn_kernel(q_ref, k_ref, v_ref, o_ref, ko_ref, vo_ref, *, scale):
    for h in range(NA_HEADS):
        cols = slice(h * LANES, (h + 1) * LANES)
        k = k_ref[:, cols]
        v = v_ref[:, cols]
        ko_ref[h] = k.astype(F32)
        vo_ref[h] = v.astype(F32)
        q = (q_ref[:, cols].astype(F32) * scale).astype(BF16)
        o_ref[:, cols] = _softmax_pv(_dot_nt(q, k), v).astype(o_ref.dtype)


def na_prompt_attention(z, batch, n):
    h = NA_HEADS
    scale = float(NA_HD ** -0.5)
    w = h * NA_HD
    cq, ck, cv = CD_NQ // w, CD_NK // w, CD_NV // w
    return pl.pallas_call(
        functools.partial(_na_prompt_attn_kernel, scale=scale),
        grid=(batch,),
        in_specs=[pl.BlockSpec((n, w), lambda b: (b, cq)),
                  pl.BlockSpec((n, w), lambda b: (b, ck)),
                  pl.BlockSpec((n, w), lambda b: (b, cv))],
        out_specs=[pl.BlockSpec((n, w), lambda b: (b, 0)),
                   pl.BlockSpec((None, h, n, NA_HD), lambda b: (b, 0, 0, 0)),
                   pl.BlockSpec((None, h, n, NA_HD), lambda b: (b, 0, 0, 0))],
        out_shape=[jax.ShapeDtypeStruct((batch * n, w), BF16),
                   jax.ShapeDtypeStruct((batch, h, n, NA_HD), F32),
                   jax.ShapeDtypeStruct((batch, h, n, NA_HD), F32)],
        compiler_params=_cparams(("parallel",)),
        name="na_prompt_attention",
    )(z, z, z)


def _dot_tn(a, b):
    return lax.dot_general(a, b, (((0,), (0,)), ((), ())), preferred_element_type=F32)


def _head_norm_gate(acc, gate_in, gain):
    outs = []
    for h in range(2):
        a = acc[:, h * LANES:(h + 1) * LANES]
        outs.append(a * lax.rsqrt(jnp.mean(a * a, axis=-1, keepdims=True) + EPS))
    g = gate_in
    return jnp.concatenate(outs, axis=1) * gain * (g * jax.nn.sigmoid(g))


RET_CHUNK = 256
FIN_ROWS = 256


def _ret_kernel(ld_ref, q_ref, k_ref, v_ref, g_ref, cos_ref, sin_ref, gain_ref, s0_ref, o_ref, sf_ref, acc_ref, *, n):
    L = RET_CHUNK
    nc = n // L
    hp = pl.program_id(1)
    lane = lax.broadcasted_iota(jnp.int32, (1, LANES), 1)
    lo_lanes = lane < RET_DK
    row128 = lax.broadcasted_iota(jnp.int32, (LANES, 1), 0)
    ri = lax.broadcasted_iota(jnp.int32, (L, L), 0)
    ci = lax.broadcasted_iota(jnp.int32, (L, L), 1)
    dist = (ri - ci).astype(F32)
    pos = lax.broadcasted_iota(jnp.int32, (L, 1), 0).astype(F32)

    consts = []
    for dr in range(2):
        lg0 = ld_ref[dr, 2 * hp]
        lg1 = ld_ref[dr, 2 * hp + 1]
        lg2 = jnp.where(lo_lanes, lg0, lg1)
        lgc = jnp.where(row128 < RET_DK, lg0, lg1)
        sd = dist if dr == 0 else -dist
        ok = sd >= 0
        intra = [jnp.where(ok, jnp.exp(jnp.where(ok, sd, 0.0) * lg), 0.0) for lg in (lg0, lg1)]
        if dr == 0:
            inter = jnp.exp((pos + 1.0) * lg2)
            to_state = jnp.exp((L - 1.0 - pos) * lg2)
        else:
            inter = jnp.exp((L - pos) * lg2)
            to_state = jnp.exp(pos * lg2)
        consts.append((intra, inter, to_state, jnp.exp(float(L) * lgc)))

    def clear(i, carry):
        acc_ref[pl.ds(pl.multiple_of(i * FIN_ROWS, FIN_ROWS), FIN_ROWS), :] = jnp.zeros((FIN_ROWS, 2 * LANES), F32)
        return carry

    lax.fori_loop(0, n // FIN_ROWS, clear, 0)

    def step(c, dr, s2):
        intra, inter, to_state, cdec = consts[dr]
        rows = pl.ds(pl.multiple_of(c * L, L), L)
        cos, sin = cos_ref[rows, :], sin_ref[rows, :]
        q = _rope(q_ref[rows, :].astype(F32) * (RET_DK ** -0.5), cos, sin)
        k = _rope(k_ref[rows, :].astype(F32), cos, sin)
        kb = k.astype(BF16)
        s2b = s2.astype(BF16)
        kt = (k * to_state).astype(BF16)
        upd = []
        for h in range(2):
            qm = jnp.where(lo_lanes if h == 0 else ~lo_lanes, q, 0.0)
            vh = v_ref[rows, h * LANES:(h + 1) * LANES].astype(BF16)
            sc = _dot_nt(qm.astype(BF16), kb) * intra[h]
            o = _dot(sc.astype(BF16), vh) + _dot((qm * inter).astype(BF16), s2b)
            acc_ref[rows, h * LANES:(h + 1) * LANES] += o
            upd.append(_dot_tn(kt, vh))
        return cdec * s2 + jnp.where(row128 < RET_DK, upd[0], upd[1])

    def body(i, carry):
        sf, sb = carry
        return step(i, 0, sf), step(nc - 1 - i, 1, sb)

    sf, sb = lax.fori_loop(0, nc, body, (s0_ref[0], s0_ref[1]))
    sf_ref[0] = sf
    sf_ref[1] = sb

    def fin(i, carry):
        rows = pl.ds(pl.multiple_of(i * FIN_ROWS, FIN_ROWS), FIN_ROWS)
        o_ref[rows, :] = _head_norm_gate(acc_ref[rows, :], g_ref[rows, :].astype(F32), gain_ref[...]).astype(o_ref.dtype)
        return carry

    lax.fori_loop(0, n // FIN_ROWS, fin, 0)


def retention_mixer(z, log_decay, ret_gain, s0, cos_t, sin_t, row0, batch, n):
    hp = RET_HEADS // 2
    r0 = row0 // n
    cq, ck, cv, cg = CD_RQ // LANES, CD_RK // LANES, CD_RV // (2 * LANES), CD_RG // (2 * LANES)
    grid_spec = pltpu.PrefetchScalarGridSpec(
        num_scalar_prefetch=1,
        grid=(batch, hp),
        in_specs=[pl.BlockSpec((n, LANES), lambda b, p, ld: (r0 + b, cq + p)),
                  pl.BlockSpec((n, LANES), lambda b, p, ld: (r0 + b, ck + p)),
                  pl.BlockSpec((n, 2 * LANES), lambda b, p, ld: (r0 + b, cv + p)),
                  pl.BlockSpec((n, 2 * LANES), lambda b, p, ld: (r0 + b, cg + p)),
                  pl.BlockSpec((n, LANES), lambda b, p, ld: (0, 0)),
                  pl.BlockSpec((n, LANES), lambda b, p, ld: (0, 0)),
                  pl.BlockSpec((1, 2 * LANES), lambda b, p, ld: (0, p)),
                  pl.BlockSpec((None, 2, LANES, LANES), lambda b, p, ld: (b, 0, p, 0))],
        out_specs=[pl.BlockSpec((n, 2 * LANES), lambda b, p, ld: (b, p)),
                   pl.BlockSpec((None, 2, LANES, LANES), lambda b, p, ld: (b, 0, p, 0))],
        scratch_shapes=[pltpu.VMEM((n, 2 * LANES), F32)],
    )
    return pl.pallas_call(
        functools.partial(_ret_kernel, n=n),
        grid_spec=grid_spec,
        out_shape=[jax.ShapeDtypeStruct((batch * n, RET_HEADS * RET_DV), BF16),
                   jax.ShapeDtypeStruct((batch, 2, RET_HEADS * RET_DK, RET_DV), F32)],
        compiler_params=_cparams(("parallel", "parallel")),
        name="retention_mixer",
    )(log_decay.astype(F32), z, z, z, z, cos_t, sin_t, ret_gain.reshape(1, -1), s0)


GLA_CHUNK = 256
GLA_SUB = 64
GLA_SAFE_EXPONENT = 60.0


def _gla_kernel(q_ref, k_ref, v_ref, g_ref, misc_ref, gw_ref, gb_ref, gain_ref, s0_ref, o_ref, sf_ref,
                la_ref, acc_ref, *, n):
    C, SB = GLA_CHUNK, GLA_SUB
    nsb = C // SB
    nc = n // C
    lane = lax.broadcasted_iota(jnp.int32, (1, LANES), 1)
    lo_lanes = lane < GLA_DK
    head_mask = (lo_lanes, ~lo_lanes)
    ri = lax.broadcasted_iota(jnp.int32, (C, C), 0)
    ci = lax.broadcasted_iota(jnp.int32, (C, C), 1)
    rowc = lax.broadcasted_iota(jnp.int32, (C, 1), 0)
    SLAB = 16
    row8 = lax.broadcasted_iota(jnp.int32, (SLAB, 1), 0)
    tri = (jnp.where(ci <= ri, 1.0, 0.0).astype(BF16), jnp.where(ci >= ri, 1.0, 0.0).astype(BF16))
    causal = (ci <= ri, ci >= ri)

    def prep(i, carry):
        rows = pl.ds(pl.multiple_of(i * FIN_ROWS, FIN_ROWS), FIN_ROWS)
        slab = misc_ref[rows, :].astype(BF16)
        for dr in range(2):
            pre = _dot(slab, gw_ref[dr]) + gb_ref[dr]
            la_ref[dr, rows, :] = (jnp.minimum(pre, 0.0) - jnp.log1p(jnp.exp(-jnp.abs(pre)))) * (1.0 / GATE_TAU)
        acc_ref[rows, :] = jnp.zeros((FIN_ROWS, 2 * LANES), F32)
        return carry

    lax.fori_loop(0, n // FIN_ROWS, prep, 0)

    def step(c, dr):
        rev = dr == 1
        base = pl.multiple_of(c * C, C)
        rows = pl.ds(base, C)
        la = la_ref[dr, rows, :]
        hi = la.astype(BF16)
        lo = (la - hi.astype(F32)).astype(BF16)
        b = _dot(tri[dr], hi) + _dot(tri[dr], lo)
        zero = jnp.zeros((1, LANES), F32)
        if not rev:
            bref = [zero if i == 0 else b[SB * i - 1:SB * i, :] for i in range(nsb)]
            bend = [b[SB * i + SB - 1:SB * i + SB, :] for i in range(nsb)]
            blast = b[C - 1:C, :]
        else:
            bref = [zero if i == nsb - 1 else b[SB * (i + 1):SB * (i + 1) + 1, :] for i in range(nsb)]
            bend = [b[SB * i:SB * i + 1, :] for i in range(nsb)]
            blast = b[0:1, :]
        spread = bref[0] - bend[0]
        for i in range(1, nsb):
            spread = jnp.maximum(spread, bref[i] - bend[i])
        worst = jnp.max(spread)

        def fast(st):
            q = q_ref[rows, :].astype(F32) * (GLA_DK ** -0.5)
            k = k_ref[rows, :].astype(F32)
            a_rows = ([], [])
            for i in range(nsb):
                qe = q[SB * i:SB * (i + 1), :] * jnp.exp(b[SB * i:SB * (i + 1), :] - bref[i])
                valid = (rowc < SB * (i + 1)) if not rev else (rowc >= SB * i)
                ke = jnp.where(valid, k * jnp.exp(jnp.where(valid, bref[i] - b, 0.0)), 0.0).astype(BF16)
                for h in range(2):
                    a_rows[h].append(_dot_nt(jnp.where(head_mask[h], qe, 0.0).astype(BF16), ke))
            qi = q * jnp.exp(b)
            ks = (k * jnp.exp(blast - b)).astype(BF16)
            stb = st.astype(BF16)
            upd = []
            for h in range(2):
                a = jnp.where(causal[dr], jnp.concatenate(a_rows[h], axis=0), 0.0)
                vh = v_ref[rows, h * LANES:(h + 1) * LANES].astype(BF16)
                o = _dot(a.astype(BF16), vh) + _dot_nt(jnp.where(head_mask[h], qi, 0.0).astype(BF16), stb)
                acc_ref[rows, h * LANES:(h + 1) * LANES] += o
                upd.append(_dot_tn(vh, ks))
            return jnp.exp(blast) * st + jnp.where(lo_lanes, upd[0], upd[1])

        def slow(st):
            def row_step(r, st):
                t = r if not rev else C - 1 - r
                slab = pl.ds(pl.multiple_of(base + (t // SLAB) * SLAB, SLAB), SLAB)
                pick = row8 == (t % SLAB)
                q8 = jnp.where(pick, q_ref[slab, :].astype(F32) * (GLA_DK ** -0.5), 0.0)
                k8 = jnp.where(pick, k_ref[slab, :].astype(F32), 0.0).astype(BF16)
                la_t = jnp.sum(jnp.where(pick, la_ref[dr, slab, :], 0.0), axis=0, keepdims=True)
                upd = []
                for h in range(2):
                    v8 = jnp.where(pick, v_ref[slab, h * LANES:(h + 1) * LANES].astype(F32), 0.0).astype(BF16)
                    upd.append(_dot_tn(v8, k8))
                st = jnp.exp(la_t) * st + jnp.where(lo_lanes, upd[0], upd[1])
                stb = st.astype(BF16)
                for h in range(2):
                    qh = jnp.where(head_mask[h], q8, 0.0).astype(BF16)
                    acc_ref[slab, h * LANES:(h + 1) * LANES] += _dot_nt(qh, stb)
                return st

            return lax.fori_loop(0, C + 0 * pl.program_id(0), row_step, st)

        return worst, fast, slow

    def body(i, carry):
        worst_f, fast_f, slow_f = step(i, 0)
        worst_b, fast_b, slow_b = step(nc - 1 - i, 1)
        return lax.cond(jnp.maximum(worst_f, worst_b) <= GLA_SAFE_EXPONENT,
                        lambda st: (fast_f(st[0]), fast_b(st[1])),
                        lambda st: (slow_f(st[0]), slow_b(st[1])), carry)

    sf, sb = lax.fori_loop(0, nc, body, (s0_ref[0].T, s0_ref[1].T))
    sf_ref[0] = sf.T
    sf_ref[1] = sb.T

    def fin(i, carry):
        rows = pl.ds(pl.multiple_of(i * FIN_ROWS, FIN_ROWS), FIN_ROWS)
        o_ref[rows, :] = _head_norm_gate(acc_ref[rows, :], g_ref[rows, :].astype(F32), gain_ref[...]).astype(o_ref.dtype)
        return carry

    lax.fori_loop(0, n // FIN_ROWS, fin, 0)


def _prep_gate(gate_w, gate_b):
    gw = jnp.zeros((2, LANES, gate_w.shape[-1]), F32)
    for dr in range(2):
        lo = MLA_ROPE + dr * GATE_RANK
        gw = gw.at[dr, lo:lo + GATE_RANK, :].set(gate_w[dr].astype(F32))
    return gw.astype(BF16), gate_b.astype(F32).reshape(2, 1, -1)


def gla_mixer(z, gate_w, gate_b, gla_gain, s0, row0, batch, n):
    hp = GLA_HEADS // 2
    r0 = row0 // n
    gw, gb = _prep_gate(gate_w, gate_b)
    cq, ck, cv, cg, cm = AB_GQ // LANES, AB_GK // LANES, AB_GV // (2 * LANES), AB_GR // (2 * LANES), AB_MISC // LANES
    return pl.pallas_call(
        functools.partial(_gla_kernel, n=n),
        grid=(batch, hp),
        in_specs=[pl.BlockSpec((n, LANES), lambda b, p: (r0 + b, cq + p)),
                  pl.BlockSpec((n, LANES), lambda b, p: (r0 + b, ck + p)),
                  pl.BlockSpec((n, 2 * LANES), lambda b, p: (r0 + b, cv + p)),
                  pl.BlockSpec((n, 2 * LANES), lambda b, p: (r0 + b, cg + p)),
                  pl.BlockSpec((n, LANES), lambda b, p: (r0 + b, cm)),
                  pl.BlockSpec((2, LANES, LANES), lambda b, p: (0, 0, p)),
                  pl.BlockSpec((2, 1, LANES), lambda b, p: (0, 0, p)),
                  pl.BlockSpec((1, 2 * LANES), lambda b, p: (0, p)),
                  pl.BlockSpec((None, 2, LANES, LANES), lambda b, p: (b, 0, p, 0))],
        out_specs=[pl.BlockSpec((n, 2 * LANES), lambda b, p: (b, p)),
                   pl.BlockSpec((None, 2, LANES, LANES), lambda b, p: (b, 0, p, 0))],
        out_shape=[jax.ShapeDtypeStruct((batch * n, GLA_HEADS * GLA_DV), BF16),
                   jax.ShapeDtypeStruct((batch, 2, GLA_HEADS * GLA_DK, GLA_DV), F32)],
        scratch_shapes=[pltpu.VMEM((2, n, LANES), F32), pltpu.VMEM((n, 2 * LANES), F32)],
        compiler_params=_cparams(("parallel", "parallel")),
        name="gla_mixer",
    )(z, z, z, z, z, gw, gb, gla_gain.reshape(1, -1), s0)


NA_QROWS = 8
NA_WROWS = 16
NA_NEG = -1e30
NA_TAB = 32


def _na_bias_table(na_bias_l):
    qc = np.arange(GRID_W)[:, None]
    kc = np.arange(GRID_W)[None, :]
    ws = np.clip(qc - NA_COLS // 2, 0, GRID_W - NA_COLS)
    col_ok = (kc >= ws) & (kc < ws + NA_COLS)
    col_off = np.clip(kc - qc + NA_COLS - 1, 0, 2 * NA_COLS - 2)
    aa = (np.arange(NA_TAB) - NA_QROWS)[:, None] + np.arange(2)[None, :]
    n_row, n_col = 2 * NA_ROWS - 1, 2 * NA_COLS - 1
    sel_row = (aa[:, :, None] == np.arange(n_row)[None, None, :]).astype(np.float32)
    sel_col = ((col_off[:, :, None] == np.arange(n_col)[None, None, :]) & col_ok[:, :, None]).astype(np.float32)
    fill = np.where(col_ok, 0.0, NA_NEG).astype(np.float32)
    tab = jnp.einsum('hac,sea,qkc->hsqek', na_bias_l.astype(F32), jnp.asarray(sel_row), jnp.asarray(sel_col),
                     precision=lax.Precision.HIGHEST)
    tab = tab + jnp.asarray(fill)[None, None, :, None, :]
    h = na_bias_l.shape[0]
    return tab.reshape(h, NA_TAB, GRID_W, 2 * GRID_W)


def _na_latent_kernel(q_ref, k_ref, v_ref, kc_ref, vc_ref, tab_ref, o_ref, s_ref, *, scale, rows_total):
    g = pl.program_id(2)
    r0 = g * NA_QROWS
    start = jnp.clip(r0 - NA_ROWS // 2, 0, rows_total - NA_WROWS)
    nloc = NA_WROWS * GRID_W
    krows = pl.ds(pl.multiple_of(start * GRID_W, GRID_W), nloc)
    q = (q_ref[...].astype(F32) * scale).astype(BF16)
    kw = k_ref[krows, :]
    s_ref[:, 0:nloc] = _dot_nt(q, kw)
    s_ref[:, nloc:] = _dot_nt(q, kc_ref[...].astype(BF16))
    lane = lax.broadcasted_iota(jnp.int32, (1, LANES), 1)
    for qr in range(NA_QROWS):
        r = r0 + qr
        rs = jnp.clip(r - NA_ROWS // 2, 0, rows_total - NA_ROWS)
        for kp in range(NA_WROWS // 2):
            k0 = start + 2 * kp
            ok0 = (k0 >= rs) & (k0 < rs + NA_ROWS)
            ok1 = (k0 + 1 >= rs) & (k0 + 1 < rs + NA_ROWS)
            neg = jnp.where(lane < GRID_W, jnp.where(ok0, 0.0, NA_NEG), jnp.where(ok1, 0.0, NA_NEG))
            slot = k0 - r + (NA_ROWS - 1) + NA_QROWS
            s_ref[qr * GRID_W:(qr + 1) * GRID_W, kp * LANES:(kp + 1) * LANES] += tab_ref[slot] + neg
    s = s_ref[...]
    m = jnp.max(s, axis=-1, keepdims=True)
    p = jnp.exp(s - m)
    l = jnp.sum(p, axis=-1, keepdims=True)
    pb = p.astype(BF16)
    o = _dot(pb[:, 0:nloc], v_ref[krows, :]) + _dot(pb[:, nloc:], vc_ref[...].astype(BF16))
    o_ref[...] = (o / l).astype(o_ref.dtype)


def na_latent_attention(z, k_ctx, v_ctx, na_bias_l, row0, batch, n):
    h = NA_HEADS
    past = k_ctx.shape[2]
    rows_total = n // GRID_W
    tq = NA_QROWS * GRID_W
    qb = n // tq
    r0 = row0 // n
    q0 = row0 // tq
    cq, ck, cv = CD_NQ // LANES, CD_NK // LANES, CD_NV // LANES
    tab = _na_bias_table(na_bias_l)
    return pl.pallas_call(
        functools.partial(_na_latent_kernel, scale=float(NA_HD ** -0.5), rows_total=rows_total),
        grid=(batch, h, qb),
        in_specs=[pl.BlockSpec((tq, LANES), lambda b, hh, g: (q0 + b * qb + g, cq + hh)),
                  pl.BlockSpec((n, LANES), lambda b, hh, g: (r0 + b, ck + hh)),
                  pl.BlockSpec((n, LANES), lambda b, hh, g: (r0 + b, cv + hh)),
                  pl.BlockSpec((None, None, past, NA_HD), lambda b, hh, g: (b, hh, 0, 0)),
                  pl.BlockSpec((None, None, past, NA_HD), lambda b, hh, g: (b, hh, 0, 0)),
                  pl.BlockSpec((None, NA_TAB, GRID_W, 2 * GRID_W), lambda b, hh, g: (hh, 0, 0, 0))],
        out_specs=pl.BlockSpec((tq, LANES), lambda b, hh, g: (b * qb + g, hh)),
        out_shape=jax.ShapeDtypeStruct((batch * n, h * NA_HD), BF16),
        scratch_shapes=[pltpu.VMEM((tq, NA_WROWS * GRID_W + past), F32)],
        compiler_params=_cparams(("parallel", "parallel", "arbitrary")),
        name="na_latent_attention",
    )(z, z, z, k_ctx, v_ctx, tab)


def _prep_w_in_ab(w):
    d = w.shape[0]
    sizes = (512, 512, 1024, 1024, 2 * GATE_RANK, MLA_Q_LORA, MLA_KV_LORA, MLA_ROPE)
    gq, gk, gv, gr, glr, cq, ckv, kpe = jnp.split(w, np.cumsum(sizes)[:-1].tolist(), axis=1)
    pad = jnp.zeros((d, AB_WIDTH - AB_MISC - MLA_ROPE - 2 * GATE_RANK), w.dtype)
    return jnp.concatenate([gq, gk, gv, gr, cq, ckv, kpe, glr, pad], axis=1).astype(BF16)


def _prep_w_uq(w):
    r = w.shape[0]
    w3 = w.reshape(r, MLA_HEADS, MLA_NOPE + MLA_ROPE)
    nope = w3[:, :, :MLA_NOPE].reshape(r, MLA_HEADS * MLA_NOPE)
    pe = jnp.pad(w3[:, :, MLA_NOPE:], ((0, 0), (0, 0), (0, LANES - MLA_ROPE))).reshape(r, MLA_HEADS * LANES)
    return jnp.concatenate([nope, pe], axis=1).astype(BF16)


def _prep_w_ukv(w):
    r = w.shape[0]
    w3 = w.reshape(r, MLA_HEADS, MLA_NOPE + MLA_DV)
    return jnp.concatenate([w3[:, :, :MLA_NOPE].reshape(r, -1), w3[:, :, MLA_NOPE:].reshape(r, -1)], axis=1).astype(BF16)


def _prep_router(router_w):
    w = jnp.pad(_member_major(router_w.astype(F32)), ((0, 0), (0, LANES - N_EXPERTS)))
    hi = w.astype(BF16)
    lo = (w - hi.astype(F32)).astype(BF16)
    return jnp.stack([hi, lo], axis=0)


def kernel(x_prompt, x_sample, state_gla, cache_mla_ckv, cache_mla_kpe, state_ret, cache_na_k, cache_na_v,
           c, c_ctx, ada_w, ada_b, norm1, norm2, w_in_ab, gla_gate_w, gla_gate_b, gla_gain, mla_q_gain,
           mla_w_uq, mla_kv_gain, mla_w_ukv, w_in_cd, ret_log_decay, ret_gain, na_bias, w_out,
           router_w, router_bias, exp_w1, exp_w3, exp_w2, final_norm):
    bp, n_p, d = x_prompt.shape
    bs, n_s, _ = x_sample.shape
    past = cache_mla_ckv.shape[2]
    depth = ada_w.shape[0]
    tp, ts = bp * n_p, bs * n_s
    ys = [x_prompt.reshape(tp, d), x_sample.reshape(ts, d)]

    cond8 = jnp.zeros((8, d), F32).at[0].set(c_ctx).at[1:1 + bs].set(c)
    mod = adaln_all(cond8, ada_w, ada_b)
    rw_split = _prep_router(router_w)
    w_out_bf = w_out.astype(BF16)
    cos_t, sin_t = rope_tables(n_s, 512)
    cos_s, sin_s = cos_t[512:], sin_t[512:]
    cos_id, sin_id = cos_t[:n_p], sin_t[:n_p]

    outs = {}
    for l in range(depth):
        i = l // 2
        mod_l = mod[l].reshape(8, 1, 6 * d)
        final = l == depth - 1
        if l % 2 == 0:
            z = win_project(ys, mod_l, norm1[l], _prep_w_in_ab(w_in_ab[i]), tp, n_s)
            s0p = jnp.zeros((bp, 2, GLA_HEADS * GLA_DK, GLA_DV), F32)
            s0s = state_gla[:, i].reshape(bs, 2, GLA_HEADS * GLA_DK, GLA_DV)
            og_p, st_p = gla_mixer(z, gla_gate_w[i], gla_gate_b[i], gla_gain[i], s0p, 0, bp, n_p)
            og_s, _ = gla_mixer(z, gla_gate_w[i], gla_gate_b[i], gla_gain[i], s0s, tp, bs, n_s)
            st_p = st_p.reshape(bp, 2, GLA_HEADS, GLA_DK, GLA_DV)
            o_a = (og_p, og_s)
            wq, wkv = _prep_w_uq(mla_w_uq[i]), _prep_w_ukv(mla_w_ukv[i])
            q_m, kv, ckvn, kpe = mla_project(z, mla_q_gain[i], mla_kv_gain[i], wq, wkv, cos_t, sin_t, tp, n_s)
            kv_ctx = ctx_kv_project(cache_mla_ckv[:, i].reshape(bs * past, -1), wkv)
            kpe_ctx = jnp.pad(cache_mla_kpe[:, i].reshape(bs * past, -1), ((0, 0), (0, LANES - MLA_ROPE)))
            om_p = mla_prompt_attention(q_m, kv, kpe, bp, n_p)
            om_s = mla_sample_attention(q_m, kv, kpe, kv_ctx, kpe_ctx, tp, bs, n_s, past)
            o_b = (om_p, om_s)
            outs.setdefault('gla', []).append(st_p)
            outs.setdefault('ckv', []).append(ckvn[:tp].reshape(bp, n_p, -1))
            outs.setdefault('kpe', []).append(z[:tp, AB_MISC:AB_MISC + MLA_ROPE].astype(F32).reshape(bp, n_p, -1))
        else:
            z = win_project(ys, mod_l, norm1[l], w_in_cd[i].astype(BF16), tp, n_s)
            s0p = jnp.zeros((bp, 2, RET_HEADS * RET_DK, RET_DV), F32)
            s0s = state_ret[:, i].reshape(bs, 2, RET_HEADS * RET_DK, RET_DV)
            or_p, st_p = retention_mixer(z, ret_log_decay[i], ret_gain[i], s0p, cos_id, sin_id, 0, bp, n_p)
            or_s, _ = retention_mixer(z, ret_log_decay[i], ret_gain[i], s0s, cos_s, sin_s, tp, bs, n_s)
            st_p = st_p.reshape(bp, 2, RET_HEADS, RET_DK, RET_DV)
            o_a = (or_p, or_s)
            on_p, kn, vn = na_prompt_attention(z, bp, n_p)
            on_s = na_latent_attention(z, cache_na_k[:, i], cache_na_v[:, i], na_bias[i], tp, bs, n_s)
            o_b = (on_p, on_s)
            outs.setdefault('ret', []).append(st_p)
            outs.setdefault('nak', []).append(kn)
            outs.setdefault('nav', []).append(vn)
        y1, h2, logits = wout_project(o_a, o_b, ys, mod_l, w_out_bf[l], norm2[l], rw_split, tp, n_s)
        y = moe_layer(y1, h2, logits, mod_l, router_bias, exp_w1, exp_w3, exp_w2, l, final_norm, tp, n_s, final)
        ys = list(y) if final else [y]

    return (ys[0].reshape(bp, n_p, d), ys[1].reshape(bs, n_s, d),
            jnp.stack(outs['gla'], axis=1), jnp.stack(outs['ckv'], axis=1), jnp.stack(outs['kpe'], axis=1),
            jnp.stack(outs['ret'], axis=1), jnp.stack(outs['nak'], axis=1), jnp.stack(outs['nav'], axis=1))
```

```python
import functools

import numpy as np
import jax
import jax.numpy as jnp
from jax import lax
from jax.experimental import pallas as pl
from jax.experimental.pallas import tpu as pltpu

F32 = jnp.float32
BF16 = jnp.bfloat16

EPS = 1e-6
GRID_W = 64
ROPE_BASE = 10000.0
GATE_RANK = 16
GATE_TAU = 16.0
GLA_HEADS, GLA_DK, GLA_DV = 8, 64, 128
MLA_HEADS, MLA_Q_LORA, MLA_KV_LORA, MLA_NOPE, MLA_ROPE, MLA_DV = 8, 512, 256, 128, 64, 128
RET_HEADS, RET_DK, RET_DV = 8, 64, 128
NA_HEADS, NA_HD, NA_ROWS, NA_COLS = 8, 128, 8, 16
N_EXPERTS, N_GROUPS, TOP_K, D_EXPERT = 32, 8, 2, 512
EXPERTS_PER_GROUP = N_EXPERTS // N_GROUPS
MOE_BLOCK = 128
LANES = 128

AB_GQ, AB_GK, AB_GV, AB_GR, AB_CQ, AB_CKV, AB_MISC, AB_WIDTH = 0, 512, 1024, 2048, 3072, 3584, 3840, 4096
CD_RQ, CD_RK, CD_RV, CD_RG, CD_NQ, CD_NK, CD_NV, CD_WIDTH = 0, 512, 1024, 2048, 3072, 4096, 5120, 6144

VMEM_LIMIT = 56 * 1024 * 1024


def _cparams(sem):
    return pltpu.CompilerParams(dimension_semantics=sem, vmem_limit_bytes=VMEM_LIMIT)


def _dot(a, b):
    return jnp.dot(a, b, preferred_element_type=F32)


def _dot_nt(a, b):
    return lax.dot_general(a, b, (((1,), (1,)), ((), ())), preferred_element_type=F32)


def _adaln_kernel(c_ref, w_ref, b_ref, o_ref):
    c = c_ref[...]
    s = (c * jax.nn.sigmoid(c)).astype(BF16)
    o_ref[...] = _dot(s, w_ref[...].astype(BF16)) + b_ref[...]


def adaln_all(cond8, ada_w, ada_b, tn=512):
    depth, d, n6 = ada_w.shape
    return pl.pallas_call(
        _adaln_kernel,
        grid=(depth, n6 // tn),
        in_specs=[pl.BlockSpec((8, d), lambda l, j: (0, 0)),
                  pl.BlockSpec((None, d, tn), lambda l, j: (l, 0, j)),
                  pl.BlockSpec((None, 1, tn), lambda l, j: (l, 0, j))],
        out_specs=pl.BlockSpec((None, 8, tn), lambda l, j: (l, 0, j)),
        out_shape=jax.ShapeDtypeStruct((depth, 8, n6), F32),
        compiler_params=_cparams(("parallel", "parallel")),
        name="adaln",
    )(cond8, ada_w, ada_b.reshape(depth, 1, n6))


def _mod_index(i, tm, n_prompt_rows, sample_rows):
    tp = n_prompt_rows // tm
    return jnp.where(i < tp, 0, 1 + (i - tp) // (sample_rows // tm))


def _split_specs(block, tiles_prompt, n_grid_axes=1):
    if n_grid_axes == 1:
        return [pl.BlockSpec(block, lambda i: (jnp.minimum(i, tiles_prompt - 1), 0)),
                pl.BlockSpec(block, lambda i: (jnp.maximum(i - tiles_prompt, 0), 0))]
    return [pl.BlockSpec(block, lambda i, j: (jnp.minimum(i, tiles_prompt - 1), 0)),
            pl.BlockSpec(block, lambda i, j: (jnp.maximum(i - tiles_prompt, 0), 0))]


def _win_kernel(*refs, d, tiles_prompt, split):
    if split:
        xp_ref, xs_ref, mod_ref, g_ref, w_ref, o_ref, h_ref = refs
    else:
        xp_ref, mod_ref, g_ref, w_ref, o_ref, h_ref = refs

    def norm_mod(x_ref):
        x = x_ref[...]
        xn = x * lax.rsqrt(jnp.mean(x * x, axis=-1, keepdims=True) + EPS) * g_ref[...]
        h = xn * (1.0 + mod_ref[:, d:2 * d]) + mod_ref[:, 0:d]
        h_ref[...] = h.astype(BF16)

    first = pl.program_id(1) == 0
    if split:
        is_prompt = pl.program_id(0) < tiles_prompt
        pl.when(first & is_prompt)(lambda: norm_mod(xp_ref))
        pl.when(first & jnp.logical_not(is_prompt))(lambda: norm_mod(xs_ref))
    else:
        pl.when(first)(lambda: norm_mod(xp_ref))

    o_ref[...] = _dot(h_ref[...], w_ref[...]).astype(o_ref.dtype)


def win_project(xs, mod_l, gain, w_bf16, n_prompt_rows, sample_rows, tm=1024, tn=512):
    split = len(xs) == 2
    d = xs[0].shape[1]
    t = sum(x.shape[0] for x in xs)
    width = w_bf16.shape[1]
    tiles_prompt = n_prompt_rows // tm
    midx = functools.partial(_mod_index, tm=tm, n_prompt_rows=n_prompt_rows, sample_rows=sample_rows)
    x_specs = (_split_specs((tm, d), tiles_prompt, 2) if split else [pl.BlockSpec((tm, d), lambda i, j: (i, 0))])
    return pl.pallas_call(
        functools.partial(_win_kernel, d=d, tiles_prompt=tiles_prompt, split=split),
        grid=(t // tm, width // tn),
        in_specs=x_specs + [pl.BlockSpec((None, 1, 6 * d), lambda i, j: (midx(i), 0, 0)),
                            pl.BlockSpec((1, d), lambda i, j: (0, 0)),
                            pl.BlockSpec((d, tn), lambda i, j: (0, j))],
        out_specs=pl.BlockSpec((tm, tn), lambda i, j: (i, j)),
        out_shape=jax.ShapeDtypeStruct((t, width), BF16),
        scratch_shapes=[pltpu.VMEM((tm, d), BF16)],
        compiler_params=_cparams(("parallel", "arbitrary")),
        name="win_project",
    )(*xs, mod_l, gain.reshape(1, d), w_bf16)


def _wout_kernel(*refs, d, tiles_prompt, split_y):
    if split_y:
        (oap_ref, oas_ref, obp_ref, obs_ref, yp_ref, ys_ref, mod_ref, wa_ref, wb_ref, g_ref, rw_ref,
         ynew_ref, h2_ref, lg_ref) = refs
    else:
        (oap_ref, oas_ref, obp_ref, obs_ref, yp_ref, mod_ref, wa_ref, wb_ref, g_ref, rw_ref,
         ynew_ref, h2_ref, lg_ref) = refs
        ys_ref = yp_ref
    is_prompt = pl.program_id(0) < tiles_prompt
    oa = jnp.where(is_prompt, oap_ref[...], oas_ref[...])
    ob = jnp.where(is_prompt, obp_ref[...], obs_ref[...])
    y = jnp.where(is_prompt, yp_ref[...], ys_ref[...]) if split_y else yp_ref[...]
    out = _dot(oa, wa_ref[...]) + _dot(ob, wb_ref[...])
    y1 = y + mod_ref[:, 2 * d:3 * d] * out
    ynew_ref[...] = y1
    xn = y1 * lax.rsqrt(jnp.mean(y1 * y1, axis=-1, keepdims=True) + EPS) * g_ref[...]
    h = xn * (1.0 + mod_ref[:, 4 * d:5 * d]) + mod_ref[:, 3 * d:4 * d]
    h2_ref[...] = h
    hh = h.astype(BF16)
    hl = (h - hh.astype(F32)).astype(BF16)
    lg_ref[...] = _dot(hh, rw_ref[0]) + _dot(hl, rw_ref[0]) + _dot(hh, rw_ref[1])


def wout_project(o_a, o_b, ys, mod_l, w_out_bf16, gain2, rw_split, n_prompt_rows, sample_rows, tm=256):
    split_y = len(ys) == 2
    d = ys[0].shape[1]
    t = sum(x.shape[0] for x in ys)
    ka, kb = o_a[0].shape[1], o_b[0].shape[1]
    tiles_prompt = n_prompt_rows // tm
    midx = functools.partial(_mod_index, tm=tm, n_prompt_rows=n_prompt_rows, sample_rows=sample_rows)
    y_specs = _split_specs((tm, d), tiles_prompt) if split_y else [pl.BlockSpec((tm, d), lambda i: (i, 0))]
    return pl.pallas_call(
        functools.partial(_wout_kernel, d=d, tiles_prompt=tiles_prompt, split_y=split_y),
        grid=(t // tm,),
        in_specs=_split_specs((tm, ka), tiles_prompt) + _split_specs((tm, kb), tiles_prompt) + y_specs + [
                  pl.BlockSpec((None, 1, 6 * d), lambda i: (midx(i), 0, 0)),
                  pl.BlockSpec((ka, d), lambda i: (0, 0)),
                  pl.BlockSpec((kb, d), lambda i: (1, 0)),
                  pl.BlockSpec((1, d), lambda i: (0, 0)),
                  pl.BlockSpec((2, d, LANES), lambda i: (0, 0, 0))],
        out_specs=[pl.BlockSpec((tm, d), lambda i: (i, 0)),
                   pl.BlockSpec((tm, d), lambda i: (i, 0)),
                   pl.BlockSpec((tm, LANES), lambda i: (i, 0))],
        out_shape=[jax.ShapeDtypeStruct((t, d), F32),
                   jax.ShapeDtypeStruct((t, d), F32),
                   jax.ShapeDtypeStruct((t, LANES), F32)],
        compiler_params=_cparams(("parallel",)),
        name="wout_project",
    )(*o_a, *o_b, *ys, mod_l, w_out_bf16, w_out_bf16, gain2.reshape(1, d), rw_split)


def _member_major(x_experts_last):
    shp = x_experts_last.shape[:-1]
    return x_experts_last.reshape(shp + (N_GROUPS, EXPERTS_PER_GROUP)).swapaxes(-1, -2).reshape(shp + (N_EXPERTS,))


def _route_kernel(lg_ref, bias_ref, info_ref, cnt_ref, run_ref, *, tm):
    @pl.when(pl.program_id(0) == 0)
    def _():
        run_ref[...] = jnp.zeros_like(run_ref)

    ng, nk = N_GROUPS, EXPERTS_PER_GROUP
    neg = jnp.float32(-jnp.inf)
    lgt = lg_ref[...].T
    aff = [jax.nn.sigmoid(lgt[k * ng:(k + 1) * ng, :]) for k in range(nk)]
    sel = [aff[k] + bias_ref[k * ng:(k + 1) * ng, :] for k in range(nk)]

    def first_max(vals):
        m = vals[0]
        for v in vals[1:]:
            m = jnp.maximum(m, v)
        idx = jnp.full(m.shape, float(nk), F32)
        for k in reversed(range(nk)):
            idx = jnp.where(vals[k] == m, float(k), idx)
        return m, idx

    m1, i1 = first_max(sel)
    m2, i2 = first_max([jnp.where(i1 == k, neg, sel[k]) for k in range(nk)])
    gscore = m1 + m2
    gid = lax.broadcasted_iota(jnp.int32, gscore.shape, 0).astype(F32)
    gmax = jnp.max(gscore, axis=0, keepdims=True)
    g = jnp.min(jnp.where(gscore == gmax, gid, 99.0), axis=0, keepdims=True)
    in_g = gid == g

    def pick(masks, vals):
        tot = jnp.zeros((1, tm), F32)
        for msk, v in zip(masks, vals):
            tot = tot + jnp.sum(jnp.where(msk, v, 0.0), axis=0, keepdims=True)
        return tot

    hit1 = [in_g & (i1 == k) for k in range(nk)]
    hit2 = [in_g & (i2 == k) for k in range(nk)]
    a1, a2 = pick(hit1, aff), pick(hit2, aff)
    den = a1 + a2
    e1 = g * nk + pick([in_g], [i1])
    e2 = g * nk + pick([in_g], [i2])
    oh = jnp.concatenate([jnp.where(hit1[k] | hit2[k], 1.0, 0.0) for k in range(nk)], axis=0)
    r_i = lax.broadcasted_iota(jnp.int32, (tm, tm), 0)
    c_i = lax.broadcasted_iota(jnp.int32, (tm, tm), 1)
    earlier = jnp.where(r_i < c_i, 1.0, 0.0).astype(BF16)
    cum = _dot(oh.astype(BF16), earlier) + run_ref[:, 0:1]
    cums = [cum[k * ng:(k + 1) * ng, :] for k in range(nk)]
    r1, r2 = pick(hit1, cums), pick(hit2, cums)
    run_new = run_ref[:, 0:1] + jnp.sum(oh, axis=1, keepdims=True)
    run_ref[...] = jnp.broadcast_to(run_new, run_ref.shape)
    cnt_ref[...] = jnp.broadcast_to(run_new, cnt_ref.shape)
    info_ref[...] = jnp.concatenate([e1, e2, a1 / den, a2 / den, r1, r2, jnp.zeros((2, tm), F32)], axis=0)


def route(logits, router_bias, tm=256):
    t = logits.shape[0]
    bias = jnp.broadcast_to(_member_major(router_bias.astype(F32))[:, None], (N_EXPERTS, tm))
    return pl.pallas_call(
        functools.partial(_route_kernel, tm=tm),
        grid=(t // tm,),
        in_specs=[pl.BlockSpec((tm, LANES), lambda i: (i, 0)),
                  pl.BlockSpec((N_EXPERTS, tm), lambda i: (0, 0))],
        out_specs=[pl.BlockSpec((8, tm), lambda i: (0, i)),
                   pl.BlockSpec((N_EXPERTS, LANES), lambda i: (0, 0))],
        out_shape=[jax.ShapeDtypeStruct((8, t), F32),
                   jax.ShapeDtypeStruct((N_EXPERTS, LANES), F32)],
        scratch_shapes=[pltpu.VMEM((N_EXPERTS, LANES), F32)],
        compiler_params=_cparams(("arbitrary",)),
        name="route",
    )(logits, bias)


MOE_AHEAD = 3
MOE_SLOTS = MOE_AHEAD + 1


def _expert_kernel(be_ref, nu_ref, tok_head_ref, tok_ahead_ref, h_hbm, w1_ref, w3_ref, w2_ref, o_ref,
                   xbuf, xcur, w1b, w3b, w2b, sem):
    i = pl.program_id(0)
    slot = i % MOE_SLOTS
    nu = nu_ref[0]

    def gather(idx_ref, base, s):
        for j in range(MOE_BLOCK):
            pltpu.make_async_copy(h_hbm.at[pl.ds(idx_ref[base + j], 1)], xbuf.at[s, pl.ds(j, 1)], sem.at[s]).start()

    def wait_block(s):
        pltpu.make_async_copy(h_hbm.at[pl.ds(0, MOE_BLOCK)], xbuf.at[s], sem.at[s]).wait()

    @pl.when(i == 0)
    def _():
        for blk in range(MOE_AHEAD):
            gather(tok_head_ref, blk * MOE_BLOCK, blk)

    @pl.when(i < nu)
    def _():
        @pl.when((i == 0) | (be_ref[i] != be_ref[jnp.maximum(i - 1, 0)]))
        def _():
            w1b[...] = w1_ref[...].astype(BF16)
            w3b[...] = w3_ref[...].astype(BF16)
            w2b[...] = w2_ref[...].astype(BF16)

        wait_block(slot)
        xcur[...] = xbuf[slot].astype(BF16)
        gather(tok_ahead_ref, 0, (i + MOE_AHEAD) % MOE_SLOTS)
        x = xcur[...]
        a = _dot(x, w1b[...])
        b = _dot(x, w3b[...])
        mid = (a * jax.nn.sigmoid(a)) * b
        o_ref[...] = _dot(mid.astype(BF16), w2b[...])

    @pl.when(i >= nu)
    def _():
        @pl.when(i < nu + MOE_AHEAD)
        def _():
            wait_block(slot)

        o_ref[...] = jnp.zeros_like(o_ref)


def expert_ffn(h2, slot_tok, block_expert, n_used, w1, w3, w2, layer):
    n_steps = slot_tok.shape[0] // MOE_BLOCK
    d, de = w1.shape[-2], w1.shape[-1]
    grid_spec = pltpu.PrefetchScalarGridSpec(
        num_scalar_prefetch=2,
        grid=(n_steps,),
        in_specs=[pl.BlockSpec((MOE_SLOTS * MOE_BLOCK,), lambda i, be, nu: (0,), memory_space=pltpu.SMEM),
                  pl.BlockSpec((MOE_BLOCK,), lambda i, be, nu: (jnp.minimum(i + MOE_AHEAD, n_steps - 1),),
                               memory_space=pltpu.SMEM),
                  pl.BlockSpec(memory_space=pl.ANY),
                  pl.BlockSpec((None, None, d, de), lambda i, be, nu: (layer, be[i], 0, 0)),
                  pl.BlockSpec((None, None, d, de), lambda i, be, nu: (layer, be[i], 0, 0)),
                  pl.BlockSpec((None, None, de, d), lambda i, be, nu: (layer, be[i], 0, 0))],
        out_specs=pl.BlockSpec((MOE_BLOCK, d), lambda i, be, nu: (i, 0)),
        scratch_shapes=[pltpu.VMEM((MOE_SLOTS, MOE_BLOCK, d), F32), pltpu.VMEM((MOE_BLOCK, d), BF16),
                        pltpu.VMEM((d, de), BF16), pltpu.VMEM((d, de), BF16), pltpu.VMEM((de, d), BF16),
                        pltpu.SemaphoreType.DMA((MOE_SLOTS,))],
    )
    return pl.pallas_call(
        _expert_kernel,
        grid_spec=grid_spec,
        out_shape=jax.ShapeDtypeStruct((n_steps * MOE_BLOCK, d), F32),
        compiler_params=_cparams(("arbitrary",)),
        name="expert_ffn",
    )(block_expert, n_used, slot_tok, slot_tok, h2, w1, w3, w2)


def _combine_kernel(dcur_ref, dnext_ref, y_ref, mod_ref, info_ref, g_ref, yb_hbm, *rest, tm, nt, d, final,
                    tiles_prompt):
    if final:
        op_ref, os_ref, buf, sem = rest
    else:
        o_ref, buf, sem = rest
    i = pl.program_id(0)
    slot = i % 2

    def row_copy(src_row, s, k, j):
        return pltpu.make_async_copy(yb_hbm.at[pl.ds(src_row, 1)], buf.at[s, k, pl.ds(j, 1)], sem.at[s])

    def issue(dref, s):
        def body(j, carry):
            for k in range(TOP_K):
                row_copy(dref[TOP_K * j + k], s, k, j).start()
            return carry

        lax.fori_loop(0, tm, body, 0, unroll=32)

    @pl.when(i == 0)
    def _():
        issue(dcur_ref, 0)

    @pl.when(i + 1 < nt)
    def _():
        issue(dnext_ref, 1 - slot)

    for k in range(TOP_K):
        pltpu.make_async_copy(yb_hbm.at[pl.ds(0, tm)], buf.at[slot, k], sem.at[slot]).wait()
    gates = info_ref[...]
    moe = gates[:, 0:1] * buf[slot, 0] + gates[:, 1:2] * buf[slot, 1]
    y2 = y_ref[...] + mod_ref[:, 5 * d:6 * d] * moe
    if not final:
        o_ref[...] = y2
        return
    y2 = y2 * lax.rsqrt(jnp.mean(y2 * y2, axis=-1, keepdims=True) + EPS) * g_ref[...]

    @pl.when(i < tiles_prompt)
    def _():
        op_ref[...] = y2

    @pl.when(i >= tiles_prompt)
    def _():
        os_ref[...] = y2


def combine(y1, mod_l, info, dest_flat, yb, final_gain, n_prompt_rows, sample_rows, final, tm=256):
    t, d = y1.shape
    nt = t // tm
    tiles_prompt = n_prompt_rows // tm
    midx = functools.partial(_mod_index, tm=tm, n_prompt_rows=n_prompt_rows, sample_rows=sample_rows)
    if final:
        out_specs = _split_specs((tm, d), tiles_prompt)
        out_shape = [jax.ShapeDtypeStruct((n_prompt_rows, d), F32), jax.ShapeDtypeStruct((t - n_prompt_rows, d), F32)]
    else:
        out_specs = pl.BlockSpec((tm, d), lambda i: (i, 0))
        out_shape = jax.ShapeDtypeStruct((t, d), F32)
    return pl.pallas_call(
        functools.partial(_combine_kernel, tm=tm, nt=nt, d=d, final=final, tiles_prompt=tiles_prompt),
        grid=(nt,),
        in_specs=[pl.BlockSpec((TOP_K * tm,), lambda i: (i,), memory_space=pltpu.SMEM),
                  pl.BlockSpec((TOP_K * tm,), lambda i: (jnp.minimum(i + 1, nt - 1),), memory_space=pltpu.SMEM),
                  pl.BlockSpec((tm, d), lambda i: (i, 0)),
                  pl.BlockSpec((None, 1, 6 * d), lambda i: (midx(i), 0, 0)),
                  pl.BlockSpec((tm, TOP_K), lambda i: (i, 0)),
                  pl.BlockSpec((1, d), lambda i: (0, 0)),
                  pl.BlockSpec(memory_space=pl.ANY)],
        out_specs=out_specs,
        out_shape=out_shape,
        scratch_shapes=[pltpu.VMEM((2, TOP_K, tm, d), F32), pltpu.SemaphoreType.DMA((2,))],
        compiler_params=_cparams(("arbitrary",)),
        name="combine",
    )(dest_flat, dest_flat, y1, mod_l, info, final_gain.reshape(1, d), yb)


def moe_layer(y1, h2p, logits, mod_l, router_bias, w1, w3, w2, layer, final_gain, n_prompt_rows, sample_rows, final):
    t, d = y1.shape
    info_t, cnt = route(logits, router_bias)
    expert = info_t[0:2].T.astype(jnp.int32)
    rank = info_t[4:6].T.astype(jnp.int32)
    info = info_t[2:4].T
    counts = cnt[:, 0].reshape(EXPERTS_PER_GROUP, N_GROUPS).T.reshape(-1).astype(jnp.int32)
    padded = (counts + MOE_BLOCK - 1) // MOE_BLOCK * MOE_BLOCK
    padded_end = jnp.cumsum(padded)
    padded_start = padded_end - padded
    dest = (padded_start[expert] + rank).reshape(-1)
    n_blocks = (t * TOP_K) // MOE_BLOCK + N_EXPERTS + MOE_AHEAD
    slots = n_blocks * MOE_BLOCK
    block_lo = jnp.arange(n_blocks, dtype=jnp.int32)[:, None] * MOE_BLOCK
    block_expert = jnp.minimum(jnp.sum((padded_end[None, :] <= block_lo).astype(jnp.int32), axis=1), N_EXPERTS - 1)
    n_used = (padded_end[-1:] // MOE_BLOCK).astype(jnp.int32)
    tok = jnp.broadcast_to(jnp.arange(t, dtype=jnp.int32)[:, None], (t, TOP_K)).reshape(-1)
    slot_tok = jnp.zeros((slots,), jnp.int32).at[dest].set(tok)
    yb = expert_ffn(h2p, slot_tok, block_expert, n_used, w1, w3, w2, layer)
    return combine(y1, mod_l, info, dest, yb, final_gain, n_prompt_rows, sample_rows, final)


def rope_tables(n, lead):
    quarter = MLA_ROPE // 4
    tpos = np.arange(n)
    row = (tpos // GRID_W).astype(np.float32)
    col = (tpos % GRID_W).astype(np.float32)
    inv_freq = (np.float32(ROPE_BASE) ** (-np.arange(quarter, dtype=np.float32) / np.float32(quarter))).astype(np.float32)
    lane = np.arange(64)
    pos = np.where(lane[None, :] < 32, row[:, None], col[:, None]).astype(np.float32)
    ang = pos * inv_freq[lane % quarter][None, :]
    cos = np.cos(ang).astype(np.float32)
    sin = np.sin(ang).astype(np.float32)
    sgn = np.where((lane % 32) < quarter, -1.0, 1.0).astype(np.float32)
    sin = sin * sgn[None, :]
    cos = np.concatenate([cos, cos], axis=1)
    sin = np.concatenate([sin, sin], axis=1)
    cos = np.concatenate([np.ones((lead, LANES), np.float32), cos], axis=0)
    sin = np.concatenate([np.zeros((lead, LANES), np.float32), sin], axis=0)
    return jnp.asarray(cos), jnp.asarray(sin)


def _rope(x, cos, sin_signed):
    lane = lax.broadcasted_iota(jnp.int32, x.shape, 1)
    partner = jnp.where((lane & 31) < 16, pltpu.roll(x, LANES - 16, axis=1), pltpu.roll(x, 16, axis=1))
    return x * cos + partner * sin_signed


def _mla_proj_kernel(cq_ref, ckv_ref, misc_ref, qg_ref, kvg_ref, wq_ref, wkv_ref, cos_ref, sin_ref,
                     q_ref, kv_ref, ckvn_ref, kpe_ref, *, scale):
    cq = cq_ref[...].astype(F32)
    qn = cq * lax.rsqrt(jnp.mean(cq * cq, axis=-1, keepdims=True) + EPS) * qg_ref[...]
    qf = _dot(qn.astype(BF16), wq_ref[...])
    half = qf.shape[1] // 2
    cos, sin = cos_ref[...], sin_ref[...]
    q_ref[:, 0:half] = (qf[:, 0:half] * scale).astype(BF16)
    for h in range(MLA_HEADS):
        pe = qf[:, half + h * LANES: half + (h + 1) * LANES]
        q_ref[:, half + h * LANES: half + (h + 1) * LANES] = (_rope(pe, cos, sin) * scale).astype(BF16)
    ckv = ckv_ref[...].astype(F32)
    ckvn = ckv * lax.rsqrt(jnp.mean(ckv * ckv, axis=-1, keepdims=True) + EPS) * kvg_ref[...]
    ckvn_ref[...] = ckvn
    kv_ref[...] = _dot(ckvn.astype(BF16), wkv_ref[...]).astype(BF16)
    misc = misc_ref[...].astype(F32)
    lane = lax.broadcasted_iota(jnp.int32, misc.shape, 1)
    kpe = jnp.where(lane < MLA_ROPE, misc, 0.0)
    kpe_ref[...] = _rope(kpe, cos, sin).astype(BF16)


def mla_project(z, q_gain, kv_gain, wq, wkv, cos_t, sin_t, n_prompt_rows, sample_rows, tm=512):
    t = z.shape[0]
    tp = n_prompt_rows // tm
    per = sample_rows // tm

    def tab(i):
        return (jnp.where(i < tp, 0, 1 + (i - tp) % per), 0)

    scale = float((MLA_NOPE + MLA_ROPE) ** -0.5)
    nq = wq.shape[1]
    return pl.pallas_call(
        functools.partial(_mla_proj_kernel, scale=scale),
        grid=(t // tm,),
        in_specs=[pl.BlockSpec((tm, MLA_Q_LORA), lambda i: (i, AB_CQ // MLA_Q_LORA)),
                  pl.BlockSpec((tm, MLA_KV_LORA), lambda i: (i, AB_CKV // MLA_KV_LORA)),
                  pl.BlockSpec((tm, LANES), lambda i: (i, AB_MISC // LANES)),
                  pl.BlockSpec((1, MLA_Q_LORA), lambda i: (0, 0)),
                  pl.BlockSpec((1, MLA_KV_LORA), lambda i: (0, 0)),
                  pl.BlockSpec(wq.shape, lambda i: (0, 0)),
                  pl.BlockSpec(wkv.shape, lambda i: (0, 0)),
                  pl.BlockSpec((tm, LANES), tab),
                  pl.BlockSpec((tm, LANES), tab)],
        out_specs=[pl.BlockSpec((tm, nq), lambda i: (i, 0)),
                   pl.BlockSpec((tm, wkv.shape[1]), lambda i: (i, 0)),
                   pl.BlockSpec((tm, MLA_KV_LORA), lambda i: (i, 0)),
                   pl.BlockSpec((tm, LANES), lambda i: (i, 0))],
        out_shape=[jax.ShapeDtypeStruct((t, nq), BF16),
                   jax.ShapeDtypeStruct((t, wkv.shape[1]), BF16),
                   jax.ShapeDtypeStruct((t, MLA_KV_LORA), F32),
                   jax.ShapeDtypeStruct((t, LANES), BF16)],
        compiler_params=_cparams(("parallel",)),
        name="mla_project",
    )(z, z, z, q_gain.reshape(1, -1), kv_gain.reshape(1, -1), wq, wkv, cos_t, sin_t)


def _ctx_kv_kernel(ckv_ref, w_ref, o_ref):
    o_ref[...] = _dot(ckv_ref[...].astype(BF16), w_ref[...]).astype(BF16)


def ctx_kv_project(ckv_c, wkv):
    rows = ckv_c.shape[0]
    return pl.pallas_call(
        _ctx_kv_kernel,
        grid=(1,),
        in_specs=[pl.BlockSpec(ckv_c.shape, lambda i: (0, 0)), pl.BlockSpec(wkv.shape, lambda i: (0, 0))],
        out_specs=pl.BlockSpec((rows, wkv.shape[1]), lambda i: (0, 0)),
        out_shape=jax.ShapeDtypeStruct((rows, wkv.shape[1]), BF16),
        compiler_params=_cparams(("arbitrary",)),
        name="ctx_kv_project",
    )(ckv_c, wkv)


def _softmax_pv(s, v):
    m = jnp.max(s, axis=-1, keepdims=True)
    p = jnp.exp(s - m)
    l = jnp.sum(p, axis=-1, keepdims=True)
    return _dot(p.astype(BF16), v) / l


def _mla_prompt_attn_kernel(q_ref, kv_ref, kp_ref, o_ref):
    nh = MLA_HEADS
    kp = kp_ref[...]
    for h in range(nh):
        q = jnp.concatenate([q_ref[:, h * LANES:(h + 1) * LANES], q_ref[:, (nh + h) * LANES:(nh + h + 1) * LANES]], axis=1)
        k = jnp.concatenate([kv_ref[:, h * LANES:(h + 1) * LANES], kp], axis=1)
        v = kv_ref[:, (nh + h) * LANES:(nh + h + 1) * LANES]
        o_ref[:, h * LANES:(h + 1) * LANES] = _softmax_pv(_dot_nt(q, k), v).astype(o_ref.dtype)


def mla_prompt_attention(q_m, kv, kpe, batch, n):
    h = MLA_HEADS
    return pl.pallas_call(
        _mla_prompt_attn_kernel,
        grid=(batch,),
        in_specs=[pl.BlockSpec((n, 2 * h * LANES), lambda b: (b, 0)),
                  pl.BlockSpec((n, 2 * h * LANES), lambda b: (b, 0)),
                  pl.BlockSpec((n, LANES), lambda b: (b, 0))],
        out_specs=pl.BlockSpec((n, h * MLA_DV), lambda b: (b, 0)),
        out_shape=jax.ShapeDtypeStruct((batch * n, h * MLA_DV), BF16),
        compiler_params=_cparams(("parallel",)),
        name="mla_prompt_attention",
    )(q_m, kv, kpe)


def _lane_tile_reduce(x, op):
    r = x[:, 0:LANES]
    for c in range(1, x.shape[1] // LANES):
        r = op(r, x[:, c * LANES:(c + 1) * LANES])
    return r


def _mla_sample_attn_kernel(qn_ref, qp_ref, kcn_ref, kcp_ref, vc_ref, kn_ref, kp_ref, v_ref, o_ref,
                            s_ref, m_ref, acc_ref, *, tk):
    past = kcn_ref.shape[0]
    nk = kn_ref.shape[0] // tk
    q = jnp.concatenate([qn_ref[...], qp_ref[...]], axis=1)
    kc = jnp.concatenate([kcn_ref[...], kcp_ref[...].astype(BF16)], axis=1)
    s = _dot_nt(q, kc)
    s_ref[:, 0:past] = s
    m_ref[...] = _lane_tile_reduce(s, jnp.maximum)

    def scores(j, carry):
        off = pl.multiple_of(j * tk, tk)
        k = jnp.concatenate([kn_ref[pl.ds(off, tk), :], kp_ref[pl.ds(off, tk), :]], axis=1)
        s = _dot_nt(q, k)
        s_ref[:, pl.ds(pl.multiple_of(past + j * tk, LANES), tk)] = s
        m_ref[...] = jnp.maximum(m_ref[...], _lane_tile_reduce(s, jnp.maximum))
        return carry

    lax.fori_loop(0, nk, scores, 0, unroll=2)
    m_ref[...] = jnp.broadcast_to(jnp.max(m_ref[...], axis=-1, keepdims=True), m_ref.shape)

    def weighted_values(s, v):
        m = m_ref[...]
        p = jnp.concatenate([jnp.exp(s[:, c * LANES:(c + 1) * LANES] - m).astype(BF16)
                             for c in range(s.shape[1] // LANES)], axis=1)
        v_ones = jnp.concatenate([v, jnp.ones((v.shape[0], LANES), BF16)], axis=1)
        return _dot(p, v_ones)

    acc_ref[...] = weighted_values(s_ref[:, 0:past], vc_ref[...])

    def weighted(j, carry):
        off = pl.multiple_of(j * tk, tk)
        acc_ref[...] += weighted_values(s_ref[:, pl.ds(pl.multiple_of(past + j * tk, LANES), tk)],
                                        v_ref[pl.ds(off, tk), :])
        return carry

    lax.fori_loop(0, nk, weighted, 0, unroll=2)
    acc = acc_ref[...]
    o_ref[...] = (acc[:, 0:MLA_DV] / acc[:, MLA_DV:]).astype(o_ref.dtype)


def mla_sample_attention(q_m, kv, kpe, kv_ctx, kpe_ctx, row0, batch, n, past, tq=1024, tk=512):
    h = MLA_HEADS
    qb = n // tq
    r0 = row0 // n
    q0 = row0 // tq
    return pl.pallas_call(
        functools.partial(_mla_sample_attn_kernel, tk=tk),
        grid=(batch, h, qb),
        in_specs=[pl.BlockSpec((tq, LANES), lambda b, hh, i: (q0 + b * qb + i, hh)),
                  pl.BlockSpec((tq, LANES), lambda b, hh, i: (q0 + b * qb + i, h + hh)),
                  pl.BlockSpec((past, LANES), lambda b, hh, i: (b, hh)),
                  pl.BlockSpec((past, LANES), lambda b, hh, i: (b, 0)),
                  pl.BlockSpec((past, LANES), lambda b, hh, i: (b, h + hh)),
                  pl.BlockSpec((n, LANES), lambda b, hh, i: (r0 + b, hh)),
                  pl.BlockSpec((n, LANES), lambda b, hh, i: (r0 + b, 0)),
                  pl.BlockSpec((n, LANES), lambda b, hh, i: (r0 + b, h + hh))],
        out_specs=pl.BlockSpec((tq, LANES), lambda b, hh, i: (b * qb + i, hh)),
        out_shape=jax.ShapeDtypeStruct((batch * n, h * MLA_DV), BF16),
        scratch_shapes=[pltpu.VMEM((tq, past + n), F32), pltpu.VMEM((tq, LANES), F32),
                        pltpu.VMEM((tq, MLA_DV + LANES), F32)],
        compiler_params=_cparams(("parallel", "parallel", "arbitrary")),
        name="mla_sample_attention",
    )(q_m, q_m, kv_ctx, kpe_ctx, kv_ctx, kv, kpe, kv)


def _na_prompt_attn_kernel(q_ref, k_ref, v_ref, o_ref, ko_ref, vo_ref, *, scale):
    for h in range(NA_HEADS):
        cols = slice(h * LANES, (h + 1) * LANES)
        k = k_ref[:, cols]
        v = v_ref[:, cols]
        ko_ref[h] = k.astype(F32)
        vo_ref[h] = v.astype(F32)
        q = (q_ref[:, cols].astype(F32) * scale).astype(BF16)
        o_ref[:, cols] = _softmax_pv(_dot_nt(q, k), v).astype(o_ref.dtype)


def na_prompt_attention(z, batch, n):
    h = NA_HEADS
    scale = float(NA_HD ** -0.5)
    w = h * NA_HD
    cq, ck, cv = CD_NQ // w, CD_NK // w, CD_NV // w
    return pl.pallas_call(
        functools.partial(_na_prompt_attn_kernel, scale=scale),
        grid=(batch,),
        in_specs=[pl.BlockSpec((n, w), lambda b: (b, cq)),
                  pl.BlockSpec((n, w), lambda b: (b, ck)),
                  pl.BlockSpec((n, w), lambda b: (b, cv))],
        out_specs=[pl.BlockSpec((n, w), lambda b: (b, 0)),
                   pl.BlockSpec((None, h, n, NA_HD), lambda b: (b, 0, 0, 0)),
                   pl.BlockSpec((None, h, n, NA_HD), lambda b: (b, 0, 0, 0))],
        out_shape=[jax.ShapeDtypeStruct((batch * n, w), BF16),
                   jax.ShapeDtypeStruct((batch, h, n, NA_HD), F32),
                   jax.ShapeDtypeStruct((batch, h, n, NA_HD), F32)],
        compiler_params=_cparams(("parallel",)),
        name="na_prompt_attention",
    )(z, z, z)


def _dot_tn(a, b):
    return lax.dot_general(a, b, (((0,), (0,)), ((), ())), preferred_element_type=F32)


def _head_norm_gate(acc, gate_in, gain):
    outs = []
    for h in range(2):
        a = acc[:, h * LANES:(h + 1) * LANES]
        outs.append(a * lax.rsqrt(jnp.mean(a * a, axis=-1, keepdims=True) + EPS))
    g = gate_in
    return jnp.concatenate(outs, axis=1) * gain * (g * jax.nn.sigmoid(g))


RET_CHUNK = 256
FIN_ROWS = 256


def _ret_kernel(ld_ref, q_ref, k_ref, v_ref, g_ref, cos_ref, sin_ref, gain_ref, s0_ref, o_ref, sf_ref, acc_ref, *, n):
    L = RET_CHUNK
    nc = n // L
    hp = pl.program_id(1)
    lane = lax.broadcasted_iota(jnp.int32, (1, LANES), 1)
    lo_lanes = lane < RET_DK
    row128 = lax.broadcasted_iota(jnp.int32, (LANES, 1), 0)
    ri = lax.broadcasted_iota(jnp.int32, (L, L), 0)
    ci = lax.broadcasted_iota(jnp.int32, (L, L), 1)
    dist = (ri - ci).astype(F32)
    pos = lax.broadcasted_iota(jnp.int32, (L, 1), 0).astype(F32)

    consts = []
    for dr in range(2):
        lg0 = ld_ref[dr, 2 * hp]
        lg1 = ld_ref[dr, 2 * hp + 1]
        lg2 = jnp.where(lo_lanes, lg0, lg1)
        lgc = jnp.where(row128 < RET_DK, lg0, lg1)
        sd = dist if dr == 0 else -dist
        ok = sd >= 0
        intra = [jnp.where(ok, jnp.exp(jnp.where(ok, sd, 0.0) * lg), 0.0) for lg in (lg0, lg1)]
        if dr == 0:
            inter = jnp.exp((pos + 1.0) * lg2)
            to_state = jnp.exp((L - 1.0 - pos) * lg2)
        else:
            inter = jnp.exp((L - pos) * lg2)
            to_state = jnp.exp(pos * lg2)
        consts.append((intra, inter, to_state, jnp.exp(float(L) * lgc)))

    def clear(i, carry):
        acc_ref[pl.ds(pl.multiple_of(i * FIN_ROWS, FIN_ROWS), FIN_ROWS), :] = jnp.zeros((FIN_ROWS, 2 * LANES), F32)
        return carry

    lax.fori_loop(0, n // FIN_ROWS, clear, 0)

    def step(c, dr, s2):
        intra, inter, to_state, cdec = consts[dr]
        rows = pl.ds(pl.multiple_of(c * L, L), L)
        cos, sin = cos_ref[rows, :], sin_ref[rows, :]
        q = _rope(q_ref[rows, :].astype(F32) * (RET_DK ** -0.5), cos, sin)
        k = _rope(k_ref[rows, :].astype(F32), cos, sin)
        kb = k.astype(BF16)
        s2b = s2.astype(BF16)
        kt = (k * to_state).astype(BF16)
        upd = []
        for h in range(2):
            qm = jnp.where(lo_lanes if h == 0 else ~lo_lanes, q, 0.0)
            vh = v_ref[rows, h * LANES:(h + 1) * LANES].astype(BF16)
            sc = _dot_nt(qm.astype(BF16), kb) * intra[h]
            o = _dot(sc.astype(BF16), vh) + _dot((qm * inter).astype(BF16), s2b)
            acc_ref[rows, h * LANES:(h + 1) * LANES] += o
            upd.append(_dot_tn(kt, vh))
        return cdec * s2 + jnp.where(row128 < RET_DK, upd[0], upd[1])

    def body(i, carry):
        sf, sb = carry
        return step(i, 0, sf), step(nc - 1 - i, 1, sb)

    sf, sb = lax.fori_loop(0, nc, body, (s0_ref[0], s0_ref[1]))
    sf_ref[0] = sf
    sf_ref[1] = sb

    def fin(i, carry):
        rows = pl.ds(pl.multiple_of(i * FIN_ROWS, FIN_ROWS), FIN_ROWS)
        o_ref[rows, :] = _head_norm_gate(acc_ref[rows, :], g_ref[rows, :].astype(F32), gain_ref[...]).astype(o_ref.dtype)
        return carry

    lax.fori_loop(0, n // FIN_ROWS, fin, 0)


def retention_mixer(z, log_decay, ret_gain, s0, cos_t, sin_t, row0, batch, n):
    hp = RET_HEADS // 2
    r0 = row0 // n
    cq, ck, cv, cg = CD_RQ // LANES, CD_RK // LANES, CD_RV // (2 * LANES), CD_RG // (2 * LANES)
    grid_spec = pltpu.PrefetchScalarGridSpec(
        num_scalar_prefetch=1,
        grid=(batch, hp),
        in_specs=[pl.BlockSpec((n, LANES), lambda b, p, ld: (r0 + b, cq + p)),
                  pl.BlockSpec((n, LANES), lambda b, p, ld: (r0 + b, ck + p)),
                  pl.BlockSpec((n, 2 * LANES), lambda b, p, ld: (r0 + b, cv + p)),
                  pl.BlockSpec((n, 2 * LANES), lambda b, p, ld: (r0 + b, cg + p)),
                  pl.BlockSpec((n, LANES), lambda b, p, ld: (0, 0)),
                  pl.BlockSpec((n, LANES), lambda b, p, ld: (0, 0)),
                  pl.BlockSpec((1, 2 * LANES), lambda b, p, ld: (0, p)),
                  pl.BlockSpec((None, 2, LANES, LANES), lambda b, p, ld: (b, 0, p, 0))],
        out_specs=[pl.BlockSpec((n, 2 * LANES), lambda b, p, ld: (b, p)),
                   pl.BlockSpec((None, 2, LANES, LANES), lambda b, p, ld: (b, 0, p, 0))],
        scratch_shapes=[pltpu.VMEM((n, 2 * LANES), F32)],
    )
    return pl.pallas_call(
        functools.partial(_ret_kernel, n=n),
        grid_spec=grid_spec,
        out_shape=[jax.ShapeDtypeStruct((batch * n, RET_HEADS * RET_DV), BF16),
                   jax.ShapeDtypeStruct((batch, 2, RET_HEADS * RET_DK, RET_DV), F32)],
        compiler_params=_cparams(("parallel", "parallel")),
        name="retention_mixer",
    )(log_decay.astype(F32), z, z, z, z, cos_t, sin_t, ret_gain.reshape(1, -1), s0)


GLA_CHUNK = 256
GLA_SUB = 64
GLA_SAFE_EXPONENT = 60.0


def _gla_kernel(q_ref, k_ref, v_ref, g_ref, misc_ref, gw_ref, gb_ref, gain_ref, s0_ref, o_ref, sf_ref,
                la_ref, acc_ref, *, n):
    C, SB = GLA_CHUNK, GLA_SUB
    nsb = C // SB
    nc = n // C
    lane = lax.broadcasted_iota(jnp.int32, (1, LANES), 1)
    lo_lanes = lane < GLA_DK
    head_mask = (lo_lanes, ~lo_lanes)
    ri = lax.broadcasted_iota(jnp.int32, (C, C), 0)
    ci = lax.broadcasted_iota(jnp.int32, (C, C), 1)
    rowc = lax.broadcasted_iota(jnp.int32, (C, 1), 0)
    SLAB = 16
    row8 = lax.broadcasted_iota(jnp.int32, (SLAB, 1), 0)
    tri = (jnp.where(ci <= ri, 1.0, 0.0).astype(BF16), jnp.where(ci >= ri, 1.0, 0.0).astype(BF16))
    causal = (ci <= ri, ci >= ri)

    def prep(i, carry):
        rows = pl.ds(pl.multiple_of(i * FIN_ROWS, FIN_ROWS), FIN_ROWS)
        slab = misc_ref[rows, :].astype(BF16)
        for dr in range(2):
            pre = _dot(slab, gw_ref[dr]) + gb_ref[dr]
            la_ref[dr, rows, :] = (jnp.minimum(pre, 0.0) - jnp.log1p(jnp.exp(-jnp.abs(pre)))) * (1.0 / GATE_TAU)
        acc_ref[rows, :] = jnp.zeros((FIN_ROWS, 2 * LANES), F32)
        return carry

    lax.fori_loop(0, n // FIN_ROWS, prep, 0)

    def step(c, dr):
        rev = dr == 1
        base = pl.multiple_of(c * C, C)
        rows = pl.ds(base, C)
        la = la_ref[dr, rows, :]
        hi = la.astype(BF16)
        lo = (la - hi.astype(F32)).astype(BF16)
        b = _dot(tri[dr], hi) + _dot(tri[dr], lo)
        zero = jnp.zeros((1, LANES), F32)
        if not rev:
            bref = [zero if i == 0 else b[SB * i - 1:SB * i, :] for i in range(nsb)]
            bend = [b[SB * i + SB - 1:SB * i + SB, :] for i in range(nsb)]
            blast = b[C - 1:C, :]
        else:
            bref = [zero if i == nsb - 1 else b[SB * (i + 1):SB * (i + 1) + 1, :] for i in range(nsb)]
            bend = [b[SB * i:SB * i + 1, :] for i in range(nsb)]
            blast = b[0:1, :]
        spread = bref[0] - bend[0]
        for i in range(1, nsb):
            spread = jnp.maximum(spread, bref[i] - bend[i])
        worst = jnp.max(spread)

        def fast(st):
            q = q_ref[rows, :].astype(F32) * (GLA_DK ** -0.5)
            k = k_ref[rows, :].astype(F32)
            a_rows = ([], [])
            for i in range(nsb):
                qe = q[SB * i:SB * (i + 1), :] * jnp.exp(b[SB * i:SB * (i + 1), :] - bref[i])
                valid = (rowc < SB * (i + 1)) if not rev else (rowc >= SB * i)
                ke = jnp.where(valid, k * jnp.exp(jnp.where(valid, bref[i] - b, 0.0)), 0.0).astype(BF16)
                for h in range(2):
                    a_rows[h].append(_dot_nt(jnp.where(head_mask[h], qe, 0.0).astype(BF16), ke))
            qi = q * jnp.exp(b)
            ks = (k * jnp.exp(blast - b)).astype(BF16)
            stb = st.astype(BF16)
            upd = []
            for h in range(2):
                a = jnp.where(causal[dr], jnp.concatenate(a_rows[h], axis=0), 0.0)
                vh = v_ref[rows, h * LANES:(h + 1) * LANES].astype(BF16)
                o = _dot(a.astype(BF16), vh) + _dot_nt(jnp.where(head_mask[h], qi, 0.0).astype(BF16), stb)
                acc_ref[rows, h * LANES:(h + 1) * LANES] += o
                upd.append(_dot_tn(vh, ks))
            return jnp.exp(blast) * st + jnp.where(lo_lanes, upd[0], upd[1])

        def slow(st):
            def row_step(r, st):
                t = r if not rev else C - 1 - r
                slab = pl.ds(pl.multiple_of(base + (t // SLAB) * SLAB, SLAB), SLAB)
                pick = row8 == (t % SLAB)
                q8 = jnp.where(pick, q_ref[slab, :].astype(F32) * (GLA_DK ** -0.5), 0.0)
                k8 = jnp.where(pick, k_ref[slab, :].astype(F32), 0.0).astype(BF16)
                la_t = jnp.sum(jnp.where(pick, la_ref[dr, slab, :], 0.0), axis=0, keepdims=True)
                upd = []
                for h in range(2):
                    v8 = jnp.where(pick, v_ref[slab, h * LANES:(h + 1) * LANES].astype(F32), 0.0).astype(BF16)
                    upd.append(_dot_tn(v8, k8))
                st = jnp.exp(la_t) * st + jnp.where(lo_lanes, upd[0], upd[1])
                stb = st.astype(BF16)
                for h in range(2):
                    qh = jnp.where(head_mask[h], q8, 0.0).astype(BF16)
                    acc_ref[slab, h * LANES:(h + 1) * LANES] += _dot_nt(qh, stb)
                return st

            return lax.fori_loop(0, C + 0 * pl.program_id(0), row_step, st)

        return worst, fast, slow

    def body(i, carry):
        worst_f, fast_f, slow_f = step(i, 0)
        worst_b, fast_b, slow_b = step(nc - 1 - i, 1)
        return lax.cond(jnp.maximum(worst_f, worst_b) <= GLA_SAFE_EXPONENT,
                        lambda st: (fast_f(st[0]), fast_b(st[1])),
                        lambda st: (slow_f(st[0]), slow_b(st[1])), carry)

    sf, sb = lax.fori_loop(0, nc, body, (s0_ref[0].T, s0_ref[1].T))
    sf_ref[0] = sf.T
    sf_ref[1] = sb.T

    def fin(i, carry):
        rows = pl.ds(pl.multiple_of(i * FIN_ROWS, FIN_ROWS), FIN_ROWS)
        o_ref[rows, :] = _head_norm_gate(acc_ref[rows, :], g_ref[rows, :].astype(F32), gain_ref[...]).astype(o_ref.dtype)
        return carry

    lax.fori_loop(0, n // FIN_ROWS, fin, 0)


def _prep_gate(gate_w, gate_b):
    gw = jnp.zeros((2, LANES, gate_w.shape[-1]), F32)
    for dr in range(2):
        lo = MLA_ROPE + dr * GATE_RANK
        gw = gw.at[dr, lo:lo + GATE_RANK, :].set(gate_w[dr].astype(F32))
    return gw.astype(BF16), gate_b.astype(F32).reshape(2, 1, -1)


def gla_mixer(z, gate_w, gate_b, gla_gain, s0, row0, batch, n):
    hp = GLA_HEADS // 2
    r0 = row0 // n
    gw, gb = _prep_gate(gate_w, gate_b)
    cq, ck, cv, cg, cm = AB_GQ // LANES, AB_GK // LANES, AB_GV // (2 * LANES), AB_GR // (2 * LANES), AB_MISC // LANES
    return pl.pallas_call(
        functools.partial(_gla_kernel, n=n),
        grid=(batch, hp),
        in_specs=[pl.BlockSpec((n, LANES), lambda b, p: (r0 + b, cq + p)),
                  pl.BlockSpec((n, LANES), lambda b, p: (r0 + b, ck + p)),
                  pl.BlockSpec((n, 2 * LANES), lambda b, p: (r0 + b, cv + p)),
                  pl.BlockSpec((n, 2 * LANES), lambda b, p: (r0 + b, cg + p)),
                  pl.BlockSpec((n, LANES), lambda b, p: (r0 + b, cm)),
                  pl.BlockSpec((2, LANES, LANES), lambda b, p: (0, 0, p)),
                  pl.BlockSpec((2, 1, LANES), lambda b, p: (0, 0, p)),
                  pl.BlockSpec((1, 2 * LANES), lambda b, p: (0, p)),
                  pl.BlockSpec((None, 2, LANES, LANES), lambda b, p: (b, 0, p, 0))],
        out_specs=[pl.BlockSpec((n, 2 * LANES), lambda b, p: (b, p)),
                   pl.BlockSpec((None, 2, LANES, LANES), lambda b, p: (b, 0, p, 0))],
        out_shape=[jax.ShapeDtypeStruct((batch * n, GLA_HEADS * GLA_DV), BF16),
                   jax.ShapeDtypeStruct((batch, 2, GLA_HEADS * GLA_DK, GLA_DV), F32)],
        scratch_shapes=[pltpu.VMEM((2, n, LANES), F32), pltpu.VMEM((n, 2 * LANES), F32)],
        compiler_params=_cparams(("parallel", "parallel")),
        name="gla_mixer",
    )(z, z, z, z, z, gw, gb, gla_gain.reshape(1, -1), s0)


NA_QROWS = 8
NA_WROWS = 16
NA_NEG = -1e30
NA_TAB = 32


def _na_bias_table(na_bias_l):
    qc = np.arange(GRID_W)[:, None]
    kc = np.arange(GRID_W)[None, :]
    ws = np.clip(qc - NA_COLS // 2, 0, GRID_W - NA_COLS)
    col_ok = (kc >= ws) & (kc < ws + NA_COLS)
    col_off = np.clip(kc - qc + NA_COLS - 1, 0, 2 * NA_COLS - 2)
    aa = (np.arange(NA_TAB) - NA_QROWS)[:, None] + np.arange(2)[None, :]
    n_row, n_col = 2 * NA_ROWS - 1, 2 * NA_COLS - 1
    sel_row = (aa[:, :, None] == np.arange(n_row)[None, None, :]).astype(np.float32)
    sel_col = ((col_off[:, :, None] == np.arange(n_col)[None, None, :]) & col_ok[:, :, None]).astype(np.float32)
    fill = np.where(col_ok, 0.0, NA_NEG).astype(np.float32)
    tab = jnp.einsum('hac,sea,qkc->hsqek', na_bias_l.astype(F32), jnp.asarray(sel_row), jnp.asarray(sel_col),
                     precision=lax.Precision.HIGHEST)
    tab = tab + jnp.asarray(fill)[None, None, :, None, :]
    h = na_bias_l.shape[0]
    return tab.reshape(h, NA_TAB, GRID_W, 2 * GRID_W)


def _na_latent_kernel(q_ref, k_ref, v_ref, kc_ref, vc_ref, tab_ref, o_ref, s_ref, *, scale, rows_total):
    g = pl.program_id(2)
    r0 = g * NA_QROWS
    start = jnp.clip(r0 - NA_ROWS // 2, 0, rows_total - NA_WROWS)
    nloc = NA_WROWS * GRID_W
    krows = pl.ds(pl.multiple_of(start * GRID_W, GRID_W), nloc)
    q = (q_ref[...].astype(F32) * scale).astype(BF16)
    kw = k_ref[krows, :]
    s_ref[:, 0:nloc] = _dot_nt(q, kw)
    s_ref[:, nloc:] = _dot_nt(q, kc_ref[...].astype(BF16))
    lane = lax.broadcasted_iota(jnp.int32, (1, LANES), 1)
    for qr in range(NA_QROWS):
        r = r0 + qr
        rs = jnp.clip(r - NA_ROWS // 2, 0, rows_total - NA_ROWS)
        for kp in range(NA_WROWS // 2):
            k0 = start + 2 * kp
            ok0 = (k0 >= rs) & (k0 < rs + NA_ROWS)
            ok1 = (k0 + 1 >= rs) & (k0 + 1 < rs + NA_ROWS)
            neg = jnp.where(lane < GRID_W, jnp.where(ok0, 0.0, NA_NEG), jnp.where(ok1, 0.0, NA_NEG))
            slot = k0 - r + (NA_ROWS - 1) + NA_QROWS
            s_ref[qr * GRID_W:(qr + 1) * GRID_W, kp * LANES:(kp + 1) * LANES] += tab_ref[slot] + neg
    s = s_ref[...]
    m = jnp.max(s, axis=-1, keepdims=True)
    p = jnp.exp(s - m)
    l = jnp.sum(p, axis=-1, keepdims=True)
    pb = p.astype(BF16)
    o = _dot(pb[:, 0:nloc], v_ref[krows, :]) + _dot(pb[:, nloc:], vc_ref[...].astype(BF16))
    o_ref[...] = (o / l).astype(o_ref.dtype)


def na_latent_attention(z, k_ctx, v_ctx, na_bias_l, row0, batch, n):
    h = NA_HEADS
    past = k_ctx.shape[2]
    rows_total = n // GRID_W
    tq = NA_QROWS * GRID_W
    qb = n // tq
    r0 = row0 // n
    q0 = row0 // tq
    cq, ck, cv = CD_NQ // LANES, CD_NK // LANES, CD_NV // LANES
    tab = _na_bias_table(na_bias_l)
    return pl.pallas_call(
        functools.partial(_na_latent_kernel, scale=float(NA_HD ** -0.5), rows_total=rows_total),
        grid=(batch, h, qb),
        in_specs=[pl.BlockSpec((tq, LANES), lambda b, hh, g: (q0 + b * qb + g, cq + hh)),
                  pl.BlockSpec((n, LANES), lambda b, hh, g: (r0 + b, ck + hh)),
                  pl.BlockSpec((n, LANES), lambda b, hh, g: (r0 + b, cv + hh)),
                  pl.BlockSpec((None, None, past, NA_HD), lambda b, hh, g: (b, hh, 0, 0)),
                  pl.BlockSpec((None, None, past, NA_HD), lambda b, hh, g: (b, hh, 0, 0)),
                  pl.BlockSpec((None, NA_TAB, GRID_W, 2 * GRID_W), lambda b, hh, g: (hh, 0, 0, 0))],
        out_specs=pl.BlockSpec((tq, LANES), lambda b, hh, g: (b * qb + g, hh)),
        out_shape=jax.ShapeDtypeStruct((batch * n, h * NA_HD), BF16),
        scratch_shapes=[pltpu.VMEM((tq, NA_WROWS * GRID_W + past), F32)],
        compiler_params=_cparams(("parallel", "parallel", "arbitrary")),
        name="na_latent_attention",
    )(z, z, z, k_ctx, v_ctx, tab)


def _prep_w_in_ab(w):
    d = w.shape[0]
    sizes = (512, 512, 1024, 1024, 2 * GATE_RANK, MLA_Q_LORA, MLA_KV_LORA, MLA_ROPE)
    gq, gk, gv, gr, glr, cq, ckv, kpe = jnp.split(w, np.cumsum(sizes)[:-1].tolist(), axis=1)
    pad = jnp.zeros((d, AB_WIDTH - AB_MISC - MLA_ROPE - 2 * GATE_RANK), w.dtype)
    return jnp.concatenate([gq, gk, gv, gr, cq, ckv, kpe, glr, pad], axis=1).astype(BF16)


def _prep_w_uq(w):
    r = w.shape[0]
    w3 = w.reshape(r, MLA_HEADS, MLA_NOPE + MLA_ROPE)
    nope = w3[:, :, :MLA_NOPE].reshape(r, MLA_HEADS * MLA_NOPE)
    pe = jnp.pad(w3[:, :, MLA_NOPE:], ((0, 0), (0, 0), (0, LANES - MLA_ROPE))).reshape(r, MLA_HEADS * LANES)
    return jnp.concatenate([nope, pe], axis=1).astype(BF16)


def _prep_w_ukv(w):
    r = w.shape[0]
    w3 = w.reshape(r, MLA_HEADS, MLA_NOPE + MLA_DV)
    return jnp.concatenate([w3[:, :, :MLA_NOPE].reshape(r, -1), w3[:, :, MLA_NOPE:].reshape(r, -1)], axis=1).astype(BF16)


def _prep_router(router_w):
    w = jnp.pad(_member_major(router_w.astype(F32)), ((0, 0), (0, LANES - N_EXPERTS)))
    hi = w.astype(BF16)
    lo = (w - hi.astype(F32)).astype(BF16)
    return jnp.stack([hi, lo], axis=0)


def kernel(x_prompt, x_sample, state_gla, cache_mla_ckv, cache_mla_kpe, state_ret, cache_na_k, cache_na_v,
           c, c_ctx, ada_w, ada_b, norm1, norm2, w_in_ab, gla_gate_w, gla_gate_b, gla_gain, mla_q_gain,
           mla_w_uq, mla_kv_gain, mla_w_ukv, w_in_cd, ret_log_decay, ret_gain, na_bias, w_out,
           router_w, router_bias, exp_w1, exp_w3, exp_w2, final_norm):
    bp, n_p, d = x_prompt.shape
    bs, n_s, _ = x_sample.shape
    past = cache_mla_ckv.shape[2]
    depth = ada_w.shape[0]
    tp, ts = bp * n_p, bs * n_s
    ys = [x_prompt.reshape(tp, d), x_sample.reshape(ts, d)]

    cond8 = jnp.zeros((8, d), F32).at[0].set(c_ctx).at[1:1 + bs].set(c)
    mod = adaln_all(cond8, ada_w, ada_b)
    rw_split = _prep_router(router_w)
    w_out_bf = w_out.astype(BF16)
    cos_t, sin_t = rope_tables(n_s, 512)
    cos_s, sin_s = cos_t[512:], sin_t[512:]
    cos_id, sin_id = cos_t[:n_p], sin_t[:n_p]

    outs = {}
    for l in range(depth):
        i = l // 2
        mod_l = mod[l].reshape(8, 1, 6 * d)
        final = l == depth - 1
        if l % 2 == 0:
            z = win_project(ys, mod_l, norm1[l], _prep_w_in_ab(w_in_ab[i]), tp, n_s)
            s0p = jnp.zeros((bp, 2, GLA_HEADS * GLA_DK, GLA_DV), F32)
            s0s = state_gla[:, i].reshape(bs, 2, GLA_HEADS * GLA_DK, GLA_DV)
            og_p, st_p = gla_mixer(z, gla_gate_w[i], gla_gate_b[i], gla_gain[i], s0p, 0, bp, n_p)
            og_s, _ = gla_mixer(z, gla_gate_w[i], gla_gate_b[i], gla_gain[i], s0s, tp, bs, n_s)
            st_p = st_p.reshape(bp, 2, GLA_HEADS, GLA_DK, GLA_DV)
            o_a = (og_p, og_s)
            wq, wkv = _prep_w_uq(mla_w_uq[i]), _prep_w_ukv(mla_w_ukv[i])
            q_m, kv, ckvn, kpe = mla_project(z, mla_q_gain[i], mla_kv_gain[i], wq, wkv, cos_t, sin_t, tp, n_s)
            kv_ctx = ctx_kv_project(cache_mla_ckv[:, i].reshape(bs * past, -1), wkv)
            kpe_ctx = jnp.pad(cache_mla_kpe[:, i].reshape(bs * past, -1), ((0, 0), (0, LANES - MLA_ROPE)))
            om_p = mla_prompt_attention(q_m, kv, kpe, bp, n_p)
            om_s = mla_sample_attention(q_m, kv, kpe, kv_ctx, kpe_ctx, tp, bs, n_s, past)
            o_b = (om_p, om_s)
            outs.setdefault('gla', []).append(st_p)
            outs.setdefault('ckv', []).append(ckvn[:tp].reshape(bp, n_p, -1))
            outs.setdefault('kpe', []).append(z[:tp, AB_MISC:AB_MISC + MLA_ROPE].astype(F32).reshape(bp, n_p, -1))
        else:
            z = win_project(ys, mod_l, norm1[l], w_in_cd[i].astype(BF16), tp, n_s)
            s0p = jnp.zeros((bp, 2, RET_HEADS * RET_DK, RET_DV), F32)
            s0s = state_ret[:, i].reshape(bs, 2, RET_HEADS * RET_DK, RET_DV)
            or_p, st_p = retention_mixer(z, ret_log_decay[i], ret_gain[i], s0p, cos_id, sin_id, 0, bp, n_p)
            or_s, _ = retention_mixer(z, ret_log_decay[i], ret_gain[i], s0s, cos_s, sin_s, tp, bs, n_s)
            st_p = st_p.reshape(bp, 2, RET_HEADS, RET_DK, RET_DV)
            o_a = (or_p, or_s)
            on_p, kn, vn = na_prompt_attention(z, bp, n_p)
            on_s = na_latent_attention(z, cache_na_k[:, i], cache_na_v[:, i], na_bias[i], tp, bs, n_s)
            o_b = (on_p, on_s)
            outs.setdefault('ret', []).append(st_p)
            outs.setdefault('nak', []).append(kn)
            outs.setdefault('nav', []).append(vn)
        y1, h2, logits = wout_project(o_a, o_b, ys, mod_l, w_out_bf[l], norm2[l], rw_split, tp, n_s)
        y = moe_layer(y1, h2, logits, mod_l, router_bias, exp_w1, exp_w3, exp_w2, l, final_norm, tp, n_s, final)
        ys = list(y) if final else [y]

    return (ys[0].reshape(bp, n_p, d), ys[1].reshape(bs, n_s, d),
            jnp.stack(outs['gla'], axis=1), jnp.stack(outs['ckv'], axis=1), jnp.stack(outs['kpe'], axis=1),
            jnp.stack(outs['ret'], axis=1), jnp.stack(outs['nak'], axis=1), jnp.stack(outs['nav'], axis=1))
```

```python
import functools

import numpy as np
import jax
import jax.numpy as jnp
from jax import lax
from jax.experimental import pallas as pl
from jax.experimental.pallas import tpu as pltpu

F32 = jnp.float32
BF16 = jnp.bfloat16

EPS = 1e-6
GRID_W = 64
ROPE_BASE = 10000.0
GATE_RANK = 16
GATE_TAU = 16.0
GLA_HEADS, GLA_DK, GLA_DV = 8, 64, 128
MLA_HEADS, MLA_Q_LORA, MLA_KV_LORA, MLA_NOPE, MLA_ROPE, MLA_DV = 8, 512, 256, 128, 64, 128
RET_HEADS, RET_DK, RET_DV = 8, 64, 128
NA_HEADS, NA_HD, NA_ROWS, NA_COLS = 8, 128, 8, 16
N_EXPERTS, N_GROUPS, TOP_K, D_EXPERT = 32, 8, 2, 512
EXPERTS_PER_GROUP = N_EXPERTS // N_GROUPS
MOE_BLOCK = 128
LANES = 128

AB_GQ, AB_GK, AB_GV, AB_GR, AB_CQ, AB_CKV, AB_MISC, AB_WIDTH = 0, 512, 1024, 2048, 3072, 3584, 3840, 4096
CD_RQ, CD_RK, CD_RV, CD_RG, CD_NQ, CD_NK, CD_NV, CD_WIDTH = 0, 512, 1024, 2048, 3072, 4096, 5120, 6144

VMEM_LIMIT = 56 * 1024 * 1024


def _cparams(sem):
    return pltpu.CompilerParams(dimension_semantics=sem, vmem_limit_bytes=VMEM_LIMIT)


def _dot(a, b):
    return jnp.dot(a, b, preferred_element_type=F32)


def _dot_nt(a, b):
    return lax.dot_general(a, b, (((1,), (1,)), ((), ())), preferred_element_type=F32)


def _adaln_kernel(c_ref, w_ref, b_ref, o_ref):
    c = c_ref[...]
    s = (c * jax.nn.sigmoid(c)).astype(BF16)
    o_ref[...] = _dot(s, w_ref[...].astype(BF16)) + b_ref[...]


def adaln_all(cond8, ada_w, ada_b, tn=512):
    depth, d, n6 = ada_w.shape
    return pl.pallas_call(
        _adaln_kernel,
        grid=(depth, n6 // tn),
        in_specs=[pl.BlockSpec((8, d), lambda l, j: (0, 0)),
                  pl.BlockSpec((None, d, tn), lambda l, j: (l, 0, j)),
                  pl.BlockSpec((None, 1, tn), lambda l, j: (l, 0, j))],
        out_specs=pl.BlockSpec((None, 8, tn), lambda l, j: (l, 0, j)),
        out_shape=jax.ShapeDtypeStruct((depth, 8, n6), F32),
        compiler_params=_cparams(("parallel", "parallel")),
        name="adaln",
    )(cond8, ada_w, ada_b.reshape(depth, 1, n6))


def _mod_index(i, tm, n_prompt_rows, sample_rows):
    tp = n_prompt_rows // tm
    return jnp.where(i < tp, 0, 1 + (i - tp) // (sample_rows // tm))


def _split_specs(block, tiles_prompt, n_grid_axes=1):
    if n_grid_axes == 1:
        return [pl.BlockSpec(block, lambda i: (jnp.minimum(i, tiles_prompt - 1), 0)),
                pl.BlockSpec(block, lambda i: (jnp.maximum(i - tiles_prompt, 0), 0))]
    return [pl.BlockSpec(block, lambda i, j: (jnp.minimum(i, tiles_prompt - 1), 0)),
            pl.BlockSpec(block, lambda i, j: (jnp.maximum(i - tiles_prompt, 0), 0))]


def _win_kernel(*refs, d, tiles_prompt, split):
    if split:
        xp_ref, xs_ref, mod_ref, g_ref, w_ref, o_ref, h_ref = refs
    else:
        xp_ref, mod_ref, g_ref, w_ref, o_ref, h_ref = refs

    def norm_mod(x_ref):
        x = x_ref[...]
        xn = x * lax.rsqrt(jnp.mean(x * x, axis=-1, keepdims=True) + EPS) * g_ref[...]
        h = xn * (1.0 + mod_ref[:, d:2 * d]) + mod_ref[:, 0:d]
        h_ref[...] = h.astype(BF16)

    first = pl.program_id(1) == 0
    if split:
        is_prompt = pl.program_id(0) < tiles_prompt
        pl.when(first & is_prompt)(lambda: norm_mod(xp_ref))
        pl.when(first & jnp.logical_not(is_prompt))(lambda: norm_mod(xs_ref))
    else:
        pl.when(first)(lambda: norm_mod(xp_ref))

    o_ref[...] = _dot(h_ref[...], w_ref[...]).astype(o_ref.dtype)


def win_project(xs, mod_l, gain, w_bf16, n_prompt_rows, sample_rows, tm=1024, tn=512):
    split = len(xs) == 2
    d = xs[0].shape[1]
    t = sum(x.shape[0] for x in xs)
    width = w_bf16.shape[1]
    tiles_prompt = n_prompt_rows // tm
    midx = functools.partial(_mod_index, tm=tm, n_prompt_rows=n_prompt_rows, sample_rows=sample_rows)
    x_specs = (_split_specs((tm, d), tiles_prompt, 2) if split else [pl.BlockSpec((tm, d), lambda i, j: (i, 0))])
    return pl.pallas_call(
        functools.partial(_win_kernel, d=d, tiles_prompt=tiles_prompt, split=split),
        grid=(t // tm, width // tn),
        in_specs=x_specs + [pl.BlockSpec((None, 1, 6 * d), lambda i, j: (midx(i), 0, 0)),
                            pl.BlockSpec((1, d), lambda i, j: (0, 0)),
                            pl.BlockSpec((d, tn), lambda i, j: (0, j))],
        out_specs=pl.BlockSpec((tm, tn), lambda i, j: (i, j)),
        out_shape=jax.ShapeDtypeStruct((t, width), BF16),
        scratch_shapes=[pltpu.VMEM((tm, d), BF16)],
        compiler_params=_cparams(("parallel", "arbitrary")),
        name="win_project",
    )(*xs, mod_l, gain.reshape(1, d), w_bf16)


def _wout_kernel(*refs, d, tiles_prompt, split_y):
    if split_y:
        (oap_ref, oas_ref, obp_ref, obs_ref, yp_ref, ys_ref, mod_ref, wa_ref, wb_ref, g_ref, rw_ref,
         ynew_ref, h2_ref, lg_ref) = refs
    else:
        (oap_ref, oas_ref, obp_ref, obs_ref, yp_ref, mod_ref, wa_ref, wb_ref, g_ref, rw_ref,
         ynew_ref, h2_ref, lg_ref) = refs
        ys_ref = yp_ref
    is_prompt = pl.program_id(0) < tiles_prompt
    oa = jnp.where(is_prompt, oap_ref[...], oas_ref[...])
    ob = jnp.where(is_prompt, obp_ref[...], obs_ref[...])
    y = jnp.where(is_prompt, yp_ref[...], ys_ref[...]) if split_y else yp_ref[...]
    out = _dot(oa, wa_ref[...]) + _dot(ob, wb_ref[...])
    y1 = y + mod_ref[:, 2 * d:3 * d] * out
    ynew_ref[...] = y1
    xn = y1 * lax.rsqrt(jnp.mean(y1 * y1, axis=-1, keepdims=True) + EPS) * g_ref[...]
    h = xn * (1.0 + mod_ref[:, 4 * d:5 * d]) + mod_ref[:, 3 * d:4 * d]
    h2_ref[...] = h
    hh = h.astype(BF16)
    hl = (h - hh.astype(F32)).astype(BF16)
    lg_ref[...] = _dot(hh, rw_ref[0]) + _dot(hl, rw_ref[0]) + _dot(hh, rw_ref[1])


def wout_project(o_a, o_b, ys, mod_l, w_out_bf16, gain2, rw_split, n_prompt_rows, sample_rows, tm=256):
    split_y = len(ys) == 2
    d = ys[0].shape[1]
    t = sum(x.shape[0] for x in ys)
    ka, kb = o_a[0].shape[1], o_b[0].shape[1]
    tiles_prompt = n_prompt_rows // tm
    midx = functools.partial(_mod_index, tm=tm, n_prompt_rows=n_prompt_rows, sample_rows=sample_rows)
    y_specs = _split_specs((tm, d), tiles_prompt) if split_y else [pl.BlockSpec((tm, d), lambda i: (i, 0))]
    return pl.pallas_call(
        functools.partial(_wout_kernel, d=d, tiles_prompt=tiles_prompt, split_y=split_y),
        grid=(t // tm,),
        in_specs=_split_specs((tm, ka), tiles_prompt) + _split_specs((tm, kb), tiles_prompt) + y_specs + [
                  pl.BlockSpec((None, 1, 6 * d), lambda i: (midx(i), 0, 0)),
                  pl.BlockSpec((ka, d), lambda i: (0, 0)),
                  pl.BlockSpec((kb, d), lambda i: (1, 0)),
                  pl.BlockSpec((1, d), lambda i: (0, 0)),
                  pl.BlockSpec((2, d, LANES), lambda i: (0, 0, 0))],
        out_specs=[pl.BlockSpec((tm, d), lambda i: (i, 0)),
                   pl.BlockSpec((tm, d), lambda i: (i, 0)),
                   pl.BlockSpec((tm, LANES), lambda i: (i, 0))],
        out_shape=[jax.ShapeDtypeStruct((t, d), F32),
                   jax.ShapeDtypeStruct((t, d), F32),
                   jax.ShapeDtypeStruct((t, LANES), F32)],
        compiler_params=_cparams(("parallel",)),
        name="wout_project",
    )(*o_a, *o_b, *ys, mod_l, w_out_bf16, w_out_bf16, gain2.reshape(1, d), rw_split)


def _member_major(x_experts_last):
    shp = x_experts_last.shape[:-1]
    return x_experts_last.reshape(shp + (N_GROUPS, EXPERTS_PER_GROUP)).swapaxes(-1, -2).reshape(shp + (N_EXPERTS,))


def _route_kernel(lg_ref, bias_ref, info_ref, cnt_ref, run_ref, *, tm):
    @pl.when(pl.program_id(0) == 0)
    def _():
        run_ref[...] = jnp.zeros_like(run_ref)

    ng, nk = N_GROUPS, EXPERTS_PER_GROUP
    neg = jnp.float32(-jnp.inf)
    lgt = lg_ref[...].T
    aff = [jax.nn.sigmoid(lgt[k * ng:(k + 1) * ng, :]) for k in range(nk)]
    sel = [aff[k] + bias_ref[k * ng:(k + 1) * ng, :] for k in range(nk)]

    def first_max(vals):
        m = vals[0]
        for v in vals[1:]:
            m = jnp.maximum(m, v)
        idx = jnp.full(m.shape, float(nk), F32)
        for k in reversed(range(nk)):
            idx = jnp.where(vals[k] == m, float(k), idx)
        return m, idx

    m1, i1 = first_max(sel)
    m2, i2 = first_max([jnp.where(i1 == k, neg, sel[k]) for k in range(nk)])
    gscore = m1 + m2
    gid = lax.broadcasted_iota(jnp.int32, gscore.shape, 0).astype(F32)
    gmax = jnp.max(gscore, axis=0, keepdims=True)
    g = jnp.min(jnp.where(gscore == gmax, gid, 99.0), axis=0, keepdims=True)
    in_g = gid == g

    def pick(masks, vals):
        tot = jnp.zeros((1, tm), F32)
        for msk, v in zip(masks, vals):
            tot = tot + jnp.sum(jnp.where(msk, v, 0.0), axis=0, keepdims=True)
        return tot

    hit1 = [in_g & (i1 == k) for k in range(nk)]
    hit2 = [in_g & (i2 == k) for k in range(nk)]
    a1, a2 = pick(hit1, aff), pick(hit2, aff)
    den = a1 + a2
    e1 = g * nk + pick([in_g], [i1])
    e2 = g * nk + pick([in_g], [i2])
    oh = jnp.concatenate([jnp.where(hit1[k] | hit2[k], 1.0, 0.0) for k in range(nk)], axis=0)
    r_i = lax.broadcasted_iota(jnp.int32, (tm, tm), 0)
    c_i = lax.broadcasted_iota(jnp.int32, (tm, tm), 1)
    earlier = jnp.where(r_i < c_i, 1.0, 0.0).astype(BF16)
    cum = _dot(oh.astype(BF16), earlier) + run_ref[:, 0:1]
    cums = [cum[k * ng:(k + 1) * ng, :] for k in range(nk)]
    r1, r2 = pick(hit1, cums), pick(hit2, cums)
    run_new = run_ref[:, 0:1] + jnp.sum(oh, axis=1, keepdims=True)
    run_ref[...] = jnp.broadcast_to(run_new, run_ref.shape)
    cnt_ref[...] = jnp.broadcast_to(run_new, cnt_ref.shape)
    info_ref[...] = jnp.concatenate([e1, e2, a1 / den, a2 / den, r1, r2, jnp.zeros((2, tm), F32)], axis=0)


def route(logits, router_bias, tm=256):
    t = logits.shape[0]
    bias = jnp.broadcast_to(_member_major(router_bias.astype(F32))[:, None], (N_EXPERTS, tm))
    return pl.pallas_call(
        functools.partial(_route_kernel, tm=tm),
        grid=(t // tm,),
        in_specs=[pl.BlockSpec((tm, LANES), lambda i: (i, 0)),
                  pl.BlockSpec((N_EXPERTS, tm), lambda i: (0, 0))],
        out_specs=[pl.BlockSpec((8, tm), lambda i: (0, i)),
                   pl.BlockSpec((N_EXPERTS, LANES), lambda i: (0, 0))],
        out_shape=[jax.ShapeDtypeStruct((8, t), F32),
                   jax.ShapeDtypeStruct((N_EXPERTS, LANES), F32)],
        scratch_shapes=[pltpu.VMEM((N_EXPERTS, LANES), F32)],
        compiler_params=_cparams(("arbitrary",)),
        name="route",
    )(logits, bias)


MOE_AHEAD = 7
MOE_SLOTS = MOE_AHEAD + 1


def _expert_kernel(be_ref, nu_ref, tok_head_ref, tok_ahead_ref, h_hbm, w1_ref, w3_ref, w2_ref, o_ref,
                   xbuf, xcur, w1b, w3b, w2b, sem):
    i = pl.program_id(0)
    slot = i % MOE_SLOTS
    nu = nu_ref[0]

    def gather(idx_ref, base, s):
        for j in range(MOE_BLOCK):
            pltpu.make_async_copy(h_hbm.at[pl.ds(idx_ref[base + j], 1)], xbuf.at[s, pl.ds(j, 1)], sem.at[s]).start()

    def wait_block(s):
        pltpu.make_async_copy(h_hbm.at[pl.ds(0, MOE_BLOCK)], xbuf.at[s], sem.at[s]).wait()

    @pl.when(i == 0)
    def _():
        for blk in range(MOE_AHEAD):
            gather(tok_head_ref, blk * MOE_BLOCK, blk)

    @pl.when(i < nu)
    def _():
        @pl.when((i == 0) | (be_ref[i] != be_ref[jnp.maximum(i - 1, 0)]))
        def _():
            w1b[...] = w1_ref[...].astype(BF16)
            w3b[...] = w3_ref[...].astype(BF16)
            w2b[...] = w2_ref[...].astype(BF16)

        wait_block(slot)
        xcur[...] = xbuf[slot].astype(BF16)
        gather(tok_ahead_ref, 0, (i + MOE_AHEAD) % MOE_SLOTS)
        x = xcur[...]
        a = _dot(x, w1b[...])
        b = _dot(x, w3b[...])
        mid = (a * jax.nn.sigmoid(a)) * b
        o_ref[...] = _dot(mid.astype(BF16), w2b[...])

    @pl.when(i >= nu)
    def _():
        @pl.when(i < nu + MOE_AHEAD)
        def _():
            wait_block(slot)

        o_ref[...] = jnp.zeros_like(o_ref)


def expert_ffn(h2, slot_tok, block_expert, n_used, w1, w3, w2, layer):
    n_steps = slot_tok.shape[0] // MOE_BLOCK
    d, de = w1.shape[-2], w1.shape[-1]
    grid_spec = pltpu.PrefetchScalarGridSpec(
        num_scalar_prefetch=2,
        grid=(n_steps,),
        in_specs=[pl.BlockSpec((MOE_SLOTS * MOE_BLOCK,), lambda i, be, nu: (0,), memory_space=pltpu.SMEM),
                  pl.BlockSpec((MOE_BLOCK,), lambda i, be, nu: (jnp.minimum(i + MOE_AHEAD, n_steps - 1),),
                               memory_space=pltpu.SMEM),
                  pl.BlockSpec(memory_space=pl.ANY),
                  pl.BlockSpec((None, None, d, de), lambda i, be, nu: (layer, be[i], 0, 0)),
                  pl.BlockSpec((None, None, d, de), lambda i, be, nu: (layer, be[i], 0, 0)),
                  pl.BlockSpec((None, None, de, d), lambda i, be, nu: (layer, be[i], 0, 0))],
        out_specs=pl.BlockSpec((MOE_BLOCK, d), lambda i, be, nu: (i, 0)),
        scratch_shapes=[pltpu.VMEM((MOE_SLOTS, MOE_BLOCK, d), F32), pltpu.VMEM((MOE_BLOCK, d), BF16),
                        pltpu.VMEM((d, de), BF16), pltpu.VMEM((d, de), BF16), pltpu.VMEM((de, d), BF16),
                        pltpu.SemaphoreType.DMA((MOE_SLOTS,))],
    )
    return pl.pallas_call(
        _expert_kernel,
        grid_spec=grid_spec,
        out_shape=jax.ShapeDtypeStruct((n_steps * MOE_BLOCK, d), F32),
        compiler_params=_cparams(("arbitrary",)),
        name="expert_ffn",
    )(block_expert, n_used, slot_tok, slot_tok, h2, w1, w3, w2)


def _combine_kernel(dcur_ref, dnext_ref, y_ref, mod_ref, info_ref, g_ref, yb_hbm, *rest, tm, nt, d, final,
                    tiles_prompt):
    if final:
        op_ref, os_ref, buf, sem = rest
    else:
        o_ref, buf, sem = rest
    i = pl.program_id(0)
    slot = i % 2

    def row_copy(src_row, s, k, j):
        return pltpu.make_async_copy(yb_hbm.at[pl.ds(src_row, 1)], buf.at[s, k, pl.ds(j, 1)], sem.at[s])

    def issue(dref, s):
        def body(j, carry):
            for k in range(TOP_K):
                row_copy(dref[TOP_K * j + k], s, k, j).start()
            return carry

        lax.fori_loop(0, tm, body, 0, unroll=32)

    @pl.when(i == 0)
    def _():
        issue(dcur_ref, 0)

    @pl.when(i + 1 < nt)
    def _():
        issue(dnext_ref, 1 - slot)

    for k in range(TOP_K):
        pltpu.make_async_copy(yb_hbm.at[pl.ds(0, tm)], buf.at[slot, k], sem.at[slot]).wait()
    gates = info_ref[...]
    moe = gates[:, 0:1] * buf[slot, 0] + gates[:, 1:2] * buf[slot, 1]
    y2 = y_ref[...] + mod_ref[:, 5 * d:6 * d] * moe
    if not final:
        o_ref[...] = y2
        return
    y2 = y2 * lax.rsqrt(jnp.mean(y2 * y2, axis=-1, keepdims=True) + EPS) * g_ref[...]

    @pl.when(i < tiles_prompt)
    def _():
        op_ref[...] = y2

    @pl.when(i >= tiles_prompt)
    def _():
        os_ref[...] = y2


def combine(y1, mod_l, info, dest_flat, yb, final_gain, n_prompt_rows, sample_rows, final, tm=256):
    t, d = y1.shape
    nt = t // tm
    tiles_prompt = n_prompt_rows // tm
    midx = functools.partial(_mod_index, tm=tm, n_prompt_rows=n_prompt_rows, sample_rows=sample_rows)
    if final:
        out_specs = _split_specs((tm, d), tiles_prompt)
        out_shape = [jax.ShapeDtypeStruct((n_prompt_rows, d), F32), jax.ShapeDtypeStruct((t - n_prompt_rows, d), F32)]
    else:
        out_specs = pl.BlockSpec((tm, d), lambda i: (i, 0))
        out_shape = jax.ShapeDtypeStruct((t, d), F32)
    return pl.pallas_call(
        functools.partial(_combine_kernel, tm=tm, nt=nt, d=d, final=final, tiles_prompt=tiles_prompt),
        grid=(nt,),
        in_specs=[pl.BlockSpec((TOP_K * tm,), lambda i: (i,), memory_space=pltpu.SMEM),
                  pl.BlockSpec((TOP_K * tm,), lambda i: (jnp.minimum(i + 1, nt - 1),), memory_space=pltpu.SMEM),
                  pl.BlockSpec((tm, d), lambda i: (i, 0)),
                  pl.BlockSpec((None, 1, 6 * d), lambda i: (midx(i), 0, 0)),
                  pl.BlockSpec((tm, TOP_K), lambda i: (i, 0)),
                  pl.BlockSpec((1, d), lambda i: (0, 0)),
                  pl.BlockSpec(memory_space=pl.ANY)],
        out_specs=out_specs,
        out_shape=out_shape,
        scratch_shapes=[pltpu.VMEM((2, TOP_K, tm, d), F32), pltpu.SemaphoreType.DMA((2,))],
        compiler_params=_cparams(("arbitrary",)),
        name="combine",
    )(dest_flat, dest_flat, y1, mod_l, info, final_gain.reshape(1, d), yb)


def moe_layer(y1, h2p, logits, mod_l, router_bias, w1, w3, w2, layer, final_gain, n_prompt_rows, sample_rows, final):
    t, d = y1.shape
    info_t, cnt = route(logits, router_bias)
    expert = info_t[0:2].T.astype(jnp.int32)
    rank = info_t[4:6].T.astype(jnp.int32)
    info = info_t[2:4].T
    counts = cnt[:, 0].reshape(EXPERTS_PER_GROUP, N_GROUPS).T.reshape(-1).astype(jnp.int32)
    padded = (counts + MOE_BLOCK - 1) // MOE_BLOCK * MOE_BLOCK
    padded_end = jnp.cumsum(padded)
    padded_start = padded_end - padded
    dest = (padded_start[expert] + rank).reshape(-1)
    n_blocks = (t * TOP_K) // MOE_BLOCK + N_EXPERTS + MOE_AHEAD
    slots = n_blocks * MOE_BLOCK
    block_lo = jnp.arange(n_blocks, dtype=jnp.int32)[:, None] * MOE_BLOCK
    block_expert = jnp.minimum(jnp.sum((padded_end[None, :] <= block_lo).astype(jnp.int32), axis=1), N_EXPERTS - 1)
    n_used = (padded_end[-1:] // MOE_BLOCK).astype(jnp.int32)
    tok = jnp.broadcast_to(jnp.arange(t, dtype=jnp.int32)[:, None], (t, TOP_K)).reshape(-1)
    slot_tok = jnp.zeros((slots,), jnp.int32).at[dest].set(tok)
    yb = expert_ffn(h2p, slot_tok, block_expert, n_used, w1, w3, w2, layer)
    return combine(y1, mod_l, info, dest, yb, final_gain, n_prompt_rows, sample_rows, final)


def rope_tables(n, lead):
    quarter = MLA_ROPE // 4
    tpos = np.arange(n)
    row = (tpos // GRID_W).astype(np.float32)
    col = (tpos % GRID_W).astype(np.float32)
    inv_freq = (np.float32(ROPE_BASE) ** (-np.arange(quarter, dtype=np.float32) / np.float32(quarter))).astype(np.float32)
    lane = np.arange(64)
    pos = np.where(lane[None, :] < 32, row[:, None], col[:, None]).astype(np.float32)
    ang = pos * inv_freq[lane % quarter][None, :]
    cos = np.cos(ang).astype(np.float32)
    sin = np.sin(ang).astype(np.float32)
    sgn = np.where((lane % 32) < quarter, -1.0, 1.0).astype(np.float32)
    sin = sin * sgn[None, :]
    cos = np.concatenate([cos, cos], axis=1)
    sin = np.concatenate([sin, sin], axis=1)
    cos = np.concatenate([np.ones((lead, LANES), np.float32), cos], axis=0)
    sin = np.concatenate([np.zeros((lead, LANES), np.float32), sin], axis=0)
    return jnp.asarray(cos), jnp.asarray(sin)


def _rope(x, cos, sin_signed):
    lane = lax.broadcasted_iota(jnp.int32, x.shape, 1)
    partner = jnp.where((lane & 31) < 16, pltpu.roll(x, LANES - 16, axis=1), pltpu.roll(x, 16, axis=1))
    return x * cos + partner * sin_signed


def _mla_proj_kernel(cq_ref, ckv_ref, misc_ref, qg_ref, kvg_ref, wq_ref, wkv_ref, cos_ref, sin_ref,
                     q_ref, kv_ref, ckvn_ref, kpe_ref, *, scale):
    cq = cq_ref[...].astype(F32)
    qn = cq * lax.rsqrt(jnp.mean(cq * cq, axis=-1, keepdims=True) + EPS) * qg_ref[...]
    qf = _dot(qn.astype(BF16), wq_ref[...])
    half = qf.shape[1] // 2
    cos, sin = cos_ref[...], sin_ref[...]
    q_ref[:, 0:half] = (qf[:, 0:half] * scale).astype(BF16)
    for h in range(MLA_HEADS):
        pe = qf[:, half + h * LANES: half + (h + 1) * LANES]
        q_ref[:, half + h * LANES: half + (h + 1) * LANES] = (_rope(pe, cos, sin) * scale).astype(BF16)
    ckv = ckv_ref[...].astype(F32)
    ckvn = ckv * lax.rsqrt(jnp.mean(ckv * ckv, axis=-1, keepdims=True) + EPS) * kvg_ref[...]
    ckvn_ref[...] = ckvn
    kv_ref[...] = _dot(ckvn.astype(BF16), wkv_ref[...]).astype(BF16)
    misc = misc_ref[...].astype(F32)
    lane = lax.broadcasted_iota(jnp.int32, misc.shape, 1)
    kpe = jnp.where(lane < MLA_ROPE, misc, 0.0)
    kpe_ref[...] = _rope(kpe, cos, sin).astype(BF16)


def mla_project(z, q_gain, kv_gain, wq, wkv, cos_t, sin_t, n_prompt_rows, sample_rows, tm=512):
    t = z.shape[0]
    tp = n_prompt_rows // tm
    per = sample_rows // tm

    def tab(i):
        return (jnp.where(i < tp, 0, 1 + (i - tp) % per), 0)

    scale = float((MLA_NOPE + MLA_ROPE) ** -0.5)
    nq = wq.shape[1]
    return pl.pallas_call(
        functools.partial(_mla_proj_kernel, scale=scale),
        grid=(t // tm,),
        in_specs=[pl.BlockSpec((tm, MLA_Q_LORA), lambda i: (i, AB_CQ // MLA_Q_LORA)),
                  pl.BlockSpec((tm, MLA_KV_LORA), lambda i: (i, AB_CKV // MLA_KV_LORA)),
                  pl.BlockSpec((tm, LANES), lambda i: (i, AB_MISC // LANES)),
                  pl.BlockSpec((1, MLA_Q_LORA), lambda i: (0, 0)),
                  pl.BlockSpec((1, MLA_KV_LORA), lambda i: (0, 0)),
                  pl.BlockSpec(wq.shape, lambda i: (0, 0)),
                  pl.BlockSpec(wkv.shape, lambda i: (0, 0)),
                  pl.BlockSpec((tm, LANES), tab),
                  pl.BlockSpec((tm, LANES), tab)],
        out_specs=[pl.BlockSpec((tm, nq), lambda i: (i, 0)),
                   pl.BlockSpec((tm, wkv.shape[1]), lambda i: (i, 0)),
                   pl.BlockSpec((tm, MLA_KV_LORA), lambda i: (i, 0)),
                   pl.BlockSpec((tm, LANES), lambda i: (i, 0))],
        out_shape=[jax.ShapeDtypeStruct((t, nq), BF16),
                   jax.ShapeDtypeStruct((t, wkv.shape[1]), BF16),
                   jax.ShapeDtypeStruct((t, MLA_KV_LORA), F32),
                   jax.ShapeDtypeStruct((t, LANES), BF16)],
        compiler_params=_cparams(("parallel",)),
        name="mla_project",
    )(z, z, z, q_gain.reshape(1, -1), kv_gain.reshape(1, -1), wq, wkv, cos_t, sin_t)


def _ctx_kv_kernel(ckv_ref, w_ref, o_ref):
    o_ref[...] = _dot(ckv_ref[...].astype(BF16), w_ref[...]).astype(BF16)


def ctx_kv_project(ckv_c, wkv):
    rows = ckv_c.shape[0]
    return pl.pallas_call(
        _ctx_kv_kernel,
        grid=(1,),
        in_specs=[pl.BlockSpec(ckv_c.shape, lambda i: (0, 0)), pl.BlockSpec(wkv.shape, lambda i: (0, 0))],
        out_specs=pl.BlockSpec((rows, wkv.shape[1]), lambda i: (0, 0)),
        out_shape=jax.ShapeDtypeStruct((rows, wkv.shape[1]), BF16),
        compiler_params=_cparams(("arbitrary",)),
        name="ctx_kv_project",
    )(ckv_c, wkv)


def _softmax_pv(s, v):
    m = jnp.max(s, axis=-1, keepdims=True)
    p = jnp.exp(s - m)
    l = jnp.sum(p, axis=-1, keepdims=True)
    return _dot(p.astype(BF16), v) / l


def _mla_prompt_attn_kernel(q_ref, kv_ref, kp_ref, o_ref):
    nh = MLA_HEADS
    kp = kp_ref[...]
    for h in range(nh):
        q = jnp.concatenate([q_ref[:, h * LANES:(h + 1) * LANES], q_ref[:, (nh + h) * LANES:(nh + h + 1) * LANES]], axis=1)
        k = jnp.concatenate([kv_ref[:, h * LANES:(h + 1) * LANES], kp], axis=1)
        v = kv_ref[:, (nh + h) * LANES:(nh + h + 1) * LANES]
        o_ref[:, h * LANES:(h + 1) * LANES] = _softmax_pv(_dot_nt(q, k), v).astype(o_ref.dtype)


def mla_prompt_attention(q_m, kv, kpe, batch, n):
    h = MLA_HEADS
    return pl.pallas_call(
        _mla_prompt_attn_kernel,
        grid=(batch,),
        in_specs=[pl.BlockSpec((n, 2 * h * LANES), lambda b: (b, 0)),
                  pl.BlockSpec((n, 2 * h * LANES), lambda b: (b, 0)),
                  pl.BlockSpec((n, LANES), lambda b: (b, 0))],
        out_specs=pl.BlockSpec((n, h * MLA_DV), lambda b: (b, 0)),
        out_shape=jax.ShapeDtypeStruct((batch * n, h * MLA_DV), BF16),
        compiler_params=_cparams(("parallel",)),
        name="mla_prompt_attention",
    )(q_m, kv, kpe)


def _lane_tile_reduce(x, op):
    r = x[:, 0:LANES]
    for c in range(1, x.shape[1] // LANES):
        r = op(r, x[:, c * LANES:(c + 1) * LANES])
    return r


def _mla_sample_attn_kernel(qn_ref, qp_ref, kcn_ref, kcp_ref, vc_ref, kn_ref, kp_ref, v_ref, o_ref,
                            s_ref, m_ref, acc_ref, *, tk):
    past = kcn_ref.shape[0]
    nk = kn_ref.shape[0] // tk
    q = jnp.concatenate([qn_ref[...], qp_ref[...]], axis=1)
    kc = jnp.concatenate([kcn_ref[...], kcp_ref[...].astype(BF16)], axis=1)
    s = _dot_nt(q, kc)
    s_ref[:, 0:past] = s
    m_ref[...] = _lane_tile_reduce(s, jnp.maximum)

    def scores(j, carry):
        off = pl.multiple_of(j * tk, tk)
        k = jnp.concatenate([kn_ref[pl.ds(off, tk), :], kp_ref[pl.ds(off, tk), :]], axis=1)
        s = _dot_nt(q, k)
        s_ref[:, pl.ds(pl.multiple_of(past + j * tk, LANES), tk)] = s
        m_ref[...] = jnp.maximum(m_ref[...], _lane_tile_reduce(s, jnp.maximum))
        return carry

    lax.fori_loop(0, nk, scores, 0, unroll=2)
    m_ref[...] = jnp.broadcast_to(jnp.max(m_ref[...], axis=-1, keepdims=True), m_ref.shape)

    def weighted_values(s, v):
        m = m_ref[...]
        p = jnp.concatenate([jnp.exp(s[:, c * LANES:(c + 1) * LANES] - m).astype(BF16)
                             for c in range(s.shape[1] // LANES)], axis=1)
        v_ones = jnp.concatenate([v, jnp.ones((v.shape[0], LANES), BF16)], axis=1)
        return _dot(p, v_ones)

    acc_ref[...] = weighted_values(s_ref[:, 0:past], vc_ref[...])

    def weighted(j, carry):
        off = pl.multiple_of(j * tk, tk)
        acc_ref[...] += weighted_values(s_ref[:, pl.ds(pl.multiple_of(past + j * tk, LANES), tk)],
                                        v_ref[pl.ds(off, tk), :])
        return carry

    lax.fori_loop(0, nk, weighted, 0, unroll=2)
    acc = acc_ref[...]
    o_ref[...] = (acc[:, 0:MLA_DV] / acc[:, MLA_DV:]).astype(o_ref.dtype)


def mla_sample_attention(q_m, kv, kpe, kv_ctx, kpe_ctx, row0, batch, n, past, tq=1024, tk=512):
    h = MLA_HEADS
    qb = n // tq
    r0 = row0 // n
    q0 = row0 // tq
    return pl.pallas_call(
        functools.partial(_mla_sample_attn_kernel, tk=tk),
        grid=(batch, h, qb),
        in_specs=[pl.BlockSpec((tq, LANES), lambda b, hh, i: (q0 + b * qb + i, hh)),
                  pl.BlockSpec((tq, LANES), lambda b, hh, i: (q0 + b * qb + i, h + hh)),
                  pl.BlockSpec((past, LANES), lambda b, hh, i: (b, hh)),
                  pl.BlockSpec((past, LANES), lambda b, hh, i: (b, 0)),
                  pl.BlockSpec((past, LANES), lambda b, hh, i: (b, h + hh)),
                  pl.BlockSpec((n, LANES), lambda b, hh, i: (r0 + b, hh)),
                  pl.BlockSpec((n, LANES), lambda b, hh, i: (r0 + b, 0)),
                  pl.BlockSpec((n, LANES), lambda b, hh, i: (r0 + b, h + hh))],
        out_specs=pl.BlockSpec((tq, LANES), lambda b, hh, i: (b * qb + i, hh)),
        out_shape=jax.ShapeDtypeStruct((batch * n, h * MLA_DV), BF16),
        scratch_shapes=[pltpu.VMEM((tq, past + n), F32), pltpu.VMEM((tq, LANES), F32),
                        pltpu.VMEM((tq, MLA_DV + LANES), F32)],
        compiler_params=_cparams(("parallel", "parallel", "arbitrary")),
        name="mla_sample_attention",
    )(q_m, q_m, kv_ctx, kpe_ctx, kv_ctx, kv, kpe, kv)


def _na_prompt_attn_kernel(q_ref, k_ref, v_ref, o_ref, ko_ref, vo_ref, *, scale):
    for h in range(NA_HEADS):
        cols = slice(h * LANES, (h + 1) * LANES)
        k = k_ref[:, cols]
        v = v_ref[:, cols]
        ko_ref[h] = k.astype(F32)
        vo_ref[h] = v.astype(F32)
        q = (q_ref[:, cols].astype(F32) * scale).astype(BF16)
        o_ref[:, cols] = _softmax_pv(_dot_nt(q, k), v).astype(o_ref.dtype)


def na_prompt_attention(z, batch, n):
    h = NA_HEADS
    scale = float(NA_HD ** -0.5)
    w = h * NA_HD
    cq, ck, cv = CD_NQ // w, CD_NK // w, CD_NV // w
    return pl.pallas_call(
        functools.partial(_na_prompt_attn_kernel, scale=scale),
        grid=(batch,),
        in_specs=[pl.BlockSpec((n, w), lambda b: (b, cq)),
                  pl.BlockSpec((n, w), lambda b: (b, ck)),
                  pl.BlockSpec((n, w), lambda b: (b, cv))],
        out_specs=[pl.BlockSpec((n, w), lambda b: (b, 0)),
                   pl.BlockSpec((None, h, n, NA_HD), lambda b: (b, 0, 0, 0)),
                   pl.BlockSpec((None, h, n, NA_HD), lambda b: (b, 0, 0, 0))],
        out_shape=[jax.ShapeDtypeStruct((batch * n, w), BF16),
                   jax.ShapeDtypeStruct((batch, h, n, NA_HD), F32),
                   jax.ShapeDtypeStruct((batch, h, n, NA_HD), F32)],
        compiler_params=_cparams(("parallel",)),
        name="na_prompt_attention",
    )(z, z, z)


def _dot_tn(a, b):
    return lax.dot_general(a, b, (((0,), (0,)), ((), ())), preferred_element_type=F32)


def _head_norm_gate(acc, gate_in, gain):
    outs = []
    for h in range(2):
        a = acc[:, h * LANES:(h + 1) * LANES]
        outs.append(a * lax.rsqrt(jnp.mean(a * a, axis=-1, keepdims=True) + EPS))
    g = gate_in
    return jnp.concatenate(outs, axis=1) * gain * (g * jax.nn.sigmoid(g))


RET_CHUNK = 256
FIN_ROWS = 256


def _ret_kernel(ld_ref, q_ref, k_ref, v_ref, g_ref, cos_ref, sin_ref, gain_ref, s0_ref, o_ref, sf_ref, acc_ref, *, n):
    L = RET_CHUNK
    nc = n // L
    hp = pl.program_id(1)
    lane = lax.broadcasted_iota(jnp.int32, (1, LANES), 1)
    lo_lanes = lane < RET_DK
    row128 = lax.broadcasted_iota(jnp.int32, (LANES, 1), 0)
    ri = lax.broadcasted_iota(jnp.int32, (L, L), 0)
    ci = lax.broadcasted_iota(jnp.int32, (L, L), 1)
    dist = (ri - ci).astype(F32)
    pos = lax.broadcasted_iota(jnp.int32, (L, 1), 0).astype(F32)

    consts = []
    for dr in range(2):
        lg0 = ld_ref[dr, 2 * hp]
        lg1 = ld_ref[dr, 2 * hp + 1]
        lg2 = jnp.where(lo_lanes, lg0, lg1)
        lgc = jnp.where(row128 < RET_DK, lg0, lg1)
        sd = dist if dr == 0 else -dist
        ok = sd >= 0
        intra = [jnp.where(ok, jnp.exp(jnp.where(ok, sd, 0.0) * lg), 0.0) for lg in (lg0, lg1)]
        if dr == 0:
            inter = jnp.exp((pos + 1.0) * lg2)
            to_state = jnp.exp((L - 1.0 - pos) * lg2)
        else:
            inter = jnp.exp((L - pos) * lg2)
            to_state = jnp.exp(pos * lg2)
        consts.append((intra, inter, to_state, jnp.exp(float(L) * lgc)))

    def clear(i, carry):
        acc_ref[pl.ds(pl.multiple_of(i * FIN_ROWS, FIN_ROWS), FIN_ROWS), :] = jnp.zeros((FIN_ROWS, 2 * LANES), F32)
        return carry

    lax.fori_loop(0, n // FIN_ROWS, clear, 0)

    def step(c, dr, s2):
        intra, inter, to_state, cdec = consts[dr]
        rows = pl.ds(pl.multiple_of(c * L, L), L)
        cos, sin = cos_ref[rows, :], sin_ref[rows, :]
        q = _rope(q_ref[rows, :].astype(F32) * (RET_DK ** -0.5), cos, sin)
        k = _rope(k_ref[rows, :].astype(F32), cos, sin)
        kb = k.astype(BF16)
        s2b = s2.astype(BF16)
        kt = (k * to_state).astype(BF16)
        upd = []
        for h in range(2):
            qm = jnp.where(lo_lanes if h == 0 else ~lo_lanes, q, 0.0)
            vh = v_ref[rows, h * LANES:(h + 1) * LANES].astype(BF16)
            sc = _dot_nt(qm.astype(BF16), kb) * intra[h]
            o = _dot(sc.astype(BF16), vh) + _dot((qm * inter).astype(BF16), s2b)
            acc_ref[rows, h * LANES:(h + 1) * LANES] += o
            upd.append(_dot_tn(kt, vh))
        return cdec * s2 + jnp.where(row128 < RET_DK, upd[0], upd[1])

    def body(i, carry):
        sf, sb = carry
        return step(i, 0, sf), step(nc - 1 - i, 1, sb)

    sf, sb = lax.fori_loop(0, nc, body, (s0_ref[0], s0_ref[1]))
    sf_ref[0] = sf
    sf_ref[1] = sb

    def fin(i, carry):
        rows = pl.ds(pl.multiple_of(i * FIN_ROWS, FIN_ROWS), FIN_ROWS)
        o_ref[rows, :] = _head_norm_gate(acc_ref[rows, :], g_ref[rows, :].astype(F32), gain_ref[...]).astype(o_ref.dtype)
        return carry

    lax.fori_loop(0, n // FIN_ROWS, fin, 0)


def retention_mixer(z, log_decay, ret_gain, s0, cos_t, sin_t, row0, batch, n):
    hp = RET_HEADS // 2
    r0 = row0 // n
    cq, ck, cv, cg = CD_RQ // LANES, CD_RK // LANES, CD_RV // (2 * LANES), CD_RG // (2 * LANES)
    grid_spec = pltpu.PrefetchScalarGridSpec(
        num_scalar_prefetch=1,
        grid=(batch, hp),
        in_specs=[pl.BlockSpec((n, LANES), lambda b, p, ld: (r0 + b, cq + p)),
                  pl.BlockSpec((n, LANES), lambda b, p, ld: (r0 + b, ck + p)),
                  pl.BlockSpec((n, 2 * LANES), lambda b, p, ld: (r0 + b, cv + p)),
                  pl.BlockSpec((n, 2 * LANES), lambda b, p, ld: (r0 + b, cg + p)),
                  pl.BlockSpec((n, LANES), lambda b, p, ld: (0, 0)),
                  pl.BlockSpec((n, LANES), lambda b, p, ld: (0, 0)),
                  pl.BlockSpec((1, 2 * LANES), lambda b, p, ld: (0, p)),
                  pl.BlockSpec((None, 2, LANES, LANES), lambda b, p, ld: (b, 0, p, 0))],
        out_specs=[pl.BlockSpec((n, 2 * LANES), lambda b, p, ld: (b, p)),
                   pl.BlockSpec((None, 2, LANES, LANES), lambda b, p, ld: (b, 0, p, 0))],
        scratch_shapes=[pltpu.VMEM((n, 2 * LANES), F32)],
    )
    return pl.pallas_call(
        functools.partial(_ret_kernel, n=n),
        grid_spec=grid_spec,
        out_shape=[jax.ShapeDtypeStruct((batch * n, RET_HEADS * RET_DV), BF16),
                   jax.ShapeDtypeStruct((batch, 2, RET_HEADS * RET_DK, RET_DV), F32)],
        compiler_params=_cparams(("parallel", "parallel")),
        name="retention_mixer",
    )(log_decay.astype(F32), z, z, z, z, cos_t, sin_t, ret_gain.reshape(1, -1), s0)


GLA_CHUNK = 256
GLA_SUB = 64
GLA_SAFE_EXPONENT = 60.0


def _gla_kernel(q_ref, k_ref, v_ref, g_ref, misc_ref, gw_ref, gb_ref, gain_ref, s0_ref, o_ref, sf_ref,
                la_ref, acc_ref, *, n):
    C, SB = GLA_CHUNK, GLA_SUB
    nsb = C // SB
    nc = n // C
    lane = lax.broadcasted_iota(jnp.int32, (1, LANES), 1)
    lo_lanes = lane < GLA_DK
    head_mask = (lo_lanes, ~lo_lanes)
    ri = lax.broadcasted_iota(jnp.int32, (C, C), 0)
    ci = lax.broadcasted_iota(jnp.int32, (C, C), 1)
    rowc = lax.broadcasted_iota(jnp.int32, (C, 1), 0)
    SLAB = 16
    row8 = lax.broadcasted_iota(jnp.int32, (SLAB, 1), 0)
    tri = (jnp.where(ci <= ri, 1.0, 0.0).astype(BF16), jnp.where(ci >= ri, 1.0, 0.0).astype(BF16))
    causal = (ci <= ri, ci >= ri)

    def prep(i, carry):
        rows = pl.ds(pl.multiple_of(i * FIN_ROWS, FIN_ROWS), FIN_ROWS)
        slab = misc_ref[rows, :].astype(BF16)
        for dr in range(2):
            pre = _dot(slab, gw_ref[dr]) + gb_ref[dr]
            la_ref[dr, rows, :] = (jnp.minimum(pre, 0.0) - jnp.log1p(jnp.exp(-jnp.abs(pre)))) * (1.0 / GATE_TAU)
        acc_ref[rows, :] = jnp.zeros((FIN_ROWS, 2 * LANES), F32)
        return carry

    lax.fori_loop(0, n // FIN_ROWS, prep, 0)

    def step(c, dr):
        rev = dr == 1
        base = pl.multiple_of(c * C, C)
        rows = pl.ds(base, C)
        la = la_ref[dr, rows, :]
        hi = la.astype(BF16)
        lo = (la - hi.astype(F32)).astype(BF16)
        b = _dot(tri[dr], hi) + _dot(tri[dr], lo)
        zero = jnp.zeros((1, LANES), F32)
        if not rev:
            bref = [zero if i == 0 else b[SB * i - 1:SB * i, :] for i in range(nsb)]
            bend = [b[SB * i + SB - 1:SB * i + SB, :] for i in range(nsb)]
            blast = b[C - 1:C, :]
        else:
            bref = [zero if i == nsb - 1 else b[SB * (i + 1):SB * (i + 1) + 1, :] for i in range(nsb)]
            bend = [b[SB * i:SB * i + 1, :] for i in range(nsb)]
            blast = b[0:1, :]
        spread = bref[0] - bend[0]
        for i in range(1, nsb):
            spread = jnp.maximum(spread, bref[i] - bend[i])
        worst = jnp.max(spread)

        def fast(st):
            q = q_ref[rows, :].astype(F32) * (GLA_DK ** -0.5)
            k = k_ref[rows, :].astype(F32)
            a_rows = ([], [])
            for i in range(nsb):
                qe = q[SB * i:SB * (i + 1), :] * jnp.exp(b[SB * i:SB * (i + 1), :] - bref[i])
                valid = (rowc < SB * (i + 1)) if not rev else (rowc >= SB * i)
                ke = jnp.where(valid, k * jnp.exp(jnp.where(valid, bref[i] - b, 0.0)), 0.0).astype(BF16)
                for h in range(2):
                    a_rows[h].append(_dot_nt(jnp.where(head_mask[h], qe, 0.0).astype(BF16), ke))
            qi = q * jnp.exp(b)
            ks = (k * jnp.exp(blast - b)).astype(BF16)
            stb = st.astype(BF16)
            upd = []
            for h in range(2):
                a = jnp.where(causal[dr], jnp.concatenate(a_rows[h], axis=0), 0.0)
                vh = v_ref[rows, h * LANES:(h + 1) * LANES].astype(BF16)
                o = _dot(a.astype(BF16), vh) + _dot_nt(jnp.where(head_mask[h], qi, 0.0).astype(BF16), stb)
                acc_ref[rows, h * LANES:(h + 1) * LANES] += o
                upd.append(_dot_tn(vh, ks))
            return jnp.exp(blast) * st + jnp.where(lo_lanes, upd[0], upd[1])

        def slow(st):
            def row_step(r, st):
                t = r if not rev else C - 1 - r
                slab = pl.ds(pl.multiple_of(base + (t // SLAB) * SLAB, SLAB), SLAB)
                pick = row8 == (t % SLAB)
                q8 = jnp.where(pick, q_ref[slab, :].astype(F32) * (GLA_DK ** -0.5), 0.0)
                k8 = jnp.where(pick, k_ref[slab, :].astype(F32), 0.0).astype(BF16)
                la_t = jnp.sum(jnp.where(pick, la_ref[dr, slab, :], 0.0), axis=0, keepdims=True)
                upd = []
                for h in range(2):
                    v8 = jnp.where(pick, v_ref[slab, h * LANES:(h + 1) * LANES].astype(F32), 0.0).astype(BF16)
                    upd.append(_dot_tn(v8, k8))
                st = jnp.exp(la_t) * st + jnp.where(lo_lanes, upd[0], upd[1])
                stb = st.astype(BF16)
                for h in range(2):
                    qh = jnp.where(head_mask[h], q8, 0.0).astype(BF16)
                    acc_ref[slab, h * LANES:(h + 1) * LANES] += _dot_nt(qh, stb)
                return st

            return lax.fori_loop(0, C + 0 * pl.program_id(0), row_step, st)

        return worst, fast, slow

    def body(i, carry):
        worst_f, fast_f, slow_f = step(i, 0)
        worst_b, fast_b, slow_b = step(nc - 1 - i, 1)
        return lax.cond(jnp.maximum(worst_f, worst_b) <= GLA_SAFE_EXPONENT,
                        lambda st: (fast_f(st[0]), fast_b(st[1])),
                        lambda st: (slow_f(st[0]), slow_b(st[1])), carry)

    sf, sb = lax.fori_loop(0, nc, body, (s0_ref[0].T, s0_ref[1].T))
    sf_ref[0] = sf.T
    sf_ref[1] = sb.T

    def fin(i, carry):
        rows = pl.ds(pl.multiple_of(i * FIN_ROWS, FIN_ROWS), FIN_ROWS)
        o_ref[rows, :] = _head_norm_gate(acc_ref[rows, :], g_ref[rows, :].astype(F32), gain_ref[...]).astype(o_ref.dtype)
        return carry

    lax.fori_loop(0, n // FIN_ROWS, fin, 0)


def _prep_gate(gate_w, gate_b):
    gw = jnp.zeros((2, LANES, gate_w.shape[-1]), F32)
    for dr in range(2):
        lo = MLA_ROPE + dr * GATE_RANK
        gw = gw.at[dr, lo:lo + GATE_RANK, :].set(gate_w[dr].astype(F32))
    return gw.astype(BF16), gate_b.astype(F32).reshape(2, 1, -1)


def gla_mixer(z, gate_w, gate_b, gla_gain, s0, row0, batch, n):
    hp = GLA_HEADS // 2
    r0 = row0 // n
    gw, gb = _prep_gate(gate_w, gate_b)
    cq, ck, cv, cg, cm = AB_GQ // LANES, AB_GK // LANES, AB_GV // (2 * LANES), AB_GR // (2 * LANES), AB_MISC // LANES
    return pl.pallas_call(
        functools.partial(_gla_kernel, n=n),
        grid=(batch, hp),
        in_specs=[pl.BlockSpec((n, LANES), lambda b, p: (r0 + b, cq + p)),
                  pl.BlockSpec((n, LANES), lambda b, p: (r0 + b, ck + p)),
                  pl.BlockSpec((n, 2 * LANES), lambda b, p: (r0 + b, cv + p)),
                  pl.BlockSpec((n, 2 * LANES), lambda b, p: (r0 + b, cg + p)),
                  pl.BlockSpec((n, LANES), lambda b, p: (r0 + b, cm)),
                  pl.BlockSpec((2, LANES, LANES), lambda b, p: (0, 0, p)),
                  pl.BlockSpec((2, 1, LANES), lambda b, p: (0, 0, p)),
                  pl.BlockSpec((1, 2 * LANES), lambda b, p: (0, p)),
                  pl.BlockSpec((None, 2, LANES, LANES), lambda b, p: (b, 0, p, 0))],
        out_specs=[pl.BlockSpec((n, 2 * LANES), lambda b, p: (b, p)),
                   pl.BlockSpec((None, 2, LANES, LANES), lambda b, p: (b, 0, p, 0))],
        out_shape=[jax.ShapeDtypeStruct((batch * n, GLA_HEADS * GLA_DV), BF16),
                   jax.ShapeDtypeStruct((batch, 2, GLA_HEADS * GLA_DK, GLA_DV), F32)],
        scratch_shapes=[pltpu.VMEM((2, n, LANES), F32), pltpu.VMEM((n, 2 * LANES), F32)],
        compiler_params=_cparams(("parallel", "parallel")),
        name="gla_mixer",
    )(z, z, z, z, z, gw, gb, gla_gain.reshape(1, -1), s0)


NA_QROWS = 8
NA_WROWS = 16
NA_NEG = -1e30
NA_TAB = 32


def _na_bias_table(na_bias_l):
    qc = np.arange(GRID_W)[:, None]
    kc = np.arange(GRID_W)[None, :]
    ws = np.clip(qc - NA_COLS // 2, 0, GRID_W - NA_COLS)
    col_ok = (kc >= ws) & (kc < ws + NA_COLS)
    col_off = np.clip(kc - qc + NA_COLS - 1, 0, 2 * NA_COLS - 2)
    aa = (np.arange(NA_TAB) - NA_QROWS)[:, None] + np.arange(2)[None, :]
    n_row, n_col = 2 * NA_ROWS - 1, 2 * NA_COLS - 1
    sel_row = (aa[:, :, None] == np.arange(n_row)[None, None, :]).astype(np.float32)
    sel_col = ((col_off[:, :, None] == np.arange(n_col)[None, None, :]) & col_ok[:, :, None]).astype(np.float32)
    fill = np.where(col_ok, 0.0, NA_NEG).astype(np.float32)
    tab = jnp.einsum('hac,sea,qkc->hsqek', na_bias_l.astype(F32), jnp.asarray(sel_row), jnp.asarray(sel_col),
                     precision=lax.Precision.HIGHEST)
    tab = tab + jnp.asarray(fill)[None, None, :, None, :]
    h = na_bias_l.shape[0]
    return tab.reshape(h, NA_TAB, GRID_W, 2 * GRID_W)


def _na_latent_kernel(q_ref, k_ref, v_ref, kc_ref, vc_ref, tab_ref, o_ref, s_ref, *, scale, rows_total):
    g = pl.program_id(2)
    r0 = g * NA_QROWS
    start = jnp.clip(r0 - NA_ROWS // 2, 0, rows_total - NA_WROWS)
    nloc = NA_WROWS * GRID_W
    krows = pl.ds(pl.multiple_of(start * GRID_W, GRID_W), nloc)
    q = (q_ref[...].astype(F32) * scale).astype(BF16)
    kw = k_ref[krows, :]
    s_ref[:, 0:nloc] = _dot_nt(q, kw)
    s_ref[:, nloc:] = _dot_nt(q, kc_ref[...].astype(BF16))
    lane = lax.broadcasted_iota(jnp.int32, (1, LANES), 1)
    for qr in range(NA_QROWS):
        r = r0 + qr
        rs = jnp.clip(r - NA_ROWS // 2, 0, rows_total - NA_ROWS)
        for kp in range(NA_WROWS // 2):
            k0 = start + 2 * kp
            ok0 = (k0 >= rs) & (k0 < rs + NA_ROWS)
            ok1 = (k0 + 1 >= rs) & (k0 + 1 < rs + NA_ROWS)
            neg = jnp.where(lane < GRID_W, jnp.where(ok0, 0.0, NA_NEG), jnp.where(ok1, 0.0, NA_NEG))
            slot = k0 - r + (NA_ROWS - 1) + NA_QROWS
            s_ref[qr * GRID_W:(qr + 1) * GRID_W, kp * LANES:(kp + 1) * LANES] += tab_ref[slot] + neg
    s = s_ref[...]
    m = jnp.max(s, axis=-1, keepdims=True)
    p = jnp.exp(s - m)
    l = jnp.sum(p, axis=-1, keepdims=True)
    pb = p.astype(BF16)
    o = _dot(pb[:, 0:nloc], v_ref[krows, :]) + _dot(pb[:, nloc:], vc_ref[...].astype(BF16))
    o_ref[...] = (o / l).astype(o_ref.dtype)


def na_latent_attention(z, k_ctx, v_ctx, na_bias_l, row0, batch, n):
    h = NA_HEADS
    past = k_ctx.shape[2]
    rows_total = n // GRID_W
    tq = NA_QROWS * GRID_W
    qb = n // tq
    r0 = row0 // n
    q0 = row0 // tq
    cq, ck, cv = CD_NQ // LANES, CD_NK // LANES, CD_NV // LANES
    tab = _na_bias_table(na_bias_l)
    return pl.pallas_call(
        functools.partial(_na_latent_kernel, scale=float(NA_HD ** -0.5), rows_total=rows_total),
        grid=(batch, h, qb),
        in_specs=[pl.BlockSpec((tq, LANES), lambda b, hh, g: (q0 + b * qb + g, cq + hh)),
                  pl.BlockSpec((n, LANES), lambda b, hh, g: (r0 + b, ck + hh)),
                  pl.BlockSpec((n, LANES), lambda b, hh, g: (r0 + b, cv + hh)),
                  pl.BlockSpec((None, None, past, NA_HD), lambda b, hh, g: (b, hh, 0, 0)),
                  pl.BlockSpec((None, None, past, NA_HD), lambda b, hh, g: (b, hh, 0, 0)),
                  pl.BlockSpec((None, NA_TAB, GRID_W, 2 * GRID_W), lambda b, hh, g: (hh, 0, 0, 0))],
        out_specs=pl.BlockSpec((tq, LANES), lambda b, hh, g: (b * qb + g, hh)),
        out_shape=jax.ShapeDtypeStruct((batch * n, h * NA_HD), BF16),
        scratch_shapes=[pltpu.VMEM((tq, NA_WROWS * GRID_W + past), F32)],
        compiler_params=_cparams(("parallel", "parallel", "arbitrary")),
        name="na_latent_attention",
    )(z, z, z, k_ctx, v_ctx, tab)


def _prep_w_in_ab(w):
    d = w.shape[0]
    sizes = (512, 512, 1024, 1024, 2 * GATE_RANK, MLA_Q_LORA, MLA_KV_LORA, MLA_ROPE)
    gq, gk, gv, gr, glr, cq, ckv, kpe = jnp.split(w, np.cumsum(sizes)[:-1].tolist(), axis=1)
    pad = jnp.zeros((d, AB_WIDTH - AB_MISC - MLA_ROPE - 2 * GATE_RANK), w.dtype)
    return jnp.concatenate([gq, gk, gv, gr, cq, ckv, kpe, glr, pad], axis=1).astype(BF16)


def _prep_w_uq(w):
    r = w.shape[0]
    w3 = w.reshape(r, MLA_HEADS, MLA_NOPE + MLA_ROPE)
    nope = w3[:, :, :MLA_NOPE].reshape(r, MLA_HEADS * MLA_NOPE)
    pe = jnp.pad(w3[:, :, MLA_NOPE:], ((0, 0), (0, 0), (0, LANES - MLA_ROPE))).reshape(r, MLA_HEADS * LANES)
    return jnp.concatenate([nope, pe], axis=1).astype(BF16)


def _prep_w_ukv(w):
    r = w.shape[0]
    w3 = w.reshape(r, MLA_HEADS, MLA_NOPE + MLA_DV)
    return jnp.concatenate([w3[:, :, :MLA_NOPE].reshape(r, -1), w3[:, :, MLA_NOPE:].reshape(r, -1)], axis=1).astype(BF16)


def _prep_router(router_w):
    w = jnp.pad(_member_major(router_w.astype(F32)), ((0, 0), (0, LANES - N_EXPERTS)))
    hi = w.astype(BF16)
    lo = (w - hi.astype(F32)).astype(BF16)
    return jnp.stack([hi, lo], axis=0)


def kernel(x_prompt, x_sample, state_gla, cache_mla_ckv, cache_mla_kpe, state_ret, cache_na_k, cache_na_v,
           c, c_ctx, ada_w, ada_b, norm1, norm2, w_in_ab, gla_gate_w, gla_gate_b, gla_gain, mla_q_gain,
           mla_w_uq, mla_kv_gain, mla_w_ukv, w_in_cd, ret_log_decay, ret_gain, na_bias, w_out,
           router_w, router_bias, exp_w1, exp_w3, exp_w2, final_norm):
    bp, n_p, d = x_prompt.shape
    bs, n_s, _ = x_sample.shape
    past = cache_mla_ckv.shape[2]
    depth = ada_w.shape[0]
    tp, ts = bp * n_p, bs * n_s
    ys = [x_prompt.reshape(tp, d), x_sample.reshape(ts, d)]

    cond8 = jnp.zeros((8, d), F32).at[0].set(c_ctx).at[1:1 + bs].set(c)
    mod = adaln_all(cond8, ada_w, ada_b)
    rw_split = _prep_router(router_w)
    w_out_bf = w_out.astype(BF16)
    cos_t, sin_t = rope_tables(n_s, 512)
    cos_s, sin_s = cos_t[512:], sin_t[512:]
    cos_id, sin_id = cos_t[:n_p], sin_t[:n_p]

    outs = {}
    for l in range(depth):
        i = l // 2
        mod_l = mod[l].reshape(8, 1, 6 * d)
        final = l == depth - 1
        if l % 2 == 0:
            z = win_project(ys, mod_l, norm1[l], _prep_w_in_ab(w_in_ab[i]), tp, n_s)
            s0p = jnp.zeros((bp, 2, GLA_HEADS * GLA_DK, GLA_DV), F32)
            s0s = state_gla[:, i].reshape(bs, 2, GLA_HEADS * GLA_DK, GLA_DV)
            og_p, st_p = gla_mixer(z, gla_gate_w[i], gla_gate_b[i], gla_gain[i], s0p, 0, bp, n_p)
            og_s, _ = gla_mixer(z, gla_gate_w[i], gla_gate_b[i], gla_gain[i], s0s, tp, bs, n_s)
            st_p = st_p.reshape(bp, 2, GLA_HEADS, GLA_DK, GLA_DV)
            o_a = (og_p, og_s)
            wq, wkv = _prep_w_uq(mla_w_uq[i]), _prep_w_ukv(mla_w_ukv[i])
            q_m, kv, ckvn, kpe = mla_project(z, mla_q_gain[i], mla_kv_gain[i], wq, wkv, cos_t, sin_t, tp, n_s)
            kv_ctx = ctx_kv_project(cache_mla_ckv[:, i].reshape(bs * past, -1), wkv)
            kpe_ctx = jnp.pad(cache_mla_kpe[:, i].reshape(bs * past, -1), ((0, 0), (0, LANES - MLA_ROPE)))
            om_p = mla_prompt_attention(q_m, kv, kpe, bp, n_p)
            om_s = mla_sample_attention(q_m, kv, kpe, kv_ctx, kpe_ctx, tp, bs, n_s, past)
            o_b = (om_p, om_s)
            outs.setdefault('gla', []).append(st_p)
            outs.setdefault('ckv', []).append(ckvn[:tp].reshape(bp, n_p, -1))
            outs.setdefault('kpe', []).append(z[:tp, AB_MISC:AB_MISC + MLA_ROPE].astype(F32).reshape(bp, n_p, -1))
        else:
            z = win_project(ys, mod_l, norm1[l], w_in_cd[i].astype(BF16), tp, n_s)
            s0p = jnp.zeros((bp, 2, RET_HEADS * RET_DK, RET_DV), F32)
            s0s = state_ret[:, i].reshape(bs, 2, RET_HEADS * RET_DK, RET_DV)
            or_p, st_p = retention_mixer(z, ret_log_decay[i], ret_gain[i], s0p, cos_id, sin_id, 0, bp, n_p)
            or_s, _ = retention_mixer(z, ret_log_decay[i], ret_gain[i], s0s, cos_s, sin_s, tp, bs, n_s)
            st_p = st_p.reshape(bp, 2, RET_HEADS, RET_DK, RET_DV)
            o_a = (or_p, or_s)
            on_p, kn, vn = na_prompt_attention(z, bp, n_p)
            on_s = na_latent_attention(z, cache_na_k[:, i], cache_na_v[:, i], na_bias[i], tp, bs, n_s)
            o_b = (on_p, on_s)
            outs.setdefault('ret', []).append(st_p)
            outs.setdefault('nak', []).append(kn)
            outs.setdefault('nav', []).append(vn)
        y1, h2, logits = wout_project(o_a, o_b, ys, mod_l, w_out_bf[l], norm2[l], rw_split, tp, n_s)
        y = moe_layer(y1, h2, logits, mod_l, router_bias, exp_w1, exp_w3, exp_w2, l, final_norm, tp, n_s, final)
        ys = list(y) if final else [y]

    return (ys[0].reshape(bp, n_p, d), ys[1].reshape(bs, n_s, d),
            jnp.stack(outs['gla'], axis=1), jnp.stack(outs['ckv'], axis=1), jnp.stack(outs['kpe'], axis=1),
            jnp.stack(outs['ret'], axis=1), jnp.stack(outs['nak'], axis=1), jnp.stack(outs['nav'], axis=1))
```

```python
import functools

import numpy as np
import jax
import jax.numpy as jnp
from jax import lax
from jax.experimental import pallas as pl
from jax.experimental.pallas import tpu as pltpu

F32 = jnp.float32
BF16 = jnp.bfloat16

EPS = 1e-6
GRID_W = 64
ROPE_BASE = 10000.0
GATE_RANK = 16
GATE_TAU = 16.0
GLA_HEADS, GLA_DK, GLA_DV = 8, 64, 128
MLA_HEADS, MLA_Q_LORA, MLA_KV_LORA, MLA_NOPE, MLA_ROPE, MLA_DV = 8, 512, 256, 128, 64, 128
RET_HEADS, RET_DK, RET_DV = 8, 64, 128
NA_HEADS, NA_HD, NA_ROWS, NA_COLS = 8, 128, 8, 16
N_EXPERTS, N_GROUPS, TOP_K, D_EXPERT = 32, 8, 2, 512
EXPERTS_PER_GROUP = N_EXPERTS // N_GROUPS
MOE_BLOCK = 128
LANES = 128

AB_GQ, AB_GK, AB_GV, AB_GR, AB_CQ, AB_CKV, AB_MISC, AB_WIDTH = 0, 512, 1024, 2048, 3072, 3584, 3840, 4096
CD_RQ, CD_RK, CD_RV, CD_RG, CD_NQ, CD_NK, CD_NV, CD_WIDTH = 0, 512, 1024, 2048, 3072, 4096, 5120, 6144

VMEM_LIMIT = 56 * 1024 * 1024


def _cparams(sem):
    return pltpu.CompilerParams(dimension_semantics=sem, vmem_limit_bytes=VMEM_LIMIT)


def _dot(a, b):
    return jnp.dot(a, b, preferred_element_type=F32)


def _dot_nt(a, b):
    return lax.dot_general(a, b, (((1,), (1,)), ((), ())), preferred_element_type=F32)


def _adaln_kernel(c_ref, w_ref, b_ref, o_ref):
    c = c_ref[...]
    s = (c * jax.nn.sigmoid(c)).astype(BF16)
    o_ref[...] = _dot(s, w_ref[...].astype(BF16)) + b_ref[...]


def adaln_all(cond8, ada_w, ada_b, tn=512):
    depth, d, n6 = ada_w.shape
    return pl.pallas_call(
        _adaln_kernel,
        grid=(depth, n6 // tn),
        in_specs=[pl.BlockSpec((8, d), lambda l, j: (0, 0)),
                  pl.BlockSpec((None, d, tn), lambda l, j: (l, 0, j)),
                  pl.BlockSpec((None, 1, tn), lambda l, j: (l, 0, j))],
        out_specs=pl.BlockSpec((None, 8, tn), lambda l, j: (l, 0, j)),
        out_shape=jax.ShapeDtypeStruct((depth, 8, n6), F32),
        compiler_params=_cparams(("parallel", "parallel")),
        name="adaln",
    )(cond8, ada_w, ada_b.reshape(depth, 1, n6))


def _mod_index(i, tm, n_prompt_rows, sample_rows):
    tp = n_prompt_rows // tm
    return jnp.where(i < tp, 0, 1 + (i - tp) // (sample_rows // tm))


def _split_specs(block, tiles_prompt, n_grid_axes=1):
    if n_grid_axes == 1:
        return [pl.BlockSpec(block, lambda i: (jnp.minimum(i, tiles_prompt - 1), 0)),
                pl.BlockSpec(block, lambda i: (jnp.maximum(i - tiles_prompt, 0), 0))]
    return [pl.BlockSpec(block, lambda i, j: (jnp.minimum(i, tiles_prompt - 1), 0)),
            pl.BlockSpec(block, lambda i, j: (jnp.maximum(i - tiles_prompt, 0), 0))]


def _win_kernel(*refs, d, tiles_prompt, split):
    if split:
        xp_ref, xs_ref, mod_ref, g_ref, w_ref, o_ref, h_ref = refs
    else:
        xp_ref, mod_ref, g_ref, w_ref, o_ref, h_ref = refs

    def norm_mod(x_ref):
        x = x_ref[...]
        xn = x * lax.rsqrt(jnp.mean(x * x, axis=-1, keepdims=True) + EPS) * g_ref[...]
        h = xn * (1.0 + mod_ref[:, d:2 * d]) + mod_ref[:, 0:d]
        h_ref[...] = h.astype(BF16)

    first = pl.program_id(1) == 0
    if split:
        is_prompt = pl.program_id(0) < tiles_prompt
        pl.when(first & is_prompt)(lambda: norm_mod(xp_ref))
        pl.when(first & jnp.logical_not(is_prompt))(lambda: norm_mod(xs_ref))
    else:
        pl.when(first)(lambda: norm_mod(xp_ref))

    o_ref[...] = _dot(h_ref[...], w_ref[...]).astype(o_ref.dtype)


def win_project(xs, mod_l, gain, w_bf16, n_prompt_rows, sample_rows, tm=1024, tn=512):
    split = len(xs) == 2
    d = xs[0].shape[1]
    t = sum(x.shape[0] for x in xs)
    width = w_bf16.shape[1]
    tiles_prompt = n_prompt_rows // tm
    midx = functools.partial(_mod_index, tm=tm, n_prompt_rows=n_prompt_rows, sample_rows=sample_rows)
    x_specs = (_split_specs((tm, d), tiles_prompt, 2) if split else [pl.BlockSpec((tm, d), lambda i, j: (i, 0))])
    return pl.pallas_call(
        functools.partial(_win_kernel, d=d, tiles_prompt=tiles_prompt, split=split),
        grid=(t // tm, width // tn),
        in_specs=x_specs + [pl.BlockSpec((None, 1, 6 * d), lambda i, j: (midx(i), 0, 0)),
                            pl.BlockSpec((1, d), lambda i, j: (0, 0)),
                            pl.BlockSpec((d, tn), lambda i, j: (0, j))],
        out_specs=pl.BlockSpec((tm, tn), lambda i, j: (i, j)),
        out_shape=jax.ShapeDtypeStruct((t, width), BF16),
        scratch_shapes=[pltpu.VMEM((tm, d), BF16)],
        compiler_params=_cparams(("parallel", "arbitrary")),
        name="win_project",
    )(*xs, mod_l, gain.reshape(1, d), w_bf16)


def _wout_kernel(*refs, d, tiles_prompt, split_y):
    if split_y:
        (oap_ref, oas_ref, obp_ref, obs_ref, yp_ref, ys_ref, mod_ref, wa_ref, wb_ref, g_ref, rw_ref,
         ynew_ref, h2_ref, lg_ref) = refs
    else:
        (oap_ref, oas_ref, obp_ref, obs_ref, yp_ref, mod_ref, wa_ref, wb_ref, g_ref, rw_ref,
         ynew_ref, h2_ref, lg_ref) = refs
        ys_ref = yp_ref
    is_prompt = pl.program_id(0) < tiles_prompt
    oa = jnp.where(is_prompt, oap_ref[...], oas_ref[...])
    ob = jnp.where(is_prompt, obp_ref[...], obs_ref[...])
    y = jnp.where(is_prompt, yp_ref[...], ys_ref[...]) if split_y else yp_ref[...]
    out = _dot(oa, wa_ref[...]) + _dot(ob, wb_ref[...])
    y1 = y + mod_ref[:, 2 * d:3 * d] * out
    ynew_ref[...] = y1
    xn = y1 * lax.rsqrt(jnp.mean(y1 * y1, axis=-1, keepdims=True) + EPS) * g_ref[...]
    h = xn * (1.0 + mod_ref[:, 4 * d:5 * d]) + mod_ref[:, 3 * d:4 * d]
    h2_ref[...] = h
    hh = h.astype(BF16)
    hl = (h - hh.astype(F32)).astype(BF16)
    lg_ref[...] = _dot(hh, rw_ref[0]) + _dot(hl, rw_ref[0]) + _dot(hh, rw_ref[1])


def wout_project(o_a, o_b, ys, mod_l, w_out_bf16, gain2, rw_split, n_prompt_rows, sample_rows, tm=256):
    split_y = len(ys) == 2
    d = ys[0].shape[1]
    t = sum(x.shape[0] for x in ys)
    ka, kb = o_a[0].shape[1], o_b[0].shape[1]
    tiles_prompt = n_prompt_rows // tm
    midx = functools.partial(_mod_index, tm=tm, n_prompt_rows=n_prompt_rows, sample_rows=sample_rows)
    y_specs = _split_specs((tm, d), tiles_prompt) if split_y else [pl.BlockSpec((tm, d), lambda i: (i, 0))]
    return pl.pallas_call(
        functools.partial(_wout_kernel, d=d, tiles_prompt=tiles_prompt, split_y=split_y),
        grid=(t // tm,),
        in_specs=_split_specs((tm, ka), tiles_prompt) + _split_specs((tm, kb), tiles_prompt) + y_specs + [
                  pl.BlockSpec((None, 1, 6 * d), lambda i: (midx(i), 0, 0)),
                  pl.BlockSpec((ka, d), lambda i: (0, 0)),
                  pl.BlockSpec((kb, d), lambda i: (1, 0)),
                  pl.BlockSpec((1, d), lambda i: (0, 0)),
                  pl.BlockSpec((2, d, LANES), lambda i: (0, 0, 0))],
        out_specs=[pl.BlockSpec((tm, d), lambda i: (i, 0)),
                   pl.BlockSpec((tm, d), lambda i: (i, 0)),
                   pl.BlockSpec((tm, LANES), lambda i: (i, 0))],
        out_shape=[jax.ShapeDtypeStruct((t, d), F32),
                   jax.ShapeDtypeStruct((t, d), F32),
                   jax.ShapeDtypeStruct((t, LANES), F32)],
        compiler_params=_cparams(("parallel",)),
        name="wout_project",
    )(*o_a, *o_b, *ys, mod_l, w_out_bf16, w_out_bf16, gain2.reshape(1, d), rw_split)


def _member_major(x_experts_last):
    shp = x_experts_last.shape[:-1]
    return x_experts_last.reshape(shp + (N_GROUPS, EXPERTS_PER_GROUP)).swapaxes(-1, -2).reshape(shp + (N_EXPERTS,))


def _route_kernel(lg_ref, bias_ref, info_ref, cnt_ref, run_ref, *, tm):
    @pl.when(pl.program_id(0) == 0)
    def _():
        run_ref[...] = jnp.zeros_like(run_ref)

    ng, nk = N_GROUPS, EXPERTS_PER_GROUP
    neg = jnp.float32(-jnp.inf)
    lgt = lg_ref[...].T
    aff = [jax.nn.sigmoid(lgt[k * ng:(k + 1) * ng, :]) for k in range(nk)]
    sel = [aff[k] + bias_ref[k * ng:(k + 1) * ng, :] for k in range(nk)]

    def first_max(vals):
        m = vals[0]
        for v in vals[1:]:
            m = jnp.maximum(m, v)
        idx = jnp.full(m.shape, float(nk), F32)
        for k in reversed(range(nk)):
            idx = jnp.where(vals[k] == m, float(k), idx)
        return m, idx

    m1, i1 = first_max(sel)
    m2, i2 = first_max([jnp.where(i1 == k, neg, sel[k]) for k in range(nk)])
    gscore = m1 + m2
    gid = lax.broadcasted_iota(jnp.int32, gscore.shape, 0).astype(F32)
    gmax = jnp.max(gscore, axis=0, keepdims=True)
    g = jnp.min(jnp.where(gscore == gmax, gid, 99.0), axis=0, keepdims=True)
    in_g = gid == g

    def pick(masks, vals):
        tot = jnp.zeros((1, tm), F32)
        for msk, v in zip(masks, vals):
            tot = tot + jnp.sum(jnp.where(msk, v, 0.0), axis=0, keepdims=True)
        return tot

    hit1 = [in_g & (i1 == k) for k in range(nk)]
    hit2 = [in_g & (i2 == k) for k in range(nk)]
    a1, a2 = pick(hit1, aff), pick(hit2, aff)
    den = a1 + a2
    e1 = g * nk + pick([in_g], [i1])
    e2 = g * nk + pick([in_g], [i2])
    oh = jnp.concatenate([jnp.where(hit1[k] | hit2[k], 1.0, 0.0) for k in range(nk)], axis=0)
    r_i = lax.broadcasted_iota(jnp.int32, (tm, tm), 0)
    c_i = lax.broadcasted_iota(jnp.int32, (tm, tm), 1)
    earlier = jnp.where(r_i < c_i, 1.0, 0.0).astype(BF16)
    cum = _dot(oh.astype(BF16), earlier) + run_ref[:, 0:1]
    cums = [cum[k * ng:(k + 1) * ng, :] for k in range(nk)]
    r1, r2 = pick(hit1, cums), pick(hit2, cums)
    run_new = run_ref[:, 0:1] + jnp.sum(oh, axis=1, keepdims=True)
    run_ref[...] = jnp.broadcast_to(run_new, run_ref.shape)
    cnt_ref[...] = jnp.broadcast_to(run_new, cnt_ref.shape)
    info_ref[...] = jnp.concatenate([e1, e2, a1 / den, a2 / den, r1, r2, jnp.zeros((2, tm), F32)], axis=0)


def route(logits, router_bias, tm=256):
    t = logits.shape[0]
    bias = jnp.broadcast_to(_member_major(router_bias.astype(F32))[:, None], (N_EXPERTS, tm))
    return pl.pallas_call(
        functools.partial(_route_kernel, tm=tm),
        grid=(t // tm,),
        in_specs=[pl.BlockSpec((tm, LANES), lambda i: (i, 0)),
                  pl.BlockSpec((N_EXPERTS, tm), lambda i: (0, 0))],
        out_specs=[pl.BlockSpec((8, tm), lambda i: (0, i)),
                   pl.BlockSpec((N_EXPERTS, LANES), lambda i: (0, 0))],
        out_shape=[jax.ShapeDtypeStruct((8, t), F32),
                   jax.ShapeDtypeStruct((N_EXPERTS, LANES), F32)],
        scratch_shapes=[pltpu.VMEM((N_EXPERTS, LANES), F32)],
        compiler_params=_cparams(("arbitrary",)),
        name="route",
    )(logits, bias)


MOE_AHEAD = 3
MOE_SLOTS = MOE_AHEAD + 1


def _expert_kernel(be_ref, nu_ref, tok_head_ref, tok_ahead_ref, h_hbm, w1_ref, w3_ref, w2_ref, o_ref,
                   xbuf, xcur, w1b, w3b, w2b, sem):
    i = pl.program_id(0)
    slot = i % MOE_SLOTS
    nu = nu_ref[0]

    def gather(idx_ref, base, s):
        for j in range(MOE_BLOCK):
            pltpu.make_async_copy(h_hbm.at[pl.ds(idx_ref[base + j], 1)], xbuf.at[s, pl.ds(j, 1)],
                                  sem.at[s]).start(priority=j % 2)

    def wait_block(s):
        pltpu.make_async_copy(h_hbm.at[pl.ds(0, MOE_BLOCK)], xbuf.at[s], sem.at[s]).wait()

    @pl.when(i == 0)
    def _():
        for blk in range(MOE_AHEAD):
            gather(tok_head_ref, blk * MOE_BLOCK, blk)

    @pl.when(i < nu)
    def _():
        @pl.when((i == 0) | (be_ref[i] != be_ref[jnp.maximum(i - 1, 0)]))
        def _():
            w1b[...] = w1_ref[...].astype(BF16)
            w3b[...] = w3_ref[...].astype(BF16)
            w2b[...] = w2_ref[...].astype(BF16)

        wait_block(slot)
        xcur[...] = xbuf[slot].astype(BF16)
        gather(tok_ahead_ref, 0, (i + MOE_AHEAD) % MOE_SLOTS)
        x = xcur[...]
        a = _dot(x, w1b[...])
        b = _dot(x, w3b[...])
        mid = (a * jax.nn.sigmoid(a)) * b
        o_ref[...] = _dot(mid.astype(BF16), w2b[...])

    @pl.when(i >= nu)
    def _():
        @pl.when(i < nu + MOE_AHEAD)
        def _():
            wait_block(slot)

        o_ref[...] = jnp.zeros_like(o_ref)


def expert_ffn(h2, slot_tok, block_expert, n_used, w1, w3, w2, layer):
    n_steps = slot_tok.shape[0] // MOE_BLOCK
    d, de = w1.shape[-2], w1.shape[-1]
    grid_spec = pltpu.PrefetchScalarGridSpec(
        num_scalar_prefetch=2,
        grid=(n_steps,),
        in_specs=[pl.BlockSpec((MOE_SLOTS * MOE_BLOCK,), lambda i, be, nu: (0,), memory_space=pltpu.SMEM),
                  pl.BlockSpec((MOE_BLOCK,), lambda i, be, nu: (jnp.minimum(i + MOE_AHEAD, n_steps - 1),),
                               memory_space=pltpu.SMEM),
                  pl.BlockSpec(memory_space=pl.ANY),
                  pl.BlockSpec((None, None, d, de), lambda i, be, nu: (layer, be[i], 0, 0)),
                  pl.BlockSpec((None, None, d, de), lambda i, be, nu: (layer, be[i], 0, 0)),
                  pl.BlockSpec((None, None, de, d), lambda i, be, nu: (layer, be[i], 0, 0))],
        out_specs=pl.BlockSpec((MOE_BLOCK, d), lambda i, be, nu: (i, 0)),
        scratch_shapes=[pltpu.VMEM((MOE_SLOTS, MOE_BLOCK, d), F32), pltpu.VMEM((MOE_BLOCK, d), BF16),
                        pltpu.VMEM((d, de), BF16), pltpu.VMEM((d, de), BF16), pltpu.VMEM((de, d), BF16),
                        pltpu.SemaphoreType.DMA((MOE_SLOTS,))],
    )
    return pl.pallas_call(
        _expert_kernel,
        grid_spec=grid_spec,
        out_shape=jax.ShapeDtypeStruct((n_steps * MOE_BLOCK, d), F32),
        compiler_params=_cparams(("arbitrary",)),
        name="expert_ffn",
    )(block_expert, n_used, slot_tok, slot_tok, h2, w1, w3, w2)


def _combine_kernel(dcur_ref, dnext_ref, y_ref, mod_ref, info_ref, g_ref, yb_hbm, *rest, tm, nt, d, final,
                    tiles_prompt):
    if final:
        op_ref, os_ref, buf, sem = rest
    else:
        o_ref, buf, sem = rest
    i = pl.program_id(0)
    slot = i % 2

    def row_copy(src_row, s, k, j):
        return pltpu.make_async_copy(yb_hbm.at[pl.ds(src_row, 1)], buf.at[s, k, pl.ds(j, 1)], sem.at[s])

    def issue(dref, s):
        def body(j, carry):
            for k in range(TOP_K):
                row_copy(dref[TOP_K * j + k], s, k, j).start(priority=k % 2)
            return carry

        lax.fori_loop(0, tm, body, 0, unroll=32)

    @pl.when(i == 0)
    def _():
        issue(dcur_ref, 0)

    @pl.when(i + 1 < nt)
    def _():
        issue(dnext_ref, 1 - slot)

    for k in range(TOP_K):
        pltpu.make_async_copy(yb_hbm.at[pl.ds(0, tm)], buf.at[slot, k], sem.at[slot]).wait()
    gates = info_ref[...]
    moe = gates[:, 0:1] * buf[slot, 0] + gates[:, 1:2] * buf[slot, 1]
    y2 = y_ref[...] + mod_ref[:, 5 * d:6 * d] * moe
    if not final:
        o_ref[...] = y2
        return
    y2 = y2 * lax.rsqrt(jnp.mean(y2 * y2, axis=-1, keepdims=True) + EPS) * g_ref[...]

    @pl.when(i < tiles_prompt)
    def _():
        op_ref[...] = y2

    @pl.when(i >= tiles_prompt)
    def _():
        os_ref[...] = y2


def combine(y1, mod_l, info, dest_flat, yb, final_gain, n_prompt_rows, sample_rows, final, tm=256):
    t, d = y1.shape
    nt = t // tm
    tiles_prompt = n_prompt_rows // tm
    midx = functools.partial(_mod_index, tm=tm, n_prompt_rows=n_prompt_rows, sample_rows=sample_rows)
    if final:
        out_specs = _split_specs((tm, d), tiles_prompt)
        out_shape = [jax.ShapeDtypeStruct((n_prompt_rows, d), F32), jax.ShapeDtypeStruct((t - n_prompt_rows, d), F32)]
    else:
        out_specs = pl.BlockSpec((tm, d), lambda i: (i, 0))
        out_shape = jax.ShapeDtypeStruct((t, d), F32)
    return pl.pallas_call(
        functools.partial(_combine_kernel, tm=tm, nt=nt, d=d, final=final, tiles_prompt=tiles_prompt),
        grid=(nt,),
        in_specs=[pl.BlockSpec((TOP_K * tm,), lambda i: (i,), memory_space=pltpu.SMEM),
                  pl.BlockSpec((TOP_K * tm,), lambda i: (jnp.minimum(i + 1, nt - 1),), memory_space=pltpu.SMEM),
                  pl.BlockSpec((tm, d), lambda i: (i, 0)),
                  pl.BlockSpec((None, 1, 6 * d), lambda i: (midx(i), 0, 0)),
                  pl.BlockSpec((tm, TOP_K), lambda i: (i, 0)),
                  pl.BlockSpec((1, d), lambda i: (0, 0)),
                  pl.BlockSpec(memory_space=pl.ANY)],
        out_specs=out_specs,
        out_shape=out_shape,
        scratch_shapes=[pltpu.VMEM((2, TOP_K, tm, d), F32), pltpu.SemaphoreType.DMA((2,))],
        compiler_params=_cparams(("arbitrary",)),
        name="combine",
    )(dest_flat, dest_flat, y1, mod_l, info, final_gain.reshape(1, d), yb)


def moe_layer(y1, h2p, logits, mod_l, router_bias, w1, w3, w2, layer, final_gain, n_prompt_rows, sample_rows, final):
    t, d = y1.shape
    info_t, cnt = route(logits, router_bias)
    expert = info_t[0:2].T.astype(jnp.int32)
    rank = info_t[4:6].T.astype(jnp.int32)
    info = info_t[2:4].T
    counts = cnt[:, 0].reshape(EXPERTS_PER_GROUP, N_GROUPS).T.reshape(-1).astype(jnp.int32)
    padded = (counts + MOE_BLOCK - 1) // MOE_BLOCK * MOE_BLOCK
    padded_end = jnp.cumsum(padded)
    padded_start = padded_end - padded
    dest = (padded_start[expert] + rank).reshape(-1)
    n_blocks = (t * TOP_K) // MOE_BLOCK + N_EXPERTS + MOE_AHEAD
    slots = n_blocks * MOE_BLOCK
    block_lo = jnp.arange(n_blocks, dtype=jnp.int32)[:, None] * MOE_BLOCK
    block_expert = jnp.minimum(jnp.sum((padded_end[None, :] <= block_lo).astype(jnp.int32), axis=1), N_EXPERTS - 1)
    n_used = (padded_end[-1:] // MOE_BLOCK).astype(jnp.int32)
    tok = jnp.broadcast_to(jnp.arange(t, dtype=jnp.int32)[:, None], (t, TOP_K)).reshape(-1)
    slot_tok = jnp.zeros((slots,), jnp.int32).at[dest].set(tok)
    yb = expert_ffn(h2p, slot_tok, block_expert, n_used, w1, w3, w2, layer)
    return combine(y1, mod_l, info, dest, yb, final_gain, n_prompt_rows, sample_rows, final)


def rope_tables(n, lead):
    quarter = MLA_ROPE // 4
    tpos = np.arange(n)
    row = (tpos // GRID_W).astype(np.float32)
    col = (tpos % GRID_W).astype(np.float32)
    inv_freq = (np.float32(ROPE_BASE) ** (-np.arange(quarter, dtype=np.float32) / np.float32(quarter))).astype(np.float32)
    lane = np.arange(64)
    pos = np.where(lane[None, :] < 32, row[:, None], col[:, None]).astype(np.float32)
    ang = pos * inv_freq[lane % quarter][None, :]
    cos = np.cos(ang).astype(np.float32)
    sin = np.sin(ang).astype(np.float32)
    sgn = np.where((lane % 32) < quarter, -1.0, 1.0).astype(np.float32)
    sin = sin * sgn[None, :]
    cos = np.concatenate([cos, cos], axis=1)
    sin = np.concatenate([sin, sin], axis=1)
    cos = np.concatenate([np.ones((lead, LANES), np.float32), cos], axis=0)
    sin = np.concatenate([np.zeros((lead, LANES), np.float32), sin], axis=0)
    return jnp.asarray(cos), jnp.asarray(sin)


def _rope(x, cos, sin_signed):
    lane = lax.broadcasted_iota(jnp.int32, x.shape, 1)
    partner = jnp.where((lane & 31) < 16, pltpu.roll(x, LANES - 16, axis=1), pltpu.roll(x, 16, axis=1))
    return x * cos + partner * sin_signed


def _mla_proj_kernel(cq_ref, ckv_ref, misc_ref, qg_ref, kvg_ref, wq_ref, wkv_ref, cos_ref, sin_ref,
                     q_ref, kv_ref, ckvn_ref, kpe_ref, *, scale):
    cq = cq_ref[...].astype(F32)
    qn = cq * lax.rsqrt(jnp.mean(cq * cq, axis=-1, keepdims=True) + EPS) * qg_ref[...]
    qf = _dot(qn.astype(BF16), wq_ref[...])
    half = qf.shape[1] // 2
    cos, sin = cos_ref[...], sin_ref[...]
    q_ref[:, 0:half] = (qf[:, 0:half] * scale).astype(BF16)
    for h in range(MLA_HEADS):
        pe = qf[:, half + h * LANES: half + (h + 1) * LANES]
        q_ref[:, half + h * LANES: half + (h + 1) * LANES] = (_rope(pe, cos, sin) * scale).astype(BF16)
    ckv = ckv_ref[...].astype(F32)
    ckvn = ckv * lax.rsqrt(jnp.mean(ckv * ckv, axis=-1, keepdims=True) + EPS) * kvg_ref[...]
    ckvn_ref[...] = ckvn
    kv_ref[...] = _dot(ckvn.astype(BF16), wkv_ref[...]).astype(BF16)
    misc = misc_ref[...].astype(F32)
    lane = lax.broadcasted_iota(jnp.int32, misc.shape, 1)
    kpe = jnp.where(lane < MLA_ROPE, misc, 0.0)
    kpe_ref[...] = _rope(kpe, cos, sin).astype(BF16)


def mla_project(z, q_gain, kv_gain, wq, wkv, cos_t, sin_t, n_prompt_rows, sample_rows, tm=512):
    t = z.shape[0]
    tp = n_prompt_rows // tm
    per = sample_rows // tm

    def tab(i):
        return (jnp.where(i < tp, 0, 1 + (i - tp) % per), 0)

    scale = float((MLA_NOPE + MLA_ROPE) ** -0.5)
    nq = wq.shape[1]
    return pl.pallas_call(
        functools.partial(_mla_proj_kernel, scale=scale),
        grid=(t // tm,),
        in_specs=[pl.BlockSpec((tm, MLA_Q_LORA), lambda i: (i, AB_CQ // MLA_Q_LORA)),
                  pl.BlockSpec((tm, MLA_KV_LORA), lambda i: (i, AB_CKV // MLA_KV_LORA)),
                  pl.BlockSpec((tm, LANES), lambda i: (i, AB_MISC // LANES)),
                  pl.BlockSpec((1, MLA_Q_LORA), lambda i: (0, 0)),
                  pl.BlockSpec((1, MLA_KV_LORA), lambda i: (0, 0)),
                  pl.BlockSpec(wq.shape, lambda i: (0, 0)),
                  pl.BlockSpec(wkv.shape, lambda i: (0, 0)),
                  pl.BlockSpec((tm, LANES), tab),
                  pl.BlockSpec((tm, LANES), tab)],
        out_specs=[pl.BlockSpec((tm, nq), lambda i: (i, 0)),
                   pl.BlockSpec((tm, wkv.shape[1]), lambda i: (i, 0)),
                   pl.BlockSpec((tm, MLA_KV_LORA), lambda i: (i, 0)),
                   pl.BlockSpec((tm, LANES), lambda i: (i, 0))],
        out_shape=[jax.ShapeDtypeStruct((t, nq), BF16),
                   jax.ShapeDtypeStruct((t, wkv.shape[1]), BF16),
                   jax.ShapeDtypeStruct((t, MLA_KV_LORA), F32),
                   jax.ShapeDtypeStruct((t, LANES), BF16)],
        compiler_params=_cparams(("parallel",)),
        name="mla_project",
    )(z, z, z, q_gain.reshape(1, -1), kv_gain.reshape(1, -1), wq, wkv, cos_t, sin_t)


def _ctx_kv_kernel(ckv_ref, w_ref, o_ref):
    o_ref[...] = _dot(ckv_ref[...].astype(BF16), w_ref[...]).astype(BF16)


def ctx_kv_project(ckv_c, wkv):
    rows = ckv_c.shape[0]
    return pl.pallas_call(
        _ctx_kv_kernel,
        grid=(1,),
        in_specs=[pl.BlockSpec(ckv_c.shape, lambda i: (0, 0)), pl.BlockSpec(wkv.shape, lambda i: (0, 0))],
        out_specs=pl.BlockSpec((rows, wkv.shape[1]), lambda i: (0, 0)),
        out_shape=jax.ShapeDtypeStruct((rows, wkv.shape[1]), BF16),
        compiler_params=_cparams(("arbitrary",)),
        name="ctx_kv_project",
    )(ckv_c, wkv)


def _softmax_pv(s, v):
    m = jnp.max(s, axis=-1, keepdims=True)
    p = jnp.exp(s - m)
    l = jnp.sum(p, axis=-1, keepdims=True)
    return _dot(p.astype(BF16), v) / l


def _mla_prompt_attn_kernel(q_ref, kv_ref, kp_ref, o_ref):
    nh = MLA_HEADS
    kp = kp_ref[...]
    for h in range(nh):
        q = jnp.concatenate([q_ref[:, h * LANES:(h + 1) * LANES], q_ref[:, (nh + h) * LANES:(nh + h + 1) * LANES]], axis=1)
        k = jnp.concatenate([kv_ref[:, h * LANES:(h + 1) * LANES], kp], axis=1)
        v = kv_ref[:, (nh + h) * LANES:(nh + h + 1) * LANES]
        o_ref[:, h * LANES:(h + 1) * LANES] = _softmax_pv(_dot_nt(q, k), v).astype(o_ref.dtype)


def mla_prompt_attention(q_m, kv, kpe, batch, n):
    h = MLA_HEADS
    return pl.pallas_call(
        _mla_prompt_attn_kernel,
        grid=(batch,),
        in_specs=[pl.BlockSpec((n, 2 * h * LANES), lambda b: (b, 0)),
                  pl.BlockSpec((n, 2 * h * LANES), lambda b: (b, 0)),
                  pl.BlockSpec((n, LANES), lambda b: (b, 0))],
        out_specs=pl.BlockSpec((n, h * MLA_DV), lambda b: (b, 0)),
        out_shape=jax.ShapeDtypeStruct((batch * n, h * MLA_DV), BF16),
        compiler_params=_cparams(("parallel",)),
        name="mla_prompt_attention",
    )(q_m, kv, kpe)


def _lane_tile_reduce(x, op):
    r = x[:, 0:LANES]
    for c in range(1, x.shape[1] // LANES):
        r = op(r, x[:, c * LANES:(c + 1) * LANES])
    return r


def _mla_sample_attn_kernel(qn_ref, qp_ref, kcn_ref, kcp_ref, vc_ref, kn_ref, kp_ref, v_ref, o_ref,
                            s_ref, m_ref, acc_ref, *, tk):
    past = kcn_ref.shape[0]
    nk = kn_ref.shape[0] // tk
    q = jnp.concatenate([qn_ref[...], qp_ref[...]], axis=1)
    kc = jnp.concatenate([kcn_ref[...], kcp_ref[...].astype(BF16)], axis=1)
    s = _dot_nt(q, kc)
    s_ref[:, 0:past] = s
    m_ref[...] = _lane_tile_reduce(s, jnp.maximum)

    def scores(j, carry):
        off = pl.multiple_of(j * tk, tk)
        k = jnp.concatenate([kn_ref[pl.ds(off, tk), :], kp_ref[pl.ds(off, tk), :]], axis=1)
        s = _dot_nt(q, k)
        s_ref[:, pl.ds(pl.multiple_of(past + j * tk, LANES), tk)] = s
        m_ref[...] = jnp.maximum(m_ref[...], _lane_tile_reduce(s, jnp.maximum))
        return carry

    lax.fori_loop(0, nk, scores, 0, unroll=2)
    m_ref[...] = jnp.broadcast_to(jnp.max(m_ref[...], axis=-1, keepdims=True), m_ref.shape)

    def weighted_values(s, v):
        m = m_ref[...]
        p = jnp.concatenate([jnp.exp(s[:, c * LANES:(c + 1) * LANES] - m).astype(BF16)
                             for c in range(s.shape[1] // LANES)], axis=1)
        v_ones = jnp.concatenate([v, jnp.ones((v.shape[0], LANES), BF16)], axis=1)
        return _dot(p, v_ones)

    acc_ref[...] = weighted_values(s_ref[:, 0:past], vc_ref[...])

    def weighted(j, carry):
        off = pl.multiple_of(j * tk, tk)
        acc_ref[...] += weighted_values(s_ref[:, pl.ds(pl.multiple_of(past + j * tk, LANES), tk)],
                                        v_ref[pl.ds(off, tk), :])
        return carry

    lax.fori_loop(0, nk, weighted, 0, unroll=2)
    acc = acc_ref[...]
    o_ref[...] = (acc[:, 0:MLA_DV] / acc[:, MLA_DV:]).astype(o_ref.dtype)


def mla_sample_attention(q_m, kv, kpe, kv_ctx, kpe_ctx, row0, batch, n, past, tq=1024, tk=512):
    h = MLA_HEADS
    qb = n // tq
    r0 = row0 // n
    q0 = row0 // tq
    return pl.pallas_call(
        functools.partial(_mla_sample_attn_kernel, tk=tk),
        grid=(batch, h, qb),
        in_specs=[pl.BlockSpec((tq, LANES), lambda b, hh, i: (q0 + b * qb + i, hh)),
                  pl.BlockSpec((tq, LANES), lambda b, hh, i: (q0 + b * qb + i, h + hh)),
                  pl.BlockSpec((past, LANES), lambda b, hh, i: (b, hh)),
                  pl.BlockSpec((past, LANES), lambda b, hh, i: (b, 0)),
                  pl.BlockSpec((past, LANES), lambda b, hh, i: (b, h + hh)),
                  pl.BlockSpec((n, LANES), lambda b, hh, i: (r0 + b, hh)),
                  pl.BlockSpec((n, LANES), lambda b, hh, i: (r0 + b, 0)),
                  pl.BlockSpec((n, LANES), lambda b, hh, i: (r0 + b, h + hh))],
        out_specs=pl.BlockSpec((tq, LANES), lambda b, hh, i: (b * qb + i, hh)),
        out_shape=jax.ShapeDtypeStruct((batch * n, h * MLA_DV), BF16),
        scratch_shapes=[pltpu.VMEM((tq, past + n), F32), pltpu.VMEM((tq, LANES), F32),
                        pltpu.VMEM((tq, MLA_DV + LANES), F32)],
        compiler_params=_cparams(("parallel", "parallel", "arbitrary")),
        name="mla_sample_attention",
    )(q_m, q_m, kv_ctx, kpe_ctx, kv_ctx, kv, kpe, kv)


def _na_prompt_attn_kernel(q_ref, k_ref, v_ref, o_ref, ko_ref, vo_ref, *, scale):
    for h in range(NA_HEADS):
        cols = slice(h * LANES, (h + 1) * LANES)
        k = k_ref[:, cols]
        v = v_ref[:, cols]
        ko_ref[h] = k.astype(F32)
        vo_ref[h] = v.astype(F32)
        q = (q_ref[:, cols].astype(F32) * scale).astype(BF16)
        o_ref[:, cols] = _softmax_pv(_dot_nt(q, k), v).astype(o_ref.dtype)


def na_prompt_attention(z, batch, n):
    h = NA_HEADS
    scale = float(NA_HD ** -0.5)
    w = h * NA_HD
    cq, ck, cv = CD_NQ // w, CD_NK // w, CD_NV // w
    return pl.pallas_call(
        functools.partial(_na_prompt_attn_kernel, scale=scale),
        grid=(batch,),
        in_specs=[pl.BlockSpec((n, w), lambda b: (b, cq)),
                  pl.BlockSpec((n, w), lambda b: (b, ck)),
                  pl.BlockSpec((n, w), lambda b: (b, cv))],
        out_specs=[pl.BlockSpec((n, w), lambda b: (b, 0)),
                   pl.BlockSpec((None, h, n, NA_HD), lambda b: (b, 0, 0, 0)),
                   pl.BlockSpec((None, h, n, NA_HD), lambda b: (b, 0, 0, 0))],
        out_shape=[jax.ShapeDtypeStruct((batch * n, w), BF16),
                   jax.ShapeDtypeStruct((batch, h, n, NA_HD), F32),
                   jax.ShapeDtypeStruct((batch, h, n, NA_HD), F32)],
        compiler_params=_cparams(("parallel",)),
        name="na_prompt_attention",
    )(z, z, z)


def _dot_tn(a, b):
    return lax.dot_general(a, b, (((0,), (0,)), ((), ())), preferred_element_type=F32)


def _head_norm_gate(acc, gate_in, gain):
    outs = []
    for h in range(2):
        a = acc[:, h * LANES:(h + 1) * LANES]
        outs.append(a * lax.rsqrt(jnp.mean(a * a, axis=-1, keepdims=True) + EPS))
    g = gate_in
    return jnp.concatenate(outs, axis=1) * gain * (g * jax.nn.sigmoid(g))


RET_CHUNK = 256
FIN_ROWS = 256


def _ret_kernel(ld_ref, q_ref, k_ref, v_ref, g_ref, cos_ref, sin_ref, gain_ref, s0_ref, o_ref, sf_ref, acc_ref, *, n):
    L = RET_CHUNK
    nc = n // L
    hp = pl.program_id(1)
    lane = lax.broadcasted_iota(jnp.int32, (1, LANES), 1)
    lo_lanes = lane < RET_DK
    row128 = lax.broadcasted_iota(jnp.int32, (LANES, 1), 0)
    ri = lax.broadcasted_iota(jnp.int32, (L, L), 0)
    ci = lax.broadcasted_iota(jnp.int32, (L, L), 1)
    dist = (ri - ci).astype(F32)
    pos = lax.broadcasted_iota(jnp.int32, (L, 1), 0).astype(F32)

    consts = []
    for dr in range(2):
        lg0 = ld_ref[dr, 2 * hp]
        lg1 = ld_ref[dr, 2 * hp + 1]
        lg2 = jnp.where(lo_lanes, lg0, lg1)
        lgc = jnp.where(row128 < RET_DK, lg0, lg1)
        sd = dist if dr == 0 else -dist
        ok = sd >= 0
        intra = [jnp.where(ok, jnp.exp(jnp.where(ok, sd, 0.0) * lg), 0.0) for lg in (lg0, lg1)]
        if dr == 0:
            inter = jnp.exp((pos + 1.0) * lg2)
            to_state = jnp.exp((L - 1.0 - pos) * lg2)
        else:
            inter = jnp.exp((L - pos) * lg2)
            to_state = jnp.exp(pos * lg2)
        consts.append((intra, inter, to_state, jnp.exp(float(L) * lgc)))

    def clear(i, carry):
        acc_ref[pl.ds(pl.multiple_of(i * FIN_ROWS, FIN_ROWS), FIN_ROWS), :] = jnp.zeros((FIN_ROWS, 2 * LANES), F32)
        return carry

    lax.fori_loop(0, n // FIN_ROWS, clear, 0)

    def step(c, dr, s2):
        intra, inter, to_state, cdec = consts[dr]
        rows = pl.ds(pl.multiple_of(c * L, L), L)
        cos, sin = cos_ref[rows, :], sin_ref[rows, :]
        q = _rope(q_ref[rows, :].astype(F32) * (RET_DK ** -0.5), cos, sin)
        k = _rope(k_ref[rows, :].astype(F32), cos, sin)
        kb = k.astype(BF16)
        s2b = s2.astype(BF16)
        kt = (k * to_state).astype(BF16)
        upd = []
        for h in range(2):
            qm = jnp.where(lo_lanes if h == 0 else ~lo_lanes, q, 0.0)
            vh = v_ref[rows, h * LANES:(h + 1) * LANES].astype(BF16)
            sc = _dot_nt(qm.astype(BF16), kb) * intra[h]
            o = _dot(sc.astype(BF16), vh) + _dot((qm * inter).astype(BF16), s2b)
            acc_ref[rows, h * LANES:(h + 1) * LANES] += o
            upd.append(_dot_tn(kt, vh))
        return cdec * s2 + jnp.where(row128 < RET_DK, upd[0], upd[1])

    def body(i, carry):
        sf, sb = carry
        return step(i, 0, sf), step(nc - 1 - i, 1, sb)

    sf, sb = lax.fori_loop(0, nc, body, (s0_ref[0], s0_ref[1]))
    sf_ref[0] = sf
    sf_ref[1] = sb

    def fin(i, carry):
        rows = pl.ds(pl.multiple_of(i * FIN_ROWS, FIN_ROWS), FIN_ROWS)
        o_ref[rows, :] = _head_norm_gate(acc_ref[rows, :], g_ref[rows, :].astype(F32), gain_ref[...]).astype(o_ref.dtype)
        return carry

    lax.fori_loop(0, n // FIN_ROWS, fin, 0)


def retention_mixer(z, log_decay, ret_gain, s0, cos_t, sin_t, row0, batch, n):
    hp = RET_HEADS // 2
    r0 = row0 // n
    cq, ck, cv, cg = CD_RQ // LANES, CD_RK // LANES, CD_RV // (2 * LANES), CD_RG // (2 * LANES)
    grid_spec = pltpu.PrefetchScalarGridSpec(
        num_scalar_prefetch=1,
        grid=(batch, hp),
        in_specs=[pl.BlockSpec((n, LANES), lambda b, p, ld: (r0 + b, cq + p)),
                  pl.BlockSpec((n, LANES), lambda b, p, ld: (r0 + b, ck + p)),
                  pl.BlockSpec((n, 2 * LANES), lambda b, p, ld: (r0 + b, cv + p)),
                  pl.BlockSpec((n, 2 * LANES), lambda b, p, ld: (r0 + b, cg + p)),
                  pl.BlockSpec((n, LANES), lambda b, p, ld: (0, 0)),
                  pl.BlockSpec((n, LANES), lambda b, p, ld: (0, 0)),
                  pl.BlockSpec((1, 2 * LANES), lambda b, p, ld: (0, p)),
                  pl.BlockSpec((None, 2, LANES, LANES), lambda b, p, ld: (b, 0, p, 0))],
        out_specs=[pl.BlockSpec((n, 2 * LANES), lambda b, p, ld: (b, p)),
                   pl.BlockSpec((None, 2, LANES, LANES), lambda b, p, ld: (b, 0, p, 0))],
        scratch_shapes=[pltpu.VMEM((n, 2 * LANES), F32)],
    )
    return pl.pallas_call(
        functools.partial(_ret_kernel, n=n),
        grid_spec=grid_spec,
        out_shape=[jax.ShapeDtypeStruct((batch * n, RET_HEADS * RET_DV), BF16),
                   jax.ShapeDtypeStruct((batch, 2, RET_HEADS * RET_DK, RET_DV), F32)],
        compiler_params=_cparams(("parallel", "parallel")),
        name="retention_mixer",
    )(log_decay.astype(F32), z, z, z, z, cos_t, sin_t, ret_gain.reshape(1, -1), s0)


GLA_CHUNK = 256
GLA_SUB = 64
GLA_SAFE_EXPONENT = 60.0


def _gla_kernel(q_ref, k_ref, v_ref, g_ref, misc_ref, gw_ref, gb_ref, gain_ref, s0_ref, o_ref, sf_ref,
                la_ref, acc_ref, *, n):
    C, SB = GLA_CHUNK, GLA_SUB
    nsb = C // SB
    nc = n // C
    lane = lax.broadcasted_iota(jnp.int32, (1, LANES), 1)
    lo_lanes = lane < GLA_DK
    head_mask = (lo_lanes, ~lo_lanes)
    ri = lax.broadcasted_iota(jnp.int32, (C, C), 0)
    ci = lax.broadcasted_iota(jnp.int32, (C, C), 1)
    rowc = lax.broadcasted_iota(jnp.int32, (C, 1), 0)
    SLAB = 16
    row8 = lax.broadcasted_iota(jnp.int32, (SLAB, 1), 0)
    tri = (jnp.where(ci <= ri, 1.0, 0.0).astype(BF16), jnp.where(ci >= ri, 1.0, 0.0).astype(BF16))
    causal = (ci <= ri, ci >= ri)

    def prep(i, carry):
        rows = pl.ds(pl.multiple_of(i * FIN_ROWS, FIN_ROWS), FIN_ROWS)
        slab = misc_ref[rows, :].astype(BF16)
        for dr in range(2):
            pre = _dot(slab, gw_ref[dr]) + gb_ref[dr]
            la_ref[dr, rows, :] = (jnp.minimum(pre, 0.0) - jnp.log1p(jnp.exp(-jnp.abs(pre)))) * (1.0 / GATE_TAU)
        acc_ref[rows, :] = jnp.zeros((FIN_ROWS, 2 * LANES), F32)
        return carry

    lax.fori_loop(0, n // FIN_ROWS, prep, 0)

    def step(c, dr):
        rev = dr == 1
        base = pl.multiple_of(c * C, C)
        rows = pl.ds(base, C)
        la = la_ref[dr, rows, :]
        hi = la.astype(BF16)
        lo = (la - hi.astype(F32)).astype(BF16)
        b = _dot(tri[dr], hi) + _dot(tri[dr], lo)
        zero = jnp.zeros((1, LANES), F32)
        if not rev:
            bref = [zero if i == 0 else b[SB * i - 1:SB * i, :] for i in range(nsb)]
            bend = [b[SB * i + SB - 1:SB * i + SB, :] for i in range(nsb)]
            blast = b[C - 1:C, :]
        else:
            bref = [zero if i == nsb - 1 else b[SB * (i + 1):SB * (i + 1) + 1, :] for i in range(nsb)]
            bend = [b[SB * i:SB * i + 1, :] for i in range(nsb)]
            blast = b[0:1, :]
        spread = bref[0] - bend[0]
        for i in range(1, nsb):
            spread = jnp.maximum(spread, bref[i] - bend[i])
        worst = jnp.max(spread)

        def fast(st):
            q = q_ref[rows, :].astype(F32) * (GLA_DK ** -0.5)
            k = k_ref[rows, :].astype(F32)
            a_rows = ([], [])
            for i in range(nsb):
                qe = q[SB * i:SB * (i + 1), :] * jnp.exp(b[SB * i:SB * (i + 1), :] - bref[i])
                valid = (rowc < SB * (i + 1)) if not rev else (rowc >= SB * i)
                ke = jnp.where(valid, k * jnp.exp(jnp.where(valid, bref[i] - b, 0.0)), 0.0).astype(BF16)
                for h in range(2):
                    a_rows[h].append(_dot_nt(jnp.where(head_mask[h], qe, 0.0).astype(BF16), ke))
            qi = q * jnp.exp(b)
            ks = (k * jnp.exp(blast - b)).astype(BF16)
            stb = st.astype(BF16)
            upd = []
            for h in range(2):
                a = jnp.where(causal[dr], jnp.concatenate(a_rows[h], axis=0), 0.0)
                vh = v_ref[rows, h * LANES:(h + 1) * LANES].astype(BF16)
                o = _dot(a.astype(BF16), vh) + _dot_nt(jnp.where(head_mask[h], qi, 0.0).astype(BF16), stb)
                acc_ref[rows, h * LANES:(h + 1) * LANES] += o
                upd.append(_dot_tn(vh, ks))
            return jnp.exp(blast) * st + jnp.where(lo_lanes, upd[0], upd[1])

        def slow(st):
            def row_step(r, st):
                t = r if not rev else C - 1 - r
                slab = pl.ds(pl.multiple_of(base + (t // SLAB) * SLAB, SLAB), SLAB)
                pick = row8 == (t % SLAB)
                q8 = jnp.where(pick, q_ref[slab, :].astype(F32) * (GLA_DK ** -0.5), 0.0)
                k8 = jnp.where(pick, k_ref[slab, :].astype(F32), 0.0).astype(BF16)
                la_t = jnp.sum(jnp.where(pick, la_ref[dr, slab, :], 0.0), axis=0, keepdims=True)
                upd = []
                for h in range(2):
                    v8 = jnp.where(pick, v_ref[slab, h * LANES:(h + 1) * LANES].astype(F32), 0.0).astype(BF16)
                    upd.append(_dot_tn(v8, k8))
                st = jnp.exp(la_t) * st + jnp.where(lo_lanes, upd[0], upd[1])
                stb = st.astype(BF16)
                for h in range(2):
                    qh = jnp.where(head_mask[h], q8, 0.0).astype(BF16)
                    acc_ref[slab, h * LANES:(h + 1) * LANES] += _dot_nt(qh, stb)
                return st

            return lax.fori_loop(0, C + 0 * pl.program_id(0), row_step, st)

        return worst, fast, slow

    def body(i, carry):
        worst_f, fast_f, slow_f = step(i, 0)
        worst_b, fast_b, slow_b = step(nc - 1 - i, 1)
        return lax.cond(jnp.maximum(worst_f, worst_b) <= GLA_SAFE_EXPONENT,
                        lambda st: (fast_f(st[0]), fast_b(st[1])),
                        lambda st: (slow_f(st[0]), slow_b(st[1])), carry)

    sf, sb = lax.fori_loop(0, nc, body, (s0_ref[0].T, s0_ref[1].T))
    sf_ref[0] = sf.T
    sf_ref[1] = sb.T

    def fin(i, carry):
        rows = pl.ds(pl.multiple_of(i * FIN_ROWS, FIN_ROWS), FIN_ROWS)
        o_ref[rows, :] = _head_norm_gate(acc_ref[rows, :], g_ref[rows, :].astype(F32), gain_ref[...]).astype(o_ref.dtype)
        return carry

    lax.fori_loop(0, n // FIN_ROWS, fin, 0)


def _prep_gate(gate_w, gate_b):
    gw = jnp.zeros((2, LANES, gate_w.shape[-1]), F32)
    for dr in range(2):
        lo = MLA_ROPE + dr * GATE_RANK
        gw = gw.at[dr, lo:lo + GATE_RANK, :].set(gate_w[dr].astype(F32))
    return gw.astype(BF16), gate_b.astype(F32).reshape(2, 1, -1)


def gla_mixer(z, gate_w, gate_b, gla_gain, s0, row0, batch, n):
    hp = GLA_HEADS // 2
    r0 = row0 // n
    gw, gb = _prep_gate(gate_w, gate_b)
    cq, ck, cv, cg, cm = AB_GQ // LANES, AB_GK // LANES, AB_GV // (2 * LANES), AB_GR // (2 * LANES), AB_MISC // LANES
    return pl.pallas_call(
        functools.partial(_gla_kernel, n=n),
        grid=(batch, hp),
        in_specs=[pl.BlockSpec((n, LANES), lambda b, p: (r0 + b, cq + p)),
                  pl.BlockSpec((n, LANES), lambda b, p: (r0 + b, ck + p)),
                  pl.BlockSpec((n, 2 * LANES), lambda b, p: (r0 + b, cv + p)),
                  pl.BlockSpec((n, 2 * LANES), lambda b, p: (r0 + b, cg + p)),
                  pl.BlockSpec((n, LANES), lambda b, p: (r0 + b, cm)),
                  pl.BlockSpec((2, LANES, LANES), lambda b, p: (0, 0, p)),
                  pl.BlockSpec((2, 1, LANES), lambda b, p: (0, 0, p)),
                  pl.BlockSpec((1, 2 * LANES), lambda b, p: (0, p)),
                  pl.BlockSpec((None, 2, LANES, LANES), lambda b, p: (b, 0, p, 0))],
        out_specs=[pl.BlockSpec((n, 2 * LANES), lambda b, p: (b, p)),
                   pl.BlockSpec((None, 2, LANES, LANES), lambda b, p: (b, 0, p, 0))],
        out_shape=[jax.ShapeDtypeStruct((batch * n, GLA_HEADS * GLA_DV), BF16),
                   jax.ShapeDtypeStruct((batch, 2, GLA_HEADS * GLA_DK, GLA_DV), F32)],
        scratch_shapes=[pltpu.VMEM((2, n, LANES), F32), pltpu.VMEM((n, 2 * LANES), F32)],
        compiler_params=_cparams(("parallel", "parallel")),
        name="gla_mixer",
    )(z, z, z, z, z, gw, gb, gla_gain.reshape(1, -1), s0)


NA_QROWS = 8
NA_WROWS = 16
NA_NEG = -1e30
NA_TAB = 32


def _na_bias_table(na_bias_l):
    qc = np.arange(GRID_W)[:, None]
    kc = np.arange(GRID_W)[None, :]
    ws = np.clip(qc - NA_COLS // 2, 0, GRID_W - NA_COLS)
    col_ok = (kc >= ws) & (kc < ws + NA_COLS)
    col_off = np.clip(kc - qc + NA_COLS - 1, 0, 2 * NA_COLS - 2)
    aa = (np.arange(NA_TAB) - NA_QROWS)[:, None] + np.arange(2)[None, :]
    n_row, n_col = 2 * NA_ROWS - 1, 2 * NA_COLS - 1
    sel_row = (aa[:, :, None] == np.arange(n_row)[None, None, :]).astype(np.float32)
    sel_col = ((col_off[:, :, None] == np.arange(n_col)[None, None, :]) & col_ok[:, :, None]).astype(np.float32)
    fill = np.where(col_ok, 0.0, NA_NEG).astype(np.float32)
    tab = jnp.einsum('hac,sea,qkc->hsqek', na_bias_l.astype(F32), jnp.asarray(sel_row), jnp.asarray(sel_col),
                     precision=lax.Precision.HIGHEST)
    tab = tab + jnp.asarray(fill)[None, None, :, None, :]
    h = na_bias_l.shape[0]
    return tab.reshape(h, NA_TAB, GRID_W, 2 * GRID_W)


def _na_latent_kernel(q_ref, k_ref, v_ref, kc_ref, vc_ref, tab_ref, o_ref, s_ref, *, scale, rows_total):
    g = pl.program_id(2)
    r0 = g * NA_QROWS
    start = jnp.clip(r0 - NA_ROWS // 2, 0, rows_total - NA_WROWS)
    nloc = NA_WROWS * GRID_W
    krows = pl.ds(pl.multiple_of(start * GRID_W, GRID_W), nloc)
    q = (q_ref[...].astype(F32) * scale).astype(BF16)
    kw = k_ref[krows, :]
    s_ref[:, 0:nloc] = _dot_nt(q, kw)
    s_ref[:, nloc:] = _dot_nt(q, kc_ref[...].astype(BF16))
    lane = lax.broadcasted_iota(jnp.int32, (1, LANES), 1)
    for qr in range(NA_QROWS):
        r = r0 + qr
        rs = jnp.clip(r - NA_ROWS // 2, 0, rows_total - NA_ROWS)
        for kp in range(NA_WROWS // 2):
            k0 = start + 2 * kp
            ok0 = (k0 >= rs) & (k0 < rs + NA_ROWS)
            ok1 = (k0 + 1 >= rs) & (k0 + 1 < rs + NA_ROWS)
            neg = jnp.where(lane < GRID_W, jnp.where(ok0, 0.0, NA_NEG), jnp.where(ok1, 0.0, NA_NEG))
            slot = k0 - r + (NA_ROWS - 1) + NA_QROWS
            s_ref[qr * GRID_W:(qr + 1) * GRID_W, kp * LANES:(kp + 1) * LANES] += tab_ref[slot] + neg
    s = s_ref[...]
    m = jnp.max(s, axis=-1, keepdims=True)
    p = jnp.exp(s - m)
    l = jnp.sum(p, axis=-1, keepdims=True)
    pb = p.astype(BF16)
    o = _dot(pb[:, 0:nloc], v_ref[krows, :]) + _dot(pb[:, nloc:], vc_ref[...].astype(BF16))
    o_ref[...] = (o / l).astype(o_ref.dtype)


def na_latent_attention(z, k_ctx, v_ctx, na_bias_l, row0, batch, n):
    h = NA_HEADS
    past = k_ctx.shape[2]
    rows_total = n // GRID_W
    tq = NA_QROWS * GRID_W
    qb = n // tq
    r0 = row0 // n
    q0 = row0 // tq
    cq, ck, cv = CD_NQ // LANES, CD_NK // LANES, CD_NV // LANES
    tab = _na_bias_table(na_bias_l)
    return pl.pallas_call(
        functools.partial(_na_latent_kernel, scale=float(NA_HD ** -0.5), rows_total=rows_total),
        grid=(batch, h, qb),
        in_specs=[pl.BlockSpec((tq, LANES), lambda b, hh, g: (q0 + b * qb + g, cq + hh)),
                  pl.BlockSpec((n, LANES), lambda b, hh, g: (r0 + b, ck + hh)),
                  pl.BlockSpec((n, LANES), lambda b, hh, g: (r0 + b, cv + hh)),
                  pl.BlockSpec((None, None, past, NA_HD), lambda b, hh, g: (b, hh, 0, 0)),
                  pl.BlockSpec((None, None, past, NA_HD), lambda b, hh, g: (b, hh, 0, 0)),
                  pl.BlockSpec((None, NA_TAB, GRID_W, 2 * GRID_W), lambda b, hh, g: (hh, 0, 0, 0))],
        out_specs=pl.BlockSpec((tq, LANES), lambda b, hh, g: (b * qb + g, hh)),
        out_shape=jax.ShapeDtypeStruct((batch * n, h * NA_HD), BF16),
        scratch_shapes=[pltpu.VMEM((tq, NA_WROWS * GRID_W + past), F32)],
        compiler_params=_cparams(("parallel", "parallel", "arbitrary")),
        name="na_latent_attention",
    )(z, z, z, k_ctx, v_ctx, tab)


def _prep_w_in_ab(w):
    d = w.shape[0]
    sizes = (512, 512, 1024, 1024, 2 * GATE_RANK, MLA_Q_LORA, MLA_KV_LORA, MLA_ROPE)
    gq, gk, gv, gr, glr, cq, ckv, kpe = jnp.split(w, np.cumsum(sizes)[:-1].tolist(), axis=1)
    pad = jnp.zeros((d, AB_WIDTH - AB_MISC - MLA_ROPE - 2 * GATE_RANK), w.dtype)
    return jnp.concatenate([gq, gk, gv, gr, cq, ckv, kpe, glr, pad], axis=1).astype(BF16)


def _prep_w_uq(w):
    r = w.shape[0]
    w3 = w.reshape(r, MLA_HEADS, MLA_NOPE + MLA_ROPE)
    nope = w3[:, :, :MLA_NOPE].reshape(r, MLA_HEADS * MLA_NOPE)
    pe = jnp.pad(w3[:, :, MLA_NOPE:], ((0, 0), (0, 0), (0, LANES - MLA_ROPE))).reshape(r, MLA_HEADS * LANES)
    return jnp.concatenate([nope, pe], axis=1).astype(BF16)


def _prep_w_ukv(w):
    r = w.shape[0]
    w3 = w.reshape(r, MLA_HEADS, MLA_NOPE + MLA_DV)
    return jnp.concatenate([w3[:, :, :MLA_NOPE].reshape(r, -1), w3[:, :, MLA_NOPE:].reshape(r, -1)], axis=1).astype(BF16)


def _prep_router(router_w):
    w = jnp.pad(_member_major(router_w.astype(F32)), ((0, 0), (0, LANES - N_EXPERTS)))
    hi = w.astype(BF16)
    lo = (w - hi.astype(F32)).astype(BF16)
    return jnp.stack([hi, lo], axis=0)


def kernel(x_prompt, x_sample, state_gla, cache_mla_ckv, cache_mla_kpe, state_ret, cache_na_k, cache_na_v,
           c, c_ctx, ada_w, ada_b, norm1, norm2, w_in_ab, gla_gate_w, gla_gate_b, gla_gain, mla_q_gain,
           mla_w_uq, mla_kv_gain, mla_w_ukv, w_in_cd, ret_log_decay, ret_gain, na_bias, w_out,
           router_w, router_bias, exp_w1, exp_w3, exp_w2, final_norm):
    bp, n_p, d = x_prompt.shape
    bs, n_s, _ = x_sample.shape
    past = cache_mla_ckv.shape[2]
    depth = ada_w.shape[0]
    tp, ts = bp * n_p, bs * n_s
    ys = [x_prompt.reshape(tp, d), x_sample.reshape(ts, d)]

    cond8 = jnp.zeros((8, d), F32).at[0].set(c_ctx).at[1:1 + bs].set(c)
    mod = adaln_all(cond8, ada_w, ada_b)
    rw_split = _prep_router(router_w)
    w_out_bf = w_out.astype(BF16)
    cos_t, sin_t = rope_tables(n_s, 512)
    cos_s, sin_s = cos_t[512:], sin_t[512:]
    cos_id, sin_id = cos_t[:n_p], sin_t[:n_p]

    outs = {}
    for l in range(depth):
        i = l // 2
        mod_l = mod[l].reshape(8, 1, 6 * d)
        final = l == depth - 1
        if l % 2 == 0:
            z = win_project(ys, mod_l, norm1[l], _prep_w_in_ab(w_in_ab[i]), tp, n_s)
            s0p = jnp.zeros((bp, 2, GLA_HEADS * GLA_DK, GLA_DV), F32)
            s0s = state_gla[:, i].reshape(bs, 2, GLA_HEADS * GLA_DK, GLA_DV)
            og_p, st_p = gla_mixer(z, gla_gate_w[i], gla_gate_b[i], gla_gain[i], s0p, 0, bp, n_p)
            og_s, _ = gla_mixer(z, gla_gate_w[i], gla_gate_b[i], gla_gain[i], s0s, tp, bs, n_s)
            st_p = st_p.reshape(bp, 2, GLA_HEADS, GLA_DK, GLA_DV)
            o_a = (og_p, og_s)
            wq, wkv = _prep_w_uq(mla_w_uq[i]), _prep_w_ukv(mla_w_ukv[i])
            q_m, kv, ckvn, kpe = mla_project(z, mla_q_gain[i], mla_kv_gain[i], wq, wkv, cos_t, sin_t, tp, n_s)
            kv_ctx = ctx_kv_project(cache_mla_ckv[:, i].reshape(bs * past, -1), wkv)
            kpe_ctx = jnp.pad(cache_mla_kpe[:, i].reshape(bs * past, -1), ((0, 0), (0, LANES - MLA_ROPE)))
            om_p = mla_prompt_attention(q_m, kv, kpe, bp, n_p)
            om_s = mla_sample_attention(q_m, kv, kpe, kv_ctx, kpe_ctx, tp, bs, n_s, past)
            o_b = (om_p, om_s)
            outs.setdefault('gla', []).append(st_p)
            outs.setdefault('ckv', []).append(ckvn[:tp].reshape(bp, n_p, -1))
            outs.setdefault('kpe', []).append(z[:tp, AB_MISC:AB_MISC + MLA_ROPE].astype(F32).reshape(bp, n_p, -1))
        else:
            z = win_project(ys, mod_l, norm1[l], w_in_cd[i].astype(BF16), tp, n_s)
            s0p = jnp.zeros((bp, 2, RET_HEADS * RET_DK, RET_DV), F32)
            s0s = state_ret[:, i].reshape(bs, 2, RET_HEADS * RET_DK, RET_DV)
            or_p, st_p = retention_mixer(z, ret_log_decay[i], ret_gain[i], s0p, cos_id, sin_id, 0, bp, n_p)
            or_s, _ = retention_mixer(z, ret_log_decay[i], ret_gain[i], s0s, cos_s, sin_s, tp, bs, n_s)
            st_p = st_p.reshape(bp, 2, RET_HEADS, RET_DK, RET_DV)
            o_a = (or_p, or_s)
            on_p, kn, vn = na_prompt_attention(z, bp, n_p)
            on_s = na_latent_attention(z, cache_na_k[:, i], cache_na_v[:, i], na_bias[i], tp, bs, n_s)
            o_b = (on_p, on_s)
            outs.setdefault('ret', []).append(st_p)
            outs.setdefault('nak', []).append(kn)
            outs.setdefault('nav', []).append(vn)
        y1, h2, logits = wout_project(o_a, o_b, ys, mod_l, w_out_bf[l], norm2[l], rw_split, tp, n_s)
        y = moe_layer(y1, h2, logits, mod_l, router_bias, exp_w1, exp_w3, exp_w2, l, final_norm, tp, n_s, final)
        ys = list(y) if final else [y]

    return (ys[0].reshape(bp, n_p, d), ys[1].reshape(bs, n_s, d),
            jnp.stack(outs['gla'], axis=1), jnp.stack(outs['ckv'], axis=1), jnp.stack(outs['kpe'], axis=1),
            jnp.stack(outs['ret'], axis=1), jnp.stack(outs['nak'], axis=1), jnp.stack(outs['nav'], axis=1))
```
